```python
import jax, jax.numpy as jnp
from jax import lax
import numpy as np

D_MODEL = 2048
BATCH = 16
SEQ = 2048
DEPTH = 4

N_MIXERS = 4
PLE_DIM = 256
D_FF = 5632
RMS_EPS = 1e-6
NEG_INF = -1e30

NSA_HEADS = 16
NSA_KV_GROUPS = 4
NSA_HEAD_DIM = D_MODEL // NSA_HEADS
NSA_CMP_LEN = 32
NSA_CMP_STRIDE = 16
NSA_CMP_HIDDEN = 2 * NSA_HEAD_DIM
NSA_SEL_BLOCK = 64
NSA_N_SEL = 8
NSA_WINDOW = 512
NSA_ROPE_DIM = NSA_HEAD_DIM // 4
ROPE_THETA = 500000.0
NSA_IN = NSA_HEADS * NSA_HEAD_DIM + 6 * NSA_KV_GROUPS * NSA_HEAD_DIM + 3 * NSA_HEADS

RWKV_HEAD_DIM = 64
RWKV_HEADS = D_MODEL // RWKV_HEAD_DIM
RWKV_DECAY_LORA = max(32, int(round(1.8 * D_MODEL ** 0.5 / 32)) * 32)
RWKV_ICL_LORA = max(32, int(round(1.8 * D_MODEL ** 0.5 / 32)) * 32)
RWKV_GATE_LORA = max(32, int(round(0.6 * D_MODEL ** 0.8 / 32)) * 32)
RWKV_GN_EPS = 64e-5

FOX_HEADS = 16
FOX_HEAD_DIM = D_MODEL // FOX_HEADS
FOX_BLOCK = 128
FOX_IN = 3 * FOX_HEADS * FOX_HEAD_DIM + FOX_HEADS

RET_HEADS = 8
RET_QK_DIM = D_MODEL // RET_HEADS
RET_V_DIM = 2 * D_MODEL // RET_HEADS
RET_CHUNK = 128
RET_THETA = 10000.0
RET_GN_EPS = 1e-5
RET_IN = 2 * RET_HEADS * RET_QK_DIM + 2 * RET_HEADS * RET_V_DIM

kernel_name = 'hybrid_nsa_rwkv7_fox_retnet_trunk'


def _mixer_layers(m):
    return len(range(m, DEPTH, N_MIXERS))


def rms_norm(x, g):
    xf = x.astype(jnp.float32)
    y = xf * lax.rsqrt(jnp.mean(xf * xf, axis=-1, keepdims=True) + RMS_EPS)
    return (y * g.astype(jnp.float32)).astype(x.dtype)


def head_norm(y, eps):
    y = y.astype(jnp.float32)
    mu = jnp.mean(y, axis=-1, keepdims=True)
    var = jnp.mean(jnp.square(y - mu), axis=-1, keepdims=True)
    return (y - mu) * lax.rsqrt(var + eps)


def swiglu(x, w_in, w_out):
    gate, up = jnp.split(x @ w_in, 2, axis=-1)
    return (jax.nn.silu(gate) * up) @ w_out


def rotary(x, pos, theta, rot_dim):
    half = rot_dim // 2
    inv_freq = theta ** (-jnp.arange(half, dtype=jnp.float32) / half)
    ang = pos.astype(jnp.float32)[:, None] * inv_freq
    shape = (1, ang.shape[0]) + (1,) * (x.ndim - 3) + (half,)
    cos = jnp.cos(ang).reshape(shape)
    sin = jnp.sin(ang).reshape(shape)
    xr = x[..., :rot_dim].astype(jnp.float32)
    x1, x2 = xr[..., :half], xr[..., half:]
    rot = jnp.concatenate([x1 * cos - x2 * sin, x2 * cos + x1 * sin], axis=-1).astype(x.dtype)
    if rot_dim == x.shape[-1]:
        return rot
    return jnp.concatenate([rot, x[..., rot_dim:]], axis=-1)


def nsa_attention(x, w_in, cmp_pe, cmp_w1, cmp_w2, w_out):
    B, S, D = x.shape
    H, G, dh = NSA_HEADS, NSA_KV_GROUPS, NSA_HEAD_DIM
    R = H // G
    L, stride, blk, W = NSA_CMP_LEN, NSA_CMP_STRIDE, NSA_SEL_BLOCK, NSA_WINDOW
    f32 = jnp.float32
    scale = dh ** -0.5
    pos = jnp.arange(S)
    proj = x @ w_in
    hq, hk = H * dh, G * dh
    q = proj[..., :hq].reshape(B, S, G, R, dh)
    kv = proj[..., hq:hq + 6 * hk].reshape(B, S, 6, G, dh)
    gates = jax.nn.sigmoid(proj[..., hq + 6 * hk:].reshape(B, S, G, R, 3))
    q = rotary(q, pos, ROPE_THETA, NSA_ROPE_DIM)
    kc, ks, kw = (rotary(kv[:, :, c], pos, ROPE_THETA, NSA_ROPE_DIM) for c in (0, 2, 4))
    vc, vs, vw = kv[:, :, 1], kv[:, :, 3], kv[:, :, 5]

    n_cmp = (S - L) // stride + 1
    idx_np = np.arange(n_cmp)[:, None] * stride + np.arange(L)[None, :]
    cmp_end = jnp.asarray(idx_np[:, -1])

    def compress(t, c):
        blocks = t[:, idx_np] + cmp_pe[c][None, None, :, None, :]
        hid = jax.nn.gelu(jnp.einsum('bnlgd,ldf->bngf', blocks, cmp_w1[c]))
        return jnp.einsum('bngf,fe->bnge', hid, cmp_w2[c])

    k_cmp, v_cmp = compress(kc, 0), compress(vc, 1)
    s_cmp = jnp.einsum('bsgrd,bngd->bgrsn', q, k_cmp).astype(f32) * scale
    m_cmp = cmp_end[None, :] <= pos[:, None]
    p_cmp = jax.nn.softmax(jnp.where(m_cmp, s_cmp, NEG_INF), axis=-1) * m_cmp
    o_cmp = jnp.einsum('bgrsn,bngd->bsgrd', p_cmp.astype(x.dtype), v_cmp)

    n_blk = S // blk
    starts = np.arange(n_cmp) * stride
    bstart = np.arange(n_blk) * blk
    overlap = (starts[:, None] <= bstart[None, :] + blk - 1) & (starts[:, None] + L - 1 >= bstart[None, :])
    imp = jnp.einsum('bgrsn,nj->bgsj', p_cmp, jnp.asarray(overlap, f32))
    imp = imp.reshape(B, G, n_blk, blk, n_blk).sum(axis=3)
    qb_i = np.arange(n_blk)[:, None]
    kb_j = np.arange(n_blk)[None, :]
    valid = kb_j <= qb_i
    forced = (kb_j == 0) | (kb_j == qb_i) | (kb_j == qb_i - 1)
    imp = jnp.where(forced, jnp.inf, jnp.where(valid, imp, -jnp.inf))
    n_sel = min(NSA_N_SEL, n_blk)
    _, sel = lax.top_k(imp, n_sel)

    ks_t = jnp.transpose(ks, (0, 2, 1, 3))
    vs_t = jnp.transpose(vs, (0, 2, 1, 3))
    kw_p = jnp.pad(kw, ((0, 0), (W, 0), (0, 0), (0, 0)))
    vw_p = jnp.pad(vw, ((0, 0), (W, 0), (0, 0), (0, 0)))
    b_idx = jnp.arange(B)[:, None, None]
    g_idx = jnp.arange(G)[None, :, None]

    def qblock(args):
        i, sel_i = args
        q0 = i * blk
        qi = lax.dynamic_slice_in_dim(q, q0, blk, axis=1)
        qpos = q0 + jnp.arange(blk)
        tok = (sel_i[..., None] * blk + jnp.arange(blk)).reshape(B, G, n_sel * blk)
        k_sel = ks_t[b_idx, g_idx, tok]
        v_sel = vs_t[b_idx, g_idx, tok]
        s = jnp.einsum('bqgrd,bgnd->bgrqn', qi, k_sel).astype(f32) * scale
        msk = tok[:, :, None, None, :] <= qpos[None, None, None, :, None]
        pr = jax.nn.softmax(jnp.where(msk, s, NEG_INF), axis=-1)
        o_sel = jnp.einsum('bgrqn,bgnd->bqgrd', pr.astype(x.dtype), v_sel)
        kwi = lax.dynamic_slice_in_dim(kw_p, q0, W + blk, axis=1)
        vwi = lax.dynamic_slice_in_dim(vw_p, q0, W + blk, axis=1)
        kpos = q0 - W + jnp.arange(W + blk)
        dist = qpos[:, None] - kpos[None, :]
        mw = (dist >= 0) & (dist < W) & (kpos[None, :] >= 0)
        s = jnp.einsum('bqgrd,bkgd->bgrqk', qi, kwi).astype(f32) * scale
        pr = jax.nn.softmax(jnp.where(mw, s, NEG_INF), axis=-1)
        o_win = jnp.einsum('bgrqk,bkgd->bqgrd', pr.astype(x.dtype), vwi)
        return o_sel, o_win

    o_sel, o_win = lax.map(qblock, (jnp.arange(n_blk), jnp.transpose(sel, (2, 0, 1, 3))))
    o_sel = jnp.moveaxis(o_sel, 0, 1).reshape(B, S, G, R, dh)
    o_win = jnp.moveaxis(o_win, 0, 1).reshape(B, S, G, R, dh)
    o = gates[..., 0, None] * o_cmp + gates[..., 1, None] * o_sel + gates[..., 2, None] * o_win
    return o.reshape(B, S, H * dh) @ w_out


def rwkv7_time_mix(x, mu, w_rkv, w0, w1, w2, a0, a1, a2, g1, g2, k_k, k_a, r_k, ln_gb, w_out):
    B, S, D = x.shape
    H, N = RWKV_HEADS, RWKV_HEAD_DIM
    f32 = jnp.float32
    xx = jnp.pad(x, ((0, 0), (1, 0), (0, 0)))[:, :-1] - x
    xm = x[None] + xx[None] * mu[:, None, None, :]
    r, k, v = jnp.einsum('cbsd,cde->cbse', xm[:3], w_rkv)
    w_log = -jax.nn.softplus(-(w0 + jnp.tanh(xm[3] @ w1) @ w2)) - 0.5
    decay = jnp.exp(-jnp.exp(w_log.astype(f32)))
    a = jax.nn.sigmoid(a0 + (xm[4] @ a1) @ a2)
    g = jax.nn.sigmoid(xm[5] @ g1) @ g2
    kk = (k * k_k).astype(f32).reshape(B, S, H, N)
    kk = kk / jnp.maximum(jnp.linalg.norm(kk, axis=-1, keepdims=True), 1e-12)
    k = k * (1 + (a - 1) * k_a)

    def heads(t):
        return t.astype(f32).reshape(B, S, H, N)

    r_h, k_h, v_h, w_h, a_h = heads(r), heads(k), heads(v), heads(decay), heads(a)

    def step(state, inp):
        r_t, w_t, k_t, v_t, kk_t, a_t = inp
        sa = jnp.einsum('bhvk,bhk->bhv', state, -kk_t)
        state = (state * w_t[:, :, None, :] + sa[..., None] * (kk_t * a_t)[:, :, None, :]
                 + v_t[..., None] * k_t[:, :, None, :])
        return state, jnp.einsum('bhvk,bhk->bhv', state, r_t)

    xs = tuple(jnp.moveaxis(t, 1, 0) for t in (r_h, w_h, k_h, v_h, kk, a_h))
    _, y = lax.scan(step, jnp.zeros((B, H, N, N), f32), xs)
    y = jnp.moveaxis(y, 0, 1)
    y = head_norm(y, RWKV_GN_EPS).reshape(B, S, D) * ln_gb[0] + ln_gb[1]
    bonus = jnp.sum(r_h * k_h * r_k, axis=-1, keepdims=True) * v_h
    y = (y + bonus.reshape(B, S, D)).astype(x.dtype) * g
    return y @ w_out


def forgetting_attention(x, w_in, b_f, w_out):
    B, S, D = x.shape
    H, dh, Q = FOX_HEADS, FOX_HEAD_DIM, FOX_BLOCK
    f32 = jnp.float32
    proj = x @ w_in
    q, k, v = (proj[..., c * H * dh:(c + 1) * H * dh].reshape(B, S, H, dh) for c in range(3))
    log_f = jax.nn.log_sigmoid((proj[..., 3 * H * dh:] + b_f).astype(f32))
    cum = jnp.transpose(jnp.cumsum(log_f, axis=1), (0, 2, 1))
    kpos = jnp.arange(S)
    scale = dh ** -0.5

    def block(i):
        q0 = i * Q
        qi = lax.dynamic_slice_in_dim(q, q0, Q, axis=1)
        ci = lax.dynamic_slice_in_dim(cum, q0, Q, axis=2)
        s = (jnp.einsum('bqhd,bkhd->bhqk', qi, k).astype(f32) * scale
             + (ci[..., :, None] - cum[..., None, :]))
        mask = kpos[None, :] <= (q0 + jnp.arange(Q))[:, None]
        pr = jax.nn.softmax(jnp.where(mask, s, NEG_INF), axis=-1)
        return jnp.einsum('bhqk,bkhd->bqhd', pr.astype(v.dtype), v)

    o = lax.map(block, jnp.arange(S // Q))
    return jnp.moveaxis(o, 0, 1).reshape(B, S, H * dh) @ w_out


def retention(x, w_in, gn_g, w_out):
    B, S, D = x.shape
    H, dk, dv, C = RET_HEADS, RET_QK_DIM, RET_V_DIM, RET_CHUNK
    f32 = jnp.float32
    proj = x @ w_in
    q = proj[..., :H * dk].reshape(B, S, H, dk)
    k = proj[..., H * dk:2 * H * dk].reshape(B, S, H, dk)
    v = proj[..., 2 * H * dk:2 * H * dk + H * dv].reshape(B, S, H, dv).astype(f32)
    g = proj[..., 2 * H * dk + H * dv:]
    pos = jnp.arange(S)
    q = rotary(q, pos, RET_THETA, dk).astype(f32)
    k = rotary(k, pos, RET_THETA, dk).astype(f32) * dk ** -0.5
    log_gamma = jnp.log(1.0 - 2.0 ** (-5.0 - jnp.arange(H, dtype=f32)))
    idx = jnp.arange(C, dtype=f32)
    diff = idx[:, None] - idx[None, :]
    decay_mask = jnp.where(diff >= 0, jnp.exp(diff * log_gamma[:, None, None]), 0.0)
    q_scale = jnp.exp((idx + 1.0)[None, :] * log_gamma[:, None]).T
    k_scale = jnp.exp((C - 1.0 - idx)[None, :] * log_gamma[:, None]).T
    chunk_decay = jnp.exp(C * log_gamma)
    n = S // C

    def chunks(t):
        return jnp.moveaxis(t.reshape(B, n, C, H, t.shape[-1]), 1, 0)

    def step(Rs, inp):
        qc, kc, vc = inp
        inner = jnp.einsum('bnhd,bmhd->bhnm', qc, kc) * decay_mask
        o = (jnp.einsum('bhnm,bmhe->bnhe', inner, vc)
             + jnp.einsum('bnhd,bhde->bnhe', qc, Rs) * q_scale[None, :, :, None])
        Rs = Rs * chunk_decay[None, :, None, None] + jnp.einsum('bmhd,bmhe->bhde', kc * k_scale[None, :, :, None], vc)
        return Rs, o

    _, o = lax.scan(step, jnp.zeros((B, H, dk, dv), f32), (chunks(q), chunks(k), chunks(v)))
    o = jnp.moveaxis(o, 0, 1).reshape(B, S, H, dv)
    o = head_norm(o, RET_GN_EPS).reshape(B, S, H * dv) * gn_g
    return (jax.nn.silu(g) * o.astype(x.dtype)) @ w_out


def setup_inputs(seed: int = 0) -> dict:
    key = jax.random.key(seed)
    keys = iter(jax.random.split(key, 48))
    f32 = jnp.float32

    def nrm(shape, scale=1.0):
        return jax.random.normal(next(keys), shape, f32) * scale

    def unif(shape, lo, hi):
        return jax.random.uniform(next(keys), shape, f32, lo, hi)

    D = D_MODEL
    nA, nB, nC, nD = (_mixer_layers(m) for m in range(N_MIXERS))
    L, dh = NSA_CMP_LEN, NSA_HEAD_DIM
    return {
        'x': nrm((BATCH, SEQ, D)),
        'p': nrm((DEPTH, BATCH, SEQ, PLE_DIM)),
        'norm_g': 1.0 + nrm((DEPTH, 8, D), 0.02),
        'ffn_w_in': nrm((DEPTH, 2, D, 2 * D_FF), D ** -0.5),
        'ffn_w_out': nrm((DEPTH, 2, D_FF, D), D_FF ** -0.5),
        'ple_w_proj': nrm((DEPTH, PLE_DIM, D), PLE_DIM ** -0.5),
        'ple_w_gate': nrm((DEPTH, D, D), D ** -0.5),
        'nsa_w_in': nrm((nA, D, NSA_IN), D ** -0.5),
        'nsa_cmp_pe': nrm((nA, 2, L, dh), 0.02),
        'nsa_cmp_w1': nrm((nA, 2, L, dh, NSA_CMP_HIDDEN), (L * dh) ** -0.5),
        'nsa_cmp_w2': nrm((nA, 2, NSA_CMP_HIDDEN, dh), NSA_CMP_HIDDEN ** -0.5),
        'nsa_w_out': nrm((nA, D, D), D ** -0.5),
        'rwkv_mu': unif((nB, 6, D), 0.0, 1.0),
        'rwkv_w_rkv': nrm((nB, 3, D, D), D ** -0.5),
        'rwkv_w0': unif((nB, D), -6.0, -1.0),
        'rwkv_w1': nrm((nB, D, RWKV_DECAY_LORA), D ** -0.5),
        'rwkv_w2': nrm((nB, RWKV_DECAY_LORA, D), 0.1 * RWKV_DECAY_LORA ** -0.5),
        'rwkv_a0': nrm((nB, D), 0.1),
        'rwkv_a1': nrm((nB, D, RWKV_ICL_LORA), D ** -0.5),
        'rwkv_a2': nrm((nB, RWKV_ICL_LORA, D), 0.1 * RWKV_ICL_LORA ** -0.5),
        'rwkv_g1': nrm((nB, D, RWKV_GATE_LORA), D ** -0.5),
        'rwkv_g2': nrm((nB, RWKV_GATE_LORA, D), RWKV_GATE_LORA ** -0.5),
        'rwkv_k_k': 0.85 + nrm((nB, D), 0.02),
        'rwkv_k_a': 1.0 + nrm((nB, D), 0.02),
        'rwkv_r_k': nrm((nB, RWKV_HEADS, RWKV_HEAD_DIM), 0.1),
        'rwkv_ln': jnp.concatenate([1.0 + nrm((nB, 1, D), 0.02), nrm((nB, 1, D), 0.02)], axis=1),
        'rwkv_w_out': nrm((nB, D, D), D ** -0.5),
        'fox_w_in': nrm((nC, D, FOX_IN), D ** -0.5),
        'fox_b_f': nrm((nC, FOX_HEADS), 0.1),
        'fox_w_out': nrm((nC, D, D), D ** -0.5),
        'ret_w_in': nrm((nD, D, RET_IN), D ** -0.5),
        'ret_gn_g': 1.0 + nrm((nD, RET_HEADS * RET_V_DIM), 0.02),
        'ret_w_out': nrm((nD, RET_HEADS * RET_V_DIM, D), (RET_HEADS * RET_V_DIM) ** -0.5),
    }


def reference(x, p, norm_g, ffn_w_in, ffn_w_out, ple_w_proj, ple_w_gate,
              nsa_w_in, nsa_cmp_pe, nsa_cmp_w1, nsa_cmp_w2, nsa_w_out,
              rwkv_mu, rwkv_w_rkv, rwkv_w0, rwkv_w1, rwkv_w2, rwkv_a0, rwkv_a1, rwkv_a2,
              rwkv_g1, rwkv_g2, rwkv_k_k, rwkv_k_a, rwkv_r_k, rwkv_ln, rwkv_w_out,
              fox_w_in, fox_b_f, fox_w_out, ret_w_in, ret_gn_g, ret_w_out):
    h = x
    for i in range(DEPTH):
        m, j = i % N_MIXERS, i // N_MIXERS
        ng = norm_g[i]
        h = h + 0.5 * rms_norm(swiglu(rms_norm(h, ng[0]), ffn_w_in[i, 0], ffn_w_out[i, 0]), ng[1])
        u = rms_norm(h, ng[2])
        if m == 0:
            y = nsa_attention(u, nsa_w_in[j], nsa_cmp_pe[j], nsa_cmp_w1[j], nsa_cmp_w2[j], nsa_w_out[j])
        elif m == 1:
            y = rwkv7_time_mix(u, rwkv_mu[j], rwkv_w_rkv[j], rwkv_w0[j], rwkv_w1[j], rwkv_w2[j],
                               rwkv_a0[j], rwkv_a1[j], rwkv_a2[j], rwkv_g1[j], rwkv_g2[j],
                               rwkv_k_k[j], rwkv_k_a[j], rwkv_r_k[j], rwkv_ln[j], rwkv_w_out[j])
        elif m == 2:
            y = forgetting_attention(u, fox_w_in[j], fox_b_f[j], fox_w_out[j])
        else:
            y = retention(u, ret_w_in[j], ret_gn_g[j], ret_w_out[j])
        h = h + rms_norm(y, ng[3])
        h = h + 0.5 * rms_norm(swiglu(rms_norm(h, ng[4]), ffn_w_in[i, 1], ffn_w_out[i, 1]), ng[5])
        e = p[i] @ ple_w_proj[i]
        gate = jax.nn.sigmoid(rms_norm(h, ng[6]) @ ple_w_gate[i])
        h = h + rms_norm(e * gate, ng[7])
    return h
```

```python
import functools

import jax
import jax.numpy as jnp
import numpy as np
from jax import lax
from jax.experimental import pallas as pl
from jax.experimental.pallas import tpu as pltpu

D_MODEL = 2048
BATCH = 16
SEQ = 2048
DEPTH = 4
N_MIXERS = 4
PLE_DIM = 256
D_FF = 5632
RMS_EPS = 1e-6
NEG_INF = -1e30

NSA_HEADS = 16
NSA_KV_GROUPS = 4
NSA_HEAD_DIM = D_MODEL // NSA_HEADS
NSA_CMP_LEN = 32
NSA_CMP_STRIDE = 16
NSA_CMP_HIDDEN = 2 * NSA_HEAD_DIM
NSA_SEL_BLOCK = 64
NSA_N_SEL = 8
NSA_WINDOW = 512
NSA_ROPE_DIM = NSA_HEAD_DIM // 4
ROPE_THETA = 500000.0
NSA_IN = NSA_HEADS * NSA_HEAD_DIM + 6 * NSA_KV_GROUPS * NSA_HEAD_DIM + 3 * NSA_HEADS

RWKV_HEAD_DIM = 64
RWKV_HEADS = D_MODEL // RWKV_HEAD_DIM
RWKV_GN_EPS = 64e-5

FOX_HEADS = 16
FOX_HEAD_DIM = D_MODEL // FOX_HEADS
FOX_BLOCK = 128
FOX_IN = 3 * FOX_HEADS * FOX_HEAD_DIM + FOX_HEADS

RET_HEADS = 8
RET_QK_DIM = D_MODEL // RET_HEADS
RET_V_DIM = 2 * D_MODEL // RET_HEADS
RET_CHUNK = 128
RET_THETA = 10000.0
RET_GN_EPS = 1e-5
RET_IN = 2 * RET_HEADS * RET_QK_DIM + 2 * RET_HEADS * RET_V_DIM

V7X_VMEM_LIMIT_BYTES = 56 * 1024 * 1024
LANE = 128

F32 = jnp.float32
BF16 = jnp.bfloat16


def _cparams(*sem):
    return pltpu.CompilerParams(dimension_semantics=sem, vmem_limit_bytes=V7X_VMEM_LIMIT_BYTES)


def _rms(x, g):
    return x * lax.rsqrt(jnp.mean(x * x, axis=-1, keepdims=True) + RMS_EPS) * g


def _ffn_body(h_ref, g0_ref, g1_ref, wg_ref, wu_ref, wo_ref, o_ref, xn_ref, acc_ref):
    f = pl.program_id(1)

    @pl.when(f == 0)
    def _():
        xn_ref[...] = _rms(h_ref[...], g0_ref[...]).astype(BF16)
        acc_ref[...] = jnp.zeros_like(acc_ref)

    xn = xn_ref[...]
    gate = jnp.dot(xn, wg_ref[...], preferred_element_type=F32)
    up = jnp.dot(xn, wu_ref[...], preferred_element_type=F32)
    act = (gate * jax.nn.sigmoid(gate)) * up
    acc_ref[...] += jnp.dot(act.astype(BF16), wo_ref[...], preferred_element_type=F32)

    @pl.when(f == pl.num_programs(1) - 1)
    def _():
        o_ref[...] = h_ref[...] + 0.5 * _rms(acc_ref[...], g1_ref[...])


def ffn_half_step(h, g0, g1, w_in, w_out, *, tm=512, tf=512):
    T, D = h.shape
    nf = D_FF // tf
    return pl.pallas_call(
        _ffn_body,
        out_shape=jax.ShapeDtypeStruct((T, D), F32),
        grid=(T // tm, nf),
        in_specs=[
            pl.BlockSpec((tm, D), lambda i, f: (i, 0)),
            pl.BlockSpec((1, D), lambda i, f: (0, 0)),
            pl.BlockSpec((1, D), lambda i, f: (0, 0)),
            pl.BlockSpec((D, tf), lambda i, f: (0, f)),
            pl.BlockSpec((D, tf), lambda i, f: (0, f + nf)),
            pl.BlockSpec((tf, D), lambda i, f: (f, 0)),
        ],
        out_specs=pl.BlockSpec((tm, D), lambda i, f: (i, 0)),
        scratch_shapes=[pltpu.VMEM((tm, D), BF16), pltpu.VMEM((tm, D), F32)],
        compiler_params=_cparams("parallel", "arbitrary"),
        name="ffn_half_step",
    )(h, g0.reshape(1, D), g1.reshape(1, D), w_in, w_in, w_out)


def _norm_mm_body(x_ref, g_ref, w_ref, o_ref, xn_ref):
    @pl.when(pl.program_id(1) == 0)
    def _():
        xn_ref[...] = _rms(x_ref[...], g_ref[...]).astype(BF16)

    o_ref[...] = jnp.dot(xn_ref[...], w_ref[...], preferred_element_type=F32).astype(o_ref.dtype)


def _mm_body(x_ref, w_ref, o_ref):
    o_ref[...] = jnp.dot(x_ref[...].astype(BF16), w_ref[...],
                         preferred_element_type=F32).astype(o_ref.dtype)


def _col_tile(n, cap=1024):
    best = LANE
    for t in range(LANE, cap + 1, LANE):
        if n % t == 0:
            best = t
    return best


def norm_matmul(x, g, w, *, out_dtype=F32, tm=1024):
    T, K = x.shape
    N = w.shape[1]
    tn = _col_tile(N)
    return pl.pallas_call(
        _norm_mm_body,
        out_shape=jax.ShapeDtypeStruct((T, N), out_dtype),
        grid=(T // tm, N // tn),
        in_specs=[
            pl.BlockSpec((tm, K), lambda i, j: (i, 0)),
            pl.BlockSpec((1, K), lambda i, j: (0, 0)),
            pl.BlockSpec((K, tn), lambda i, j: (0, j)),
        ],
        out_specs=pl.BlockSpec((tm, tn), lambda i, j: (i, j)),
        scratch_shapes=[pltpu.VMEM((tm, K), BF16)],
        compiler_params=_cparams("parallel", "arbitrary"),
        name="norm_matmul",
    )(x, g.reshape(1, K), w)


def matmul(x, w, *, out_dtype=F32, tm=1024):
    T, K = x.shape
    N = w.shape[1]
    tn = _col_tile(N)
    return pl.pallas_call(
        _mm_body,
        out_shape=jax.ShapeDtypeStruct((T, N), out_dtype),
        grid=(T // tm, N // tn),
        in_specs=[
            pl.BlockSpec((tm, K), lambda i, j: (i, 0)),
            pl.BlockSpec((K, tn), lambda i, j: (0, j)),
        ],
        out_specs=pl.BlockSpec((tm, tn), lambda i, j: (i, j)),
        compiler_params=_cparams("parallel", "arbitrary"),
        name="matmul",
    )(x, w)


def _mm_res_body(y_ref, w_ref, g_ref, h_ref, o_ref, acc_ref):
    k = pl.program_id(1)

    @pl.when(k == 0)
    def _():
        acc_ref[...] = jnp.zeros_like(acc_ref)

    acc_ref[...] += jnp.dot(y_ref[...].astype(BF16), w_ref[...], preferred_element_type=F32)

    @pl.when(k == pl.num_programs(1) - 1)
    def _():
        o_ref[...] = h_ref[...] + _rms(acc_ref[...], g_ref[...])


def matmul_norm_residual(y, w, g, h, *, tm=512, tk=1024):
    T, K = y.shape
    D = w.shape[1]
    return pl.pallas_call(
        _mm_res_body,
        out_shape=jax.ShapeDtypeStruct((T, D), F32),
        grid=(T // tm, K // tk),
        in_specs=[
            pl.BlockSpec((tm, tk), lambda i, k: (i, k)),
            pl.BlockSpec((tk, D), lambda i, k: (k, 0)),
            pl.BlockSpec((1, D), lambda i, k: (0, 0)),
            pl.BlockSpec((tm, D), lambda i, k: (i, 0)),
        ],
        out_specs=pl.BlockSpec((tm, D), lambda i, k: (i, 0)),
        scratch_shapes=[pltpu.VMEM((tm, D), F32)],
        compiler_params=_cparams("parallel", "arbitrary"),
        name="matmul_norm_residual",
    )(y, w, g.reshape(1, D), h)


def _ple_body(h_ref, p_ref, g6_ref, g7_ref, wp_ref, wg_ref, o_ref):
    h = h_ref[...]
    xn = _rms(h, g6_ref[...]).astype(BF16)
    z = jnp.dot(xn, wg_ref[...], preferred_element_type=F32)
    e = jnp.dot(p_ref[...].astype(BF16), wp_ref[...], preferred_element_type=F32)
    o_ref[...] = h + _rms(e * jax.nn.sigmoid(z), g7_ref[...])


def ple_step(h, p, g6, g7, wp, wg, *, tm=256):
    T, D = h.shape
    P = p.shape[1]
    return pl.pallas_call(
        _ple_body,
        out_shape=jax.ShapeDtypeStruct((T, D), F32),
        grid=(T // tm,),
        in_specs=[
            pl.BlockSpec((tm, D), lambda i: (i, 0)),
            pl.BlockSpec((tm, P), lambda i: (i, 0)),
            pl.BlockSpec((1, D), lambda i: (0, 0)),
            pl.BlockSpec((1, D), lambda i: (0, 0)),
            pl.BlockSpec((P, D), lambda i: (0, 0)),
            pl.BlockSpec((D, D), lambda i: (0, 0)),
        ],
        out_specs=pl.BlockSpec((tm, D), lambda i: (i, 0)),
        compiler_params=_cparams("parallel"),
        name="ple_step",
    )(h, p, g6.reshape(1, D), g7.reshape(1, D), wp, wg)


def _head_norm(y, eps):
    y = y.astype(F32)
    mu = jnp.mean(y, axis=-1, keepdims=True)
    var = jnp.mean(jnp.square(y - mu), axis=-1, keepdims=True)
    return (y - mu) * lax.rsqrt(var + eps)


def _rotary(x, pos, theta, rot_dim):
    half = rot_dim // 2
    inv_freq = theta ** (-jnp.arange(half, dtype=F32) / half)
    ang = pos.astype(F32)[:, None] * inv_freq
    shape = (1, ang.shape[0]) + (1,) * (x.ndim - 3) + (half,)
    cos = jnp.cos(ang).reshape(shape)
    sin = jnp.sin(ang).reshape(shape)
    xr = x[..., :rot_dim].astype(F32)
    x1, x2 = xr[..., :half], xr[..., half:]
    rot = jnp.concatenate([x1 * cos - x2 * sin, x2 * cos + x1 * sin], axis=-1).astype(x.dtype)
    if rot_dim == x.shape[-1]:
        return rot
    return jnp.concatenate([rot, x[..., rot_dim:]], axis=-1)


def _pad_cols(w, n):
    return jnp.pad(w, ((0, 0), (0, n - w.shape[1])))


def _nsa_core(proj, cmp_pe, cmp_w1, cmp_w2):
    B, S = BATCH, SEQ
    H, G, dh = NSA_HEADS, NSA_KV_GROUPS, NSA_HEAD_DIM
    R = H // G
    L, stride, blk, W = NSA_CMP_LEN, NSA_CMP_STRIDE, NSA_SEL_BLOCK, NSA_WINDOW
    scale = dh ** -0.5
    pos = jnp.arange(S)
    hq, hk = H * dh, G * dh
    q = proj[..., :hq].reshape(B, S, G, R, dh)
    kv = proj[..., hq:hq + 6 * hk].reshape(B, S, 6, G, dh)
    gates = jax.nn.sigmoid(proj[..., hq + 6 * hk:hq + 6 * hk + 3 * H].reshape(B, S, G, R, 3))
    q = _rotary(q, pos, ROPE_THETA, NSA_ROPE_DIM)
    kc, ks, kw = (_rotary(kv[:, :, c], pos, ROPE_THETA, NSA_ROPE_DIM) for c in (0, 2, 4))
    vc, vs, vw = kv[:, :, 1], kv[:, :, 3], kv[:, :, 5]

    n_cmp = (S - L) // stride + 1
    idx_np = np.arange(n_cmp)[:, None] * stride + np.arange(L)[None, :]
    cmp_end = jnp.asarray(idx_np[:, -1])

    def compress(t, c):
        blocks = t[:, idx_np] + cmp_pe[c][None, None, :, None, :]
        hid = jax.nn.gelu(jnp.einsum('bnlgd,ldf->bngf', blocks, cmp_w1[c]))
        return jnp.einsum('bngf,fe->bnge', hid, cmp_w2[c])

    k_cmp, v_cmp = compress(kc, 0), compress(vc, 1)
    s_cmp = jnp.einsum('bsgrd,bngd->bgrsn', q, k_cmp).astype(F32) * scale
    m_cmp = cmp_end[None, :] <= pos[:, None]
    p_cmp = jax.nn.softmax(jnp.where(m_cmp, s_cmp, NEG_INF), axis=-1) * m_cmp
    o_cmp = jnp.einsum('bgrsn,bngd->bsgrd', p_cmp, v_cmp)

    n_blk = S // blk
    starts = np.arange(n_cmp) * stride
    bstart = np.arange(n_blk) * blk
    overlap = (starts[:, None] <= bstart[None, :] + blk - 1) & (starts[:, None] + L - 1 >= bstart[None, :])
    imp = jnp.einsum('bgrsn,nj->bgsj', p_cmp, jnp.asarray(overlap, F32))
    imp = imp.reshape(B, G, n_blk, blk, n_blk).sum(axis=3)
    qb_i = np.arange(n_blk)[:, None]
    kb_j = np.arange(n_blk)[None, :]
    valid = kb_j <= qb_i
    forced = (kb_j == 0) | (kb_j == qb_i) | (kb_j == qb_i - 1)
    imp = jnp.where(forced, jnp.inf, jnp.where(valid, imp, -jnp.inf))
    n_sel = min(NSA_N_SEL, n_blk)
    _, sel = lax.top_k(imp, n_sel)

    ks_t = jnp.transpose(ks, (0, 2, 1, 3))
    vs_t = jnp.transpose(vs, (0, 2, 1, 3))
    kw_p = jnp.pad(kw, ((0, 0), (W, 0), (0, 0), (0, 0)))
    vw_p = jnp.pad(vw, ((0, 0), (W, 0), (0, 0), (0, 0)))
    b_idx = jnp.arange(B)[:, None, None]
    g_idx = jnp.arange(G)[None, :, None]

    def qblock(args):
        i, sel_i = args
        q0 = i * blk
        qi = lax.dynamic_slice_in_dim(q, q0, blk, axis=1)
        qpos = q0 + jnp.arange(blk)
        tok = (sel_i[..., None] * blk + jnp.arange(blk)).reshape(B, G, n_sel * blk)
        k_sel = ks_t[b_idx, g_idx, tok]
        v_sel = vs_t[b_idx, g_idx, tok]
        s = jnp.einsum('bqgrd,bgnd->bgrqn', qi, k_sel).astype(F32) * scale
        msk = tok[:, :, None, None, :] <= qpos[None, None, None, :, None]
        pr = jax.nn.softmax(jnp.where(msk, s, NEG_INF), axis=-1)
        o_sel = jnp.einsum('bgrqn,bgnd->bqgrd', pr, v_sel)
        kwi = lax.dynamic_slice_in_dim(kw_p, q0, W + blk, axis=1)
        vwi = lax.dynamic_slice_in_dim(vw_p, q0, W + blk, axis=1)
        kpos = q0 - W + jnp.arange(W + blk)
        dist = qpos[:, None] - kpos[None, :]
        mw = (dist >= 0) & (dist < W) & (kpos[None, :] >= 0)
        s = jnp.einsum('bqgrd,bkgd->bgrqk', qi, kwi).astype(F32) * scale
        pr = jax.nn.softmax(jnp.where(mw, s, NEG_INF), axis=-1)
        o_win = jnp.einsum('bgrqk,bkgd->bqgrd', pr, vwi)
        return o_sel, o_win

    o_sel, o_win = lax.map(qblock, (jnp.arange(n_blk), jnp.transpose(sel, (2, 0, 1, 3))))
    o_sel = jnp.moveaxis(o_sel, 0, 1).reshape(B, S, G, R, dh)
    o_win = jnp.moveaxis(o_win, 0, 1).reshape(B, S, G, R, dh)
    o = gates[..., 0, None] * o_cmp + gates[..., 1, None] * o_sel + gates[..., 2, None] * o_win
    return o.reshape(B * S, H * dh)


def _rwkv_core(x, mu, w_rkv, w0, w1, w2, a0, a1, a2, g1, g2, k_k, k_a, r_k, ln_gb):
    B, S, D = x.shape
    H, N = RWKV_HEADS, RWKV_HEAD_DIM
    T = B * S
    xx = jnp.pad(x, ((0, 0), (1, 0), (0, 0)))[:, :-1] - x
    xm = (x[None] + xx[None] * mu[:, None, None, :]).reshape(6, T, D)
    r, k, v = (matmul(xm[c], w_rkv[c]).reshape(B, S, D) for c in range(3))
    w_log = -jax.nn.softplus(-(w0 + matmul(jnp.tanh(matmul(xm[3], w1)), w2))) - 0.5
    decay = jnp.exp(-jnp.exp(w_log)).reshape(B, S, D)
    a = jax.nn.sigmoid(a0 + matmul(matmul(xm[4], a1), a2)).reshape(B, S, D)
    g = matmul(jax.nn.sigmoid(matmul(xm[5], g1)), g2).reshape(B, S, D)
    kk = (k * k_k).reshape(B, S, H, N)
    kk = kk / jnp.maximum(jnp.linalg.norm(kk, axis=-1, keepdims=True), 1e-12)
    k = k * (1 + (a - 1) * k_a)

    def heads(t):
        return t.reshape(B, S, H, N)

    r_h, k_h, v_h, w_h, a_h = heads(r), heads(k), heads(v), heads(decay), heads(a)

    def step(state, inp):
        r_t, w_t, k_t, v_t, kk_t, a_t = inp
        sa = jnp.einsum('bhvk,bhk->bhv', state, -kk_t)
        state = (state * w_t[:, :, None, :] + sa[..., None] * (kk_t * a_t)[:, :, None, :]
                 + v_t[..., None] * k_t[:, :, None, :])
        return state, jnp.einsum('bhvk,bhk->bhv', state, r_t)

    xs = tuple(jnp.moveaxis(t, 1, 0) for t in (r_h, w_h, k_h, v_h, kk, a_h))
    _, y = lax.scan(step, jnp.zeros((B, H, N, N), F32), xs)
    y = jnp.moveaxis(y, 0, 1)
    y = _head_norm(y, RWKV_GN_EPS).reshape(B, S, D) * ln_gb[0] + ln_gb[1]
    bonus = jnp.sum(r_h * k_h * r_k, axis=-1, keepdims=True) * v_h
    y = (y + bonus.reshape(B, S, D)) * g
    return y.reshape(T, D)


def _fox_core(proj, b_f):
    B, S = BATCH, SEQ
    H, dh, Q = FOX_HEADS, FOX_HEAD_DIM, FOX_BLOCK
    q, k, v = (proj[..., c * H * dh:(c + 1) * H * dh].reshape(B, S, H, dh) for c in range(3))
    log_f = jax.nn.log_sigmoid(proj[..., 3 * H * dh:3 * H * dh + H] + b_f)
    cum = jnp.transpose(jnp.cumsum(log_f, axis=1), (0, 2, 1))
    kpos = jnp.arange(S)
    scale = dh ** -0.5

    def block(i):
        q0 = i * Q
        qi = lax.dynamic_slice_in_dim(q, q0, Q, axis=1)
        ci = lax.dynamic_slice_in_dim(cum, q0, Q, axis=2)
        s = (jnp.einsum('bqhd,bkhd->bhqk', qi, k).astype(F32) * scale
             + (ci[..., :, None] - cum[..., None, :]))
        mask = kpos[None, :] <= (q0 + jnp.arange(Q))[:, None]
        pr = jax.nn.softmax(jnp.where(mask, s, NEG_INF), axis=-1)
        return jnp.einsum('bhqk,bkhd->bqhd', pr, v)

    o = lax.map(block, jnp.arange(S // Q))
    return jnp.moveaxis(o, 0, 1).reshape(B * S, H * dh)


def _ret_core(proj, gn_g):
    B, S = BATCH, SEQ
    H, dk, dv, C = RET_HEADS, RET_QK_DIM, RET_V_DIM, RET_CHUNK
    q = proj[..., :H * dk].reshape(B, S, H, dk)
    k = proj[..., H * dk:2 * H * dk].reshape(B, S, H, dk)
    v = proj[..., 2 * H * dk:2 * H * dk + H * dv].reshape(B, S, H, dv)
    g = proj[..., 2 * H * dk + H * dv:]
    pos = jnp.arange(S)
    q = _rotary(q, pos, RET_THETA, dk)
    k = _rotary(k, pos, RET_THETA, dk) * dk ** -0.5
    log_gamma = jnp.log(1.0 - 2.0 ** (-5.0 - jnp.arange(H, dtype=F32)))
    idx = jnp.arange(C, dtype=F32)
    diff = idx[:, None] - idx[None, :]
    decay_mask = jnp.where(diff >= 0, jnp.exp(diff * log_gamma[:, None, None]), 0.0)
    q_scale = jnp.exp((idx + 1.0)[None, :] * log_gamma[:, None]).T
    k_scale = jnp.exp((C - 1.0 - idx)[None, :] * log_gamma[:, None]).T
    chunk_decay = jnp.exp(C * log_gamma)
    n = S // C

    def chunks(t):
        return jnp.moveaxis(t.reshape(B, n, C, H, t.shape[-1]), 1, 0)

    def step(Rs, inp):
        qc, kc, vc = inp
        inner = jnp.einsum('bnhd,bmhd->bhnm', qc, kc) * decay_mask
        o = (jnp.einsum('bhnm,bmhe->bnhe', inner, vc)
             + jnp.einsum('bnhd,bhde->bnhe', qc, Rs) * q_scale[None, :, :, None])
        Rs = Rs * chunk_decay[None, :, None, None] + jnp.einsum(
            'bmhd,bmhe->bhde', kc * k_scale[None, :, :, None], vc)
        return Rs, o

    _, o = lax.scan(step, jnp.zeros((B, H, dk, dv), F32), (chunks(q), chunks(k), chunks(v)))
    o = jnp.moveaxis(o, 0, 1).reshape(B, S, H, dv)
    o = _head_norm(o, RET_GN_EPS).reshape(B, S, H * dv) * gn_g
    return (jax.nn.silu(g) * o).reshape(B * S, H * dv)


def kernel(x, p, norm_g, ffn_w_in, ffn_w_out, ple_w_proj, ple_w_gate, nsa_w_in, nsa_cmp_pe, nsa_cmp_w1, nsa_cmp_w2, nsa_w_out, rwkv_mu, rwkv_w_rkv, rwkv_w0, rwkv_w1, rwkv_w2, rwkv_a0, rwkv_a1, rwkv_a2, rwkv_g1, rwkv_g2, rwkv_k_k, rwkv_k_a, rwkv_r_k, rwkv_ln, rwkv_w_out, fox_w_in, fox_b_f, fox_w_out, ret_w_in, ret_gn_g, ret_w_out):
    B, S, D = x.shape
    T = B * S
    h = x.reshape(T, D)
    bf = lambda w: w.astype(BF16)
    for i in range(DEPTH):
        m, j = i % N_MIXERS, i // N_MIXERS
        ng = norm_g[i]
        h = ffn_half_step(h, ng[0], ng[1], bf(ffn_w_in[i, 0]), bf(ffn_w_out[i, 0]))
        if m == 0:
            proj = norm_matmul(h, ng[2], bf(_pad_cols(nsa_w_in[j], 5376)))
            y = _nsa_core(proj.reshape(B, S, -1), nsa_cmp_pe[j], nsa_cmp_w1[j], nsa_cmp_w2[j])
            w_out = nsa_w_out[j]
        elif m == 1:
            u = (_rms(h, ng[2].reshape(1, D))).reshape(B, S, D)
            y = _rwkv_core(u, rwkv_mu[j], bf(rwkv_w_rkv[j]), rwkv_w0[j], bf(_pad_cols(rwkv_w1[j], LANE)),
                           bf(jnp.pad(rwkv_w2[j], ((0, LANE - rwkv_w2.shape[1]), (0, 0)))),
                           rwkv_a0[j], bf(_pad_cols(rwkv_a1[j], LANE)),
                           bf(jnp.pad(rwkv_a2[j], ((0, LANE - rwkv_a2.shape[1]), (0, 0)))),
                           bf(rwkv_g1[j]), bf(rwkv_g2[j]),
                           rwkv_k_k[j], rwkv_k_a[j], rwkv_r_k[j], rwkv_ln[j])
            w_out = rwkv_w_out[j]
        elif m == 2:
            proj = norm_matmul(h, ng[2], bf(_pad_cols(fox_w_in[j], 6272)))
            y = _fox_core(proj.reshape(B, S, -1), fox_b_f[j])
            w_out = fox_w_out[j]
        else:
            proj = norm_matmul(h, ng[2], bf(ret_w_in[j]))
            y = _ret_core(proj.reshape(B, S, -1), ret_gn_g[j])
            w_out = ret_w_out[j]
        h = matmul_norm_residual(y, bf(w_out), ng[3], h)
        h = ffn_half_step(h, ng[4], ng[5], bf(ffn_w_in[i, 1]), bf(ffn_w_out[i, 1]))
        h = ple_step(h, p[i].reshape(T, PLE_DIM), ng[6], ng[7], bf(ple_w_proj[i]), bf(ple_w_gate[i]))
    return h.reshape(B, S, D)
```

```python
import functools

import jax
import jax.numpy as jnp
import numpy as np
from jax import lax
from jax.experimental import pallas as pl
from jax.experimental.pallas import tpu as pltpu

D_MODEL = 2048
BATCH = 16
SEQ = 2048
DEPTH = 4
N_MIXERS = 4
PLE_DIM = 256
D_FF = 5632
RMS_EPS = 1e-6
NEG_INF = -1e30

NSA_HEADS = 16
NSA_KV_GROUPS = 4
NSA_HEAD_DIM = D_MODEL // NSA_HEADS
NSA_CMP_LEN = 32
NSA_CMP_STRIDE = 16
NSA_CMP_HIDDEN = 2 * NSA_HEAD_DIM
NSA_SEL_BLOCK = 64
NSA_N_SEL = 8
NSA_WINDOW = 512
NSA_ROPE_DIM = NSA_HEAD_DIM // 4
ROPE_THETA = 500000.0
NSA_IN = NSA_HEADS * NSA_HEAD_DIM + 6 * NSA_KV_GROUPS * NSA_HEAD_DIM + 3 * NSA_HEADS

RWKV_HEAD_DIM = 64
RWKV_HEADS = D_MODEL // RWKV_HEAD_DIM
RWKV_GN_EPS = 64e-5

FOX_HEADS = 16
FOX_HEAD_DIM = D_MODEL // FOX_HEADS
FOX_BLOCK = 128
FOX_IN = 3 * FOX_HEADS * FOX_HEAD_DIM + FOX_HEADS

RET_HEADS = 8
RET_QK_DIM = D_MODEL // RET_HEADS
RET_V_DIM = 2 * D_MODEL // RET_HEADS
RET_CHUNK = 128
RET_THETA = 10000.0
RET_GN_EPS = 1e-5
RET_IN = 2 * RET_HEADS * RET_QK_DIM + 2 * RET_HEADS * RET_V_DIM

V7X_VMEM_LIMIT_BYTES = 56 * 1024 * 1024
LANE = 128

F32 = jnp.float32
BF16 = jnp.bfloat16


def _cparams(*sem):
    return pltpu.CompilerParams(dimension_semantics=sem, vmem_limit_bytes=V7X_VMEM_LIMIT_BYTES)


def _rms(x, g):
    return x * lax.rsqrt(jnp.mean(x * x, axis=-1, keepdims=True) + RMS_EPS) * g


def _ffn_body(h_ref, g0_ref, g1_ref, wg_ref, wu_ref, wo_ref, o_ref, xn_ref, acc_ref):
    f = pl.program_id(1)

    @pl.when(f == 0)
    def _():
        xn_ref[...] = _rms(h_ref[...], g0_ref[...]).astype(BF16)
        acc_ref[...] = jnp.zeros_like(acc_ref)

    xn = xn_ref[...]
    gate = jnp.dot(xn, wg_ref[...], preferred_element_type=F32)
    up = jnp.dot(xn, wu_ref[...], preferred_element_type=F32)
    act = (gate * jax.nn.sigmoid(gate)) * up
    acc_ref[...] += jnp.dot(act.astype(BF16), wo_ref[...], preferred_element_type=F32)

    @pl.when(f == pl.num_programs(1) - 1)
    def _():
        o_ref[...] = h_ref[...] + 0.5 * _rms(acc_ref[...], g1_ref[...])


def ffn_half_step(h, g0, g1, w_in, w_out, *, tm=512, tf=512):
    T, D = h.shape
    nf = D_FF // tf
    return pl.pallas_call(
        _ffn_body,
        out_shape=jax.ShapeDtypeStruct((T, D), F32),
        grid=(T // tm, nf),
        in_specs=[
            pl.BlockSpec((tm, D), lambda i, f: (i, 0)),
            pl.BlockSpec((1, D), lambda i, f: (0, 0)),
            pl.BlockSpec((1, D), lambda i, f: (0, 0)),
            pl.BlockSpec((D, tf), lambda i, f: (0, f)),
            pl.BlockSpec((D, tf), lambda i, f: (0, f + nf)),
            pl.BlockSpec((tf, D), lambda i, f: (f, 0)),
        ],
        out_specs=pl.BlockSpec((tm, D), lambda i, f: (i, 0)),
        scratch_shapes=[pltpu.VMEM((tm, D), BF16), pltpu.VMEM((tm, D), F32)],
        compiler_params=_cparams("parallel", "arbitrary"),
        name="ffn_half_step",
    )(h, g0.reshape(1, D), g1.reshape(1, D), w_in, w_in, w_out)


def _norm_mm_body(x_ref, g_ref, w_ref, o_ref, xn_ref):
    @pl.when(pl.program_id(1) == 0)
    def _():
        xn_ref[...] = _rms(x_ref[...], g_ref[...]).astype(BF16)

    o_ref[...] = jnp.dot(xn_ref[...], w_ref[...], preferred_element_type=F32).astype(o_ref.dtype)


def _mm_body(x_ref, w_ref, o_ref):
    o_ref[...] = jnp.dot(x_ref[...].astype(BF16), w_ref[...],
                         preferred_element_type=F32).astype(o_ref.dtype)


def _col_tile(n, cap=1024):
    best = LANE
    for t in range(LANE, cap + 1, LANE):
        if n % t == 0:
            best = t
    return best


def norm_matmul(x, g, w, *, out_dtype=F32, tm=1024):
    T, K = x.shape
    N = w.shape[1]
    tn = _col_tile(N)
    return pl.pallas_call(
        _norm_mm_body,
        out_shape=jax.ShapeDtypeStruct((T, N), out_dtype),
        grid=(T // tm, N // tn),
        in_specs=[
            pl.BlockSpec((tm, K), lambda i, j: (i, 0)),
            pl.BlockSpec((1, K), lambda i, j: (0, 0)),
            pl.BlockSpec((K, tn), lambda i, j: (0, j)),
        ],
        out_specs=pl.BlockSpec((tm, tn), lambda i, j: (i, j)),
        scratch_shapes=[pltpu.VMEM((tm, K), BF16)],
        compiler_params=_cparams("parallel", "arbitrary"),
        name="norm_matmul",
    )(x, g.reshape(1, K), w)


def matmul(x, w, *, out_dtype=F32, tm=1024):
    T, K = x.shape
    N = w.shape[1]
    tn = _col_tile(N)
    return pl.pallas_call(
        _mm_body,
        out_shape=jax.ShapeDtypeStruct((T, N), out_dtype),
        grid=(T // tm, N // tn),
        in_specs=[
            pl.BlockSpec((tm, K), lambda i, j: (i, 0)),
            pl.BlockSpec((K, tn), lambda i, j: (0, j)),
        ],
        out_specs=pl.BlockSpec((tm, tn), lambda i, j: (i, j)),
        compiler_params=_cparams("parallel", "arbitrary"),
        name="matmul",
    )(x, w)


def _mm_res_body(y_ref, w_ref, g_ref, h_ref, o_ref, acc_ref):
    k = pl.program_id(1)

    @pl.when(k == 0)
    def _():
        acc_ref[...] = jnp.zeros_like(acc_ref)

    acc_ref[...] += jnp.dot(y_ref[...].astype(BF16), w_ref[...], preferred_element_type=F32)

    @pl.when(k == pl.num_programs(1) - 1)
    def _():
        o_ref[...] = h_ref[...] + _rms(acc_ref[...], g_ref[...])


def matmul_norm_residual(y, w, g, h, *, tm=512, tk=1024):
    T, K = y.shape
    D = w.shape[1]
    return pl.pallas_call(
        _mm_res_body,
        out_shape=jax.ShapeDtypeStruct((T, D), F32),
        grid=(T // tm, K // tk),
        in_specs=[
            pl.BlockSpec((tm, tk), lambda i, k: (i, k)),
            pl.BlockSpec((tk, D), lambda i, k: (k, 0)),
            pl.BlockSpec((1, D), lambda i, k: (0, 0)),
            pl.BlockSpec((tm, D), lambda i, k: (i, 0)),
        ],
        out_specs=pl.BlockSpec((tm, D), lambda i, k: (i, 0)),
        scratch_shapes=[pltpu.VMEM((tm, D), F32)],
        compiler_params=_cparams("parallel", "arbitrary"),
        name="matmul_norm_residual",
    )(y, w, g.reshape(1, D), h)


def _ple_body(h_ref, p_ref, g6_ref, g7_ref, wp_ref, wg_ref, o_ref):
    h = h_ref[...]
    xn = _rms(h, g6_ref[...]).astype(BF16)
    z = jnp.dot(xn, wg_ref[...], preferred_element_type=F32)
    e = jnp.dot(p_ref[...].astype(BF16), wp_ref[...], preferred_element_type=F32)
    o_ref[...] = h + _rms(e * jax.nn.sigmoid(z), g7_ref[...])


def ple_step(h, p, g6, g7, wp, wg, *, tm=256):
    T, D = h.shape
    P = p.shape[1]
    return pl.pallas_call(
        _ple_body,
        out_shape=jax.ShapeDtypeStruct((T, D), F32),
        grid=(T // tm,),
        in_specs=[
            pl.BlockSpec((tm, D), lambda i: (i, 0)),
            pl.BlockSpec((tm, P), lambda i: (i, 0)),
            pl.BlockSpec((1, D), lambda i: (0, 0)),
            pl.BlockSpec((1, D), lambda i: (0, 0)),
            pl.BlockSpec((P, D), lambda i: (0, 0)),
            pl.BlockSpec((D, D), lambda i: (0, 0)),
        ],
        out_specs=pl.BlockSpec((tm, D), lambda i: (i, 0)),
        compiler_params=_cparams("parallel"),
        name="ple_step",
    )(h, p, g6.reshape(1, D), g7.reshape(1, D), wp, wg)


RWKV_CHUNK = 64
RWKV_PAIRS_PER_STEP = 4
RWKV_TIME_BLOCK = 512


def _rwkv_body(r_ref, lw_ref, k_ref, v_ref, kk_ref, a_ref, y_ref, s_ref):
    C = RWKV_CHUNK
    N = RWKV_HEAD_DIM
    lane = lax.broadcasted_iota(jnp.int32, (C, 2 * N), 1)
    row = lax.broadcasted_iota(jnp.int32, (C, 2 * N), 0)
    first_head = lane < N
    ri = lax.broadcasted_iota(jnp.int32, (2 * C, 2 * C), 0)
    ci = lax.broadcasted_iota(jnp.int32, (2 * C, 2 * C), 1)
    strict = ri > ci
    incl = ri >= ci
    eye = jnp.where(ri == ci, 1.0, 0.0).astype(F32)
    corner = [((ri >> (lvl + 1)) == (ci >> (lvl + 1))) & ((ri & (1 << lvl)) != 0) & ((ci & (1 << lvl)) == 0)
              for lvl in range(C.bit_length() - 1)]

    def stack(x):
        return jnp.concatenate([jnp.where(first_head, x, 0.0), jnp.where(first_head, 0.0, x)], axis=0)

    def nt(a, b):
        return lax.dot_general(a.astype(BF16), b.astype(BF16), (((1,), (1,)), ((), ())),
                               preferred_element_type=F32)

    def nn(a, b):
        return jnp.dot(a.astype(BF16), b.astype(BF16), preferred_element_type=F32)

    def tn(a, b):
        return lax.dot_general(a.astype(BF16), b.astype(BF16), (((0,), (0,)), ((), ())),
                               preferred_element_type=F32)

    n_pairs = r_ref.shape[2] // (2 * N)
    lanes = [slice(2 * N * j, 2 * N * (j + 1)) for j in range(n_pairs)]

    def prep(sl, ln):
        r, lw, k, v, kk, a = (ref[0, sl, ln] for ref in (r_ref, lw_ref, k_ref, v_ref, kk_ref, a_ref))
        cl = lw
        for sh in (1, 2, 4, 8, 16, 32):
            cl = cl + jnp.where(row >= sh, pltpu.roll(cl, sh, axis=0), 0.0)
        mid = cl[C // 2 - 1:C // 2, :]
        last = cl[C - 1:C, :]
        e_neg = jnp.exp(mid - cl)
        e_end = jnp.exp(last - mid)
        b_til = stack(kk * a * e_neg)
        k_til = stack(k * e_neg)
        return dict(
            a_bar=stack(-kk * jnp.exp(cl - lw - mid)), r_bar=stack(r * jnp.exp(cl - mid)),
            b_til=b_til, k_til=k_til, v_st=stack(v), e_mid=jnp.exp(mid), w_tot=jnp.exp(last),
            bk_end=jnp.concatenate([b_til * e_end, k_til * e_end], axis=0))

    def chunk(c, states):
        sl = pl.ds(pl.multiple_of(c * C, C), C)
        ps = [prep(sl, ln) for ln in lanes]
        gs = [nt(jnp.concatenate([p["a_bar"], p["r_bar"]], axis=0),
                 jnp.concatenate([p["b_til"], p["k_til"]], axis=0)) for p in ps]
        a_ab = [jnp.where(strict, g[:2 * C, :2 * C], 0.0) for g in gs]
        a_ak = [jnp.where(strict, g[:2 * C, 2 * C:], 0.0) for g in gs]
        a_rb = [jnp.where(incl, g[2 * C:, :2 * C], 0.0) for g in gs]
        a_rk = [jnp.where(incl, g[2 * C:, 2 * C:], 0.0) for g in gs]
        inv = [eye + jnp.where(corner[0], x, 0.0) for x in a_ab]
        for lvl in range(1, len(corner)):
            tmp = [nn(jnp.where(corner[lvl], x, 0.0), t) for x, t in zip(a_ab, inv)]
            inv = [t + nn(t, x) for t, x in zip(inv, tmp)]
        s_mid = [st * p["e_mid"] for st, p in zip(states, ps)]
        rhs = [nt(p["a_bar"], sm) + nn(ak, p["v_st"]) for p, sm, ak in zip(ps, s_mid, a_ak)]
        us = [nn(t, x) for t, x in zip(inv, rhs)]
        ys = [nt(p["r_bar"], sm) + nn(rb, u) + nn(rk, p["v_st"])
              for p, sm, rb, rk, u in zip(ps, s_mid, a_rb, a_rk, us)]
        for ln, y in zip(lanes, ys):
            y_ref[0, sl, ln] = y[:C] + y[C:]
        return tuple(st * p["w_tot"] + tn(jnp.concatenate([u, p["v_st"]], axis=0), p["bk_end"])
                     for st, p, u in zip(states, ps, us))

    @pl.when(pl.program_id(2) == 0)
    def _():
        s_ref[...] = jnp.zeros_like(s_ref)

    states = lax.fori_loop(0, r_ref.shape[1] // C, chunk, tuple(s_ref[j] for j in range(n_pairs)))
    for j, st in enumerate(states):
        s_ref[j] = st


def rwkv_recurrence(r, lw, k, v, kk, a):
    B, S, D = r.shape
    lanes = 2 * RWKV_HEAD_DIM * RWKV_PAIRS_PER_STEP
    ts = min(S, RWKV_TIME_BLOCK)
    spec = pl.BlockSpec((1, ts, lanes), lambda b, j, t: (b, t, j))
    return pl.pallas_call(
        _rwkv_body,
        out_shape=jax.ShapeDtypeStruct((B, S, D), F32),
        grid=(B, D // lanes, S // ts),
        in_specs=[spec] * 6,
        out_specs=spec,
        scratch_shapes=[pltpu.VMEM((RWKV_PAIRS_PER_STEP, 2 * RWKV_HEAD_DIM, 2 * RWKV_HEAD_DIM), F32)],
        compiler_params=_cparams("parallel", "parallel", "arbitrary"),
        name="rwkv_recurrence",
    )(r, lw, k, v, kk, a)


def _masked_softmax_pv(s, mask, v):
    s = jnp.where(mask, s, NEG_INF)
    m = jnp.max(s, axis=-1, keepdims=True)
    p = jnp.exp(s - m)
    l = jnp.sum(p, axis=-1, keepdims=True)
    return jnp.dot(p.astype(BF16), v, preferred_element_type=F32) / l


def _nsa_sel_win_body(sel_ref, q_ref, ks_ref, vs_ref, kw_ref, vw_ref, osel_ref, owin_ref):
    blk, n_sel, W = NSA_SEL_BLOCK, NSA_N_SEL, NSA_WINDOW
    R, dh = NSA_HEADS // NSA_KV_GROUPS, NSA_HEAD_DIM
    S = q_ref.shape[1]
    n_blk = S // blk
    base = (pl.program_id(0) * NSA_KV_GROUPS + pl.program_id(1)) * (n_blk * n_sel)
    row_t = lax.broadcasted_iota(jnp.int32, (R * blk, 1), 0) & (blk - 1)
    col_s = lax.broadcasted_iota(jnp.int32, (1, n_sel * blk), 1)
    col_w = lax.broadcasted_iota(jnp.int32, (1, W + blk), 1)

    def nt(a, b):
        return lax.dot_general(a, b, (((1,), (1,)), ((), ())), preferred_element_type=F32)

    def qblock(i, carry):
        q0 = pl.multiple_of(i * blk, blk)
        qb = q_ref[0, pl.ds(q0, blk), :]
        qs = jnp.concatenate([qb[:, r * dh:(r + 1) * dh] for r in range(R)], axis=0)
        qpos = q0 + row_t

        starts = [sel_ref[base + i * n_sel + j] * blk for j in range(n_sel)]
        k_sel = jnp.concatenate([ks_ref[0, pl.ds(pl.multiple_of(st, blk), blk), :] for st in starts], axis=0)
        v_sel = jnp.concatenate([vs_ref[0, pl.ds(pl.multiple_of(st, blk), blk), :] for st in starts], axis=0)
        tok = col_s & (blk - 1)
        for j in range(n_sel):
            tok = tok + jnp.where((col_s >> (blk.bit_length() - 1)) == j, starts[j], 0)
        o = _masked_softmax_pv(nt(qs, k_sel), tok <= qpos, v_sel)
        for r in range(R):
            osel_ref[0, pl.ds(q0, blk), r * dh:(r + 1) * dh] = o[r * blk:(r + 1) * blk]

        w0 = pl.multiple_of(jnp.maximum(q0 - W, 0), blk)
        k_win = kw_ref[0, pl.ds(w0, W + blk), :]
        v_win = vw_ref[0, pl.ds(w0, W + blk), :]
        dist = qpos - (w0 + col_w)
        o = _masked_softmax_pv(nt(qs, k_win), (dist >= 0) & (dist < W), v_win)
        for r in range(R):
            owin_ref[0, pl.ds(q0, blk), r * dh:(r + 1) * dh] = o[r * blk:(r + 1) * blk]
        return carry

    lax.fori_loop(0, n_blk, qblock, 0)


def nsa_selected_window(sel, q, ks, vs, kw, vw):
    B, S, HD = q.shape
    G, dh = NSA_KV_GROUPS, NSA_HEAD_DIM
    qspec = pl.BlockSpec((1, S, HD // G), lambda b, g, sel: (b, 0, g))
    kspec = pl.BlockSpec((1, S, dh), lambda b, g, sel: (b, 0, g))
    return pl.pallas_call(
        _nsa_sel_win_body,
        out_shape=(jax.ShapeDtypeStruct((B, S, HD), F32), jax.ShapeDtypeStruct((B, S, HD), F32)),
        grid_spec=pltpu.PrefetchScalarGridSpec(
            num_scalar_prefetch=1,
            grid=(B, G),
            in_specs=[qspec, kspec, kspec, kspec, kspec],
            out_specs=(qspec, qspec),
        ),
        compiler_params=_cparams("parallel", "parallel"),
        name="nsa_selected_window",
    )(sel.reshape(-1), q, ks, vs, kw, vw)


FOX_TILE = 256


def _fox_body(q_ref, k_ref, v_ref, cq_ref, ck_ref, o_ref):
    t = FOX_TILE
    i = pl.program_id(2)
    q = (q_ref[0] * (FOX_HEAD_DIM ** -0.5)).astype(BF16)
    cq = cq_ref[0, 0]
    causal = (lax.broadcasted_iota(jnp.int32, (t, t), 0) >= lax.broadcasted_iota(jnp.int32, (t, t), 1))

    def scores(j):
        k0 = pl.multiple_of(j * t, t)
        k = k_ref[0, pl.ds(k0, t), :].astype(BF16)
        v = v_ref[0, pl.ds(k0, t), :].astype(BF16)
        s = lax.dot_general(q, k, (((1,), (1,)), ((), ())), preferred_element_type=F32)
        return s + (cq - ck_ref[0, 0, :, pl.ds(k0, t)]), v

    def update(carry, s, v):
        m, l, acc = carry
        m_new = jnp.maximum(m, jnp.max(s, axis=-1, keepdims=True))
        corr = jnp.exp(m - m_new)
        p = jnp.exp(s - m_new)
        l = corr * l + jnp.sum(p, axis=-1, keepdims=True)
        acc = corr * acc + jnp.dot(p.astype(BF16), v, preferred_element_type=F32)
        return m_new, l, acc

    def body(j, carry):
        s, v = scores(j)
        return update(carry, s, v)

    init = (jnp.full((t, 1), NEG_INF, F32), jnp.zeros((t, 1), F32), jnp.zeros((t, FOX_HEAD_DIM), F32))
    carry = lax.fori_loop(0, i, body, init)
    s, v = scores(i)
    _, l, acc = update(carry, jnp.where(causal, s, NEG_INF), v)
    o_ref[0] = acc / l


def fox_attention(proj, cum):
    B, S = proj.shape[:2]
    H, dh, t = FOX_HEADS, FOX_HEAD_DIM, FOX_TILE
    return pl.pallas_call(
        _fox_body,
        out_shape=jax.ShapeDtypeStruct((B, S, H * dh), F32),
        grid=(B, H, S // t),
        in_specs=[
            pl.BlockSpec((1, t, dh), lambda b, h, i: (b, i, h)),
            pl.BlockSpec((1, S, dh), lambda b, h, i: (b, 0, H + h)),
            pl.BlockSpec((1, S, dh), lambda b, h, i: (b, 0, 2 * H + h)),
            pl.BlockSpec((1, 1, t, 1), lambda b, h, i: (b, h, i, 0)),
            pl.BlockSpec((1, 1, 1, S), lambda b, h, i: (b, h, 0, 0)),
        ],
        out_specs=pl.BlockSpec((1, t, dh), lambda b, h, i: (b, i, h)),
        compiler_params=_cparams("parallel", "parallel", "arbitrary"),
        name="fox_attention",
    )(proj, proj, proj, cum[..., None], cum[:, :, None, :])


def _head_norm(y, eps):
    y = y.astype(F32)
    mu = jnp.mean(y, axis=-1, keepdims=True)
    var = jnp.mean(jnp.square(y - mu), axis=-1, keepdims=True)
    return (y - mu) * lax.rsqrt(var + eps)


def _rotary(x, pos, theta, rot_dim):
    half = rot_dim // 2
    inv_freq = theta ** (-jnp.arange(half, dtype=F32) / half)
    ang = pos.astype(F32)[:, None] * inv_freq
    shape = (1, ang.shape[0]) + (1,) * (x.ndim - 3) + (half,)
    cos = jnp.cos(ang).reshape(shape)
    sin = jnp.sin(ang).reshape(shape)
    xr = x[..., :rot_dim].astype(F32)
    x1, x2 = xr[..., :half], xr[..., half:]
    rot = jnp.concatenate([x1 * cos - x2 * sin, x2 * cos + x1 * sin], axis=-1).astype(x.dtype)
    if rot_dim == x.shape[-1]:
        return rot
    return jnp.concatenate([rot, x[..., rot_dim:]], axis=-1)


def _pad_cols(w, n):
    return jnp.pad(w, ((0, 0), (0, n - w.shape[1])))


def _nsa_core(proj, cmp_pe, cmp_w1, cmp_w2):
    B, S = proj.shape[:2]
    H, G, dh = NSA_HEADS, NSA_KV_GROUPS, NSA_HEAD_DIM
    R = H // G
    L, stride, blk, W = NSA_CMP_LEN, NSA_CMP_STRIDE, NSA_SEL_BLOCK, NSA_WINDOW
    scale = dh ** -0.5
    pos = jnp.arange(S)
    hq, hk = H * dh, G * dh
    q = proj[..., :hq].reshape(B, S, G, R, dh)
    kv = proj[..., hq:hq + 6 * hk].reshape(B, S, 6, G, dh)
    gates = jax.nn.sigmoid(proj[..., hq + 6 * hk:hq + 6 * hk + 3 * H].reshape(B, S, G, R, 3))
    q = _rotary(q, pos, ROPE_THETA, NSA_ROPE_DIM)
    kc, ks, kw = (_rotary(kv[:, :, c], pos, ROPE_THETA, NSA_ROPE_DIM) for c in (0, 2, 4))
    vc, vs, vw = kv[:, :, 1], kv[:, :, 3], kv[:, :, 5]

    n_cmp = (S - L) // stride + 1
    idx_np = np.arange(n_cmp)[:, None] * stride + np.arange(L)[None, :]
    cmp_end = jnp.asarray(idx_np[:, -1])

    def compress(t, c):
        blocks = t[:, idx_np] + cmp_pe[c][None, None, :, None, :]
        hid = jax.nn.gelu(jnp.einsum('bnlgd,ldf->bngf', blocks, cmp_w1[c]))
        return jnp.einsum('bngf,fe->bnge', hid, cmp_w2[c])

    k_cmp, v_cmp = compress(kc, 0), compress(vc, 1)
    s_cmp = jnp.einsum('bsgrd,bngd->bgrsn', q, k_cmp).astype(F32) * scale
    m_cmp = cmp_end[None, :] <= pos[:, None]
    p_cmp = jax.nn.softmax(jnp.where(m_cmp, s_cmp, NEG_INF), axis=-1) * m_cmp
    o_cmp = jnp.einsum('bgrsn,bngd->bsgrd', p_cmp, v_cmp)

    n_blk = S // blk
    starts = np.arange(n_cmp) * stride
    bstart = np.arange(n_blk) * blk
    overlap = (starts[:, None] <= bstart[None, :] + blk - 1) & (starts[:, None] + L - 1 >= bstart[None, :])
    imp = jnp.einsum('bgrsn,nj->bgsj', p_cmp, jnp.asarray(overlap, F32))
    imp = imp.reshape(B, G, n_blk, blk, n_blk).sum(axis=3)
    qb_i = np.arange(n_blk)[:, None]
    kb_j = np.arange(n_blk)[None, :]
    valid = kb_j <= qb_i
    forced = (kb_j == 0) | (kb_j == qb_i) | (kb_j == qb_i - 1)
    imp = jnp.where(forced, jnp.inf, jnp.where(valid, imp, -jnp.inf))
    n_sel = min(NSA_N_SEL, n_blk)
    _, sel = lax.top_k(imp, n_sel)

    flat = lambda t: t.reshape(B, S, -1).astype(BF16)
    o_sel, o_win = nsa_selected_window(sel, flat(q * scale), flat(ks), flat(vs), flat(kw), flat(vw))
    o_sel = o_sel.reshape(B, S, G, R, dh)
    o_win = o_win.reshape(B, S, G, R, dh)
    o = gates[..., 0, None] * o_cmp + gates[..., 1, None] * o_sel + gates[..., 2, None] * o_win
    return o.reshape(B * S, H * dh)


def _rwkv_core(x, mu, w_rkv, w0, w1, w2, a0, a1, a2, g1, g2, k_k, k_a, r_k, ln_gb):
    B, S, D = x.shape
    H, N = RWKV_HEADS, RWKV_HEAD_DIM
    T = B * S
    xx = jnp.pad(x, ((0, 0), (1, 0), (0, 0)))[:, :-1] - x
    xm = (x[None] + xx[None] * mu[:, None, None, :]).reshape(6, T, D)
    r, k, v = (matmul(xm[c], w_rkv[c]).reshape(B, S, D) for c in range(3))
    w_log = -jax.nn.softplus(-(w0 + matmul(jnp.tanh(matmul(xm[3], w1)), w2))) - 0.5
    log_decay = (-jnp.exp(w_log)).reshape(B, S, D)
    a = jax.nn.sigmoid(a0 + matmul(matmul(xm[4], a1), a2)).reshape(B, S, D)
    g = matmul(jax.nn.sigmoid(matmul(xm[5], g1)), g2).reshape(B, S, D)
    kk = (k * k_k).reshape(B, S, H, N)
    kk = (kk / jnp.maximum(jnp.linalg.norm(kk, axis=-1, keepdims=True), 1e-12)).reshape(B, S, D)
    k = k * (1 + (a - 1) * k_a)

    def heads(t):
        return t.reshape(B, S, H, N)

    r_h, k_h, v_h = heads(r), heads(k), heads(v)
    y = heads(rwkv_recurrence(r, log_decay, k, v, kk, a))
    y = _head_norm(y, RWKV_GN_EPS).reshape(B, S, D) * ln_gb[0] + ln_gb[1]
    bonus = jnp.sum(r_h * k_h * r_k, axis=-1, keepdims=True) * v_h
    y = (y + bonus.reshape(B, S, D)) * g
    return y.reshape(T, D)


def _fox_core(proj, b_f):
    B, S = proj.shape[:2]
    H, dh, Q = FOX_HEADS, FOX_HEAD_DIM, FOX_BLOCK
    log_f = jax.nn.log_sigmoid(proj[..., 3 * H * dh:3 * H * dh + H] + b_f)
    cum = jnp.transpose(jnp.cumsum(log_f, axis=1), (0, 2, 1))
    return fox_attention(proj, cum).reshape(B * S, H * dh)


def _ret_core(proj, gn_g):
    B, S = proj.shape[:2]
    H, dk, dv, C = RET_HEADS, RET_QK_DIM, RET_V_DIM, RET_CHUNK
    q = proj[..., :H * dk].reshape(B, S, H, dk)
    k = proj[..., H * dk:2 * H * dk].reshape(B, S, H, dk)
    v = proj[..., 2 * H * dk:2 * H * dk + H * dv].reshape(B, S, H, dv)
    g = proj[..., 2 * H * dk + H * dv:]
    pos = jnp.arange(S)
    q = _rotary(q, pos, RET_THETA, dk)
    k = _rotary(k, pos, RET_THETA, dk) * dk ** -0.5
    log_gamma = jnp.log(1.0 - 2.0 ** (-5.0 - jnp.arange(H, dtype=F32)))
    idx = jnp.arange(C, dtype=F32)
    diff = idx[:, None] - idx[None, :]
    decay_mask = jnp.where(diff >= 0, jnp.exp(diff * log_gamma[:, None, None]), 0.0)
    q_scale = jnp.exp((idx + 1.0)[None, :] * log_gamma[:, None]).T
    k_scale = jnp.exp((C - 1.0 - idx)[None, :] * log_gamma[:, None]).T
    chunk_decay = jnp.exp(C * log_gamma)
    n = S // C

    def chunks(t):
        return jnp.moveaxis(t.reshape(B, n, C, H, t.shape[-1]), 1, 0)

    def step(Rs, inp):
        qc, kc, vc = inp
        inner = jnp.einsum('bnhd,bmhd->bhnm', qc, kc) * decay_mask
        o = (jnp.einsum('bhnm,bmhe->bnhe', inner, vc)
             + jnp.einsum('bnhd,bhde->bnhe', qc, Rs) * q_scale[None, :, :, None])
        Rs = Rs * chunk_decay[None, :, None, None] + jnp.einsum(
            'bmhd,bmhe->bhde', kc * k_scale[None, :, :, None], vc)
        return Rs, o

    _, o = lax.scan(step, jnp.zeros((B, H, dk, dv), F32), (chunks(q), chunks(k), chunks(v)))
    o = jnp.moveaxis(o, 0, 1).reshape(B, S, H, dv)
    o = _head_norm(o, RET_GN_EPS).reshape(B, S, H * dv) * gn_g
    return (jax.nn.silu(g) * o).reshape(B * S, H * dv)


def kernel(x, p, norm_g, ffn_w_in, ffn_w_out, ple_w_proj, ple_w_gate, nsa_w_in, nsa_cmp_pe, nsa_cmp_w1, nsa_cmp_w2, nsa_w_out, rwkv_mu, rwkv_w_rkv, rwkv_w0, rwkv_w1, rwkv_w2, rwkv_a0, rwkv_a1, rwkv_a2, rwkv_g1, rwkv_g2, rwkv_k_k, rwkv_k_a, rwkv_r_k, rwkv_ln, rwkv_w_out, fox_w_in, fox_b_f, fox_w_out, ret_w_in, ret_gn_g, ret_w_out):
    B, S, D = x.shape
    T = B * S
    h = x.reshape(T, D)
    bf = lambda w: w.astype(BF16)
    for i in range(DEPTH):
        m, j = i % N_MIXERS, i // N_MIXERS
        ng = norm_g[i]
        h = ffn_half_step(h, ng[0], ng[1], bf(ffn_w_in[i, 0]), bf(ffn_w_out[i, 0]))
        if m == 0:
            proj = norm_matmul(h, ng[2], bf(_pad_cols(nsa_w_in[j], 5376)))
            y = _nsa_core(proj.reshape(B, S, -1), nsa_cmp_pe[j], nsa_cmp_w1[j], nsa_cmp_w2[j])
            w_out = nsa_w_out[j]
        elif m == 1:
            u = (_rms(h, ng[2].reshape(1, D))).reshape(B, S, D)
            y = _rwkv_core(u, rwkv_mu[j], bf(rwkv_w_rkv[j]), rwkv_w0[j], bf(_pad_cols(rwkv_w1[j], LANE)),
                           bf(jnp.pad(rwkv_w2[j], ((0, LANE - rwkv_w2.shape[1]), (0, 0)))),
                           rwkv_a0[j], bf(_pad_cols(rwkv_a1[j], LANE)),
                           bf(jnp.pad(rwkv_a2[j], ((0, LANE - rwkv_a2.shape[1]), (0, 0)))),
                           bf(rwkv_g1[j]), bf(rwkv_g2[j]),
                           rwkv_k_k[j], rwkv_k_a[j], rwkv_r_k[j], rwkv_ln[j])
            w_out = rwkv_w_out[j]
        elif m == 2:
            proj = norm_matmul(h, ng[2], bf(_pad_cols(fox_w_in[j], 6272)))
            y = _fox_core(proj.reshape(B, S, -1), fox_b_f[j])
            w_out = fox_w_out[j]
        else:
            proj = norm_matmul(h, ng[2], bf(ret_w_in[j]))
            y = _ret_core(proj.reshape(B, S, -1), ret_gn_g[j])
            w_out = ret_w_out[j]
        h = matmul_norm_residual(y, bf(w_out), ng[3], h)
        h = ffn_half_step(h, ng[4], ng[5], bf(ffn_w_in[i, 1]), bf(ffn_w_out[i, 1]))
        h = ple_step(h, p[i].reshape(T, PLE_DIM), ng[6], ng[7], bf(ple_w_proj[i]), bf(ple_w_gate[i]))
    return h.reshape(B, S, D)
```

```python
import functools

import jax
import jax.numpy as jnp
import numpy as np
from jax import lax
from jax.experimental import pallas as pl
from jax.experimental.pallas import tpu as pltpu

D_MODEL = 2048
BATCH = 16
SEQ = 2048
DEPTH = 4
N_MIXERS = 4
PLE_DIM = 256
D_FF = 5632
RMS_EPS = 1e-6
NEG_INF = -1e30

NSA_HEADS = 16
NSA_KV_GROUPS = 4
NSA_HEAD_DIM = D_MODEL // NSA_HEADS
NSA_CMP_LEN = 32
NSA_CMP_STRIDE = 16
NSA_CMP_HIDDEN = 2 * NSA_HEAD_DIM
NSA_SEL_BLOCK = 64
NSA_N_SEL = 8
NSA_WINDOW = 512
NSA_ROPE_DIM = NSA_HEAD_DIM // 4
ROPE_THETA = 500000.0
NSA_IN = NSA_HEADS * NSA_HEAD_DIM + 6 * NSA_KV_GROUPS * NSA_HEAD_DIM + 3 * NSA_HEADS

RWKV_HEAD_DIM = 64
RWKV_HEADS = D_MODEL // RWKV_HEAD_DIM
RWKV_GN_EPS = 64e-5

FOX_HEADS = 16
FOX_HEAD_DIM = D_MODEL // FOX_HEADS
FOX_BLOCK = 128
FOX_IN = 3 * FOX_HEADS * FOX_HEAD_DIM + FOX_HEADS

RET_HEADS = 8
RET_QK_DIM = D_MODEL // RET_HEADS
RET_V_DIM = 2 * D_MODEL // RET_HEADS
RET_CHUNK = 128
RET_THETA = 10000.0
RET_GN_EPS = 1e-5
RET_IN = 2 * RET_HEADS * RET_QK_DIM + 2 * RET_HEADS * RET_V_DIM

V7X_VMEM_LIMIT_BYTES = 56 * 1024 * 1024
LANE = 128

F32 = jnp.float32
BF16 = jnp.bfloat16


def _cparams(*sem):
    return pltpu.CompilerParams(dimension_semantics=sem, vmem_limit_bytes=V7X_VMEM_LIMIT_BYTES)


def _rms(x, g):
    return x * lax.rsqrt(jnp.mean(x * x, axis=-1, keepdims=True) + RMS_EPS) * g


def _ffn_body(h_ref, g0_ref, g1_ref, wg_ref, wu_ref, wo_ref, o_ref, xn_ref, acc_ref):
    f = pl.program_id(1)

    @pl.when(f == 0)
    def _():
        xn_ref[...] = _rms(h_ref[...], g0_ref[...]).astype(BF16)
        acc_ref[...] = jnp.zeros_like(acc_ref)

    xn = xn_ref[...]
    gate = jnp.dot(xn, wg_ref[...], preferred_element_type=F32)
    up = jnp.dot(xn, wu_ref[...], preferred_element_type=F32)
    act = (gate * jax.nn.sigmoid(gate)) * up
    acc_ref[...] += jnp.dot(act.astype(BF16), wo_ref[...], preferred_element_type=F32)

    @pl.when(f == pl.num_programs(1) - 1)
    def _():
        o_ref[...] = h_ref[...] + 0.5 * _rms(acc_ref[...], g1_ref[...])


def ffn_half_step(h, g0, g1, w_in, w_out, *, tm=512, tf=512):
    T, D = h.shape
    nf = D_FF // tf
    return pl.pallas_call(
        _ffn_body,
        out_shape=jax.ShapeDtypeStruct((T, D), F32),
        grid=(T // tm, nf),
        in_specs=[
            pl.BlockSpec((tm, D), lambda i, f: (i, 0)),
            pl.BlockSpec((1, D), lambda i, f: (0, 0)),
            pl.BlockSpec((1, D), lambda i, f: (0, 0)),
            pl.BlockSpec((D, tf), lambda i, f: (0, f)),
            pl.BlockSpec((D, tf), lambda i, f: (0, f + nf)),
            pl.BlockSpec((tf, D), lambda i, f: (f, 0)),
        ],
        out_specs=pl.BlockSpec((tm, D), lambda i, f: (i, 0)),
        scratch_shapes=[pltpu.VMEM((tm, D), BF16), pltpu.VMEM((tm, D), F32)],
        compiler_params=_cparams("parallel", "arbitrary"),
        name="ffn_half_step",
    )(h, g0.reshape(1, D), g1.reshape(1, D), w_in, w_in, w_out)


def _norm_mm_body(x_ref, g_ref, w_ref, o_ref, xn_ref):
    @pl.when(pl.program_id(1) == 0)
    def _():
        xn_ref[...] = _rms(x_ref[...], g_ref[...]).astype(BF16)

    o_ref[...] = jnp.dot(xn_ref[...], w_ref[...], preferred_element_type=F32).astype(o_ref.dtype)


def _mm_body(x_ref, w_ref, o_ref):
    o_ref[...] = jnp.dot(x_ref[...].astype(BF16), w_ref[...],
                         preferred_element_type=F32).astype(o_ref.dtype)


def _col_tile(n, cap=1024):
    best = LANE
    for t in range(LANE, cap + 1, LANE):
        if n % t == 0:
            best = t
    return best


def norm_matmul(x, g, w, *, out_dtype=F32, tm=1024):
    T, K = x.shape
    N = w.shape[1]
    tn = _col_tile(N)
    return pl.pallas_call(
        _norm_mm_body,
        out_shape=jax.ShapeDtypeStruct((T, N), out_dtype),
        grid=(T // tm, N // tn),
        in_specs=[
            pl.BlockSpec((tm, K), lambda i, j: (i, 0)),
            pl.BlockSpec((1, K), lambda i, j: (0, 0)),
            pl.BlockSpec((K, tn), lambda i, j: (0, j)),
        ],
        out_specs=pl.BlockSpec((tm, tn), lambda i, j: (i, j)),
        scratch_shapes=[pltpu.VMEM((tm, K), BF16)],
        compiler_params=_cparams("parallel", "arbitrary"),
        name="norm_matmul",
    )(x, g.reshape(1, K), w)


def matmul(x, w, *, out_dtype=F32, tm=1024):
    T, K = x.shape
    N = w.shape[1]
    tn = _col_tile(N)
    return pl.pallas_call(
        _mm_body,
        out_shape=jax.ShapeDtypeStruct((T, N), out_dtype),
        grid=(T // tm, N // tn),
        in_specs=[
            pl.BlockSpec((tm, K), lambda i, j: (i, 0)),
            pl.BlockSpec((K, tn), lambda i, j: (0, j)),
        ],
        out_specs=pl.BlockSpec((tm, tn), lambda i, j: (i, j)),
        compiler_params=_cparams("parallel", "arbitrary"),
        name="matmul",
    )(x, w)


def _mm_res_body(y_ref, w_ref, g_ref, h_ref, o_ref, acc_ref):
    k = pl.program_id(1)

    @pl.when(k == 0)
    def _():
        acc_ref[...] = jnp.zeros_like(acc_ref)

    acc_ref[...] += jnp.dot(y_ref[...].astype(BF16), w_ref[...], preferred_element_type=F32)

    @pl.when(k == pl.num_programs(1) - 1)
    def _():
        o_ref[...] = h_ref[...] + _rms(acc_ref[...], g_ref[...])


def matmul_norm_residual(y, w, g, h, *, tm=512, tk=1024):
    T, K = y.shape
    D = w.shape[1]
    return pl.pallas_call(
        _mm_res_body,
        out_shape=jax.ShapeDtypeStruct((T, D), F32),
        grid=(T // tm, K // tk),
        in_specs=[
            pl.BlockSpec((tm, tk), lambda i, k: (i, k)),
            pl.BlockSpec((tk, D), lambda i, k: (k, 0)),
            pl.BlockSpec((1, D), lambda i, k: (0, 0)),
            pl.BlockSpec((tm, D), lambda i, k: (i, 0)),
        ],
        out_specs=pl.BlockSpec((tm, D), lambda i, k: (i, 0)),
        scratch_shapes=[pltpu.VMEM((tm, D), F32)],
        compiler_params=_cparams("parallel", "arbitrary"),
        name="matmul_norm_residual",
    )(y, w, g.reshape(1, D), h)


def _ple_body(h_ref, p_ref, g6_ref, g7_ref, wp_ref, wg_ref, o_ref):
    h = h_ref[...]
    xn = _rms(h, g6_ref[...]).astype(BF16)
    z = jnp.dot(xn, wg_ref[...], preferred_element_type=F32)
    e = jnp.dot(p_ref[...].astype(BF16), wp_ref[...], preferred_element_type=F32)
    o_ref[...] = h + _rms(e * jax.nn.sigmoid(z), g7_ref[...])


def ple_step(h, p, g6, g7, wp, wg, *, tm=256):
    T, D = h.shape
    P = p.shape[1]
    return pl.pallas_call(
        _ple_body,
        out_shape=jax.ShapeDtypeStruct((T, D), F32),
        grid=(T // tm,),
        in_specs=[
            pl.BlockSpec((tm, D), lambda i: (i, 0)),
            pl.BlockSpec((tm, P), lambda i: (i, 0)),
            pl.BlockSpec((1, D), lambda i: (0, 0)),
            pl.BlockSpec((1, D), lambda i: (0, 0)),
            pl.BlockSpec((P, D), lambda i: (0, 0)),
            pl.BlockSpec((D, D), lambda i: (0, 0)),
        ],
        out_specs=pl.BlockSpec((tm, D), lambda i: (i, 0)),
        compiler_params=_cparams("parallel"),
        name="ple_step",
    )(h, p, g6.reshape(1, D), g7.reshape(1, D), wp, wg)


RWKV_CHUNK = 64
RWKV_PAIRS_PER_STEP = 4
RWKV_TIME_BLOCK = 512


def _rwkv_body(r_ref, lw_ref, k_ref, v_ref, kk_ref, a_ref, y_ref, s_ref):
    C = RWKV_CHUNK
    N = RWKV_HEAD_DIM
    lane = lax.broadcasted_iota(jnp.int32, (C, 2 * N), 1)
    row = lax.broadcasted_iota(jnp.int32, (C, 2 * N), 0)
    first_head = lane < N
    ri = lax.broadcasted_iota(jnp.int32, (2 * C, 2 * C), 0)
    ci = lax.broadcasted_iota(jnp.int32, (2 * C, 2 * C), 1)
    strict = ri > ci
    incl = ri >= ci
    eye = jnp.where(ri == ci, 1.0, 0.0).astype(F32)
    corner = [((ri >> (lvl + 1)) == (ci >> (lvl + 1))) & ((ri & (1 << lvl)) != 0) & ((ci & (1 << lvl)) == 0)
              for lvl in range(C.bit_length() - 1)]

    def stack(x):
        return jnp.concatenate([jnp.where(first_head, x, 0.0), jnp.where(first_head, 0.0, x)], axis=0)

    def nt(a, b):
        return lax.dot_general(a.astype(BF16), b.astype(BF16), (((1,), (1,)), ((), ())),
                               preferred_element_type=F32)

    def nn(a, b):
        return jnp.dot(a.astype(BF16), b.astype(BF16), preferred_element_type=F32)

    def tn(a, b):
        return lax.dot_general(a.astype(BF16), b.astype(BF16), (((0,), (0,)), ((), ())),
                               preferred_element_type=F32)

    n_pairs = r_ref.shape[2] // (2 * N)
    lanes = [slice(2 * N * j, 2 * N * (j + 1)) for j in range(n_pairs)]

    def prep(sl, ln):
        r, lw, k, v, kk, a = (ref[0, sl, ln] for ref in (r_ref, lw_ref, k_ref, v_ref, kk_ref, a_ref))
        cl = lw
        for sh in (1, 2, 4, 8, 16, 32):
            cl = cl + jnp.where(row >= sh, pltpu.roll(cl, sh, axis=0), 0.0)
        mid = cl[C // 2 - 1:C // 2, :]
        last = cl[C - 1:C, :]
        e_neg = jnp.exp(mid - cl)
        e_end = jnp.exp(last - mid)
        b_til = stack(kk * a * e_neg)
        k_til = stack(k * e_neg)
        return dict(
            a_bar=stack(-kk * jnp.exp(cl - lw - mid)), r_bar=stack(r * jnp.exp(cl - mid)),
            b_til=b_til, k_til=k_til, v_st=stack(v), e_mid=jnp.exp(mid), w_tot=jnp.exp(last),
            bk_end=jnp.concatenate([b_til * e_end, k_til * e_end], axis=0))

    def chunk(c, states):
        sl = pl.ds(pl.multiple_of(c * C, C), C)
        ps = [prep(sl, ln) for ln in lanes]
        gs = [nt(jnp.concatenate([p["a_bar"], p["r_bar"]], axis=0),
                 jnp.concatenate([p["b_til"], p["k_til"]], axis=0)) for p in ps]
        a_ab = [jnp.where(strict, g[:2 * C, :2 * C], 0.0) for g in gs]
        a_ak = [jnp.where(strict, g[:2 * C, 2 * C:], 0.0) for g in gs]
        a_rb = [jnp.where(incl, g[2 * C:, :2 * C], 0.0) for g in gs]
        a_rk = [jnp.where(incl, g[2 * C:, 2 * C:], 0.0) for g in gs]
        inv = [eye + jnp.where(corner[0], x, 0.0) for x in a_ab]
        for lvl in range(1, len(corner)):
            tmp = [nn(jnp.where(corner[lvl], x, 0.0), t) for x, t in zip(a_ab, inv)]
            inv = [t + nn(t, x) for t, x in zip(inv, tmp)]
        s_mid = [st * p["e_mid"] for st, p in zip(states, ps)]
        rhs = [nt(p["a_bar"], sm) + nn(ak, p["v_st"]) for p, sm, ak in zip(ps, s_mid, a_ak)]
        us = [nn(t, x) for t, x in zip(inv, rhs)]
        ys = [nt(p["r_bar"], sm) + nn(rb, u) + nn(rk, p["v_st"])
              for p, sm, rb, rk, u in zip(ps, s_mid, a_rb, a_rk, us)]
        for ln, y in zip(lanes, ys):
            y_ref[0, sl, ln] = y[:C] + y[C:]
        return tuple(st * p["w_tot"] + tn(jnp.concatenate([u, p["v_st"]], axis=0), p["bk_end"])
                     for st, p, u in zip(states, ps, us))

    @pl.when(pl.program_id(2) == 0)
    def _():
        s_ref[...] = jnp.zeros_like(s_ref)

    states = lax.fori_loop(0, r_ref.shape[1] // C, chunk, tuple(s_ref[j] for j in range(n_pairs)))
    for j, st in enumerate(states):
        s_ref[j] = st


def rwkv_recurrence(proj, lw, k, kk, a):
    B, S, D = lw.shape
    lanes = 2 * RWKV_HEAD_DIM * RWKV_PAIRS_PER_STEP
    ts = min(S, RWKV_TIME_BLOCK)
    spec = pl.BlockSpec((1, ts, lanes), lambda b, j, t: (b, t, j))
    v_spec = pl.BlockSpec((1, ts, lanes), lambda b, j, t: (b, t, 2 * D // lanes + j))
    return pl.pallas_call(
        _rwkv_body,
        out_shape=jax.ShapeDtypeStruct((B, S, D), F32),
        grid=(B, D // lanes, S // ts),
        in_specs=[spec, spec, spec, v_spec, spec, spec],
        out_specs=spec,
        scratch_shapes=[pltpu.VMEM((RWKV_PAIRS_PER_STEP, 2 * RWKV_HEAD_DIM, 2 * RWKV_HEAD_DIM), F32)],
        compiler_params=_cparams("parallel", "parallel", "arbitrary"),
        name="rwkv_recurrence",
    )(proj, lw, k, proj, kk, a)


RWKV_PROJ_TN = 512
RWKV_HIDDEN = 4 * LANE


def _rwkv_proj_body(tiles_per_seq, h_ref, hp_ref, g_ref, mu_ref, w_ref, o_ref, xm_ref):
    i, j = pl.program_id(0), pl.program_id(1)
    tm, D = h_ref.shape
    n_main = 3 * D // RWKV_PROJ_TN

    @pl.when(j == 0)
    def _():
        u = _rms(h_ref[...], g_ref[...])
        prev = _rms(hp_ref[7:8, :], g_ref[...])
        prev = jnp.where(lax.rem(i, tiles_per_seq) == 0, 0.0, prev)
        row = lax.broadcasted_iota(jnp.int32, (tm, D), 0)
        xx = jnp.where(row == 0, prev, pltpu.roll(u, 1, axis=0)) - u
        for c in range(6):
            xm_ref[c] = (u + xx * mu_ref[c:c + 1, :]).astype(BF16)

    @pl.when(j < n_main)
    def _():
        o_ref[...] = jnp.dot(xm_ref[j // (n_main // 3)], w_ref[...], preferred_element_type=F32)

    @pl.when(j == n_main)
    def _():
        w = w_ref[...]
        o_ref[:, :LANE] = jnp.tanh(jnp.dot(xm_ref[3], w[:, :LANE], preferred_element_type=F32))
        o_ref[:, LANE:2 * LANE] = jnp.dot(xm_ref[4], w[:, LANE:2 * LANE], preferred_element_type=F32)
        o_ref[:, 2 * LANE:] = jax.nn.sigmoid(jnp.dot(xm_ref[5], w[:, 2 * LANE:], preferred_element_type=F32))


def rwkv_mix_project(h, g, mu, w_all, seq_len, *, tm=512):
    T, D = h.shape
    N = w_all.shape[1]
    tn = RWKV_PROJ_TN
    return pl.pallas_call(
        functools.partial(_rwkv_proj_body, seq_len // tm),
        out_shape=jax.ShapeDtypeStruct((T, N), F32),
        grid=(T // tm, N // tn),
        in_specs=[
            pl.BlockSpec((tm, D), lambda i, j: (i, 0)),
            pl.BlockSpec((8, D), lambda i, j: (jnp.maximum(i * (tm // 8) - 1, 0), 0)),
            pl.BlockSpec((1, D), lambda i, j: (0, 0)),
            pl.BlockSpec((6, D), lambda i, j: (0, 0)),
            pl.BlockSpec((D, tn), lambda i, j: (0, j)),
        ],
        out_specs=pl.BlockSpec((tm, tn), lambda i, j: (i, j)),
        scratch_shapes=[pltpu.VMEM((6, tm, D), BF16)],
        compiler_params=_cparams("parallel", "arbitrary"),
        name="rwkv_mix_project",
    )(h, h, g.reshape(1, D), mu, w_all)


def _head_sum(x, ones_bd):
    hi = x.astype(BF16)
    lo = (x - hi.astype(F32)).astype(BF16)
    return (jnp.dot(hi, ones_bd, preferred_element_type=F32) + jnp.dot(lo, ones_bd, preferred_element_type=F32))


def _head_ones():
    shift = RWKV_HEAD_DIM.bit_length() - 1
    r = lax.broadcasted_iota(jnp.int32, (LANE, LANE), 0) >> shift
    c = lax.broadcasted_iota(jnp.int32, (LANE, LANE), 1) >> shift
    return jnp.where(r == c, 1.0, 0.0).astype(BF16)


def _rwkv_gates_body(k_ref, hid_ref, vec_ref, w2_ref, a2_ref, g2_ref, lw_ref, a_ref, kk_ref, k2_ref, gate_ref):
    hid = hid_ref[...].astype(BF16)
    z = vec_ref[0:1, :] + jnp.dot(hid[:, :LANE], w2_ref[...], preferred_element_type=F32)
    softplus = jnp.maximum(-z, 0.0) + jnp.log(1.0 + jnp.exp(-jnp.abs(z)))
    lw_ref[...] = -jnp.exp(-softplus - 0.5)
    a = jax.nn.sigmoid(vec_ref[1:2, :] + jnp.dot(hid[:, LANE:2 * LANE], a2_ref[...], preferred_element_type=F32))
    a_ref[...] = a
    gate_ref[...] = jnp.dot(hid[:, 2 * LANE:], g2_ref[...], preferred_element_type=F32)
    k = k_ref[...]
    k2_ref[...] = k * (1.0 + (a - 1.0) * vec_ref[3:4, :])
    kk = k * vec_ref[2:3, :]
    ones_bd = _head_ones()
    for c in range(k.shape[1] // LANE):
        blk = kk[:, c * LANE:(c + 1) * LANE]
        norm = jnp.sqrt(_head_sum(blk * blk, ones_bd))
        kk_ref[:, c * LANE:(c + 1) * LANE] = blk / jnp.maximum(norm, 1e-12)


def rwkv_gates(proj, vec, w2, a2, g2, *, tm=256):
    T = proj.shape[0]
    D = w2.shape[1]
    row = pl.BlockSpec((tm, D), lambda i: (i, 0))
    full = lambda w: pl.BlockSpec(w.shape, lambda i: (0, 0))
    return pl.pallas_call(
        _rwkv_gates_body,
        out_shape=tuple(jax.ShapeDtypeStruct((T, D), F32) for _ in range(5)),
        grid=(T // tm,),
        in_specs=[
            pl.BlockSpec((tm, D), lambda i: (i, 1)),
            pl.BlockSpec((tm, RWKV_HIDDEN), lambda i: (i, 3 * D // RWKV_HIDDEN)),
            full(vec), full(w2), full(a2), full(g2),
        ],
        out_specs=(row,) * 5,
        compiler_params=_cparams("parallel"),
        name="rwkv_gates",
    )(proj, proj, vec, w2, a2, g2)


def _rwkv_post_body(y_ref, r_ref, v_ref, k2_ref, gate_ref, vec_ref, o_ref):
    ones_bd = _head_ones()
    inv_n = 1.0 / RWKV_HEAD_DIM
    for c in range(y_ref.shape[1] // LANE):
        sl = slice(c * LANE, (c + 1) * LANE)
        y = y_ref[:, sl]
        d = y - _head_sum(y, ones_bd) * inv_n
        yn = d * lax.rsqrt(_head_sum(d * d, ones_bd) * inv_n + RWKV_GN_EPS)
        bonus = _head_sum(r_ref[:, sl] * k2_ref[:, sl] * vec_ref[2:3, sl], ones_bd) * v_ref[:, sl]
        o_ref[:, sl] = ((yn * vec_ref[0:1, sl] + vec_ref[1:2, sl] + bonus) * gate_ref[:, sl]).astype(o_ref.dtype)


def rwkv_post(y, proj, k2, gate, vec, *, tm=256):
    T, D = y.shape
    row = pl.BlockSpec((tm, D), lambda i: (i, 0))
    return pl.pallas_call(
        _rwkv_post_body,
        out_shape=jax.ShapeDtypeStruct((T, D), BF16),
        grid=(T // tm,),
        in_specs=[row, row, pl.BlockSpec((tm, D), lambda i: (i, 2)), row, row,
                  pl.BlockSpec(vec.shape, lambda i: (0, 0))],
        out_specs=row,
        compiler_params=_cparams("parallel"),
        name="rwkv_post",
    )(y, proj, proj, k2, gate, vec)


def _masked_softmax_pv(s, mask, v):
    s = jnp.where(mask, s, NEG_INF)
    m = jnp.max(s, axis=-1, keepdims=True)
    p = jnp.exp(s - m)
    l = jnp.sum(p, axis=-1, keepdims=True)
    return jnp.dot(p.astype(BF16), v, preferred_element_type=F32) / l


def _nsa_sel_win_body(sel_ref, q_ref, ks_ref, vs_ref, kw_ref, vw_ref, osel_ref, owin_ref):
    blk, n_sel, W = NSA_SEL_BLOCK, NSA_N_SEL, NSA_WINDOW
    R, dh = NSA_HEADS // NSA_KV_GROUPS, NSA_HEAD_DIM
    S = q_ref.shape[1]
    n_blk = S // blk
    base = (pl.program_id(0) * NSA_KV_GROUPS + pl.program_id(1)) * (n_blk * n_sel)
    row_t = lax.broadcasted_iota(jnp.int32, (R * blk, 1), 0) & (blk - 1)
    col_s = lax.broadcasted_iota(jnp.int32, (1, n_sel * blk), 1)
    col_w = lax.broadcasted_iota(jnp.int32, (1, W + blk), 1)

    def nt(a, b):
        return lax.dot_general(a, b, (((1,), (1,)), ((), ())), preferred_element_type=F32)

    def qblock(i, carry):
        q0 = pl.multiple_of(i * blk, blk)
        qb = q_ref[0, pl.ds(q0, blk), :]
        qs = jnp.concatenate([qb[:, r * dh:(r + 1) * dh] for r in range(R)], axis=0)
        qpos = q0 + row_t

        starts = [sel_ref[base + i * n_sel + j] * blk for j in range(n_sel)]
        k_sel = jnp.concatenate([ks_ref[0, pl.ds(pl.multiple_of(st, blk), blk), :] for st in starts], axis=0)
        v_sel = jnp.concatenate([vs_ref[0, pl.ds(pl.multiple_of(st, blk), blk), :] for st in starts], axis=0)
        tok = col_s & (blk - 1)
        for j in range(n_sel):
            tok = tok + jnp.where((col_s >> (blk.bit_length() - 1)) == j, starts[j], 0)
        o = _masked_softmax_pv(nt(qs, k_sel), tok <= qpos, v_sel)
        for r in range(R):
            osel_ref[0, pl.ds(q0, blk), r * dh:(r + 1) * dh] = o[r * blk:(r + 1) * blk]

        w0 = pl.multiple_of(jnp.maximum(q0 - W, 0), blk)
        k_win = kw_ref[0, pl.ds(w0, W + blk), :]
        v_win = vw_ref[0, pl.ds(w0, W + blk), :]
        dist = qpos - (w0 + col_w)
        o = _masked_softmax_pv(nt(qs, k_win), (dist >= 0) & (dist < W), v_win)
        for r in range(R):
            owin_ref[0, pl.ds(q0, blk), r * dh:(r + 1) * dh] = o[r * blk:(r + 1) * blk]
        return carry

    lax.fori_loop(0, n_blk, qblock, 0)


def nsa_selected_window(sel, q, ks, vs, kw, vw):
    B, S, HD = q.shape
    G, dh = NSA_KV_GROUPS, NSA_HEAD_DIM
    qspec = pl.BlockSpec((1, S, HD // G), lambda b, g, sel: (b, 0, g))
    kspec = pl.BlockSpec((1, S, dh), lambda b, g, sel: (b, 0, g))
    return pl.pallas_call(
        _nsa_sel_win_body,
        out_shape=(jax.ShapeDtypeStruct((B, S, HD), F32), jax.ShapeDtypeStruct((B, S, HD), F32)),
        grid_spec=pltpu.PrefetchScalarGridSpec(
            num_scalar_prefetch=1,
            grid=(B, G),
            in_specs=[qspec, kspec, kspec, kspec, kspec],
            out_specs=(qspec, qspec),
        ),
        compiler_params=_cparams("parallel", "parallel"),
        name="nsa_selected_window",
    )(sel.reshape(-1), q, ks, vs, kw, vw)


FOX_TILE = 256
FOX_HEADS_PER_STEP = 2


def _fox_body(q_ref, k_ref, v_ref, cq_ref, ck_ref, o_ref):
    t, dh = FOX_TILE, FOX_HEAD_DIM
    nh = q_ref.shape[2] // dh
    heads = [slice(dh * j, dh * (j + 1)) for j in range(nh)]
    i = pl.program_id(2)
    qs = [q_ref[0, :, h] for h in heads]
    cqs = [cq_ref[0, j] for j in range(nh)]
    causal = (lax.broadcasted_iota(jnp.int32, (t, t), 0) >= lax.broadcasted_iota(jnp.int32, (t, t), 1))
    ones = jnp.ones((t, dh), BF16)

    def step(j, carry, diagonal):
        k0 = pl.multiple_of(j * t, t)
        ss = [lax.dot_general(q, k_ref[0, pl.ds(k0, t), h], (((1,), (1,)), ((), ())),
                              preferred_element_type=F32) + (cq - ck_ref[0, n, :, pl.ds(k0, t)])
              for n, (q, cq, h) in enumerate(zip(qs, cqs, heads))]
        if diagonal:
            ss = [jnp.where(causal, s, NEG_INF) for s in ss]
        m_new = [jnp.maximum(m, jnp.max(s, axis=-1, keepdims=True)) for (m, _), s in zip(carry, ss)]
        ps = [jnp.exp(s - m).astype(BF16) for s, m in zip(ss, m_new)]
        pv = [jnp.dot(p, jnp.concatenate([v_ref[0, pl.ds(k0, t), h], ones], axis=1),
                      preferred_element_type=F32) for p, h in zip(ps, heads)]
        return tuple((mn, jnp.exp(m - mn) * acc + x) for (m, acc), mn, x in zip(carry, m_new, pv))

    init = tuple((jnp.full((t, 1), NEG_INF, F32), jnp.zeros((t, 2 * dh), F32)) for _ in heads)
    carry = lax.fori_loop(0, i, lambda j, c: step(j, c, False), init)
    carry = step(i, carry, True)
    for (_, acc), h in zip(carry, heads):
        o_ref[0, :, h] = (acc[:, :dh] / acc[:, dh:]).astype(o_ref.dtype)


def fox_attention(proj, cum):
    B, S = proj.shape[:2]
    H, dh, t, nh = FOX_HEADS, FOX_HEAD_DIM, FOX_TILE, FOX_HEADS_PER_STEP
    G = H // nh
    return pl.pallas_call(
        _fox_body,
        out_shape=jax.ShapeDtypeStruct((B, S, H * dh), BF16),
        grid=(B, G, S // t),
        in_specs=[
            pl.BlockSpec((1, t, nh * dh), lambda b, h, i: (b, i, h)),
            pl.BlockSpec((1, S, nh * dh), lambda b, h, i: (b, 0, G + h)),
            pl.BlockSpec((1, S, nh * dh), lambda b, h, i: (b, 0, 2 * G + h)),
            pl.BlockSpec((1, nh, t, 1), lambda b, h, i: (b, h, i, 0)),
            pl.BlockSpec((1, nh, 1, S), lambda b, h, i: (b, h, 0, 0)),
        ],
        out_specs=pl.BlockSpec((1, t, nh * dh), lambda b, h, i: (b, i, h)),
        compiler_params=_cparams("parallel", "parallel", "arbitrary"),
        name="fox_attention",
    )(proj, proj, proj, cum[..., None], cum[:, :, None, :])


def _ret_body(lg_ref, q_ref, k_ref, v_ref, g_ref, cos_ref, sin_ref, gn_ref, o_ref, r_ref):
    C, dk = RET_CHUNK, RET_QK_DIM
    half = dk // 2
    lg = lg_ref[pl.program_id(1)]
    ii = lax.broadcasted_iota(jnp.int32, (C, C), 0)
    jj = lax.broadcasted_iota(jnp.int32, (C, C), 1)
    decay_mask = jnp.where(ii >= jj, jnp.exp((ii - jj).astype(F32) * lg), 0.0)
    ti = lax.broadcasted_iota(jnp.int32, (C, 1), 0).astype(F32)
    q_scale = jnp.exp((ti + 1.0) * lg)
    k_scale = jnp.exp((C - 1.0 - ti) * lg)
    chunk_decay = jnp.exp(jnp.full((1, 1), C, F32) * lg)
    r_ref[...] = jnp.zeros_like(r_ref)

    def rot(x, cos, sin):
        x1, x2 = x[:, :half], x[:, half:]
        return jnp.concatenate([x1 * cos - x2 * sin, x2 * cos + x1 * sin], axis=1)

    def chunk(c, carry):
        sl = pl.ds(pl.multiple_of(c * C, C), C)
        cos, sin = cos_ref[sl, :], sin_ref[sl, :]
        q = rot(q_ref[0, sl, :].astype(F32), cos, sin)
        k = rot(k_ref[0, sl, :].astype(F32), cos, sin) * (dk ** -0.5)
        v = v_ref[0, sl, :]
        qb = q.astype(BF16)
        inner = lax.dot_general(qb, k.astype(BF16), (((1,), (1,)), ((), ())),
                                preferred_element_type=F32) * decay_mask
        state = r_ref[...]
        o = (jnp.dot(inner.astype(BF16), v, preferred_element_type=F32)
             + jnp.dot(qb, state.astype(BF16), preferred_element_type=F32) * q_scale)
        r_ref[...] = state * chunk_decay + lax.dot_general(
            (k * k_scale).astype(BF16), v, (((0,), (0,)), ((), ())), preferred_element_type=F32)
        mu = jnp.mean(o, axis=-1, keepdims=True)
        d = o - mu
        on = d * lax.rsqrt(jnp.mean(d * d, axis=-1, keepdims=True) + RET_GN_EPS)
        g = g_ref[0, sl, :].astype(F32)
        o_ref[0, sl, :] = ((g * jax.nn.sigmoid(g)) * (on * gn_ref[...])).astype(o_ref.dtype)
        return carry

    lax.fori_loop(0, q_ref.shape[1] // C, chunk, 0)


def retention_core(proj, gn_g):
    B, S = proj.shape[:2]
    H, dk, dv = RET_HEADS, RET_QK_DIM, RET_V_DIM
    pos = jnp.arange(S, dtype=F32)
    inv_freq = RET_THETA ** (-jnp.arange(dk // 2, dtype=F32) / (dk // 2))
    ang = pos[:, None] * inv_freq
    log_gamma = jnp.log(1.0 - 2.0 ** (-5.0 - jnp.arange(H, dtype=F32)))
    tab = pl.BlockSpec((S, dk // 2), lambda b, h: (0, 0))
    return pl.pallas_call(
        _ret_body,
        out_shape=jax.ShapeDtypeStruct((B, S, H * dv), BF16),
        grid=(B, H),
        in_specs=[
            pl.BlockSpec(memory_space=pltpu.SMEM),
            pl.BlockSpec((1, S, dk), lambda b, h: (b, 0, h)),
            pl.BlockSpec((1, S, dk), lambda b, h: (b, 0, H + h)),
            pl.BlockSpec((1, S, dv), lambda b, h: (b, 0, 2 * H * dk // dv + h)),
            pl.BlockSpec((1, S, dv), lambda b, h: (b, 0, 2 * H * dk // dv + H + h)),
            tab, tab,
            pl.BlockSpec((1, dv), lambda b, h: (0, h)),
        ],
        out_specs=pl.BlockSpec((1, S, dv), lambda b, h: (b, 0, h)),
        scratch_shapes=[pltpu.VMEM((dk, dv), F32)],
        compiler_params=_cparams("parallel", "parallel"),
        name="retention_core",
    )(log_gamma, proj, proj, proj, proj, jnp.cos(ang), jnp.sin(ang), gn_g.reshape(1, H * dv))


def _head_norm(y, eps):
    y = y.astype(F32)
    mu = jnp.mean(y, axis=-1, keepdims=True)
    var = jnp.mean(jnp.square(y - mu), axis=-1, keepdims=True)
    return (y - mu) * lax.rsqrt(var + eps)


def _rotary(x, pos, theta, rot_dim):
    half = rot_dim // 2
    inv_freq = theta ** (-jnp.arange(half, dtype=F32) / half)
    ang = pos.astype(F32)[:, None] * inv_freq
    shape = (1, ang.shape[0]) + (1,) * (x.ndim - 3) + (half,)
    cos = jnp.cos(ang).reshape(shape)
    sin = jnp.sin(ang).reshape(shape)
    xr = x[..., :rot_dim].astype(F32)
    x1, x2 = xr[..., :half], xr[..., half:]
    rot = jnp.concatenate([x1 * cos - x2 * sin, x2 * cos + x1 * sin], axis=-1).astype(x.dtype)
    if rot_dim == x.shape[-1]:
        return rot
    return jnp.concatenate([rot, x[..., rot_dim:]], axis=-1)


def _pad_cols(w, n):
    return jnp.pad(w, ((0, 0), (0, n - w.shape[1])))


def _nsa_core(proj, cmp_pe, cmp_w1, cmp_w2):
    B, S = proj.shape[:2]
    H, G, dh = NSA_HEADS, NSA_KV_GROUPS, NSA_HEAD_DIM
    R = H // G
    L, stride, blk, W = NSA_CMP_LEN, NSA_CMP_STRIDE, NSA_SEL_BLOCK, NSA_WINDOW
    scale = dh ** -0.5
    pos = jnp.arange(S)
    hq, hk = H * dh, G * dh
    q = proj[..., :hq].reshape(B, S, G, R, dh)
    kv = proj[..., hq:hq + 6 * hk].reshape(B, S, 6, G, dh)
    gates = jax.nn.sigmoid(proj[..., hq + 6 * hk:hq + 6 * hk + 3 * H].reshape(B, S, G, R, 3))
    q = _rotary(q, pos, ROPE_THETA, NSA_ROPE_DIM)
    kc, ks, kw = (_rotary(kv[:, :, c], pos, ROPE_THETA, NSA_ROPE_DIM) for c in (0, 2, 4))
    vc, vs, vw = kv[:, :, 1], kv[:, :, 3], kv[:, :, 5]

    n_cmp = (S - L) // stride + 1
    idx_np = np.arange(n_cmp)[:, None] * stride + np.arange(L)[None, :]
    cmp_end = jnp.asarray(idx_np[:, -1])

    def compress(t, c):
        blocks = t[:, idx_np] + cmp_pe[c][None, None, :, None, :]
        hid = jax.nn.gelu(jnp.einsum('bnlgd,ldf->bngf', blocks, cmp_w1[c]))
        return jnp.einsum('bngf,fe->bnge', hid, cmp_w2[c])

    k_cmp, v_cmp = compress(kc, 0), compress(vc, 1)
    s_cmp = jnp.einsum('bsgrd,bngd->bgrsn', q, k_cmp).astype(F32) * scale
    m_cmp = cmp_end[None, :] <= pos[:, None]
    p_cmp = jax.nn.softmax(jnp.where(m_cmp, s_cmp, NEG_INF), axis=-1) * m_cmp
    o_cmp = jnp.einsum('bgrsn,bngd->bsgrd', p_cmp, v_cmp)

    n_blk = S // blk
    starts = np.arange(n_cmp) * stride
    bstart = np.arange(n_blk) * blk
    overlap = (starts[:, None] <= bstart[None, :] + blk - 1) & (starts[:, None] + L - 1 >= bstart[None, :])
    imp = jnp.einsum('bgrsn,nj->bgsj', p_cmp, jnp.asarray(overlap, F32))
    imp = imp.reshape(B, G, n_blk, blk, n_blk).sum(axis=3)
    qb_i = np.arange(n_blk)[:, None]
    kb_j = np.arange(n_blk)[None, :]
    valid = kb_j <= qb_i
    forced = (kb_j == 0) | (kb_j == qb_i) | (kb_j == qb_i - 1)
    imp = jnp.where(forced, jnp.inf, jnp.where(valid, imp, -jnp.inf))
    n_sel = min(NSA_N_SEL, n_blk)
    _, sel = lax.top_k(imp, n_sel)

    flat = lambda t: t.reshape(B, S, -1).astype(BF16)
    o_sel, o_win = nsa_selected_window(sel, flat(q * scale), flat(ks), flat(vs), flat(kw), flat(vw))
    o_sel = o_sel.reshape(B, S, G, R, dh)
    o_win = o_win.reshape(B, S, G, R, dh)
    o = gates[..., 0, None] * o_cmp + gates[..., 1, None] * o_sel + gates[..., 2, None] * o_win
    return o.reshape(B * S, H * dh)


def _rwkv_mixer(h, g, mu, w_rkv, w0, w1, w2, a0, a1, a2, g1, g2, k_k, k_a, r_k, ln_gb, B, S):
    T, D = h.shape
    pad_c = lambda w: _pad_cols(w, LANE)
    pad_r = lambda w: jnp.pad(w, ((0, LANE - w.shape[0]), (0, 0)))
    w_all = jnp.concatenate([w_rkv[0], w_rkv[1], w_rkv[2], pad_c(w1), pad_c(a1), g1], axis=1).astype(BF16)
    proj = rwkv_mix_project(h, g, mu, w_all, S)
    lw, a, kk, k2, gate = rwkv_gates(proj, jnp.stack([w0, a0, k_k, k_a]), pad_r(w2).astype(BF16),
                                     pad_r(a2).astype(BF16), g2.astype(BF16))
    as3 = lambda t: t.reshape(B, S, -1)
    y = rwkv_recurrence(as3(proj), as3(lw), as3(k2), as3(kk), as3(a))
    vec = jnp.stack([ln_gb[0], ln_gb[1], r_k.reshape(D)])
    return rwkv_post(y.reshape(T, D), proj, k2, gate, vec)


def _fox_mixer(h, g, w_in, b_f, B, S):
    HD = FOX_HEADS * FOX_HEAD_DIM
    w_qkv = jnp.concatenate([w_in[:, :HD] * FOX_HEAD_DIM ** -0.5, w_in[:, HD:3 * HD]], axis=1)
    proj = norm_matmul(h, g, w_qkv.astype(BF16), out_dtype=BF16)
    z = norm_matmul(h, g, _pad_cols(w_in[:, 3 * HD:], LANE).astype(BF16))
    log_f = jax.nn.log_sigmoid(z[:, :FOX_HEADS].reshape(B, S, FOX_HEADS) + b_f)
    cum = jnp.transpose(jnp.cumsum(log_f, axis=1), (0, 2, 1))
    return fox_attention(proj.reshape(B, S, 3 * HD), cum).reshape(B * S, HD)


def kernel(x, p, norm_g, ffn_w_in, ffn_w_out, ple_w_proj, ple_w_gate, nsa_w_in, nsa_cmp_pe, nsa_cmp_w1, nsa_cmp_w2, nsa_w_out, rwkv_mu, rwkv_w_rkv, rwkv_w0, rwkv_w1, rwkv_w2, rwkv_a0, rwkv_a1, rwkv_a2, rwkv_g1, rwkv_g2, rwkv_k_k, rwkv_k_a, rwkv_r_k, rwkv_ln, rwkv_w_out, fox_w_in, fox_b_f, fox_w_out, ret_w_in, ret_gn_g, ret_w_out):
    B, S, D = x.shape
    T = B * S
    h = x.reshape(T, D)
    bf = lambda w: w.astype(BF16)
    for i in range(DEPTH):
        m, j = i % N_MIXERS, i // N_MIXERS
        ng = norm_g[i]
        h = ffn_half_step(h, ng[0], ng[1], bf(ffn_w_in[i, 0]), bf(ffn_w_out[i, 0]))
        if m == 0:
            proj = norm_matmul(h, ng[2], bf(_pad_cols(nsa_w_in[j], 5376)))
            y = _nsa_core(proj.reshape(B, S, -1), nsa_cmp_pe[j], nsa_cmp_w1[j], nsa_cmp_w2[j])
            w_out = nsa_w_out[j]
        elif m == 1:
            y = _rwkv_mixer(h, ng[2], rwkv_mu[j], rwkv_w_rkv[j], rwkv_w0[j], rwkv_w1[j], rwkv_w2[j],
                            rwkv_a0[j], rwkv_a1[j], rwkv_a2[j], rwkv_g1[j], rwkv_g2[j],
                            rwkv_k_k[j], rwkv_k_a[j], rwkv_r_k[j], rwkv_ln[j], B, S)
            w_out = rwkv_w_out[j]
        elif m == 2:
            y = _fox_mixer(h, ng[2], fox_w_in[j], fox_b_f[j], B, S)
            w_out = fox_w_out[j]
        else:
            proj = norm_matmul(h, ng[2], bf(ret_w_in[j]), out_dtype=BF16)
            y = retention_core(proj.reshape(B, S, RET_IN), ret_gn_g[j]).reshape(T, -1)
            w_out = ret_w_out[j]
        h = matmul_norm_residual(y, bf(w_out), ng[3], h)
        h = ffn_half_step(h, ng[4], ng[5], bf(ffn_w_in[i, 1]), bf(ffn_w_out[i, 1]))
        h = ple_step(h, p[i].reshape(T, PLE_DIM), ng[6], ng[7], bf(ple_w_proj[i]), bf(ple_w_gate[i]))
    return h.reshape(B, S, D)
```

```python
import functools

import jax
import jax.numpy as jnp
import numpy as np
from jax import lax
from jax.experimental import pallas as pl
from jax.experimental.pallas import tpu as pltpu

D_MODEL = 2048
BATCH = 16
SEQ = 2048
DEPTH = 4
N_MIXERS = 4
PLE_DIM = 256
D_FF = 5632
RMS_EPS = 1e-6
NEG_INF = -1e30

NSA_HEADS = 16
NSA_KV_GROUPS = 4
NSA_HEAD_DIM = D_MODEL // NSA_HEADS
NSA_CMP_LEN = 32
NSA_CMP_STRIDE = 16
NSA_CMP_HIDDEN = 2 * NSA_HEAD_DIM
NSA_SEL_BLOCK = 64
NSA_N_SEL = 8
NSA_WINDOW = 512
NSA_ROPE_DIM = NSA_HEAD_DIM // 4
ROPE_THETA = 500000.0
NSA_IN = NSA_HEADS * NSA_HEAD_DIM + 6 * NSA_KV_GROUPS * NSA_HEAD_DIM + 3 * NSA_HEADS

RWKV_HEAD_DIM = 64
RWKV_HEADS = D_MODEL // RWKV_HEAD_DIM
RWKV_GN_EPS = 64e-5

FOX_HEADS = 16
FOX_HEAD_DIM = D_MODEL // FOX_HEADS
FOX_BLOCK = 128
FOX_IN = 3 * FOX_HEADS * FOX_HEAD_DIM + FOX_HEADS

RET_HEADS = 8
RET_QK_DIM = D_MODEL // RET_HEADS
RET_V_DIM = 2 * D_MODEL // RET_HEADS
RET_CHUNK = 128
RET_THETA = 10000.0
RET_GN_EPS = 1e-5
RET_IN = 2 * RET_HEADS * RET_QK_DIM + 2 * RET_HEADS * RET_V_DIM

V7X_VMEM_LIMIT_BYTES = 56 * 1024 * 1024
LANE = 128

F32 = jnp.float32
BF16 = jnp.bfloat16


def _cparams(*sem):
    return pltpu.CompilerParams(dimension_semantics=sem, vmem_limit_bytes=V7X_VMEM_LIMIT_BYTES)


def _rms(x, g):
    return x * lax.rsqrt(jnp.mean(x * x, axis=-1, keepdims=True) + RMS_EPS) * g


def _ffn_body(h_ref, g0_ref, g1_ref, wg_ref, wu_ref, wo_ref, o_ref, xn_ref, acc_ref):
    f = pl.program_id(1)

    @pl.when(f == 0)
    def _():
        xn_ref[...] = _rms(h_ref[...], g0_ref[...]).astype(BF16)
        acc_ref[...] = jnp.zeros_like(acc_ref)

    xn = xn_ref[...]
    gate = jnp.dot(xn, wg_ref[...], preferred_element_type=F32)
    up = jnp.dot(xn, wu_ref[...], preferred_element_type=F32)
    act = (gate * jax.nn.sigmoid(gate)) * up
    acc_ref[...] += jnp.dot(act.astype(BF16), wo_ref[...], preferred_element_type=F32)

    @pl.when(f == pl.num_programs(1) - 1)
    def _():
        o_ref[...] = h_ref[...] + 0.5 * _rms(acc_ref[...], g1_ref[...])


def ffn_half_step(h, g0, g1, w_in, w_out, *, tm=512, tf=512):
    T, D = h.shape
    nf = D_FF // tf
    return pl.pallas_call(
        _ffn_body,
        out_shape=jax.ShapeDtypeStruct((T, D), F32),
        grid=(T // tm, nf),
        in_specs=[
            pl.BlockSpec((tm, D), lambda i, f: (i, 0)),
            pl.BlockSpec((1, D), lambda i, f: (0, 0)),
            pl.BlockSpec((1, D), lambda i, f: (0, 0)),
            pl.BlockSpec((D, tf), lambda i, f: (0, f)),
            pl.BlockSpec((D, tf), lambda i, f: (0, f + nf)),
            pl.BlockSpec((tf, D), lambda i, f: (f, 0)),
        ],
        out_specs=pl.BlockSpec((tm, D), lambda i, f: (i, 0)),
        scratch_shapes=[pltpu.VMEM((tm, D), BF16), pltpu.VMEM((tm, D), F32)],
        compiler_params=_cparams("parallel", "arbitrary"),
        name="ffn_half_step",
    )(h, g0.reshape(1, D), g1.reshape(1, D), w_in, w_in, w_out)


def _norm_mm_body(x_ref, g_ref, w_ref, o_ref, xn_ref):
    @pl.when(pl.program_id(1) == 0)
    def _():
        xn_ref[...] = _rms(x_ref[...], g_ref[...]).astype(BF16)

    o_ref[...] = jnp.dot(xn_ref[...], w_ref[...], preferred_element_type=F32).astype(o_ref.dtype)


def _mm_body(x_ref, w_ref, o_ref):
    o_ref[...] = jnp.dot(x_ref[...].astype(BF16), w_ref[...],
                         preferred_element_type=F32).astype(o_ref.dtype)


def _col_tile(n, cap=1024):
    best = LANE
    for t in range(LANE, cap + 1, LANE):
        if n % t == 0:
            best = t
    return best


def norm_matmul(x, g, w, *, out_dtype=F32, tm=1024):
    T, K = x.shape
    N = w.shape[1]
    tn = _col_tile(N)
    return pl.pallas_call(
        _norm_mm_body,
        out_shape=jax.ShapeDtypeStruct((T, N), out_dtype),
        grid=(T // tm, N // tn),
        in_specs=[
            pl.BlockSpec((tm, K), lambda i, j: (i, 0)),
            pl.BlockSpec((1, K), lambda i, j: (0, 0)),
            pl.BlockSpec((K, tn), lambda i, j: (0, j)),
        ],
        out_specs=pl.BlockSpec((tm, tn), lambda i, j: (i, j)),
        scratch_shapes=[pltpu.VMEM((tm, K), BF16)],
        compiler_params=_cparams("parallel", "arbitrary"),
        name="norm_matmul",
    )(x, g.reshape(1, K), w)


def matmul(x, w, *, out_dtype=F32, tm=1024):
    T, K = x.shape
    N = w.shape[1]
    tn = _col_tile(N)
    return pl.pallas_call(
        _mm_body,
        out_shape=jax.ShapeDtypeStruct((T, N), out_dtype),
        grid=(T // tm, N // tn),
        in_specs=[
            pl.BlockSpec((tm, K), lambda i, j: (i, 0)),
            pl.BlockSpec((K, tn), lambda i, j: (0, j)),
        ],
        out_specs=pl.BlockSpec((tm, tn), lambda i, j: (i, j)),
        compiler_params=_cparams("parallel", "arbitrary"),
        name="matmul",
    )(x, w)


def _mm_res_body(y_ref, w_ref, g_ref, h_ref, o_ref, acc_ref):
    k = pl.program_id(1)

    @pl.when(k == 0)
    def _():
        acc_ref[...] = jnp.zeros_like(acc_ref)

    acc_ref[...] += jnp.dot(y_ref[...].astype(BF16), w_ref[...], preferred_element_type=F32)

    @pl.when(k == pl.num_programs(1) - 1)
    def _():
        o_ref[...] = h_ref[...] + _rms(acc_ref[...], g_ref[...])


def matmul_norm_residual(y, w, g, h, *, tm=512, tk=1024):
    T, K = y.shape
    D = w.shape[1]
    return pl.pallas_call(
        _mm_res_body,
        out_shape=jax.ShapeDtypeStruct((T, D), F32),
        grid=(T // tm, K // tk),
        in_specs=[
            pl.BlockSpec((tm, tk), lambda i, k: (i, k)),
            pl.BlockSpec((tk, D), lambda i, k: (k, 0)),
            pl.BlockSpec((1, D), lambda i, k: (0, 0)),
            pl.BlockSpec((tm, D), lambda i, k: (i, 0)),
        ],
        out_specs=pl.BlockSpec((tm, D), lambda i, k: (i, 0)),
        scratch_shapes=[pltpu.VMEM((tm, D), F32)],
        compiler_params=_cparams("parallel", "arbitrary"),
        name="matmul_norm_residual",
    )(y, w, g.reshape(1, D), h)


def _ple_body(h_ref, p_ref, g6_ref, g7_ref, wp_ref, wg_ref, o_ref):
    h = h_ref[...]
    xn = _rms(h, g6_ref[...]).astype(BF16)
    z = jnp.dot(xn, wg_ref[...], preferred_element_type=F32)
    e = jnp.dot(p_ref[...].astype(BF16), wp_ref[...], preferred_element_type=F32)
    o_ref[...] = h + _rms(e * jax.nn.sigmoid(z), g7_ref[...])


def ple_step(h, p, g6, g7, wp, wg, *, tm=256):
    T, D = h.shape
    P = p.shape[1]
    return pl.pallas_call(
        _ple_body,
        out_shape=jax.ShapeDtypeStruct((T, D), F32),
        grid=(T // tm,),
        in_specs=[
            pl.BlockSpec((tm, D), lambda i: (i, 0)),
            pl.BlockSpec((tm, P), lambda i: (i, 0)),
            pl.BlockSpec((1, D), lambda i: (0, 0)),
            pl.BlockSpec((1, D), lambda i: (0, 0)),
            pl.BlockSpec((P, D), lambda i: (0, 0)),
            pl.BlockSpec((D, D), lambda i: (0, 0)),
        ],
        out_specs=pl.BlockSpec((tm, D), lambda i: (i, 0)),
        compiler_params=_cparams("parallel"),
        name="ple_step",
    )(h, p, g6.reshape(1, D), g7.reshape(1, D), wp, wg)


RWKV_CHUNK = 64
RWKV_PAIRS_PER_STEP = 16
RWKV_TIME_BLOCK = 256


def _rwkv_body(r_ref, lw_ref, k_ref, v_ref, kk_ref, a_ref, y_ref, s_ref):
    C = RWKV_CHUNK
    N = RWKV_HEAD_DIM
    lane = lax.broadcasted_iota(jnp.int32, (C, 2 * N), 1)
    row = lax.broadcasted_iota(jnp.int32, (C, 2 * N), 0)
    first_head = lane < N
    ri = lax.broadcasted_iota(jnp.int32, (2 * C, 2 * C), 0)
    ci = lax.broadcasted_iota(jnp.int32, (2 * C, 2 * C), 1)
    strict = ri > ci
    incl = ri >= ci
    eye = jnp.where(ri == ci, 1.0, 0.0).astype(F32)
    corner = [((ri >> (lvl + 1)) == (ci >> (lvl + 1))) & ((ri & (1 << lvl)) != 0) & ((ci & (1 << lvl)) == 0)
              for lvl in range(C.bit_length() - 1)]

    def stack(x):
        return jnp.concatenate([jnp.where(first_head, x, 0.0), jnp.where(first_head, 0.0, x)], axis=0)

    def nt(a, b):
        return lax.dot_general(a.astype(BF16), b.astype(BF16), (((1,), (1,)), ((), ())),
                               preferred_element_type=F32)

    def nn(a, b):
        return jnp.dot(a.astype(BF16), b.astype(BF16), preferred_element_type=F32)

    def tn(a, b):
        return lax.dot_general(a.astype(BF16), b.astype(BF16), (((0,), (0,)), ((), ())),
                               preferred_element_type=F32)

    n_pairs = r_ref.shape[2] // (2 * N)
    lanes = [slice(2 * N * j, 2 * N * (j + 1)) for j in range(n_pairs)]

    def prep(sl, ln):
        r, lw, k, v, kk, a = (ref[0, sl, ln] for ref in (r_ref, lw_ref, k_ref, v_ref, kk_ref, a_ref))
        cl = lw
        for sh in (1, 2, 4, 8, 16, 32):
            cl = cl + jnp.where(row >= sh, pltpu.roll(cl, sh, axis=0), 0.0)
        mid = cl[C // 2 - 1:C // 2, :]
        last = cl[C - 1:C, :]
        e_neg = jnp.exp(mid - cl)
        e_end = jnp.exp(last - mid)
        b_til = stack(kk * a * e_neg)
        k_til = stack(k * e_neg)
        return dict(
            a_bar=stack(-kk * jnp.exp(cl - lw - mid)), r_bar=stack(r * jnp.exp(cl - mid)),
            b_til=b_til, k_til=k_til, v_st=stack(v), e_mid=jnp.exp(mid), w_tot=jnp.exp(last),
            bk_end=jnp.concatenate([b_til * e_end, k_til * e_end], axis=0))

    def chunk(c, states):
        sl = pl.ds(pl.multiple_of(c * C, C), C)
        ps = [prep(sl, ln) for ln in lanes]
        gs = [nt(jnp.concatenate([p["a_bar"], p["r_bar"]], axis=0),
                 jnp.concatenate([p["b_til"], p["k_til"]], axis=0)) for p in ps]
        a_ab = [jnp.where(strict, g[:2 * C, :2 * C], 0.0) for g in gs]
        a_ak = [jnp.where(strict, g[:2 * C, 2 * C:], 0.0) for g in gs]
        a_rb = [jnp.where(incl, g[2 * C:, :2 * C], 0.0) for g in gs]
        a_rk = [jnp.where(incl, g[2 * C:, 2 * C:], 0.0) for g in gs]
        inv = [eye + jnp.where(corner[0], x, 0.0) for x in a_ab]
        for lvl in range(1, len(corner)):
            tmp = [nn(jnp.where(corner[lvl], x, 0.0), t) for x, t in zip(a_ab, inv)]
            inv = [t + nn(t, x) for t, x in zip(inv, tmp)]
        s_mid = [st * p["e_mid"] for st, p in zip(states, ps)]
        rhs = [nt(p["a_bar"], sm) + nn(ak, p["v_st"]) for p, sm, ak in zip(ps, s_mid, a_ak)]
        us = [nn(t, x) for t, x in zip(inv, rhs)]
        ys = [nt(p["r_bar"], sm) + nn(rb, u) + nn(rk, p["v_st"])
              for p, sm, rb, rk, u in zip(ps, s_mid, a_rb, a_rk, us)]
        for ln, y in zip(lanes, ys):
            y_ref[0, sl, ln] = y[:C] + y[C:]
        return tuple(st * p["w_tot"] + tn(jnp.concatenate([u, p["v_st"]], axis=0), p["bk_end"])
                     for st, p, u in zip(states, ps, us))

    @pl.when(pl.program_id(2) == 0)
    def _():
        s_ref[...] = jnp.zeros_like(s_ref)

    states = lax.fori_loop(0, r_ref.shape[1] // C, chunk, tuple(s_ref[j] for j in range(n_pairs)))
    for j, st in enumerate(states):
        s_ref[j] = st


def rwkv_recurrence(proj, lw, k, kk, a):
    B, S, D = lw.shape
    lanes = 2 * RWKV_HEAD_DIM * RWKV_PAIRS_PER_STEP
    ts = min(S, RWKV_TIME_BLOCK)
    spec = pl.BlockSpec((1, ts, lanes), lambda b, j, t: (b, t, j))
    v_spec = pl.BlockSpec((1, ts, lanes), lambda b, j, t: (b, t, 2 * D // lanes + j))
    return pl.pallas_call(
        _rwkv_body,
        out_shape=jax.ShapeDtypeStruct((B, S, D), F32),
        grid=(B, D // lanes, S // ts),
        in_specs=[spec, spec, spec, v_spec, spec, spec],
        out_specs=spec,
        scratch_shapes=[pltpu.VMEM((RWKV_PAIRS_PER_STEP, 2 * RWKV_HEAD_DIM, 2 * RWKV_HEAD_DIM), F32)],
        compiler_params=_cparams("parallel", "parallel", "arbitrary"),
        name="rwkv_recurrence",
    )(proj, lw, k, proj, kk, a)


RWKV_PROJ_TN = 512
RWKV_HIDDEN = 4 * LANE


def _rwkv_proj_body(tiles_per_seq, h_ref, hp_ref, g_ref, mu_ref, w_ref, o_ref, xm_ref):
    i, j = pl.program_id(0), pl.program_id(1)
    tm, D = h_ref.shape
    n_main = 3 * D // RWKV_PROJ_TN

    @pl.when(j == 0)
    def _():
        u = _rms(h_ref[...], g_ref[...])
        prev = _rms(hp_ref[7:8, :], g_ref[...])
        prev = jnp.where(lax.rem(i, tiles_per_seq) == 0, 0.0, prev)
        row = lax.broadcasted_iota(jnp.int32, (tm, D), 0)
        xx = jnp.where(row == 0, prev, pltpu.roll(u, 1, axis=0)) - u
        for c in range(6):
            xm_ref[c] = (u + xx * mu_ref[c:c + 1, :]).astype(BF16)

    @pl.when(j < n_main)
    def _():
        o_ref[...] = jnp.dot(xm_ref[j // (n_main // 3)], w_ref[...], preferred_element_type=F32)

    @pl.when(j == n_main)
    def _():
        w = w_ref[...]
        o_ref[:, :LANE] = jnp.tanh(jnp.dot(xm_ref[3], w[:, :LANE], preferred_element_type=F32))
        o_ref[:, LANE:2 * LANE] = jnp.dot(xm_ref[4], w[:, LANE:2 * LANE], preferred_element_type=F32)
        o_ref[:, 2 * LANE:] = jax.nn.sigmoid(jnp.dot(xm_ref[5], w[:, 2 * LANE:], preferred_element_type=F32))


def rwkv_mix_project(h, g, mu, w_all, seq_len, *, tm=512):
    T, D = h.shape
    N = w_all.shape[1]
    tn = RWKV_PROJ_TN
    return pl.pallas_call(
        functools.partial(_rwkv_proj_body, seq_len // tm),
        out_shape=jax.ShapeDtypeStruct((T, N), F32),
        grid=(T // tm, N // tn),
        in_specs=[
            pl.BlockSpec((tm, D), lambda i, j: (i, 0)),
            pl.BlockSpec((8, D), lambda i, j: (jnp.maximum(i * (tm // 8) - 1, 0), 0)),
            pl.BlockSpec((1, D), lambda i, j: (0, 0)),
            pl.BlockSpec((6, D), lambda i, j: (0, 0)),
            pl.BlockSpec((D, tn), lambda i, j: (0, j)),
        ],
        out_specs=pl.BlockSpec((tm, tn), lambda i, j: (i, j)),
        scratch_shapes=[pltpu.VMEM((6, tm, D), BF16)],
        compiler_params=_cparams("parallel", "arbitrary"),
        name="rwkv_mix_project",
    )(h, h, g.reshape(1, D), mu, w_all)


def _head_sum(x, ones_bd):
    hi = x.astype(BF16)
    lo = (x - hi.astype(F32)).astype(BF16)
    return (jnp.dot(hi, ones_bd, preferred_element_type=F32) + jnp.dot(lo, ones_bd, preferred_element_type=F32))


def _head_ones():
    shift = RWKV_HEAD_DIM.bit_length() - 1
    r = lax.broadcasted_iota(jnp.int32, (LANE, LANE), 0) >> shift
    c = lax.broadcasted_iota(jnp.int32, (LANE, LANE), 1) >> shift
    return jnp.where(r == c, 1.0, 0.0).astype(BF16)


def _rwkv_gates_body(k_ref, hid_ref, vec_ref, w2_ref, a2_ref, g2_ref, lw_ref, a_ref, kk_ref, k2_ref, gate_ref):
    hid = hid_ref[...].astype(BF16)
    z = vec_ref[0:1, :] + jnp.dot(hid[:, :LANE], w2_ref[...], preferred_element_type=F32)
    softplus = jnp.maximum(-z, 0.0) + jnp.log(1.0 + jnp.exp(-jnp.abs(z)))
    lw_ref[...] = -jnp.exp(-softplus - 0.5)
    a = jax.nn.sigmoid(vec_ref[1:2, :] + jnp.dot(hid[:, LANE:2 * LANE], a2_ref[...], preferred_element_type=F32))
    a_ref[...] = a
    gate_ref[...] = jnp.dot(hid[:, 2 * LANE:], g2_ref[...], preferred_element_type=F32)
    k = k_ref[...]
    k2_ref[...] = k * (1.0 + (a - 1.0) * vec_ref[3:4, :])
    kk = k * vec_ref[2:3, :]
    ones_bd = _head_ones()
    for c in range(k.shape[1] // LANE):
        blk = kk[:, c * LANE:(c + 1) * LANE]
        norm = jnp.sqrt(_head_sum(blk * blk, ones_bd))
        kk_ref[:, c * LANE:(c + 1) * LANE] = blk / jnp.maximum(norm, 1e-12)


def rwkv_gates(proj, vec, w2, a2, g2, *, tm=256):
    T = proj.shape[0]
    D = w2.shape[1]
    row = pl.BlockSpec((tm, D), lambda i: (i, 0))
    full = lambda w: pl.BlockSpec(w.shape, lambda i: (0, 0))
    return pl.pallas_call(
        _rwkv_gates_body,
        out_shape=tuple(jax.ShapeDtypeStruct((T, D), F32) for _ in range(5)),
        grid=(T // tm,),
        in_specs=[
            pl.BlockSpec((tm, D), lambda i: (i, 1)),
            pl.BlockSpec((tm, RWKV_HIDDEN), lambda i: (i, 3 * D // RWKV_HIDDEN)),
            full(vec), full(w2), full(a2), full(g2),
        ],
        out_specs=(row,) * 5,
        compiler_params=_cparams("parallel"),
        name="rwkv_gates",
    )(proj, proj, vec, w2, a2, g2)


def _rwkv_post_body(y_ref, r_ref, v_ref, k2_ref, gate_ref, vec_ref, o_ref):
    ones_bd = _head_ones()
    inv_n = 1.0 / RWKV_HEAD_DIM
    for c in range(y_ref.shape[1] // LANE):
        sl = slice(c * LANE, (c + 1) * LANE)
        y = y_ref[:, sl]
        d = y - _head_sum(y, ones_bd) * inv_n
        yn = d * lax.rsqrt(_head_sum(d * d, ones_bd) * inv_n + RWKV_GN_EPS)
        bonus = _head_sum(r_ref[:, sl] * k2_ref[:, sl] * vec_ref[2:3, sl], ones_bd) * v_ref[:, sl]
        o_ref[:, sl] = ((yn * vec_ref[0:1, sl] + vec_ref[1:2, sl] + bonus) * gate_ref[:, sl]).astype(o_ref.dtype)


def rwkv_post(y, proj, k2, gate, vec, *, tm=256):
    T, D = y.shape
    row = pl.BlockSpec((tm, D), lambda i: (i, 0))
    return pl.pallas_call(
        _rwkv_post_body,
        out_shape=jax.ShapeDtypeStruct((T, D), BF16),
        grid=(T // tm,),
        in_specs=[row, row, pl.BlockSpec((tm, D), lambda i: (i, 2)), row, row,
                  pl.BlockSpec(vec.shape, lambda i: (0, 0))],
        out_specs=row,
        compiler_params=_cparams("parallel"),
        name="rwkv_post",
    )(y, proj, proj, k2, gate, vec)


def _masked_softmax_pv(s, mask, v):
    s = jnp.where(mask, s, NEG_INF)
    m = jnp.max(s, axis=-1, keepdims=True)
    p = jnp.exp(s - m)
    l = jnp.sum(p, axis=-1, keepdims=True)
    return jnp.dot(p.astype(BF16), v, preferred_element_type=F32) / l


def _nsa_sel_win_body(sel_ref, q_ref, ks_ref, vs_ref, kw_ref, vw_ref, osel_ref, owin_ref):
    blk, n_sel, W = NSA_SEL_BLOCK, NSA_N_SEL, NSA_WINDOW
    R, dh = NSA_HEADS // NSA_KV_GROUPS, NSA_HEAD_DIM
    S = q_ref.shape[1]
    n_blk = S // blk
    base = (pl.program_id(0) * NSA_KV_GROUPS + pl.program_id(1)) * (n_blk * n_sel)
    row_t = lax.broadcasted_iota(jnp.int32, (R * blk, 1), 0) & (blk - 1)
    col_s = lax.broadcasted_iota(jnp.int32, (1, n_sel * blk), 1)
    col_w = lax.broadcasted_iota(jnp.int32, (1, W + blk), 1)

    def nt(a, b):
        return lax.dot_general(a, b, (((1,), (1,)), ((), ())), preferred_element_type=F32)

    def qblock(i, carry):
        q0 = pl.multiple_of(i * blk, blk)
        qb = q_ref[0, pl.ds(q0, blk), :]
        qs = jnp.concatenate([qb[:, r * dh:(r + 1) * dh] for r in range(R)], axis=0)
        qpos = q0 + row_t

        starts = [sel_ref[base + i * n_sel + j] * blk for j in range(n_sel)]
        k_sel = jnp.concatenate([ks_ref[0, pl.ds(pl.multiple_of(st, blk), blk), :] for st in starts], axis=0)
        v_sel = jnp.concatenate([vs_ref[0, pl.ds(pl.multiple_of(st, blk), blk), :] for st in starts], axis=0)
        tok = col_s & (blk - 1)
        for j in range(n_sel):
            tok = tok + jnp.where((col_s >> (blk.bit_length() - 1)) == j, starts[j], 0)
        o = _masked_softmax_pv(nt(qs, k_sel), tok <= qpos, v_sel)
        for r in range(R):
            osel_ref[0, pl.ds(q0, blk), r * dh:(r + 1) * dh] = o[r * blk:(r + 1) * blk].astype(osel_ref.dtype)

        w0 = pl.multiple_of(jnp.maximum(q0 - W, 0), blk)
        k_win = kw_ref[0, pl.ds(w0, W + blk), :]
        v_win = vw_ref[0, pl.ds(w0, W + blk), :]
        dist = qpos - (w0 + col_w)
        o = _masked_softmax_pv(nt(qs, k_win), (dist >= 0) & (dist < W), v_win)
        for r in range(R):
            owin_ref[0, pl.ds(q0, blk), r * dh:(r + 1) * dh] = o[r * blk:(r + 1) * blk].astype(owin_ref.dtype)
        return carry

    lax.fori_loop(0, n_blk, qblock, 0)


def nsa_selected_window(sel, qkv):
    B, S = qkv.shape[:2]
    G, dh, HD = NSA_KV_GROUPS, NSA_HEAD_DIM, NSA_HEADS * NSA_HEAD_DIM
    qspec = pl.BlockSpec((1, S, HD // G), lambda b, g, sel: (b, 0, g))
    kspec = lambda c: pl.BlockSpec((1, S, dh), lambda b, g, sel: (b, 0, HD // dh + c * G + g))
    return pl.pallas_call(
        _nsa_sel_win_body,
        out_shape=(jax.ShapeDtypeStruct((B, S, HD), BF16), jax.ShapeDtypeStruct((B, S, HD), BF16)),
        grid_spec=pltpu.PrefetchScalarGridSpec(
            num_scalar_prefetch=1,
            grid=(B, G),
            in_specs=[qspec, kspec(2), kspec(3), kspec(4), kspec(5)],
            out_specs=(qspec, qspec),
        ),
        compiler_params=_cparams("parallel", "parallel"),
        name="nsa_selected_window",
    )(sel.reshape(-1), qkv, qkv, qkv, qkv, qkv)


def _nsa_rope_tables(S):
    half = NSA_ROPE_DIM // 2
    inv_freq = ROPE_THETA ** (-jnp.arange(half, dtype=F32) / half)
    ang = jnp.arange(S, dtype=F32)[:, None] * inv_freq
    cos, sin = jnp.cos(ang), jnp.sin(ang)
    rest = NSA_HEAD_DIM - NSA_ROPE_DIM
    cos_t = jnp.concatenate([cos, cos, jnp.ones((S, rest), F32)], axis=1)
    sin_t = jnp.concatenate([-sin, sin, jnp.zeros((S, rest), F32)], axis=1)
    return cos_t, sin_t


def _nsa_prep_body(x_ref, cos_ref, sin_ref, o_ref, kv_ref):
    dh, G = NSA_HEAD_DIM, NSA_KV_GROUPS
    half = NSA_ROPE_DIM // 2
    n_q = NSA_HEADS
    cos, sin = cos_ref[...], sin_ref[...]
    low = lax.broadcasted_iota(jnp.int32, cos.shape, 1) < half
    for blk in range(n_q + 6 * G):
        sl = slice(blk * dh, (blk + 1) * dh)
        x = x_ref[:, sl]
        c = (blk - n_q) // G
        if blk < n_q or c in (0, 2, 4):
            swapped = jnp.where(low, pltpu.roll(x, dh - half, axis=1), pltpu.roll(x, half, axis=1))
            x = x * cos + swapped * sin
        if blk < n_q:
            x = x * dh ** -0.5
        elif c in (0, 1):
            kv_ref[:, (blk - n_q) * dh:(blk - n_q + 1) * dh] = x
        o_ref[:, sl] = x.astype(o_ref.dtype)


def nsa_prepare(proj, S, *, tm=256):
    T = proj.shape[0]
    dh, G = NSA_HEAD_DIM, NSA_KV_GROUPS
    n = (NSA_HEADS + 6 * G) * dh
    cos_t, sin_t = _nsa_rope_tables(S)
    tab = pl.BlockSpec((tm, dh), lambda i: (i % (S // tm), 0))
    return pl.pallas_call(
        _nsa_prep_body,
        out_shape=(jax.ShapeDtypeStruct((T, n), BF16), jax.ShapeDtypeStruct((T, 2 * G * dh), F32)),
        grid=(T // tm,),
        in_specs=[pl.BlockSpec((tm, n), lambda i: (i, 0)), tab, tab],
        out_specs=(pl.BlockSpec((tm, n), lambda i: (i, 0)), pl.BlockSpec((tm, 2 * G * dh), lambda i: (i, 0))),
        compiler_params=_cparams("parallel"),
        name="nsa_prepare",
    )(proj, cos_t, sin_t)


def _nsa_compress_body(kc_ref, vc_ref, pe_ref, w1_ref, w2_ref, o_ref):
    L, stride = NSA_CMP_LEN, NSA_CMP_STRIDE
    n_grp = kc_ref.shape[1] // stride
    for c, ref in enumerate((kc_ref, vc_ref)):
        first = jnp.zeros((n_grp, NSA_CMP_HIDDEN), F32)
        second = jnp.zeros((n_grp, NSA_CMP_HIDDEN), F32)
        for l in range(stride):
            rows = ref[0, pl.ds(l, n_grp, stride=stride), :]
            first += jnp.dot((rows + pe_ref[c, l:l + 1, :]).astype(BF16), w1_ref[c, l],
                             preferred_element_type=F32)
            second += jnp.dot((rows + pe_ref[c, stride + l:stride + l + 1, :]).astype(BF16),
                              w1_ref[c, stride + l], preferred_element_type=F32)
        hid = jax.nn.gelu(first + pltpu.roll(second, n_grp - 1, axis=0))
        o_ref[0, 0, c] = jnp.dot(hid.astype(BF16), w2_ref[c], preferred_element_type=F32)


def nsa_compress(kcvc, pe, w1, w2):
    B, S = kcvc.shape[:2]
    G, dh = NSA_KV_GROUPS, NSA_HEAD_DIM
    full = lambda w: pl.BlockSpec(w.shape, lambda b, g: (0,) * w.ndim)
    return pl.pallas_call(
        _nsa_compress_body,
        out_shape=jax.ShapeDtypeStruct((B, G, 2, S // NSA_CMP_STRIDE, dh), F32),
        grid=(B, G),
        in_specs=[pl.BlockSpec((1, S, dh), lambda b, g: (b, 0, g)),
                  pl.BlockSpec((1, S, dh), lambda b, g: (b, 0, G + g)),
                  full(pe), full(w1), full(w2)],
        out_specs=pl.BlockSpec((1, 1, 2, S // NSA_CMP_STRIDE, dh), lambda b, g: (b, g, 0, 0, 0)),
        compiler_params=_cparams("parallel", "parallel"),
        name="nsa_compress",
    )(kcvc, kcvc, pe, w1, w2)


NSA_CMP_TILE = 512


def _nsa_cmp_body(q_ref, cmp_ref, o_ref, sel_ref):
    dh, blk = NSA_HEAD_DIM, NSA_SEL_BLOCK
    R = NSA_HEADS // NSA_KV_GROUPS
    tq = q_ref.shape[1]
    n_cmp = cmp_ref.shape[3]
    nq = tq // blk
    i = pl.program_id(2)
    k_cmp = cmp_ref[0, 0, 0].astype(BF16)
    v_cmp = cmp_ref[0, 0, 1].astype(BF16)
    pos = i * tq + lax.broadcasted_iota(jnp.int32, (tq, 1), 0)
    n_id = lax.broadcasted_iota(jnp.int32, (1, n_cmp), 1)
    visible = n_id * NSA_CMP_STRIDE + (NSA_CMP_LEN - 1) <= pos
    start = lax.broadcasted_iota(jnp.int32, (n_cmp, LANE), 0) * NSA_CMP_STRIDE
    bstart = lax.broadcasted_iota(jnp.int32, (n_cmp, LANE), 1) * blk
    overlap = jnp.where((start <= bstart + blk - 1) & (start + NSA_CMP_LEN - 1 >= bstart), 1.0, 0.0).astype(BF16)
    imp = jnp.zeros((tq, LANE), F32)
    for r in range(R):
        s = lax.dot_general(q_ref[0, :, r * dh:(r + 1) * dh], k_cmp, (((1,), (1,)), ((), ())),
                            preferred_element_type=F32)
        s = jnp.where(visible, s, NEG_INF)
        e = jnp.exp(s - jnp.max(s, axis=-1, keepdims=True))
        p = jnp.where(visible, e / jnp.sum(e, axis=-1, keepdims=True), 0.0).astype(BF16)
        o_ref[0, :, r * dh:(r + 1) * dh] = jnp.dot(p, v_cmp, preferred_element_type=F32).astype(o_ref.dtype)
        imp += jnp.dot(p, overlap, preferred_element_type=F32)
    imp = imp.reshape(nq, blk, LANE).sum(axis=1)
    qb = i * nq + lax.broadcasted_iota(jnp.int32, (nq, 1), 0)
    kb = lax.broadcasted_iota(jnp.int32, (nq, LANE), 1)
    forced = (kb == 0) | (kb == qb) | (kb == qb - 1)
    val = jnp.where(forced, jnp.inf, jnp.where(kb <= qb, imp, -jnp.inf))
    kb_f = kb.astype(F32)
    avail = kb >= 0
    picks = jnp.zeros((nq, LANE), F32)
    for t in range(NSA_N_SEL):
        best = jnp.max(jnp.where(avail, val, -jnp.inf), axis=-1, keepdims=True)
        pick = jnp.min(jnp.where(avail & (val == best), kb_f, float(LANE)), axis=-1, keepdims=True)
        picks = jnp.where(kb == t, pick, picks)
        avail = avail & (kb_f != pick)
    sel_ref[0, 0] = picks.astype(jnp.int32)


def nsa_compressed_attention(qkv, cmp):
    B, S = qkv.shape[:2]
    G, dh, HD, tq = NSA_KV_GROUPS, NSA_HEAD_DIM, NSA_HEADS * NSA_HEAD_DIM, NSA_CMP_TILE
    nq = tq // NSA_SEL_BLOCK
    qspec = pl.BlockSpec((1, tq, HD // G), lambda b, g, i: (b, i, g))
    return pl.pallas_call(
        _nsa_cmp_body,
        out_shape=(jax.ShapeDtypeStruct((B, S, HD), BF16),
                   jax.ShapeDtypeStruct((B, G, S // NSA_SEL_BLOCK, LANE), jnp.int32)),
        grid=(B, G, S // tq),
        in_specs=[qspec, pl.BlockSpec((1, 1) + cmp.shape[2:], lambda b, g, i: (b, g, 0, 0, 0))],
        out_specs=(qspec, pl.BlockSpec((1, 1, nq, LANE), lambda b, g, i: (b, g, i, 0))),
        compiler_params=_cparams("parallel", "parallel", "arbitrary"),
        name="nsa_compressed_attention",
    )(qkv, cmp)


def _nsa_combine_body(z_ref, oc_ref, os_ref, ow_ref, o_ref):
    dh = NSA_HEAD_DIM
    gates = jax.nn.sigmoid(z_ref[...])
    for hh in range(NSA_HEADS):
        sl = slice(hh * dh, (hh + 1) * dh)
        o_ref[:, sl] = (gates[:, 3 * hh:3 * hh + 1] * oc_ref[:, sl].astype(F32)
                        + gates[:, 3 * hh + 1:3 * hh + 2] * os_ref[:, sl].astype(F32)
                        + gates[:, 3 * hh + 2:3 * hh + 3] * ow_ref[:, sl].astype(F32)).astype(o_ref.dtype)


def nsa_combine(proj, o_cmp, o_sel, o_win, *, tm=256):
    T, HD = o_cmp.shape
    row = pl.BlockSpec((tm, HD), lambda i: (i, 0))
    gate_blk = (NSA_HEADS + 6 * NSA_KV_GROUPS) * NSA_HEAD_DIM // LANE
    return pl.pallas_call(
        _nsa_combine_body,
        out_shape=jax.ShapeDtypeStruct((T, HD), BF16),
        grid=(T // tm,),
        in_specs=[pl.BlockSpec((tm, LANE), lambda i: (i, gate_blk)), row, row, row],
        out_specs=row,
        compiler_params=_cparams("parallel"),
        name="nsa_combine",
    )(proj, o_cmp, o_sel, o_win)


FOX_TILE = 256
FOX_HEADS_PER_STEP = 4


def _fox_body(q_ref, k_ref, v_ref, cq_ref, ck_ref, o_ref):
    t, dh = FOX_TILE, FOX_HEAD_DIM
    nh = q_ref.shape[2] // dh
    heads = [slice(dh * j, dh * (j + 1)) for j in range(nh)]
    i = pl.program_id(2)
    qs = [q_ref[0, :, h] for h in heads]
    cqs = [cq_ref[0, j] for j in range(nh)]
    causal = (lax.broadcasted_iota(jnp.int32, (t, t), 0) >= lax.broadcasted_iota(jnp.int32, (t, t), 1))
    ones = jnp.ones((t, dh), BF16)

    def step(j, carry, diagonal):
        k0 = pl.multiple_of(j * t, t)
        ss = [lax.dot_general(q, k_ref[0, pl.ds(k0, t), h], (((1,), (1,)), ((), ())),
                              preferred_element_type=F32) + (cq - ck_ref[0, n, :, pl.ds(k0, t)])
              for n, (q, cq, h) in enumerate(zip(qs, cqs, heads))]
        if diagonal:
            ss = [jnp.where(causal, s, NEG_INF) for s in ss]
        m_new = [jnp.maximum(m, jnp.max(s, axis=-1, keepdims=True)) for (m, _), s in zip(carry, ss)]
        ps = [jnp.exp(s - m).astype(BF16) for s, m in zip(ss, m_new)]
        pv = [jnp.dot(p, jnp.concatenate([v_ref[0, pl.ds(k0, t), h], ones], axis=1),
                      preferred_element_type=F32) for p, h in zip(ps, heads)]
        return tuple((mn, jnp.exp(m - mn) * acc + x) for (m, acc), mn, x in zip(carry, m_new, pv))

    init = tuple((jnp.full((t, 1), NEG_INF, F32), jnp.zeros((t, 2 * dh), F32)) for _ in heads)
    carry = lax.fori_loop(0, i, lambda j, c: step(j, c, False), init)
    carry = step(i, carry, True)
    for (_, acc), h in zip(carry, heads):
        o_ref[0, :, h] = (acc[:, :dh] / acc[:, dh:]).astype(o_ref.dtype)


def fox_attention(proj, cum):
    B, S = proj.shape[:2]
    H, dh, t, nh = FOX_HEADS, FOX_HEAD_DIM, FOX_TILE, FOX_HEADS_PER_STEP
    G = H // nh
    return pl.pallas_call(
        _fox_body,
        out_shape=jax.ShapeDtypeStruct((B, S, H * dh), BF16),
        grid=(B, G, S // t),
        in_specs=[
            pl.BlockSpec((1, t, nh * dh), lambda b, h, i: (b, i, h)),
            pl.BlockSpec((1, S, nh * dh), lambda b, h, i: (b, 0, G + h)),
            pl.BlockSpec((1, S, nh * dh), lambda b, h, i: (b, 0, 2 * G + h)),
            pl.BlockSpec((1, nh, t, 1), lambda b, h, i: (b, h, i, 0)),
            pl.BlockSpec((1, nh, 1, S), lambda b, h, i: (b, h, 0, 0)),
        ],
        out_specs=pl.BlockSpec((1, t, nh * dh), lambda b, h, i: (b, i, h)),
        compiler_params=_cparams("parallel", "parallel", "arbitrary"),
        name="fox_attention",
    )(proj, proj, proj, cum[..., None], cum[:, :, None, :])


def _ret_body(lg_ref, q_ref, k_ref, v_ref, g_ref, cos_ref, sin_ref, gn_ref, o_ref, r_ref):
    C, dk = RET_CHUNK, RET_QK_DIM
    half = dk // 2
    lg = lg_ref[pl.program_id(1)]
    ii = lax.broadcasted_iota(jnp.int32, (C, C), 0)
    jj = lax.broadcasted_iota(jnp.int32, (C, C), 1)
    decay_mask = jnp.where(ii >= jj, jnp.exp((ii - jj).astype(F32) * lg), 0.0)
    ti = lax.broadcasted_iota(jnp.int32, (C, 1), 0).astype(F32)
    q_scale = jnp.exp((ti + 1.0) * lg)
    k_scale = jnp.exp((C - 1.0 - ti) * lg)
    chunk_decay = jnp.exp(jnp.full((1, 1), C, F32) * lg)
    r_ref[...] = jnp.zeros_like(r_ref)

    def rot(x, cos, sin):
        x1, x2 = x[:, :half], x[:, half:]
        return jnp.concatenate([x1 * cos - x2 * sin, x2 * cos + x1 * sin], axis=1)

    def chunk(c, carry):
        sl = pl.ds(pl.multiple_of(c * C, C), C)
        cos, sin = cos_ref[sl, :], sin_ref[sl, :]
        q = rot(q_ref[0, sl, :].astype(F32), cos, sin)
        k = rot(k_ref[0, sl, :].astype(F32), cos, sin) * (dk ** -0.5)
        v = v_ref[0, sl, :]
        qb = q.astype(BF16)
        inner = lax.dot_general(qb, k.astype(BF16), (((1,), (1,)), ((), ())),
                                preferred_element_type=F32) * decay_mask
        state = r_ref[...]
        o = (jnp.dot(inner.astype(BF16), v, preferred_element_type=F32)
             + jnp.dot(qb, state.astype(BF16), preferred_element_type=F32) * q_scale)
        r_ref[...] = state * chunk_decay + lax.dot_general(
            (k * k_scale).astype(BF16), v, (((0,), (0,)), ((), ())), preferred_element_type=F32)
        mu = jnp.mean(o, axis=-1, keepdims=True)
        d = o - mu
        on = d * lax.rsqrt(jnp.mean(d * d, axis=-1, keepdims=True) + RET_GN_EPS)
        g = g_ref[0, sl, :].astype(F32)
        o_ref[0, sl, :] = ((g * jax.nn.sigmoid(g)) * (on * gn_ref[...])).astype(o_ref.dtype)
        return carry

    lax.fori_loop(0, q_ref.shape[1] // C, chunk, 0)


def retention_core(proj, gn_g):
    B, S = proj.shape[:2]
    H, dk, dv = RET_HEADS, RET_QK_DIM, RET_V_DIM
    pos = jnp.arange(S, dtype=F32)
    inv_freq = RET_THETA ** (-jnp.arange(dk // 2, dtype=F32) / (dk // 2))
    ang = pos[:, None] * inv_freq
    log_gamma = jnp.log(1.0 - 2.0 ** (-5.0 - jnp.arange(H, dtype=F32)))
    tab = pl.BlockSpec((S, dk // 2), lambda b, h: (0, 0))
    return pl.pallas_call(
        _ret_body,
        out_shape=jax.ShapeDtypeStruct((B, S, H * dv), BF16),
        grid=(B, H),
        in_specs=[
            pl.BlockSpec(memory_space=pltpu.SMEM),
            pl.BlockSpec((1, S, dk), lambda b, h: (b, 0, h)),
            pl.BlockSpec((1, S, dk), lambda b, h: (b, 0, H + h)),
            pl.BlockSpec((1, S, dv), lambda b, h: (b, 0, 2 * H * dk // dv + h)),
            pl.BlockSpec((1, S, dv), lambda b, h: (b, 0, 2 * H * dk // dv + H + h)),
            tab, tab,
            pl.BlockSpec((1, dv), lambda b, h: (0, h)),
        ],
        out_specs=pl.BlockSpec((1, S, dv), lambda b, h: (b, 0, h)),
        scratch_shapes=[pltpu.VMEM((dk, dv), F32)],
        compiler_params=_cparams("parallel", "parallel"),
        name="retention_core",
    )(log_gamma, proj, proj, proj, proj, jnp.cos(ang), jnp.sin(ang), gn_g.reshape(1, H * dv))


def _head_norm(y, eps):
    y = y.astype(F32)
    mu = jnp.mean(y, axis=-1, keepdims=True)
    var = jnp.mean(jnp.square(y - mu), axis=-1, keepdims=True)
    return (y - mu) * lax.rsqrt(var + eps)


def _rotary(x, pos, theta, rot_dim):
    half = rot_dim // 2
    inv_freq = theta ** (-jnp.arange(half, dtype=F32) / half)
    ang = pos.astype(F32)[:, None] * inv_freq
    shape = (1, ang.shape[0]) + (1,) * (x.ndim - 3) + (half,)
    cos = jnp.cos(ang).reshape(shape)
    sin = jnp.sin(ang).reshape(shape)
    xr = x[..., :rot_dim].astype(F32)
    x1, x2 = xr[..., :half], xr[..., half:]
    rot = jnp.concatenate([x1 * cos - x2 * sin, x2 * cos + x1 * sin], axis=-1).astype(x.dtype)
    if rot_dim == x.shape[-1]:
        return rot
    return jnp.concatenate([rot, x[..., rot_dim:]], axis=-1)


def _pad_cols(w, n):
    return jnp.pad(w, ((0, 0), (0, n - w.shape[1])))


def _nsa_core(proj, cmp_pe, cmp_w1, cmp_w2):
    B, S = proj.shape[:2]
    H, G, dh = NSA_HEADS, NSA_KV_GROUPS, NSA_HEAD_DIM
    R = H // G
    L, stride, blk, W = NSA_CMP_LEN, NSA_CMP_STRIDE, NSA_SEL_BLOCK, NSA_WINDOW
    scale = dh ** -0.5
    pos = jnp.arange(S)
    hq, hk = H * dh, G * dh
    q = proj[..., :hq].reshape(B, S, G, R, dh)
    kv = proj[..., hq:hq + 6 * hk].reshape(B, S, 6, G, dh)
    gates = jax.nn.sigmoid(proj[..., hq + 6 * hk:hq + 6 * hk + 3 * H].reshape(B, S, G, R, 3))
    q = _rotary(q, pos, ROPE_THETA, NSA_ROPE_DIM)
    kc, ks, kw = (_rotary(kv[:, :, c], pos, ROPE_THETA, NSA_ROPE_DIM) for c in (0, 2, 4))
    vc, vs, vw = kv[:, :, 1], kv[:, :, 3], kv[:, :, 5]

    n_cmp = (S - L) // stride + 1
    idx_np = np.arange(n_cmp)[:, None] * stride + np.arange(L)[None, :]
    cmp_end = jnp.asarray(idx_np[:, -1])

    def compress(t, c):
        blocks = t[:, idx_np] + cmp_pe[c][None, None, :, None, :]
        hid = jax.nn.gelu(jnp.einsum('bnlgd,ldf->bngf', blocks, cmp_w1[c]))
        return jnp.einsum('bngf,fe->bnge', hid, cmp_w2[c])

    k_cmp, v_cmp = compress(kc, 0), compress(vc, 1)
    s_cmp = jnp.einsum('bsgrd,bngd->bgrsn', q, k_cmp).astype(F32) * scale
    m_cmp = cmp_end[None, :] <= pos[:, None]
    p_cmp = jax.nn.softmax(jnp.where(m_cmp, s_cmp, NEG_INF), axis=-1) * m_cmp
    o_cmp = jnp.einsum('bgrsn,bngd->bsgrd', p_cmp, v_cmp)

    n_blk = S // blk
    starts = np.arange(n_cmp) * stride
    bstart = np.arange(n_blk) * blk
    overlap = (starts[:, None] <= bstart[None, :] + blk - 1) & (starts[:, None] + L - 1 >= bstart[None, :])
    imp = jnp.einsum('bgrsn,nj->bgsj', p_cmp, jnp.asarray(overlap, F32))
    imp = imp.reshape(B, G, n_blk, blk, n_blk).sum(axis=3)
    qb_i = np.arange(n_blk)[:, None]
    kb_j = np.arange(n_blk)[None, :]
    valid = kb_j <= qb_i
    forced = (kb_j == 0) | (kb_j == qb_i) | (kb_j == qb_i - 1)
    imp = jnp.where(forced, jnp.inf, jnp.where(valid, imp, -jnp.inf))
    n_sel = min(NSA_N_SEL, n_blk)
    _, sel = lax.top_k(imp, n_sel)

    flat = lambda t: t.reshape(B, S, -1).astype(BF16)
    o_sel, o_win = nsa_selected_window(sel, flat(q * scale), flat(ks), flat(vs), flat(kw), flat(vw))
    o_sel = o_sel.reshape(B, S, G, R, dh)
    o_win = o_win.reshape(B, S, G, R, dh)
    o = gates[..., 0, None] * o_cmp + gates[..., 1, None] * o_sel + gates[..., 2, None] * o_win
    return o.reshape(B * S, H * dh)


NSA_IN_PADDED = 42 * LANE


def _nsa_mixer(h, g, w_in, cmp_pe, cmp_w1, cmp_w2, B, S):
    T = h.shape[0]
    proj = norm_matmul(h, g, _pad_cols(w_in, NSA_IN_PADDED).astype(BF16))
    qkv, kcvc = nsa_prepare(proj, S)
    cmp = nsa_compress(kcvc.reshape(B, S, -1), cmp_pe, cmp_w1.astype(BF16), cmp_w2.astype(BF16))
    qkv = qkv.reshape(B, S, -1)
    o_cmp, sel = nsa_compressed_attention(qkv, cmp)
    o_sel, o_win = nsa_selected_window(sel[..., :NSA_N_SEL], qkv)
    return nsa_combine(proj, o_cmp.reshape(T, -1), o_sel.reshape(T, -1), o_win.reshape(T, -1))


def _rwkv_mixer(h, g, mu, w_rkv, w0, w1, w2, a0, a1, a2, g1, g2, k_k, k_a, r_k, ln_gb, B, S):
    T, D = h.shape
    pad_c = lambda w: _pad_cols(w, LANE)
    pad_r = lambda w: jnp.pad(w, ((0, LANE - w.shape[0]), (0, 0)))
    w_all = jnp.concatenate([w_rkv[0], w_rkv[1], w_rkv[2], pad_c(w1), pad_c(a1), g1], axis=1).astype(BF16)
    proj = rwkv_mix_project(h, g, mu, w_all, S)
    lw, a, kk, k2, gate = rwkv_gates(proj, jnp.stack([w0, a0, k_k, k_a]), pad_r(w2).astype(BF16),
                                     pad_r(a2).astype(BF16), g2.astype(BF16))
    as3 = lambda t: t.reshape(B, S, -1)
    y = rwkv_recurrence(as3(proj), as3(lw), as3(k2), as3(kk), as3(a))
    vec = jnp.stack([ln_gb[0], ln_gb[1], r_k.reshape(D)])
    return rwkv_post(y.reshape(T, D), proj, k2, gate, vec)


def _fox_mixer(h, g, w_in, b_f, B, S):
    HD = FOX_HEADS * FOX_HEAD_DIM
    w_qkv = jnp.concatenate([w_in[:, :HD] * FOX_HEAD_DIM ** -0.5, w_in[:, HD:3 * HD]], axis=1)
    proj = norm_matmul(h, g, w_qkv.astype(BF16), out_dtype=BF16)
    z = norm_matmul(h, g, _pad_cols(w_in[:, 3 * HD:], LANE).astype(BF16))
    log_f = jax.nn.log_sigmoid(z[:, :FOX_HEADS].reshape(B, S, FOX_HEADS) + b_f)
    cum = jnp.transpose(jnp.cumsum(log_f, axis=1), (0, 2, 1))
    return fox_attention(proj.reshape(B, S, 3 * HD), cum).reshape(B * S, HD)


def kernel(x, p, norm_g, ffn_w_in, ffn_w_out, ple_w_proj, ple_w_gate, nsa_w_in, nsa_cmp_pe, nsa_cmp_w1, nsa_cmp_w2, nsa_w_out, rwkv_mu, rwkv_w_rkv, rwkv_w0, rwkv_w1, rwkv_w2, rwkv_a0, rwkv_a1, rwkv_a2, rwkv_g1, rwkv_g2, rwkv_k_k, rwkv_k_a, rwkv_r_k, rwkv_ln, rwkv_w_out, fox_w_in, fox_b_f, fox_w_out, ret_w_in, ret_gn_g, ret_w_out):
    B, S, D = x.shape
    T = B * S
    h = x.reshape(T, D)
    bf = lambda w: w.astype(BF16)
    for i in range(DEPTH):
        m, j = i % N_MIXERS, i // N_MIXERS
        ng = norm_g[i]
        h = ffn_half_step(h, ng[0], ng[1], bf(ffn_w_in[i, 0]), bf(ffn_w_out[i, 0]))
        if m == 0:
            y = _nsa_mixer(h, ng[2], nsa_w_in[j], nsa_cmp_pe[j], nsa_cmp_w1[j], nsa_cmp_w2[j], B, S)
            w_out = nsa_w_out[j]
        elif m == 1:
            y = _rwkv_mixer(h, ng[2], rwkv_mu[j], rwkv_w_rkv[j], rwkv_w0[j], rwkv_w1[j], rwkv_w2[j],
                            rwkv_a0[j], rwkv_a1[j], rwkv_a2[j], rwkv_g1[j], rwkv_g2[j],
                            rwkv_k_k[j], rwkv_k_a[j], rwkv_r_k[j], rwkv_ln[j], B, S)
            w_out = rwkv_w_out[j]
        elif m == 2:
            y = _fox_mixer(h, ng[2], fox_w_in[j], fox_b_f[j], B, S)
            w_out = fox_w_out[j]
        else:
            proj = norm_matmul(h, ng[2], bf(ret_w_in[j]), out_dtype=BF16)
            y = retention_core(proj.reshape(B, S, RET_IN), ret_gn_g[j]).reshape(T, -1)
            w_out = ret_w_out[j]
        h = matmul_norm_residual(y, bf(w_out), ng[3], h)
        h = ffn_half_step(h, ng[4], ng[5], bf(ffn_w_in[i, 1]), bf(ffn_w_out[i, 1]))
        h = ple_step(h, p[i].reshape(T, PLE_DIM), ng[6], ng[7], bf(ple_w_proj[i]), bf(ple_w_gate[i]))
    return h.reshape(B, S, D)
```

```python
import functools

import jax
import jax.numpy as jnp
import numpy as np
from jax import lax
from jax.experimental import pallas as pl
from jax.experimental.pallas import tpu as pltpu

D_MODEL = 2048
BATCH = 16
SEQ = 2048
DEPTH = 4
N_MIXERS = 4
PLE_DIM = 256
D_FF = 5632
RMS_EPS = 1e-6
NEG_INF = -1e30

NSA_HEADS = 16
NSA_KV_GROUPS = 4
NSA_HEAD_DIM = D_MODEL // NSA_HEADS
NSA_CMP_LEN = 32
NSA_CMP_STRIDE = 16
NSA_CMP_HIDDEN = 2 * NSA_HEAD_DIM
NSA_SEL_BLOCK = 64
NSA_N_SEL = 8
NSA_WINDOW = 512
NSA_ROPE_DIM = NSA_HEAD_DIM // 4
ROPE_THETA = 500000.0
NSA_IN = NSA_HEADS * NSA_HEAD_DIM + 6 * NSA_KV_GROUPS * NSA_HEAD_DIM + 3 * NSA_HEADS

RWKV_HEAD_DIM = 64
RWKV_HEADS = D_MODEL // RWKV_HEAD_DIM
RWKV_GN_EPS = 64e-5

FOX_HEADS = 16
FOX_HEAD_DIM = D_MODEL // FOX_HEADS
FOX_BLOCK = 128
FOX_IN = 3 * FOX_HEADS * FOX_HEAD_DIM + FOX_HEADS

RET_HEADS = 8
RET_QK_DIM = D_MODEL // RET_HEADS
RET_V_DIM = 2 * D_MODEL // RET_HEADS
RET_CHUNK = 128
RET_THETA = 10000.0
RET_GN_EPS = 1e-5
RET_IN = 2 * RET_HEADS * RET_QK_DIM + 2 * RET_HEADS * RET_V_DIM

V7X_VMEM_LIMIT_BYTES = 56 * 1024 * 1024
LANE = 128

F32 = jnp.float32
BF16 = jnp.bfloat16


def _cparams(*sem):
    return pltpu.CompilerParams(dimension_semantics=sem, vmem_limit_bytes=V7X_VMEM_LIMIT_BYTES)


def _rms(x, g):
    return x * lax.rsqrt(jnp.mean(x * x, axis=-1, keepdims=True) + RMS_EPS) * g


def _ffn_body(h_ref, g0_ref, g1_ref, wg_ref, wu_ref, wo_ref, o_ref, xn_ref, act_ref, acc_ref):
    f = pl.program_id(1)
    last = pl.num_programs(1) - 1

    def activation():
        xn = xn_ref[...]
        gate = jnp.dot(xn, wg_ref[...], preferred_element_type=F32)
        up = jnp.dot(xn, wu_ref[...], preferred_element_type=F32)
        return ((gate * jax.nn.sigmoid(gate)) * up).astype(BF16)

    def down():
        return jnp.dot(act_ref[...], wo_ref[...], preferred_element_type=F32)

    @pl.when(f == 0)
    def _():
        xn_ref[...] = _rms(h_ref[...], g0_ref[...]).astype(BF16)
        acc_ref[...] = jnp.zeros_like(acc_ref)
        act_ref[...] = activation()

    @pl.when((f > 0) & (f < last))
    def _():
        acc_ref[...] += down()
        act_ref[...] = activation()

    @pl.when(f == last)
    def _():
        o_ref[...] = h_ref[...] + 0.5 * _rms(acc_ref[...] + down(), g1_ref[...])


def ffn_half_step(h, g0, g1, w_in, w_out, *, tm=512, tf=512):
    T, D = h.shape
    nf = D_FF // tf
    return pl.pallas_call(
        _ffn_body,
        out_shape=jax.ShapeDtypeStruct((T, D), F32),
        grid=(T // tm, nf + 1),
        in_specs=[
            pl.BlockSpec((tm, D), lambda i, f: (i, 0)),
            pl.BlockSpec((1, D), lambda i, f: (0, 0)),
            pl.BlockSpec((1, D), lambda i, f: (0, 0)),
            pl.BlockSpec((D, tf), lambda i, f: (0, jnp.minimum(f, nf - 1))),
            pl.BlockSpec((D, tf), lambda i, f: (0, jnp.minimum(f, nf - 1) + nf)),
            pl.BlockSpec((tf, D), lambda i, f: (jnp.maximum(f - 1, 0), 0)),
        ],
        out_specs=pl.BlockSpec((tm, D), lambda i, f: (i, 0)),
        scratch_shapes=[pltpu.VMEM((tm, D), BF16), pltpu.VMEM((tm, tf), BF16), pltpu.VMEM((tm, D), F32)],
        compiler_params=_cparams("parallel", "arbitrary"),
        name="ffn_half_step",
    )(h, g0.reshape(1, D), g1.reshape(1, D), w_in, w_in, w_out)


def _norm_mm_body(x_ref, g_ref, w_ref, o_ref, xn_ref):
    @pl.when(pl.program_id(1) == 0)
    def _():
        xn_ref[...] = _rms(x_ref[...], g_ref[...]).astype(BF16)

    o_ref[...] = jnp.dot(xn_ref[...], w_ref[...], preferred_element_type=F32).astype(o_ref.dtype)


def _mm_body(x_ref, w_ref, o_ref):
    o_ref[...] = jnp.dot(x_ref[...].astype(BF16), w_ref[...],
                         preferred_element_type=F32).astype(o_ref.dtype)


def _col_tile(n, cap=1024):
    best = LANE
    for t in range(LANE, cap + 1, LANE):
        if n % t == 0:
            best = t
    return best


def norm_matmul(x, g, w, *, out_dtype=F32, tm=1024):
    T, K = x.shape
    N = w.shape[1]
    tn = _col_tile(N)
    return pl.pallas_call(
        _norm_mm_body,
        out_shape=jax.ShapeDtypeStruct((T, N), out_dtype),
        grid=(T // tm, N // tn),
        in_specs=[
            pl.BlockSpec((tm, K), lambda i, j: (i, 0)),
            pl.BlockSpec((1, K), lambda i, j: (0, 0)),
            pl.BlockSpec((K, tn), lambda i, j: (0, j)),
        ],
        out_specs=pl.BlockSpec((tm, tn), lambda i, j: (i, j)),
        scratch_shapes=[pltpu.VMEM((tm, K), BF16)],
        compiler_params=_cparams("parallel", "arbitrary"),
        name="norm_matmul",
    )(x, g.reshape(1, K), w)


def matmul(x, w, *, out_dtype=F32, tm=1024):
    T, K = x.shape
    N = w.shape[1]
    tn = _col_tile(N)
    return pl.pallas_call(
        _mm_body,
        out_shape=jax.ShapeDtypeStruct((T, N), out_dtype),
        grid=(T // tm, N // tn),
        in_specs=[
            pl.BlockSpec((tm, K), lambda i, j: (i, 0)),
            pl.BlockSpec((K, tn), lambda i, j: (0, j)),
        ],
        out_specs=pl.BlockSpec((tm, tn), lambda i, j: (i, j)),
        compiler_params=_cparams("parallel", "arbitrary"),
        name="matmul",
    )(x, w)


def _mm_res_body(y_ref, w_ref, g_ref, h_ref, o_ref, acc_ref):
    k = pl.program_id(1)

    @pl.when(k == 0)
    def _():
        acc_ref[...] = jnp.zeros_like(acc_ref)

    acc_ref[...] += jnp.dot(y_ref[...].astype(BF16), w_ref[...], preferred_element_type=F32)

    @pl.when(k == pl.num_programs(1) - 1)
    def _():
        o_ref[...] = h_ref[...] + _rms(acc_ref[...], g_ref[...])


def _mm_res_single_body(y_ref, w_ref, g_ref, h_ref, o_ref):
    acc = jnp.dot(y_ref[...].astype(BF16), w_ref[...], preferred_element_type=F32)
    o_ref[...] = h_ref[...] + _rms(acc, g_ref[...])


def matmul_norm_residual(y, w, g, h, *, tm=512, tk=2048):
    T, K = y.shape
    D = w.shape[1]
    if K == tk:
        return pl.pallas_call(
            _mm_res_single_body,
            out_shape=jax.ShapeDtypeStruct((T, D), F32),
            grid=(T // tm,),
            in_specs=[
                pl.BlockSpec((tm, K), lambda i: (i, 0)),
                pl.BlockSpec((K, D), lambda i: (0, 0)),
                pl.BlockSpec((1, D), lambda i: (0, 0)),
                pl.BlockSpec((tm, D), lambda i: (i, 0)),
            ],
            out_specs=pl.BlockSpec((tm, D), lambda i: (i, 0)),
            compiler_params=_cparams("parallel"),
            name="matmul_norm_residual",
        )(y, w, g.reshape(1, D), h)
    return pl.pallas_call(
        _mm_res_body,
        out_shape=jax.ShapeDtypeStruct((T, D), F32),
        grid=(T // tm, K // tk),
        in_specs=[
            pl.BlockSpec((tm, tk), lambda i, k: (i, k)),
            pl.BlockSpec((tk, D), lambda i, k: (k, 0)),
            pl.BlockSpec((1, D), lambda i, k: (0, 0)),
            pl.BlockSpec((tm, D), lambda i, k: (i, 0)),
        ],
        out_specs=pl.BlockSpec((tm, D), lambda i, k: (i, 0)),
        scratch_shapes=[pltpu.VMEM((tm, D), F32)],
        compiler_params=_cparams("parallel", "arbitrary"),
        name="matmul_norm_residual",
    )(y, w, g.reshape(1, D), h)


def _ple_body(h_ref, p_ref, g6_ref, g7_ref, wp_ref, wg_ref, o_ref):
    h = h_ref[...]
    xn = _rms(h, g6_ref[...]).astype(BF16)
    z = jnp.dot(xn, wg_ref[...], preferred_element_type=F32)
    e = jnp.dot(p_ref[...].astype(BF16), wp_ref[...], preferred_element_type=F32)
    o_ref[...] = h + _rms(e * jax.nn.sigmoid(z), g7_ref[...])


def ple_step(h, p, g6, g7, wp, wg, *, tm=256):
    T, D = h.shape
    P = p.shape[1]
    return pl.pallas_call(
        _ple_body,
        out_shape=jax.ShapeDtypeStruct((T, D), F32),
        grid=(T // tm,),
        in_specs=[
            pl.BlockSpec((tm, D), lambda i: (i, 0)),
            pl.BlockSpec((tm, P), lambda i: (i, 0)),
            pl.BlockSpec((1, D), lambda i: (0, 0)),
            pl.BlockSpec((1, D), lambda i: (0, 0)),
            pl.BlockSpec((P, D), lambda i: (0, 0)),
            pl.BlockSpec((D, D), lambda i: (0, 0)),
        ],
        out_specs=pl.BlockSpec((tm, D), lambda i: (i, 0)),
        compiler_params=_cparams("parallel"),
        name="ple_step",
    )(h, p, g6.reshape(1, D), g7.reshape(1, D), wp, wg)


RWKV_CHUNK = 64
RWKV_PAIRS_PER_STEP = 16
RWKV_TIME_BLOCK = 256


def _rwkv_body(r_ref, lw_ref, k_ref, v_ref, kk_ref, a_ref, y_ref, s_ref):
    C = RWKV_CHUNK
    N = RWKV_HEAD_DIM
    lane = lax.broadcasted_iota(jnp.int32, (C, 2 * N), 1)
    row = lax.broadcasted_iota(jnp.int32, (C, 2 * N), 0)
    first_head = lane < N
    ri = lax.broadcasted_iota(jnp.int32, (2 * C, 2 * C), 0)
    ci = lax.broadcasted_iota(jnp.int32, (2 * C, 2 * C), 1)
    strict = ri > ci
    incl = ri >= ci
    eye = jnp.where(ri == ci, 1.0, 0.0).astype(F32)
    corner = [((ri >> (lvl + 1)) == (ci >> (lvl + 1))) & ((ri & (1 << lvl)) != 0) & ((ci & (1 << lvl)) == 0)
              for lvl in range(C.bit_length() - 1)]

    def stack(x):
        return jnp.concatenate([jnp.where(first_head, x, 0.0), jnp.where(first_head, 0.0, x)], axis=0)

    def nt(a, b):
        return lax.dot_general(a.astype(BF16), b.astype(BF16), (((1,), (1,)), ((), ())),
                               preferred_element_type=F32)

    def nn(a, b):
        return jnp.dot(a.astype(BF16), b.astype(BF16), preferred_element_type=F32)

    def tn(a, b):
        return lax.dot_general(a.astype(BF16), b.astype(BF16), (((0,), (0,)), ((), ())),
                               preferred_element_type=F32)

    n_pairs = r_ref.shape[2] // (2 * N)
    lanes = [slice(2 * N * j, 2 * N * (j + 1)) for j in range(n_pairs)]

    def prep(sl, ln):
        r, lw, k, v, kk, a = (ref[0, sl, ln] for ref in (r_ref, lw_ref, k_ref, v_ref, kk_ref, a_ref))
        cl = lw
        for sh in (1, 2, 4, 8, 16, 32):
            cl = cl + jnp.where(row >= sh, pltpu.roll(cl, sh, axis=0), 0.0)
        mid = cl[C // 2 - 1:C // 2, :]
        last = cl[C - 1:C, :]
        e_neg = jnp.exp(mid - cl)
        e_end = jnp.exp(last - mid)
        b_til = stack(kk * a * e_neg)
        k_til = stack(k * e_neg)
        return dict(
            a_bar=stack(-kk * jnp.exp(cl - lw - mid)), r_bar=stack(r * jnp.exp(cl - mid)),
            b_til=b_til, k_til=k_til, v_st=stack(v), e_mid=jnp.exp(mid), w_tot=jnp.exp(last),
            bk_end=jnp.concatenate([b_til * e_end, k_til * e_end], axis=0))

    def chunk(c, states):
        sl = pl.ds(pl.multiple_of(c * C, C), C)
        ps = [prep(sl, ln) for ln in lanes]
        gs = [nt(jnp.concatenate([p["a_bar"], p["r_bar"]], axis=0),
                 jnp.concatenate([p["b_til"], p["k_til"]], axis=0)) for p in ps]
        a_ab = [jnp.where(strict, g[:2 * C, :2 * C], 0.0) for g in gs]
        a_ak = [jnp.where(strict, g[:2 * C, 2 * C:], 0.0) for g in gs]
        a_rb = [jnp.where(incl, g[2 * C:, :2 * C], 0.0) for g in gs]
        a_rk = [jnp.where(incl, g[2 * C:, 2 * C:], 0.0) for g in gs]
        inv = [eye + jnp.where(corner[0], x, 0.0) for x in a_ab]
        for lvl in range(1, len(corner)):
            tmp = [nn(jnp.where(corner[lvl], x, 0.0), t) for x, t in zip(a_ab, inv)]
            inv = [t + nn(t, x) for t, x in zip(inv, tmp)]
        s_mid = [st * p["e_mid"] for st, p in zip(states, ps)]
        rhs = [nt(p["a_bar"], sm) + nn(ak, p["v_st"]) for p, sm, ak in zip(ps, s_mid, a_ak)]
        us = [nn(t, x) for t, x in zip(inv, rhs)]
        ys = [nt(p["r_bar"], sm) + nn(rb, u) + nn(rk, p["v_st"])
              for p, sm, rb, rk, u in zip(ps, s_mid, a_rb, a_rk, us)]
        for ln, y in zip(lanes, ys):
            y_ref[0, sl, ln] = y[:C] + y[C:]
        return tuple(st * p["w_tot"] + tn(jnp.concatenate([u, p["v_st"]], axis=0), p["bk_end"])
                     for st, p, u in zip(states, ps, us))

    @pl.when(pl.program_id(2) == 0)
    def _():
        s_ref[...] = jnp.zeros_like(s_ref)

    states = lax.fori_loop(0, r_ref.shape[1] // C, chunk, tuple(s_ref[j] for j in range(n_pairs)))
    for j, st in enumerate(states):
        s_ref[j] = st


def rwkv_recurrence(proj, lw, k, kk, a):
    B, S, D = lw.shape
    lanes = 2 * RWKV_HEAD_DIM * RWKV_PAIRS_PER_STEP
    ts = min(S, RWKV_TIME_BLOCK)
    spec = pl.BlockSpec((1, ts, lanes), lambda b, j, t: (b, t, j))
    v_spec = pl.BlockSpec((1, ts, lanes), lambda b, j, t: (b, t, 2 * D // lanes + j))
    return pl.pallas_call(
        _rwkv_body,
        out_shape=jax.ShapeDtypeStruct((B, S, D), F32),
        grid=(B, D // lanes, S // ts),
        in_specs=[spec, spec, spec, v_spec, spec, spec],
        out_specs=spec,
        scratch_shapes=[pltpu.VMEM((RWKV_PAIRS_PER_STEP, 2 * RWKV_HEAD_DIM, 2 * RWKV_HEAD_DIM), F32)],
        compiler_params=_cparams("parallel", "parallel", "arbitrary"),
        name="rwkv_recurrence",
    )(proj, lw, k, proj, kk, a)


RWKV_PROJ_TN = 1024
RWKV_HIDDEN = 4 * LANE


def _rwkv_shift(first_of_seq, h_ref, hp_ref, g_ref):
    u = _rms(h_ref[...], g_ref[...])
    prev = jnp.where(first_of_seq, 0.0, _rms(hp_ref[7:8, :], g_ref[...]))
    row = lax.broadcasted_iota(jnp.int32, u.shape, 0)
    return u, jnp.where(row == 0, prev, pltpu.roll(u, 1, axis=0)) - u


def _rwkv_proj_body(tiles_per_seq, h_ref, hp_ref, g_ref, mu_ref, w_ref, o_ref, xm_ref):
    i, j = pl.program_id(0), pl.program_id(1)

    @pl.when(j == 0)
    def _():
        u, xx = _rwkv_shift(lax.rem(i, tiles_per_seq) == 0, h_ref, hp_ref, g_ref)
        for c in range(3):
            xm_ref[c] = (u + xx * mu_ref[c:c + 1, :]).astype(BF16)

    o_ref[...] = jnp.dot(xm_ref[j // (pl.num_programs(1) // 3)], w_ref[...], preferred_element_type=F32)


def _rwkv_hidden_body(tiles_per_seq, h_ref, hp_ref, g_ref, mu_ref, w_ref, o_ref):
    u, xx = _rwkv_shift(lax.rem(pl.program_id(0), tiles_per_seq) == 0, h_ref, hp_ref, g_ref)
    mix = lambda c: (u + xx * mu_ref[c:c + 1, :]).astype(BF16)
    o_ref[:, :LANE] = jnp.tanh(jnp.dot(mix(3), w_ref[:, :LANE], preferred_element_type=F32))
    o_ref[:, LANE:2 * LANE] = jnp.dot(mix(4), w_ref[:, LANE:2 * LANE], preferred_element_type=F32)
    o_ref[:, 2 * LANE:] = jax.nn.sigmoid(jnp.dot(mix(5), w_ref[:, 2 * LANE:], preferred_element_type=F32))


def rwkv_mix_project(h, g, mu, w_rkv, w_hidden, seq_len, *, tm=512):
    T, D = h.shape
    tn = RWKV_PROJ_TN
    prev_rows = lambda i, *_: (jnp.maximum(i * (tm // 8) - 1, 0), 0)
    rkv = pl.pallas_call(
        functools.partial(_rwkv_proj_body, seq_len // tm),
        out_shape=jax.ShapeDtypeStruct((T, 3 * D), F32),
        grid=(T // tm, 3 * D // tn),
        in_specs=[
            pl.BlockSpec((tm, D), lambda i, j: (i, 0)),
            pl.BlockSpec((8, D), prev_rows),
            pl.BlockSpec((1, D), lambda i, j: (0, 0)),
            pl.BlockSpec((6, D), lambda i, j: (0, 0)),
            pl.BlockSpec((D, tn), lambda i, j: (0, j)),
        ],
        out_specs=pl.BlockSpec((tm, tn), lambda i, j: (i, j)),
        scratch_shapes=[pltpu.VMEM((3, tm, D), BF16)],
        compiler_params=_cparams("parallel", "arbitrary"),
        name="rwkv_mix_project",
    )(h, h, g.reshape(1, D), mu, w_rkv)
    hidden = pl.pallas_call(
        functools.partial(_rwkv_hidden_body, seq_len // tm),
        out_shape=jax.ShapeDtypeStruct((T, RWKV_HIDDEN), F32),
        grid=(T // tm,),
        in_specs=[
            pl.BlockSpec((tm, D), lambda i: (i, 0)),
            pl.BlockSpec((8, D), prev_rows),
            pl.BlockSpec((1, D), lambda i: (0, 0)),
            pl.BlockSpec((6, D), lambda i: (0, 0)),
            pl.BlockSpec((D, RWKV_HIDDEN), lambda i: (0, 0)),
        ],
        out_specs=pl.BlockSpec((tm, RWKV_HIDDEN), lambda i: (i, 0)),
        compiler_params=_cparams("parallel"),
        name="rwkv_mix_hidden",
    )(h, h, g.reshape(1, D), mu, w_hidden)
    return rkv, hidden


def _head_sum(x, ones_bd):
    hi = x.astype(BF16)
    lo = (x - hi.astype(F32)).astype(BF16)
    return (jnp.dot(hi, ones_bd, preferred_element_type=F32) + jnp.dot(lo, ones_bd, preferred_element_type=F32))


def _head_ones():
    shift = RWKV_HEAD_DIM.bit_length() - 1
    r = lax.broadcasted_iota(jnp.int32, (LANE, LANE), 0) >> shift
    c = lax.broadcasted_iota(jnp.int32, (LANE, LANE), 1) >> shift
    return jnp.where(r == c, 1.0, 0.0).astype(BF16)


def _rwkv_gates_body(k_ref, hid_ref, vec_ref, w2_ref, a2_ref, g2_ref, lw_ref, a_ref, kk_ref, k2_ref, gate_ref):
    hid = hid_ref[...].astype(BF16)
    z = vec_ref[0:1, :] + jnp.dot(hid[:, :LANE], w2_ref[...], preferred_element_type=F32)
    softplus = jnp.maximum(-z, 0.0) + jnp.log(1.0 + jnp.exp(-jnp.abs(z)))
    lw_ref[...] = -jnp.exp(-softplus - 0.5)
    a = jax.nn.sigmoid(vec_ref[1:2, :] + jnp.dot(hid[:, LANE:2 * LANE], a2_ref[...], preferred_element_type=F32))
    a_ref[...] = a
    gate_ref[...] = jnp.dot(hid[:, 2 * LANE:], g2_ref[...], preferred_element_type=F32)
    k = k_ref[...]
    k2_ref[...] = k * (1.0 + (a - 1.0) * vec_ref[3:4, :])
    kk = k * vec_ref[2:3, :]
    ones_bd = _head_ones()
    for c in range(k.shape[1] // LANE):
        blk = kk[:, c * LANE:(c + 1) * LANE]
        norm = jnp.sqrt(_head_sum(blk * blk, ones_bd))
        kk_ref[:, c * LANE:(c + 1) * LANE] = blk / jnp.maximum(norm, 1e-12)


def rwkv_gates(proj, hidden, vec, w2, a2, g2, *, tm=256):
    T = proj.shape[0]
    D = w2.shape[1]
    row = pl.BlockSpec((tm, D), lambda i: (i, 0))
    full = lambda w: pl.BlockSpec(w.shape, lambda i: (0, 0))
    return pl.pallas_call(
        _rwkv_gates_body,
        out_shape=tuple(jax.ShapeDtypeStruct((T, D), F32) for _ in range(5)),
        grid=(T // tm,),
        in_specs=[
            pl.BlockSpec((tm, D), lambda i: (i, 1)),
            pl.BlockSpec((tm, RWKV_HIDDEN), lambda i: (i, 0)),
            full(vec), full(w2), full(a2), full(g2),
        ],
        out_specs=(row,) * 5,
        compiler_params=_cparams("parallel"),
        name="rwkv_gates",
    )(proj, hidden, vec, w2, a2, g2)


def _rwkv_post_body(y_ref, r_ref, v_ref, k2_ref, gate_ref, vec_ref, o_ref):
    ones_bd = _head_ones()
    inv_n = 1.0 / RWKV_HEAD_DIM
    for c in range(y_ref.shape[1] // LANE):
        sl = slice(c * LANE, (c + 1) * LANE)
        y = y_ref[:, sl]
        d = y - _head_sum(y, ones_bd) * inv_n
        yn = d * lax.rsqrt(_head_sum(d * d, ones_bd) * inv_n + RWKV_GN_EPS)
        bonus = _head_sum(r_ref[:, sl] * k2_ref[:, sl] * vec_ref[2:3, sl], ones_bd) * v_ref[:, sl]
        o_ref[:, sl] = ((yn * vec_ref[0:1, sl] + vec_ref[1:2, sl] + bonus) * gate_ref[:, sl]).astype(o_ref.dtype)


def rwkv_post(y, proj, k2, gate, vec, *, tm=256):
    T, D = y.shape
    row = pl.BlockSpec((tm, D), lambda i: (i, 0))
    return pl.pallas_call(
        _rwkv_post_body,
        out_shape=jax.ShapeDtypeStruct((T, D), BF16),
        grid=(T // tm,),
        in_specs=[row, row, pl.BlockSpec((tm, D), lambda i: (i, 2)), row, row,
                  pl.BlockSpec(vec.shape, lambda i: (0, 0))],
        out_specs=row,
        compiler_params=_cparams("parallel"),
        name="rwkv_post",
    )(y, proj, proj, k2, gate, vec)


def _nsa_sel_win_body(sel_ref, q_ref, ks_ref, vs_ref, kw_ref, vw_ref, osel_ref, owin_ref):
    blk, n_sel, W = NSA_SEL_BLOCK, NSA_N_SEL, NSA_WINDOW
    R, dh = NSA_HEADS // NSA_KV_GROUPS, NSA_HEAD_DIM
    S = q_ref.shape[1]
    n_blk = S // blk
    base = (pl.program_id(0) * NSA_KV_GROUPS + pl.program_id(1)) * (n_blk * n_sel)
    row_t = lax.broadcasted_iota(jnp.int32, (R * blk, 1), 0) & (blk - 1)
    col_s = lax.broadcasted_iota(jnp.int32, (1, n_sel * blk), 1)
    col_w = lax.broadcasted_iota(jnp.int32, (1, W + blk), 1)
    ones_s = jnp.ones((n_sel * blk, dh), BF16)
    ones_w = jnp.ones((W + blk, dh), BF16)

    def nt(a, b):
        return lax.dot_general(a, b, (((1,), (1,)), ((), ())), preferred_element_type=F32)

    def probs(s, mask):
        s = jnp.where(mask, s, NEG_INF)
        return jnp.exp(s - jnp.max(s, axis=-1, keepdims=True)).astype(BF16)

    def qblock(i, carry):
        q0 = pl.multiple_of(i * blk, blk)
        qb = q_ref[0, pl.ds(q0, blk), :]
        qs = jnp.concatenate([qb[:, r * dh:(r + 1) * dh] for r in range(R)], axis=0)
        qpos = q0 + row_t
        starts = [sel_ref[base + i * n_sel + j] * blk for j in range(n_sel)]
        w0 = pl.multiple_of(jnp.maximum(q0 - W, 0), blk)
        k_sel = jnp.concatenate([ks_ref[0, pl.ds(pl.multiple_of(st, blk), blk), :] for st in starts], axis=0)
        s_sel = nt(qs, k_sel)
        s_win = nt(qs, kw_ref[0, pl.ds(w0, W + blk), :])
        tok = col_s & (blk - 1)
        for j in range(n_sel):
            tok = tok + jnp.where((col_s >> (blk.bit_length() - 1)) == j, starts[j], 0)
        dist = qpos - (w0 + col_w)
        p_sel = probs(s_sel, tok <= qpos)
        p_win = probs(s_win, (dist >= 0) & (dist < W))
        v_sel = jnp.concatenate([vs_ref[0, pl.ds(pl.multiple_of(st, blk), blk), :] for st in starts], axis=0)
        o_sel = jnp.dot(p_sel, jnp.concatenate([v_sel, ones_s], axis=1), preferred_element_type=F32)
        o_win = jnp.dot(p_win, jnp.concatenate([vw_ref[0, pl.ds(w0, W + blk), :], ones_w], axis=1),
                        preferred_element_type=F32)
        o_sel = (o_sel[:, :dh] / o_sel[:, dh:]).astype(osel_ref.dtype)
        o_win = (o_win[:, :dh] / o_win[:, dh:]).astype(owin_ref.dtype)
        for r in range(R):
            osel_ref[0, pl.ds(q0, blk), r * dh:(r + 1) * dh] = o_sel[r * blk:(r + 1) * blk]
            owin_ref[0, pl.ds(q0, blk), r * dh:(r + 1) * dh] = o_win[r * blk:(r + 1) * blk]
        return carry

    lax.fori_loop(0, n_blk, qblock, 0)


def nsa_selected_window(sel, qkv):
    B, S = qkv.shape[:2]
    G, dh, HD = NSA_KV_GROUPS, NSA_HEAD_DIM, NSA_HEADS * NSA_HEAD_DIM
    qspec = pl.BlockSpec((1, S, HD // G), lambda b, g, sel: (b, 0, g))
    kspec = lambda c: pl.BlockSpec((1, S, dh), lambda b, g, sel: (b, 0, HD // dh + c * G + g))
    return pl.pallas_call(
        _nsa_sel_win_body,
        out_shape=(jax.ShapeDtypeStruct((B, S, HD), BF16), jax.ShapeDtypeStruct((B, S, HD), BF16)),
        grid_spec=pltpu.PrefetchScalarGridSpec(
            num_scalar_prefetch=1,
            grid=(B, G),
            in_specs=[qspec, kspec(2), kspec(3), kspec(4), kspec(5)],
            out_specs=(qspec, qspec),
        ),
        compiler_params=_cparams("parallel", "parallel"),
        name="nsa_selected_window",
    )(sel.reshape(-1), qkv, qkv, qkv, qkv, qkv)


def _nsa_rope_tables(S):
    half = NSA_ROPE_DIM // 2
    inv_freq = ROPE_THETA ** (-jnp.arange(half, dtype=F32) / half)
    ang = jnp.arange(S, dtype=F32)[:, None] * inv_freq
    cos, sin = jnp.cos(ang), jnp.sin(ang)
    rest = NSA_HEAD_DIM - NSA_ROPE_DIM
    cos_t = jnp.concatenate([cos, cos, jnp.ones((S, rest), F32)], axis=1)
    sin_t = jnp.concatenate([-sin, sin, jnp.zeros((S, rest), F32)], axis=1)
    return cos_t, sin_t


def _nsa_prep_body(x_ref, cos_ref, sin_ref, o_ref, kv_ref):
    dh, G = NSA_HEAD_DIM, NSA_KV_GROUPS
    half = NSA_ROPE_DIM // 2
    n_q = NSA_HEADS
    cos, sin = cos_ref[...], sin_ref[...]
    low = lax.broadcasted_iota(jnp.int32, cos.shape, 1) < half
    for blk in range(n_q + 6 * G):
        sl = slice(blk * dh, (blk + 1) * dh)
        x = x_ref[:, sl]
        c = (blk - n_q) // G
        if blk < n_q or c in (0, 2, 4):
            swapped = jnp.where(low, pltpu.roll(x, dh - half, axis=1), pltpu.roll(x, half, axis=1))
            x = x * cos + swapped * sin
        if blk < n_q:
            x = x * dh ** -0.5
        elif c in (0, 1):
            kv_ref[:, (blk - n_q) * dh:(blk - n_q + 1) * dh] = x
        o_ref[:, sl] = x.astype(o_ref.dtype)


def nsa_prepare(proj, S, *, tm=256):
    T = proj.shape[0]
    dh, G = NSA_HEAD_DIM, NSA_KV_GROUPS
    n = (NSA_HEADS + 6 * G) * dh
    cos_t, sin_t = _nsa_rope_tables(S)
    tab = pl.BlockSpec((tm, dh), lambda i: (i % (S // tm), 0))
    return pl.pallas_call(
        _nsa_prep_body,
        out_shape=(jax.ShapeDtypeStruct((T, n), BF16), jax.ShapeDtypeStruct((T, 2 * G * dh), F32)),
        grid=(T // tm,),
        in_specs=[pl.BlockSpec((tm, n), lambda i: (i, 0)), tab, tab],
        out_specs=(pl.BlockSpec((tm, n), lambda i: (i, 0)), pl.BlockSpec((tm, 2 * G * dh), lambda i: (i, 0))),
        compiler_params=_cparams("parallel"),
        name="nsa_prepare",
    )(proj, cos_t, sin_t)


def _nsa_compress_body(kc_ref, vc_ref, pe_ref, w1_ref, w2_ref, o_ref):
    L, stride = NSA_CMP_LEN, NSA_CMP_STRIDE
    n_grp = kc_ref.shape[1] // stride
    for c, ref in enumerate((kc_ref, vc_ref)):
        first = jnp.zeros((n_grp, NSA_CMP_HIDDEN), F32)
        second = jnp.zeros((n_grp, NSA_CMP_HIDDEN), F32)
        for l in range(stride):
            rows = ref[0, pl.ds(l, n_grp, stride=stride), :]
            first += jnp.dot((rows + pe_ref[c, l:l + 1, :]).astype(BF16), w1_ref[c, l],
                             preferred_element_type=F32)
            second += jnp.dot((rows + pe_ref[c, stride + l:stride + l + 1, :]).astype(BF16),
                              w1_ref[c, stride + l], preferred_element_type=F32)
        hid = jax.nn.gelu(first + pltpu.roll(second, n_grp - 1, axis=0))
        o_ref[0, 0, c] = jnp.dot(hid.astype(BF16), w2_ref[c], preferred_element_type=F32)


def nsa_compress(kcvc, pe, w1, w2):
    B, S = kcvc.shape[:2]
    G, dh = NSA_KV_GROUPS, NSA_HEAD_DIM
    full = lambda w: pl.BlockSpec(w.shape, lambda b, g: (0,) * w.ndim)
    return pl.pallas_call(
        _nsa_compress_body,
        out_shape=jax.ShapeDtypeStruct((B, G, 2, S // NSA_CMP_STRIDE, dh), F32),
        grid=(B, G),
        in_specs=[pl.BlockSpec((1, S, dh), lambda b, g: (b, 0, g)),
                  pl.BlockSpec((1, S, dh), lambda b, g: (b, 0, G + g)),
                  full(pe), full(w1), full(w2)],
        out_specs=pl.BlockSpec((1, 1, 2, S // NSA_CMP_STRIDE, dh), lambda b, g: (b, g, 0, 0, 0)),
        compiler_params=_cparams("parallel", "parallel"),
        name="nsa_compress",
    )(kcvc, kcvc, pe, w1, w2)


NSA_CMP_TILE = 512


def _nsa_cmp_body(q_ref, cmp_ref, o_ref, sel_ref):
    dh, blk = NSA_HEAD_DIM, NSA_SEL_BLOCK
    R = NSA_HEADS // NSA_KV_GROUPS
    tq = q_ref.shape[1]
    n_cmp = cmp_ref.shape[3]
    nq = tq // blk
    i = pl.program_id(2)
    k_cmp = cmp_ref[0, 0, 0].astype(BF16)
    v_cmp = cmp_ref[0, 0, 1].astype(BF16)
    pos = i * tq + lax.broadcasted_iota(jnp.int32, (tq, 1), 0)
    n_id = lax.broadcasted_iota(jnp.int32, (1, n_cmp), 1)
    visible = n_id * NSA_CMP_STRIDE + (NSA_CMP_LEN - 1) <= pos
    start = lax.broadcasted_iota(jnp.int32, (n_cmp, LANE), 0) * NSA_CMP_STRIDE
    bstart = lax.broadcasted_iota(jnp.int32, (n_cmp, LANE), 1) * blk
    overlap = jnp.where((start <= bstart + blk - 1) & (start + NSA_CMP_LEN - 1 >= bstart), 1.0, 0.0).astype(BF16)
    imp = jnp.zeros((tq, LANE), F32)
    for r in range(R):
        s = lax.dot_general(q_ref[0, :, r * dh:(r + 1) * dh], k_cmp, (((1,), (1,)), ((), ())),
                            preferred_element_type=F32)
        s = jnp.where(visible, s, NEG_INF)
        e = jnp.exp(s - jnp.max(s, axis=-1, keepdims=True))
        p = jnp.where(visible, e / jnp.sum(e, axis=-1, keepdims=True), 0.0).astype(BF16)
        o_ref[0, :, r * dh:(r + 1) * dh] = jnp.dot(p, v_cmp, preferred_element_type=F32).astype(o_ref.dtype)
        imp += jnp.dot(p, overlap, preferred_element_type=F32)
    imp = imp.reshape(nq, blk, LANE).sum(axis=1)
    qb = i * nq + lax.broadcasted_iota(jnp.int32, (nq, 1), 0)
    kb = lax.broadcasted_iota(jnp.int32, (nq, LANE), 1)
    forced = (kb == 0) | (kb == qb) | (kb == qb - 1)
    val = jnp.where(forced, jnp.inf, jnp.where(kb <= qb, imp, -jnp.inf))
    kb_f = kb.astype(F32)
    avail = kb >= 0
    picks = jnp.zeros((nq, LANE), F32)
    for t in range(NSA_N_SEL):
        best = jnp.max(jnp.where(avail, val, -jnp.inf), axis=-1, keepdims=True)
        pick = jnp.min(jnp.where(avail & (val == best), kb_f, float(LANE)), axis=-1, keepdims=True)
        picks = jnp.where(kb == t, pick, picks)
        avail = avail & (kb_f != pick)
    sel_ref[0, 0] = picks.astype(jnp.int32)


def nsa_compressed_attention(qkv, cmp):
    B, S = qkv.shape[:2]
    G, dh, HD, tq = NSA_KV_GROUPS, NSA_HEAD_DIM, NSA_HEADS * NSA_HEAD_DIM, NSA_CMP_TILE
    nq = tq // NSA_SEL_BLOCK
    qspec = pl.BlockSpec((1, tq, HD // G), lambda b, g, i: (b, i, g))
    return pl.pallas_call(
        _nsa_cmp_body,
        out_shape=(jax.ShapeDtypeStruct((B, S, HD), BF16),
                   jax.ShapeDtypeStruct((B, G, S // NSA_SEL_BLOCK, LANE), jnp.int32)),
        grid=(B, G, S // tq),
        in_specs=[qspec, pl.BlockSpec((1, 1) + cmp.shape[2:], lambda b, g, i: (b, g, 0, 0, 0))],
        out_specs=(qspec, pl.BlockSpec((1, 1, nq, LANE), lambda b, g, i: (b, g, i, 0))),
        compiler_params=_cparams("parallel", "parallel", "arbitrary"),
        name="nsa_compressed_attention",
    )(qkv, cmp)


def _nsa_combine_body(z_ref, oc_ref, os_ref, ow_ref, o_ref):
    dh = NSA_HEAD_DIM
    gates = jax.nn.sigmoid(z_ref[...])
    for hh in range(NSA_HEADS):
        sl = slice(hh * dh, (hh + 1) * dh)
        o_ref[:, sl] = (gates[:, 3 * hh:3 * hh + 1] * oc_ref[:, sl].astype(F32)
                        + gates[:, 3 * hh + 1:3 * hh + 2] * os_ref[:, sl].astype(F32)
                        + gates[:, 3 * hh + 2:3 * hh + 3] * ow_ref[:, sl].astype(F32)).astype(o_ref.dtype)


def nsa_combine(proj, o_cmp, o_sel, o_win, *, tm=256):
    T, HD = o_cmp.shape
    row = pl.BlockSpec((tm, HD), lambda i: (i, 0))
    gate_blk = (NSA_HEADS + 6 * NSA_KV_GROUPS) * NSA_HEAD_DIM // LANE
    return pl.pallas_call(
        _nsa_combine_body,
        out_shape=jax.ShapeDtypeStruct((T, HD), BF16),
        grid=(T // tm,),
        in_specs=[pl.BlockSpec((tm, LANE), lambda i: (i, gate_blk)), row, row, row],
        out_specs=row,
        compiler_params=_cparams("parallel"),
        name="nsa_combine",
    )(proj, o_cmp, o_sel, o_win)


FOX_TILE = 256
FOX_HEADS_PER_STEP = 4


def _fox_body(q_ref, k_ref, v_ref, cq_ref, ck_ref, o_ref):
    t, dh = FOX_TILE, FOX_HEAD_DIM
    nh = q_ref.shape[2] // dh
    heads = [slice(dh * j, dh * (j + 1)) for j in range(nh)]
    i = pl.program_id(2)
    qs = [q_ref[0, :, h] for h in heads]
    cqs = [cq_ref[0, j] for j in range(nh)]
    causal = (lax.broadcasted_iota(jnp.int32, (t, t), 0) >= lax.broadcasted_iota(jnp.int32, (t, t), 1))
    ones = jnp.ones((t, dh), BF16)

    def step(j, carry, diagonal):
        k0 = pl.multiple_of(j * t, t)
        ss = [lax.dot_general(q, k_ref[0, pl.ds(k0, t), h], (((1,), (1,)), ((), ())),
                              preferred_element_type=F32) + (cq - ck_ref[0, n, :, pl.ds(k0, t)])
              for n, (q, cq, h) in enumerate(zip(qs, cqs, heads))]
        if diagonal:
            ss = [jnp.where(causal, s, NEG_INF) for s in ss]
        m_new = [jnp.maximum(m, jnp.max(s, axis=-1, keepdims=True)) for (m, _), s in zip(carry, ss)]
        ps = [jnp.exp(s - m).astype(BF16) for s, m in zip(ss, m_new)]
        pv = [jnp.dot(p, jnp.concatenate([v_ref[0, pl.ds(k0, t), h], ones], axis=1),
                      preferred_element_type=F32) for p, h in zip(ps, heads)]
        return tuple((mn, jnp.exp(m - mn) * acc + x) for (m, acc), mn, x in zip(carry, m_new, pv))

    init = tuple((jnp.full((t, 1), NEG_INF, F32), jnp.zeros((t, 2 * dh), F32)) for _ in heads)
    carry = lax.fori_loop(0, i, lambda j, c: step(j, c, False), init)
    carry = step(i, carry, True)
    for (_, acc), h in zip(carry, heads):
        o_ref[0, :, h] = (acc[:, :dh] / acc[:, dh:]).astype(o_ref.dtype)


def fox_attention(proj, cum):
    B, S = proj.shape[:2]
    H, dh, t, nh = FOX_HEADS, FOX_HEAD_DIM, FOX_TILE, FOX_HEADS_PER_STEP
    G = H // nh
    return pl.pallas_call(
        _fox_body,
        out_shape=jax.ShapeDtypeStruct((B, S, H * dh), BF16),
        grid=(B, G, S // t),
        in_specs=[
            pl.BlockSpec((1, t, nh * dh), lambda b, h, i: (b, i, h)),
            pl.BlockSpec((1, S, nh * dh), lambda b, h, i: (b, 0, G + h)),
            pl.BlockSpec((1, S, nh * dh), lambda b, h, i: (b, 0, 2 * G + h)),
            pl.BlockSpec((1, nh, t, 1), lambda b, h, i: (b, h, i, 0)),
            pl.BlockSpec((1, nh, 1, S), lambda b, h, i: (b, h, 0, 0)),
        ],
        out_specs=pl.BlockSpec((1, t, nh * dh), lambda b, h, i: (b, i, h)),
        compiler_params=_cparams("parallel", "parallel", "arbitrary"),
        name="fox_attention",
    )(proj, proj, proj, cum[..., None], cum[:, :, None, :])


def _ret_body(lg_ref, q_ref, k_ref, v_ref, g_ref, cos_ref, sin_ref, gn_ref, o_ref, r_ref):
    C, dk = RET_CHUNK, RET_QK_DIM
    half = dk // 2
    lg = lg_ref[pl.program_id(1)]
    ii = lax.broadcasted_iota(jnp.int32, (C, C), 0)
    jj = lax.broadcasted_iota(jnp.int32, (C, C), 1)
    decay_mask = jnp.where(ii >= jj, jnp.exp((ii - jj).astype(F32) * lg), 0.0)
    ti = lax.broadcasted_iota(jnp.int32, (C, 1), 0).astype(F32)
    q_scale = jnp.exp((ti + 1.0) * lg)
    k_scale = jnp.exp((C - 1.0 - ti) * lg)
    chunk_decay = jnp.exp(jnp.full((1, 1), C, F32) * lg)
    r_ref[...] = jnp.zeros_like(r_ref)

    def rot(x, cos, sin):
        x1, x2 = x[:, :half], x[:, half:]
        return jnp.concatenate([x1 * cos - x2 * sin, x2 * cos + x1 * sin], axis=1)

    def chunk(c, carry):
        sl = pl.ds(pl.multiple_of(c * C, C), C)
        cos, sin = cos_ref[sl, :], sin_ref[sl, :]
        q = rot(q_ref[0, sl, :].astype(F32), cos, sin)
        k = rot(k_ref[0, sl, :].astype(F32), cos, sin) * (dk ** -0.5)
        v = v_ref[0, sl, :]
        qb = q.astype(BF16)
        inner = lax.dot_general(qb, k.astype(BF16), (((1,), (1,)), ((), ())),
                                preferred_element_type=F32) * decay_mask
        state = r_ref[...]
        o = (jnp.dot(inner.astype(BF16), v, preferred_element_type=F32)
             + jnp.dot(qb, state.astype(BF16), preferred_element_type=F32) * q_scale)
        r_ref[...] = state * chunk_decay + lax.dot_general(
            (k * k_scale).astype(BF16), v, (((0,), (0,)), ((), ())), preferred_element_type=F32)
        mu = jnp.mean(o, axis=-1, keepdims=True)
        d = o - mu
        on = d * lax.rsqrt(jnp.mean(d * d, axis=-1, keepdims=True) + RET_GN_EPS)
        g = g_ref[0, sl, :].astype(F32)
        o_ref[0, sl, :] = ((g * jax.nn.sigmoid(g)) * (on * gn_ref[...])).astype(o_ref.dtype)
        return carry

    lax.fori_loop(0, q_ref.shape[1] // C, chunk, 0)


def retention_core(proj, gn_g):
    B, S = proj.shape[:2]
    H, dk, dv = RET_HEADS, RET_QK_DIM, RET_V_DIM
    pos = jnp.arange(S, dtype=F32)
    inv_freq = RET_THETA ** (-jnp.arange(dk // 2, dtype=F32) / (dk // 2))
    ang = pos[:, None] * inv_freq
    log_gamma = jnp.log(1.0 - 2.0 ** (-5.0 - jnp.arange(H, dtype=F32)))
    tab = pl.BlockSpec((S, dk // 2), lambda b, h: (0, 0))
    return pl.pallas_call(
        _ret_body,
        out_shape=jax.ShapeDtypeStruct((B, S, H * dv), BF16),
        grid=(B, H),
        in_specs=[
            pl.BlockSpec(memory_space=pltpu.SMEM),
            pl.BlockSpec((1, S, dk), lambda b, h: (b, 0, h)),
            pl.BlockSpec((1, S, dk), lambda b, h: (b, 0, H + h)),
            pl.BlockSpec((1, S, dv), lambda b, h: (b, 0, 2 * H * dk // dv + h)),
            pl.BlockSpec((1, S, dv), lambda b, h: (b, 0, 2 * H * dk // dv + H + h)),
            tab, tab,
            pl.BlockSpec((1, dv), lambda b, h: (0, h)),
        ],
        out_specs=pl.BlockSpec((1, S, dv), lambda b, h: (b, 0, h)),
        scratch_shapes=[pltpu.VMEM((dk, dv), F32)],
        compiler_params=_cparams("parallel", "parallel"),
        name="retention_core",
    )(log_gamma, proj, proj, proj, proj, jnp.cos(ang), jnp.sin(ang), gn_g.reshape(1, H * dv))


def _head_norm(y, eps):
    y = y.astype(F32)
    mu = jnp.mean(y, axis=-1, keepdims=True)
    var = jnp.mean(jnp.square(y - mu), axis=-1, keepdims=True)
    return (y - mu) * lax.rsqrt(var + eps)


def _rotary(x, pos, theta, rot_dim):
    half = rot_dim // 2
    inv_freq = theta ** (-jnp.arange(half, dtype=F32) / half)
    ang = pos.astype(F32)[:, None] * inv_freq
    shape = (1, ang.shape[0]) + (1,) * (x.ndim - 3) + (half,)
    cos = jnp.cos(ang).reshape(shape)
    sin = jnp.sin(ang).reshape(shape)
    xr = x[..., :rot_dim].astype(F32)
    x1, x2 = xr[..., :half], xr[..., half:]
    rot = jnp.concatenate([x1 * cos - x2 * sin, x2 * cos + x1 * sin], axis=-1).astype(x.dtype)
    if rot_dim == x.shape[-1]:
        return rot
    return jnp.concatenate([rot, x[..., rot_dim:]], axis=-1)


def _pad_cols(w, n):
    return jnp.pad(w, ((0, 0), (0, n - w.shape[1])))


def _nsa_core(proj, cmp_pe, cmp_w1, cmp_w2):
    B, S = proj.shape[:2]
    H, G, dh = NSA_HEADS, NSA_KV_GROUPS, NSA_HEAD_DIM
    R = H // G
    L, stride, blk, W = NSA_CMP_LEN, NSA_CMP_STRIDE, NSA_SEL_BLOCK, NSA_WINDOW
    scale = dh ** -0.5
    pos = jnp.arange(S)
    hq, hk = H * dh, G * dh
    q = proj[..., :hq].reshape(B, S, G, R, dh)
    kv = proj[..., hq:hq + 6 * hk].reshape(B, S, 6, G, dh)
    gates = jax.nn.sigmoid(proj[..., hq + 6 * hk:hq + 6 * hk + 3 * H].reshape(B, S, G, R, 3))
    q = _rotary(q, pos, ROPE_THETA, NSA_ROPE_DIM)
    kc, ks, kw = (_rotary(kv[:, :, c], pos, ROPE_THETA, NSA_ROPE_DIM) for c in (0, 2, 4))
    vc, vs, vw = kv[:, :, 1], kv[:, :, 3], kv[:, :, 5]

    n_cmp = (S - L) // stride + 1
    idx_np = np.arange(n_cmp)[:, None] * stride + np.arange(L)[None, :]
    cmp_end = jnp.asarray(idx_np[:, -1])

    def compress(t, c):
        blocks = t[:, idx_np] + cmp_pe[c][None, None, :, None, :]
        hid = jax.nn.gelu(jnp.einsum('bnlgd,ldf->bngf', blocks, cmp_w1[c]))
        return jnp.einsum('bngf,fe->bnge', hid, cmp_w2[c])

    k_cmp, v_cmp = compress(kc, 0), compress(vc, 1)
    s_cmp = jnp.einsum('bsgrd,bngd->bgrsn', q, k_cmp).astype(F32) * scale
    m_cmp = cmp_end[None, :] <= pos[:, None]
    p_cmp = jax.nn.softmax(jnp.where(m_cmp, s_cmp, NEG_INF), axis=-1) * m_cmp
    o_cmp = jnp.einsum('bgrsn,bngd->bsgrd', p_cmp, v_cmp)

    n_blk = S // blk
    starts = np.arange(n_cmp) * stride
    bstart = np.arange(n_blk) * blk
    overlap = (starts[:, None] <= bstart[None, :] + blk - 1) & (starts[:, None] + L - 1 >= bstart[None, :])
    imp = jnp.einsum('bgrsn,nj->bgsj', p_cmp, jnp.asarray(overlap, F32))
    imp = imp.reshape(B, G, n_blk, blk, n_blk).sum(axis=3)
    qb_i = np.arange(n_blk)[:, None]
    kb_j = np.arange(n_blk)[None, :]
    valid = kb_j <= qb_i
    forced = (kb_j == 0) | (kb_j == qb_i) | (kb_j == qb_i - 1)
    imp = jnp.where(forced, jnp.inf, jnp.where(valid, imp, -jnp.inf))
    n_sel = min(NSA_N_SEL, n_blk)
    _, sel = lax.top_k(imp, n_sel)

    flat = lambda t: t.reshape(B, S, -1).astype(BF16)
    o_sel, o_win = nsa_selected_window(sel, flat(q * scale), flat(ks), flat(vs), flat(kw), flat(vw))
    o_sel = o_sel.reshape(B, S, G, R, dh)
    o_win = o_win.reshape(B, S, G, R, dh)
    o = gates[..., 0, None] * o_cmp + gates[..., 1, None] * o_sel + gates[..., 2, None] * o_win
    return o.reshape(B * S, H * dh)


NSA_IN_PADDED = 42 * LANE


def _nsa_mixer(h, g, w_in, cmp_pe, cmp_w1, cmp_w2, B, S):
    T = h.shape[0]
    proj = norm_matmul(h, g, _pad_cols(w_in, NSA_IN_PADDED).astype(BF16))
    qkv, kcvc = nsa_prepare(proj, S)
    cmp = nsa_compress(kcvc.reshape(B, S, -1), cmp_pe, cmp_w1.astype(BF16), cmp_w2.astype(BF16))
    qkv = qkv.reshape(B, S, -1)
    o_cmp, sel = nsa_compressed_attention(qkv, cmp)
    o_sel, o_win = nsa_selected_window(sel[..., :NSA_N_SEL], qkv)
    return nsa_combine(proj, o_cmp.reshape(T, -1), o_sel.reshape(T, -1), o_win.reshape(T, -1))


def _rwkv_mixer(h, g, mu, w_rkv, w0, w1, w2, a0, a1, a2, g1, g2, k_k, k_a, r_k, ln_gb, B, S):
    T, D = h.shape
    pad_c = lambda w: _pad_cols(w, LANE)
    pad_r = lambda w: jnp.pad(w, ((0, LANE - w.shape[0]), (0, 0)))
    w_main = jnp.concatenate([w_rkv[0], w_rkv[1], w_rkv[2]], axis=1).astype(BF16)
    w_hidden = jnp.concatenate([pad_c(w1), pad_c(a1), g1], axis=1).astype(BF16)
    proj, hidden = rwkv_mix_project(h, g, mu, w_main, w_hidden, S)
    lw, a, kk, k2, gate = rwkv_gates(proj, hidden, jnp.stack([w0, a0, k_k, k_a]), pad_r(w2).astype(BF16),
                                     pad_r(a2).astype(BF16), g2.astype(BF16))
    as3 = lambda t: t.reshape(B, S, -1)
    y = rwkv_recurrence(as3(proj), as3(lw), as3(k2), as3(kk), as3(a))
    vec = jnp.stack([ln_gb[0], ln_gb[1], r_k.reshape(D)])
    return rwkv_post(y.reshape(T, D), proj, k2, gate, vec)


def _fox_mixer(h, g, w_in, b_f, B, S):
    HD = FOX_HEADS * FOX_HEAD_DIM
    w_qkv = jnp.concatenate([w_in[:, :HD] * FOX_HEAD_DIM ** -0.5, w_in[:, HD:3 * HD]], axis=1)
    proj = norm_matmul(h, g, w_qkv.astype(BF16), out_dtype=BF16)
    z = norm_matmul(h, g, _pad_cols(w_in[:, 3 * HD:], LANE).astype(BF16))
    log_f = jax.nn.log_sigmoid(z[:, :FOX_HEADS].reshape(B, S, FOX_HEADS) + b_f)
    cum = jnp.transpose(jnp.cumsum(log_f, axis=1), (0, 2, 1))
    return fox_attention(proj.reshape(B, S, 3 * HD), cum).reshape(B * S, HD)


def kernel(x, p, norm_g, ffn_w_in, ffn_w_out, ple_w_proj, ple_w_gate, nsa_w_in, nsa_cmp_pe, nsa_cmp_w1, nsa_cmp_w2, nsa_w_out, rwkv_mu, rwkv_w_rkv, rwkv_w0, rwkv_w1, rwkv_w2, rwkv_a0, rwkv_a1, rwkv_a2, rwkv_g1, rwkv_g2, rwkv_k_k, rwkv_k_a, rwkv_r_k, rwkv_ln, rwkv_w_out, fox_w_in, fox_b_f, fox_w_out, ret_w_in, ret_gn_g, ret_w_out):
    B, S, D = x.shape
    T = B * S
    h = x.reshape(T, D)
    bf = lambda w: w.astype(BF16)
    for i in range(DEPTH):
        m, j = i % N_MIXERS, i // N_MIXERS
        ng = norm_g[i]
        h = ffn_half_step(h, ng[0], ng[1], bf(ffn_w_in[i, 0]), bf(ffn_w_out[i, 0]))
        if m == 0:
            y = _nsa_mixer(h, ng[2], nsa_w_in[j], nsa_cmp_pe[j], nsa_cmp_w1[j], nsa_cmp_w2[j], B, S)
            w_out = nsa_w_out[j]
        elif m == 1:
            y = _rwkv_mixer(h, ng[2], rwkv_mu[j], rwkv_w_rkv[j], rwkv_w0[j], rwkv_w1[j], rwkv_w2[j],
                            rwkv_a0[j], rwkv_a1[j], rwkv_a2[j], rwkv_g1[j], rwkv_g2[j],
                            rwkv_k_k[j], rwkv_k_a[j], rwkv_r_k[j], rwkv_ln[j], B, S)
            w_out = rwkv_w_out[j]
        elif m == 2:
            y = _fox_mixer(h, ng[2], fox_w_in[j], fox_b_f[j], B, S)
            w_out = fox_w_out[j]
        else:
            proj = norm_matmul(h, ng[2], bf(ret_w_in[j]), out_dtype=BF16)
            y = retention_core(proj.reshape(B, S, RET_IN), ret_gn_g[j]).reshape(T, -1)
            w_out = ret_w_out[j]
        h = matmul_norm_residual(y, bf(w_out), ng[3], h)
        h = ffn_half_step(h, ng[4], ng[5], bf(ffn_w_in[i, 1]), bf(ffn_w_out[i, 1]))
        h = ple_step(h, p[i].reshape(T, PLE_DIM), ng[6], ng[7], bf(ple_w_proj[i]), bf(ple_w_gate[i]))
    return h.reshape(B, S, D)
```

```python
import functools

import jax
import jax.numpy as jnp
import numpy as np
from jax import lax
from jax.experimental import pallas as pl
from jax.experimental.pallas import tpu as pltpu

D_MODEL = 2048
BATCH = 16
SEQ = 2048
DEPTH = 4
N_MIXERS = 4
PLE_DIM = 256
D_FF = 5632
RMS_EPS = 1e-6
NEG_INF = -1e30

NSA_HEADS = 16
NSA_KV_GROUPS = 4
NSA_HEAD_DIM = D_MODEL // NSA_HEADS
NSA_CMP_LEN = 32
NSA_CMP_STRIDE = 16
NSA_CMP_HIDDEN = 2 * NSA_HEAD_DIM
NSA_SEL_BLOCK = 64
NSA_N_SEL = 8
NSA_WINDOW = 512
NSA_ROPE_DIM = NSA_HEAD_DIM // 4
ROPE_THETA = 500000.0
NSA_IN = NSA_HEADS * NSA_HEAD_DIM + 6 * NSA_KV_GROUPS * NSA_HEAD_DIM + 3 * NSA_HEADS

RWKV_HEAD_DIM = 64
RWKV_HEADS = D_MODEL // RWKV_HEAD_DIM
RWKV_GN_EPS = 64e-5

FOX_HEADS = 16
FOX_HEAD_DIM = D_MODEL // FOX_HEADS
FOX_BLOCK = 128
FOX_IN = 3 * FOX_HEADS * FOX_HEAD_DIM + FOX_HEADS

RET_HEADS = 8
RET_QK_DIM = D_MODEL // RET_HEADS
RET_V_DIM = 2 * D_MODEL // RET_HEADS
RET_CHUNK = 128
RET_THETA = 10000.0
RET_GN_EPS = 1e-5
RET_IN = 2 * RET_HEADS * RET_QK_DIM + 2 * RET_HEADS * RET_V_DIM

V7X_VMEM_LIMIT_BYTES = 56 * 1024 * 1024
LANE = 128

F32 = jnp.float32
BF16 = jnp.bfloat16


def _cparams(*sem):
    return pltpu.CompilerParams(dimension_semantics=sem, vmem_limit_bytes=V7X_VMEM_LIMIT_BYTES)


def _rms(x, g):
    return x * lax.rsqrt(jnp.mean(x * x, axis=-1, keepdims=True) + RMS_EPS) * g


def _ffn_body(h_ref, g0_ref, g1_ref, wg_ref, wu_ref, wo_ref, o_ref, xn_ref, acc_ref):
    f = pl.program_id(1)

    @pl.when(f == 0)
    def _():
        xn_ref[...] = _rms(h_ref[...], g0_ref[...]).astype(BF16)
        acc_ref[...] = jnp.zeros_like(acc_ref)

    xn = xn_ref[...]
    gate = jnp.dot(xn, wg_ref[...], preferred_element_type=F32)
    up = jnp.dot(xn, wu_ref[...], preferred_element_type=F32)
    act = (gate * jax.nn.sigmoid(gate)) * up
    acc_ref[...] += jnp.dot(act.astype(BF16), wo_ref[...], preferred_element_type=F32)

    @pl.when(f == pl.num_programs(1) - 1)
    def _():
        o_ref[...] = h_ref[...] + 0.5 * _rms(acc_ref[...], g1_ref[...])


def ffn_half_step(h, g0, g1, w_in, w_out, layer, half, *, tm=512, tf=512):
    T, D = h.shape
    nf = D_FF // tf
    return pl.pallas_call(
        _ffn_body,
        out_shape=jax.ShapeDtypeStruct((T, D), F32),
        grid=(T // tm, nf),
        in_specs=[
            pl.BlockSpec((tm, D), lambda i, f: (i, 0)),
            pl.BlockSpec((1, D), lambda i, f: (0, 0)),
            pl.BlockSpec((1, D), lambda i, f: (0, 0)),
            pl.BlockSpec((None, None, D, tf), lambda i, f: (layer, half, 0, f)),
            pl.BlockSpec((None, None, D, tf), lambda i, f: (layer, half, 0, f + nf)),
            pl.BlockSpec((None, None, tf, D), lambda i, f: (layer, half, f, 0)),
        ],
        out_specs=pl.BlockSpec((tm, D), lambda i, f: (i, 0)),
        scratch_shapes=[pltpu.VMEM((tm, D), BF16), pltpu.VMEM((tm, D), F32)],
        compiler_params=_cparams("parallel", "arbitrary"),
        name="ffn_half_step",
    )(h, g0.reshape(1, D), g1.reshape(1, D), w_in, w_in, w_out)


def _norm_mm_body(x_ref, g_ref, w_ref, o_ref, xn_ref):
    @pl.when(pl.program_id(1) == 0)
    def _():
        xn_ref[...] = _rms(x_ref[...], g_ref[...]).astype(BF16)

    o_ref[...] = jnp.dot(xn_ref[...], w_ref[...], preferred_element_type=F32).astype(o_ref.dtype)


def _mm_body(x_ref, w_ref, o_ref):
    o_ref[...] = jnp.dot(x_ref[...].astype(BF16), w_ref[...],
                         preferred_element_type=F32).astype(o_ref.dtype)


def _col_tile(n, cap=1024):
    best = LANE
    for t in range(LANE, cap + 1, LANE):
        if n % t == 0:
            best = t
    return best


def norm_matmul(x, g, w, *, out_dtype=F32, tm=1024):
    T, K = x.shape
    N = w.shape[1]
    tn = _col_tile(N)
    return pl.pallas_call(
        _norm_mm_body,
        out_shape=jax.ShapeDtypeStruct((T, N), out_dtype),
        grid=(T // tm, N // tn),
        in_specs=[
            pl.BlockSpec((tm, K), lambda i, j: (i, 0)),
            pl.BlockSpec((1, K), lambda i, j: (0, 0)),
            pl.BlockSpec((K, tn), lambda i, j: (0, j)),
        ],
        out_specs=pl.BlockSpec((tm, tn), lambda i, j: (i, j)),
        scratch_shapes=[pltpu.VMEM((tm, K), BF16)],
        compiler_params=_cparams("parallel", "arbitrary"),
        name="norm_matmul",
    )(x, g.reshape(1, K), w)


def matmul(x, w, *, out_dtype=F32, tm=1024):
    T, K = x.shape
    N = w.shape[1]
    tn = _col_tile(N)
    return pl.pallas_call(
        _mm_body,
        out_shape=jax.ShapeDtypeStruct((T, N), out_dtype),
        grid=(T // tm, N // tn),
        in_specs=[
            pl.BlockSpec((tm, K), lambda i, j: (i, 0)),
            pl.BlockSpec((K, tn), lambda i, j: (0, j)),
        ],
        out_specs=pl.BlockSpec((tm, tn), lambda i, j: (i, j)),
        compiler_params=_cparams("parallel", "arbitrary"),
        name="matmul",
    )(x, w)


def _mm_res_body(y_ref, w_ref, g_ref, h_ref, o_ref, acc_ref):
    k = pl.program_id(1)

    @pl.when(k == 0)
    def _():
        acc_ref[...] = jnp.zeros_like(acc_ref)

    acc_ref[...] += jnp.dot(y_ref[...].astype(BF16), w_ref[...], preferred_element_type=F32)

    @pl.when(k == pl.num_programs(1) - 1)
    def _():
        o_ref[...] = h_ref[...] + _rms(acc_ref[...], g_ref[...])


def _mm_res_single_body(y_ref, w_ref, g_ref, h_ref, o_ref):
    acc = jnp.dot(y_ref[...].astype(BF16), w_ref[...], preferred_element_type=F32)
    o_ref[...] = h_ref[...] + _rms(acc, g_ref[...])


def matmul_norm_residual(y, w, g, h, *, tm=512, tk=2048):
    T, K = y.shape
    D = w.shape[1]
    if K == tk:
        return pl.pallas_call(
            _mm_res_single_body,
            out_shape=jax.ShapeDtypeStruct((T, D), F32),
            grid=(T // tm,),
            in_specs=[
                pl.BlockSpec((tm, K), lambda i: (i, 0)),
                pl.BlockSpec((K, D), lambda i: (0, 0)),
                pl.BlockSpec((1, D), lambda i: (0, 0)),
                pl.BlockSpec((tm, D), lambda i: (i, 0)),
            ],
            out_specs=pl.BlockSpec((tm, D), lambda i: (i, 0)),
            compiler_params=_cparams("parallel"),
            name="matmul_norm_residual",
        )(y, w, g.reshape(1, D), h)
    return pl.pallas_call(
        _mm_res_body,
        out_shape=jax.ShapeDtypeStruct((T, D), F32),
        grid=(T // tm, K // tk),
        in_specs=[
            pl.BlockSpec((tm, tk), lambda i, k: (i, k)),
            pl.BlockSpec((tk, D), lambda i, k: (k, 0)),
            pl.BlockSpec((1, D), lambda i, k: (0, 0)),
            pl.BlockSpec((tm, D), lambda i, k: (i, 0)),
        ],
        out_specs=pl.BlockSpec((tm, D), lambda i, k: (i, 0)),
        scratch_shapes=[pltpu.VMEM((tm, D), F32)],
        compiler_params=_cparams("parallel", "arbitrary"),
        name="matmul_norm_residual",
    )(y, w, g.reshape(1, D), h)


def _ple_body(h_ref, p_ref, g6_ref, g7_ref, wp_ref, wg_ref, o_ref):
    h = h_ref[...]
    xn = _rms(h, g6_ref[...]).astype(BF16)
    z = jnp.dot(xn, wg_ref[...], preferred_element_type=F32)
    e = jnp.dot(p_ref[...].astype(BF16), wp_ref[...], preferred_element_type=F32)
    o_ref[...] = h + _rms(e * jax.nn.sigmoid(z), g7_ref[...])


def ple_step(h, p, g6, g7, wp, wg, layer, *, tm=256):
    T, D = h.shape
    P = p.shape[2]
    return pl.pallas_call(
        _ple_body,
        out_shape=jax.ShapeDtypeStruct((T, D), F32),
        grid=(T // tm,),
        in_specs=[
            pl.BlockSpec((tm, D), lambda i: (i, 0)),
            pl.BlockSpec((None, tm, P), lambda i: (layer, i, 0)),
            pl.BlockSpec((1, D), lambda i: (0, 0)),
            pl.BlockSpec((1, D), lambda i: (0, 0)),
            pl.BlockSpec((None, P, D), lambda i: (layer, 0, 0)),
            pl.BlockSpec((None, D, D), lambda i: (layer, 0, 0)),
        ],
        out_specs=pl.BlockSpec((tm, D), lambda i: (i, 0)),
        compiler_params=_cparams("parallel"),
        name="ple_step",
    )(h, p, g6.reshape(1, D), g7.reshape(1, D), wp, wg)


RWKV_CHUNK = 64
RWKV_PAIRS_PER_STEP = 16
RWKV_TIME_BLOCK = 256


def _rwkv_body(r_ref, lw_ref, k_ref, v_ref, kk_ref, a_ref, y_ref, s_ref):
    C = RWKV_CHUNK
    N = RWKV_HEAD_DIM
    lane = lax.broadcasted_iota(jnp.int32, (C, 2 * N), 1)
    row = lax.broadcasted_iota(jnp.int32, (C, 2 * N), 0)
    first_head = lane < N
    ri = lax.broadcasted_iota(jnp.int32, (2 * C, 2 * C), 0)
    ci = lax.broadcasted_iota(jnp.int32, (2 * C, 2 * C), 1)
    strict = ri > ci
    incl = ri >= ci
    eye = jnp.where(ri == ci, 1.0, 0.0).astype(F32)
    corner = [((ri >> (lvl + 1)) == (ci >> (lvl + 1))) & ((ri & (1 << lvl)) != 0) & ((ci & (1 << lvl)) == 0)
              for lvl in range(C.bit_length() - 1)]

    def stack(x):
        return jnp.concatenate([jnp.where(first_head, x, 0.0), jnp.where(first_head, 0.0, x)], axis=0)

    def nt(a, b):
        return lax.dot_general(a.astype(BF16), b.astype(BF16), (((1,), (1,)), ((), ())),
                               preferred_element_type=F32)

    def nn(a, b):
        return jnp.dot(a.astype(BF16), b.astype(BF16), preferred_element_type=F32)

    def tn(a, b):
        return lax.dot_general(a.astype(BF16), b.astype(BF16), (((0,), (0,)), ((), ())),
                               preferred_element_type=F32)

    n_pairs = r_ref.shape[2] // (2 * N)
    lanes = [slice(2 * N * j, 2 * N * (j + 1)) for j in range(n_pairs)]

    def prep(sl, ln):
        r, lw, k, v, kk, a = (ref[0, sl, ln] for ref in (r_ref, lw_ref, k_ref, v_ref, kk_ref, a_ref))
        cl = lw
        for sh in (1, 2, 4, 8, 16, 32):
            cl = cl + jnp.where(row >= sh, pltpu.roll(cl, sh, axis=0), 0.0)
        mid = cl[C // 2 - 1:C // 2, :]
        last = cl[C - 1:C, :]
        e_neg = jnp.exp(mid - cl)
        e_end = jnp.exp(last - mid)
        b_til = stack(kk * a * e_neg)
        k_til = stack(k * e_neg)
        return dict(
            a_bar=stack(-kk * jnp.exp(cl - lw - mid)), r_bar=stack(r * jnp.exp(cl - mid)),
            b_til=b_til, k_til=k_til, v_st=stack(v), e_mid=jnp.exp(mid), w_tot=jnp.exp(last),
            bk_end=jnp.concatenate([b_til * e_end, k_til * e_end], axis=0))

    def chunk(c, states):
        sl = pl.ds(pl.multiple_of(c * C, C), C)
        ps = [prep(sl, ln) for ln in lanes]
        gs = [nt(jnp.concatenate([p["a_bar"], p["r_bar"]], axis=0),
                 jnp.concatenate([p["b_til"], p["k_til"]], axis=0)) for p in ps]
        a_ab = [jnp.where(strict, g[:2 * C, :2 * C], 0.0) for g in gs]
        a_ak = [jnp.where(strict, g[:2 * C, 2 * C:], 0.0) for g in gs]
        a_rb = [jnp.where(incl, g[2 * C:, :2 * C], 0.0) for g in gs]
        a_rk = [jnp.where(incl, g[2 * C:, 2 * C:], 0.0) for g in gs]
        inv = [eye + jnp.where(corner[0], x, 0.0) for x in a_ab]
        for lvl in range(1, len(corner)):
            tmp = [nn(jnp.where(corner[lvl], x, 0.0), t) for x, t in zip(a_ab, inv)]
            inv = [t + nn(t, x) for t, x in zip(inv, tmp)]
        s_mid = [st * p["e_mid"] for st, p in zip(states, ps)]
        rhs = [nt(p["a_bar"], sm) + nn(ak, p["v_st"]) for p, sm, ak in zip(ps, s_mid, a_ak)]
        us = [nn(t, x) for t, x in zip(inv, rhs)]
        ys = [nt(p["r_bar"], sm) + nn(rb, u) + nn(rk, p["v_st"])
              for p, sm, rb, rk, u in zip(ps, s_mid, a_rb, a_rk, us)]
        for ln, y in zip(lanes, ys):
            y_ref[0, sl, ln] = y[:C] + y[C:]
        return tuple(st * p["w_tot"] + tn(jnp.concatenate([u, p["v_st"]], axis=0), p["bk_end"])
                     for st, p, u in zip(states, ps, us))

    @pl.when(pl.program_id(2) == 0)
    def _():
        s_ref[...] = jnp.zeros_like(s_ref)

    states = lax.fori_loop(0, r_ref.shape[1] // C, chunk, tuple(s_ref[j] for j in range(n_pairs)))
    for j, st in enumerate(states):
        s_ref[j] = st


def rwkv_recurrence(proj, lw, k, kk, a):
    B, S, D = lw.shape
    lanes = 2 * RWKV_HEAD_DIM * RWKV_PAIRS_PER_STEP
    ts = min(S, RWKV_TIME_BLOCK)
    spec = pl.BlockSpec((1, ts, lanes), lambda b, j, t: (b, t, j))
    v_spec = pl.BlockSpec((1, ts, lanes), lambda b, j, t: (b, t, 2 * D // lanes + j))
    return pl.pallas_call(
        _rwkv_body,
        out_shape=jax.ShapeDtypeStruct((B, S, D), F32),
        grid=(B, D // lanes, S // ts),
        in_specs=[spec, spec, spec, v_spec, spec, spec],
        out_specs=spec,
        scratch_shapes=[pltpu.VMEM((RWKV_PAIRS_PER_STEP, 2 * RWKV_HEAD_DIM, 2 * RWKV_HEAD_DIM), F32)],
        compiler_params=_cparams("parallel", "parallel", "arbitrary"),
        name="rwkv_recurrence",
    )(proj, lw, k, proj, kk, a)


RWKV_PROJ_TN = 1024
RWKV_HIDDEN = 4 * LANE


def _rwkv_shift(first_of_seq, h_ref, hp_ref, g_ref):
    u = _rms(h_ref[...], g_ref[...])
    prev = jnp.where(first_of_seq, 0.0, _rms(hp_ref[7:8, :], g_ref[...]))
    row = lax.broadcasted_iota(jnp.int32, u.shape, 0)
    return u, jnp.where(row == 0, prev, pltpu.roll(u, 1, axis=0)) - u


def _rwkv_proj_body(tiles_per_seq, h_ref, hp_ref, g_ref, mu_ref, w_ref, o_ref, xm_ref):
    i, j = pl.program_id(0), pl.program_id(1)

    @pl.when(j == 0)
    def _():
        u, xx = _rwkv_shift(lax.rem(i, tiles_per_seq) == 0, h_ref, hp_ref, g_ref)
        for c in range(3):
            xm_ref[c] = (u + xx * mu_ref[c:c + 1, :]).astype(BF16)

    o_ref[...] = jnp.dot(xm_ref[j // (pl.num_programs(1) // 3)], w_ref[...], preferred_element_type=F32)


def _rwkv_hidden_body(tiles_per_seq, h_ref, hp_ref, g_ref, mu_ref, w_ref, o_ref):
    u, xx = _rwkv_shift(lax.rem(pl.program_id(0), tiles_per_seq) == 0, h_ref, hp_ref, g_ref)
    mix = lambda c: (u + xx * mu_ref[c:c + 1, :]).astype(BF16)
    o_ref[:, :LANE] = jnp.tanh(jnp.dot(mix(3), w_ref[:, :LANE], preferred_element_type=F32))
    o_ref[:, LANE:2 * LANE] = jnp.dot(mix(4), w_ref[:, LANE:2 * LANE], preferred_element_type=F32)
    o_ref[:, 2 * LANE:] = jax.nn.sigmoid(jnp.dot(mix(5), w_ref[:, 2 * LANE:], preferred_element_type=F32))


def rwkv_mix_project(h, g, mu, w_rkv, w_hidden, seq_len, *, tm=512):
    T, D = h.shape
    tn = RWKV_PROJ_TN
    prev_rows = lambda i, *_: (jnp.maximum(i * (tm // 8) - 1, 0), 0)
    rkv = pl.pallas_call(
        functools.partial(_rwkv_proj_body, seq_len // tm),
        out_shape=jax.ShapeDtypeStruct((T, 3 * D), F32),
        grid=(T // tm, 3 * D // tn),
        in_specs=[
            pl.BlockSpec((tm, D), lambda i, j: (i, 0)),
            pl.BlockSpec((8, D), prev_rows),
            pl.BlockSpec((1, D), lambda i, j: (0, 0)),
            pl.BlockSpec((6, D), lambda i, j: (0, 0)),
            pl.BlockSpec((D, tn), lambda i, j: (0, j)),
        ],
        out_specs=pl.BlockSpec((tm, tn), lambda i, j: (i, j)),
        scratch_shapes=[pltpu.VMEM((3, tm, D), BF16)],
        compiler_params=_cparams("parallel", "arbitrary"),
        name="rwkv_mix_project",
    )(h, h, g.reshape(1, D), mu, w_rkv)
    hidden = pl.pallas_call(
        functools.partial(_rwkv_hidden_body, seq_len // tm),
        out_shape=jax.ShapeDtypeStruct((T, RWKV_HIDDEN), F32),
        grid=(T // tm,),
        in_specs=[
            pl.BlockSpec((tm, D), lambda i: (i, 0)),
            pl.BlockSpec((8, D), prev_rows),
            pl.BlockSpec((1, D), lambda i: (0, 0)),
            pl.BlockSpec((6, D), lambda i: (0, 0)),
            pl.BlockSpec((D, RWKV_HIDDEN), lambda i: (0, 0)),
        ],
        out_specs=pl.BlockSpec((tm, RWKV_HIDDEN), lambda i: (i, 0)),
        compiler_params=_cparams("parallel"),
        name="rwkv_mix_hidden",
    )(h, h, g.reshape(1, D), mu, w_hidden)
    return rkv, hidden


def _head_sum(x, ones_bd):
    hi = x.astype(BF16)
    lo = (x - hi.astype(F32)).astype(BF16)
    return (jnp.dot(hi, ones_bd, preferred_element_type=F32) + jnp.dot(lo, ones_bd, preferred_element_type=F32))


def _head_ones():
    shift = RWKV_HEAD_DIM.bit_length() - 1
    r = lax.broadcasted_iota(jnp.int32, (LANE, LANE), 0) >> shift
    c = lax.broadcasted_iota(jnp.int32, (LANE, LANE), 1) >> shift
    return jnp.where(r == c, 1.0, 0.0).astype(BF16)


def _rwkv_gates_body(k_ref, hid_ref, vec_ref, w2_ref, a2_ref, g2_ref, lw_ref, a_ref, kk_ref, k2_ref, gate_ref):
    hid = hid_ref[...].astype(BF16)
    z = vec_ref[0:1, :] + jnp.dot(hid[:, :LANE], w2_ref[...], preferred_element_type=F32)
    softplus = jnp.maximum(-z, 0.0) + jnp.log(1.0 + jnp.exp(-jnp.abs(z)))
    lw_ref[...] = -jnp.exp(-softplus - 0.5)
    a = jax.nn.sigmoid(vec_ref[1:2, :] + jnp.dot(hid[:, LANE:2 * LANE], a2_ref[...], preferred_element_type=F32))
    a_ref[...] = a
    gate_ref[...] = jnp.dot(hid[:, 2 * LANE:], g2_ref[...], preferred_element_type=F32)
    k = k_ref[...]
    k2_ref[...] = k * (1.0 + (a - 1.0) * vec_ref[3:4, :])
    kk = k * vec_ref[2:3, :]
    ones_bd = _head_ones()
    for c in range(k.shape[1] // LANE):
        blk = kk[:, c * LANE:(c + 1) * LANE]
        norm = jnp.sqrt(_head_sum(blk * blk, ones_bd))
        kk_ref[:, c * LANE:(c + 1) * LANE] = blk / jnp.maximum(norm, 1e-12)


def rwkv_gates(proj, hidden, vec, w2, a2, g2, *, tm=256):
    T = proj.shape[0]
    D = w2.shape[1]
    row = pl.BlockSpec((tm, D), lambda i: (i, 0))
    full = lambda w: pl.BlockSpec(w.shape, lambda i: (0, 0))
    return pl.pallas_call(
        _rwkv_gates_body,
        out_shape=tuple(jax.ShapeDtypeStruct((T, D), F32) for _ in range(5)),
        grid=(T // tm,),
        in_specs=[
            pl.BlockSpec((tm, D), lambda i: (i, 1)),
            pl.BlockSpec((tm, RWKV_HIDDEN), lambda i: (i, 0)),
            full(vec), full(w2), full(a2), full(g2),
        ],
        out_specs=(row,) * 5,
        compiler_params=_cparams("parallel"),
        name="rwkv_gates",
    )(proj, hidden, vec, w2, a2, g2)


def _rwkv_post_body(y_ref, r_ref, v_ref, k2_ref, gate_ref, vec_ref, o_ref):
    ones_bd = _head_ones()
    inv_n = 1.0 / RWKV_HEAD_DIM
    for c in range(y_ref.shape[1] // LANE):
        sl = slice(c * LANE, (c + 1) * LANE)
        y = y_ref[:, sl]
        d = y - _head_sum(y, ones_bd) * inv_n
        yn = d * lax.rsqrt(_head_sum(d * d, ones_bd) * inv_n + RWKV_GN_EPS)
        bonus = _head_sum(r_ref[:, sl] * k2_ref[:, sl] * vec_ref[2:3, sl], ones_bd) * v_ref[:, sl]
        o_ref[:, sl] = ((yn * vec_ref[0:1, sl] + vec_ref[1:2, sl] + bonus) * gate_ref[:, sl]).astype(o_ref.dtype)


def rwkv_post(y, proj, k2, gate, vec, *, tm=256):
    T, D = y.shape
    row = pl.BlockSpec((tm, D), lambda i: (i, 0))
    return pl.pallas_call(
        _rwkv_post_body,
        out_shape=jax.ShapeDtypeStruct((T, D), BF16),
        grid=(T // tm,),
        in_specs=[row, row, pl.BlockSpec((tm, D), lambda i: (i, 2)), row, row,
                  pl.BlockSpec(vec.shape, lambda i: (0, 0))],
        out_specs=row,
        compiler_params=_cparams("parallel"),
        name="rwkv_post",
    )(y, proj, proj, k2, gate, vec)


def _nsa_sel_win_body(sel_ref, q_ref, ks_ref, vs_ref, kw_ref, vw_ref, osel_ref, owin_ref):
    blk, n_sel, W = NSA_SEL_BLOCK, NSA_N_SEL, NSA_WINDOW
    R, dh = NSA_HEADS // NSA_KV_GROUPS, NSA_HEAD_DIM
    S = q_ref.shape[1]
    n_blk = S // blk
    base = (pl.program_id(0) * NSA_KV_GROUPS + pl.program_id(1)) * (n_blk * n_sel)
    row_t = lax.broadcasted_iota(jnp.int32, (R * blk, 1), 0) & (blk - 1)
    col_s = lax.broadcasted_iota(jnp.int32, (1, n_sel * blk), 1)
    col_w = lax.broadcasted_iota(jnp.int32, (1, W + blk), 1)
    ones_s = jnp.ones((n_sel * blk, dh), BF16)
    ones_w = jnp.ones((W + blk, dh), BF16)

    def nt(a, b):
        return lax.dot_general(a, b, (((1,), (1,)), ((), ())), preferred_element_type=F32)

    def probs(s, mask):
        s = jnp.where(mask, s, NEG_INF)
        return jnp.exp(s - jnp.max(s, axis=-1, keepdims=True)).astype(BF16)

    def qblock(i, carry):
        q0 = pl.multiple_of(i * blk, blk)
        qb = q_ref[0, pl.ds(q0, blk), :]
        qs = jnp.concatenate([qb[:, r * dh:(r + 1) * dh] for r in range(R)], axis=0)
        qpos = q0 + row_t
        starts = [sel_ref[base + i * n_sel + j] * blk for j in range(n_sel)]
        w0 = pl.multiple_of(jnp.maximum(q0 - W, 0), blk)
        k_sel = jnp.concatenate([ks_ref[0, pl.ds(pl.multiple_of(st, blk), blk), :] for st in starts], axis=0)
        s_sel = nt(qs, k_sel)
        s_win = nt(qs, kw_ref[0, pl.ds(w0, W + blk), :])
        tok = col_s & (blk - 1)
        for j in range(n_sel):
            tok = tok + jnp.where((col_s >> (blk.bit_length() - 1)) == j, starts[j], 0)
        dist = qpos - (w0 + col_w)
        p_sel = probs(s_sel, tok <= qpos)
        p_win = probs(s_win, (dist >= 0) & (dist < W))
        v_sel = jnp.concatenate([vs_ref[0, pl.ds(pl.multiple_of(st, blk), blk), :] for st in starts], axis=0)
        o_sel = jnp.dot(p_sel, jnp.concatenate([v_sel, ones_s], axis=1), preferred_element_type=F32)
        o_win = jnp.dot(p_win, jnp.concatenate([vw_ref[0, pl.ds(w0, W + blk), :], ones_w], axis=1),
                        preferred_element_type=F32)
        o_sel = (o_sel[:, :dh] / o_sel[:, dh:]).astype(osel_ref.dtype)
        o_win = (o_win[:, :dh] / o_win[:, dh:]).astype(owin_ref.dtype)
        for r in range(R):
            osel_ref[0, pl.ds(q0, blk), r * dh:(r + 1) * dh] = o_sel[r * blk:(r + 1) * blk]
            owin_ref[0, pl.ds(q0, blk), r * dh:(r + 1) * dh] = o_win[r * blk:(r + 1) * blk]
        return carry

    lax.fori_loop(0, n_blk, qblock, 0)


def nsa_selected_window(sel, qkv):
    B, S = qkv.shape[:2]
    G, dh, HD = NSA_KV_GROUPS, NSA_HEAD_DIM, NSA_HEADS * NSA_HEAD_DIM
    qspec = pl.BlockSpec((1, S, HD // G), lambda b, g, sel: (b, 0, g))
    kspec = lambda c: pl.BlockSpec((1, S, dh), lambda b, g, sel: (b, 0, HD // dh + c * G + g))
    return pl.pallas_call(
        _nsa_sel_win_body,
        out_shape=(jax.ShapeDtypeStruct((B, S, HD), BF16), jax.ShapeDtypeStruct((B, S, HD), BF16)),
        grid_spec=pltpu.PrefetchScalarGridSpec(
            num_scalar_prefetch=1,
            grid=(B, G),
            in_specs=[qspec, kspec(2), kspec(3), kspec(4), kspec(5)],
            out_specs=(qspec, qspec),
        ),
        compiler_params=_cparams("parallel", "parallel"),
        name="nsa_selected_window",
    )(sel.reshape(-1), qkv, qkv, qkv, qkv, qkv)


def _nsa_rope_tables(S):
    half = NSA_ROPE_DIM // 2
    inv_freq = ROPE_THETA ** (-jnp.arange(half, dtype=F32) / half)
    ang = jnp.arange(S, dtype=F32)[:, None] * inv_freq
    cos, sin = jnp.cos(ang), jnp.sin(ang)
    rest = NSA_HEAD_DIM - NSA_ROPE_DIM
    cos_t = jnp.concatenate([cos, cos, jnp.ones((S, rest), F32)], axis=1)
    sin_t = jnp.concatenate([-sin, sin, jnp.zeros((S, rest), F32)], axis=1)
    return cos_t, sin_t


def _nsa_prep_body(x_ref, cos_ref, sin_ref, o_ref, kv_ref):
    dh, G = NSA_HEAD_DIM, NSA_KV_GROUPS
    half = NSA_ROPE_DIM // 2
    n_q = NSA_HEADS
    cos, sin = cos_ref[...], sin_ref[...]
    low = lax.broadcasted_iota(jnp.int32, cos.shape, 1) < half
    for blk in range(n_q + 6 * G):
        sl = slice(blk * dh, (blk + 1) * dh)
        x = x_ref[:, sl]
        c = (blk - n_q) // G
        if blk < n_q or c in (0, 2, 4):
            swapped = jnp.where(low, pltpu.roll(x, dh - half, axis=1), pltpu.roll(x, half, axis=1))
            x = x * cos + swapped * sin
        if blk < n_q:
            x = x * dh ** -0.5
        elif c in (0, 1):
            kv_ref[:, (blk - n_q) * dh:(blk - n_q + 1) * dh] = x
        o_ref[:, sl] = x.astype(o_ref.dtype)


def nsa_prepare(proj, S, *, tm=256):
    T = proj.shape[0]
    dh, G = NSA_HEAD_DIM, NSA_KV_GROUPS
    n = (NSA_HEADS + 6 * G) * dh
    cos_t, sin_t = _nsa_rope_tables(S)
    tab = pl.BlockSpec((tm, dh), lambda i: (i % (S // tm), 0))
    return pl.pallas_call(
        _nsa_prep_body,
        out_shape=(jax.ShapeDtypeStruct((T, n), BF16), jax.ShapeDtypeStruct((T, 2 * G * dh), F32)),
        grid=(T // tm,),
        in_specs=[pl.BlockSpec((tm, n), lambda i: (i, 0)), tab, tab],
        out_specs=(pl.BlockSpec((tm, n), lambda i: (i, 0)), pl.BlockSpec((tm, 2 * G * dh), lambda i: (i, 0))),
        compiler_params=_cparams("parallel"),
        name="nsa_prepare",
    )(proj, cos_t, sin_t)


def _nsa_compress_body(kc_ref, vc_ref, pe_ref, w1_ref, w2_ref, o_ref):
    L, stride = NSA_CMP_LEN, NSA_CMP_STRIDE
    n_grp = kc_ref.shape[1] // stride
    for c, ref in enumerate((kc_ref, vc_ref)):
        first = jnp.zeros((n_grp, NSA_CMP_HIDDEN), F32)
        second = jnp.zeros((n_grp, NSA_CMP_HIDDEN), F32)
        for l in range(stride):
            rows = ref[0, pl.ds(l, n_grp, stride=stride), :]
            first += jnp.dot((rows + pe_ref[c, l:l + 1, :]).astype(BF16), w1_ref[c, l],
                             preferred_element_type=F32)
            second += jnp.dot((rows + pe_ref[c, stride + l:stride + l + 1, :]).astype(BF16),
                              w1_ref[c, stride + l], preferred_element_type=F32)
        hid = jax.nn.gelu(first + pltpu.roll(second, n_grp - 1, axis=0))
        o_ref[0, 0, c] = jnp.dot(hid.astype(BF16), w2_ref[c], preferred_element_type=F32)


def nsa_compress(kcvc, pe, w1, w2):
    B, S = kcvc.shape[:2]
    G, dh = NSA_KV_GROUPS, NSA_HEAD_DIM
    full = lambda w: pl.BlockSpec(w.shape, lambda b, g: (0,) * w.ndim)
    return pl.pallas_call(
        _nsa_compress_body,
        out_shape=jax.ShapeDtypeStruct((B, G, 2, S // NSA_CMP_STRIDE, dh), F32),
        grid=(B, G),
        in_specs=[pl.BlockSpec((1, S, dh), lambda b, g: (b, 0, g)),
                  pl.BlockSpec((1, S, dh), lambda b, g: (b, 0, G + g)),
                  full(pe), full(w1), full(w2)],
        out_specs=pl.BlockSpec((1, 1, 2, S // NSA_CMP_STRIDE, dh), lambda b, g: (b, g, 0, 0, 0)),
        compiler_params=_cparams("parallel", "parallel"),
        name="nsa_compress",
    )(kcvc, kcvc, pe, w1, w2)


NSA_CMP_TILE = 512


def _nsa_cmp_body(q_ref, cmp_ref, o_ref, sel_ref):
    dh, blk = NSA_HEAD_DIM, NSA_SEL_BLOCK
    R = NSA_HEADS // NSA_KV_GROUPS
    tq = q_ref.shape[1]
    n_cmp = cmp_ref.shape[3]
    nq = tq // blk
    i = pl.program_id(2)
    k_cmp = cmp_ref[0, 0, 0].astype(BF16)
    v_cmp = cmp_ref[0, 0, 1].astype(BF16)
    pos = i * tq + lax.broadcasted_iota(jnp.int32, (tq, 1), 0)
    n_id = lax.broadcasted_iota(jnp.int32, (1, n_cmp), 1)
    visible = n_id * NSA_CMP_STRIDE + (NSA_CMP_LEN - 1) <= pos
    start = lax.broadcasted_iota(jnp.int32, (n_cmp, LANE), 0) * NSA_CMP_STRIDE
    bstart = lax.broadcasted_iota(jnp.int32, (n_cmp, LANE), 1) * blk
    overlap = jnp.where((start <= bstart + blk - 1) & (start + NSA_CMP_LEN - 1 >= bstart), 1.0, 0.0).astype(BF16)
    imp = jnp.zeros((tq, LANE), F32)
    for r in range(R):
        s = lax.dot_general(q_ref[0, :, r * dh:(r + 1) * dh], k_cmp, (((1,), (1,)), ((), ())),
                            preferred_element_type=F32)
        s = jnp.where(visible, s, NEG_INF)
        e = jnp.exp(s - jnp.max(s, axis=-1, keepdims=True))
        p = jnp.where(visible, e / jnp.sum(e, axis=-1, keepdims=True), 0.0).astype(BF16)
        o_ref[0, :, r * dh:(r + 1) * dh] = jnp.dot(p, v_cmp, preferred_element_type=F32).astype(o_ref.dtype)
        imp += jnp.dot(p, overlap, preferred_element_type=F32)
    imp = imp.reshape(nq, blk, LANE).sum(axis=1)
    qb = i * nq + lax.broadcasted_iota(jnp.int32, (nq, 1), 0)
    kb = lax.broadcasted_iota(jnp.int32, (nq, LANE), 1)
    forced = (kb == 0) | (kb == qb) | (kb == qb - 1)
    val = jnp.where(forced, jnp.inf, jnp.where(kb <= qb, imp, -jnp.inf))
    kb_f = kb.astype(F32)
    avail = kb >= 0
    picks = jnp.zeros((nq, LANE), F32)
    for t in range(NSA_N_SEL):
        best = jnp.max(jnp.where(avail, val, -jnp.inf), axis=-1, keepdims=True)
        pick = jnp.min(jnp.where(avail & (val == best), kb_f, float(LANE)), axis=-1, keepdims=True)
        picks = jnp.where(kb == t, pick, picks)
        avail = avail & (kb_f != pick)
    sel_ref[0, 0] = picks.astype(jnp.int32)


def nsa_compressed_attention(qkv, cmp):
    B, S = qkv.shape[:2]
    G, dh, HD, tq = NSA_KV_GROUPS, NSA_HEAD_DIM, NSA_HEADS * NSA_HEAD_DIM, NSA_CMP_TILE
    nq = tq // NSA_SEL_BLOCK
    qspec = pl.BlockSpec((1, tq, HD // G), lambda b, g, i: (b, i, g))
    return pl.pallas_call(
        _nsa_cmp_body,
        out_shape=(jax.ShapeDtypeStruct((B, S, HD), BF16),
                   jax.ShapeDtypeStruct((B, G, S // NSA_SEL_BLOCK, LANE), jnp.int32)),
        grid=(B, G, S // tq),
        in_specs=[qspec, pl.BlockSpec((1, 1) + cmp.shape[2:], lambda b, g, i: (b, g, 0, 0, 0))],
        out_specs=(qspec, pl.BlockSpec((1, 1, nq, LANE), lambda b, g, i: (b, g, i, 0))),
        compiler_params=_cparams("parallel", "parallel", "arbitrary"),
        name="nsa_compressed_attention",
    )(qkv, cmp)


def _nsa_combine_body(z_ref, oc_ref, os_ref, ow_ref, o_ref):
    dh = NSA_HEAD_DIM
    gates = jax.nn.sigmoid(z_ref[...])
    for hh in range(NSA_HEADS):
        sl = slice(hh * dh, (hh + 1) * dh)
        o_ref[:, sl] = (gates[:, 3 * hh:3 * hh + 1] * oc_ref[:, sl].astype(F32)
                        + gates[:, 3 * hh + 1:3 * hh + 2] * os_ref[:, sl].astype(F32)
                        + gates[:, 3 * hh + 2:3 * hh + 3] * ow_ref[:, sl].astype(F32)).astype(o_ref.dtype)


def nsa_combine(proj, o_cmp, o_sel, o_win, *, tm=256):
    T, HD = o_cmp.shape
    row = pl.BlockSpec((tm, HD), lambda i: (i, 0))
    gate_blk = (NSA_HEADS + 6 * NSA_KV_GROUPS) * NSA_HEAD_DIM // LANE
    return pl.pallas_call(
        _nsa_combine_body,
        out_shape=jax.ShapeDtypeStruct((T, HD), BF16),
        grid=(T // tm,),
        in_specs=[pl.BlockSpec((tm, LANE), lambda i: (i, gate_blk)), row, row, row],
        out_specs=row,
        compiler_params=_cparams("parallel"),
        name="nsa_combine",
    )(proj, o_cmp, o_sel, o_win)


FOX_TILE = 256
FOX_HEADS_PER_STEP = 4


def _fox_body(q_ref, k_ref, v_ref, cq_ref, ck_ref, o_ref):
    t, dh = FOX_TILE, FOX_HEAD_DIM
    nh = q_ref.shape[2] // dh
    heads = [slice(dh * j, dh * (j + 1)) for j in range(nh)]
    i = pl.program_id(2)
    qs = [q_ref[0, :, h] for h in heads]
    cq_all = cq_ref[0]
    head_lane = lax.broadcasted_iota(jnp.int32, cq_all.shape, 1) - pl.program_id(1) * nh
    cqs = [jnp.sum(jnp.where(head_lane == j, cq_all, 0.0), axis=-1, keepdims=True) for j in range(nh)]
    causal = (lax.broadcasted_iota(jnp.int32, (t, t), 0) >= lax.broadcasted_iota(jnp.int32, (t, t), 1))
    ones = jnp.ones((t, dh), BF16)

    def step(j, carry, diagonal):
        k0 = pl.multiple_of(j * t, t)
        ss = [lax.dot_general(q, k_ref[0, pl.ds(k0, t), h], (((1,), (1,)), ((), ())),
                              preferred_element_type=F32) + (cq - ck_ref[0, n, :, pl.ds(k0, t)])
              for n, (q, cq, h) in enumerate(zip(qs, cqs, heads))]
        if diagonal:
            ss = [jnp.where(causal, s, NEG_INF) for s in ss]
        m_new = [jnp.maximum(m, jnp.max(s, axis=-1, keepdims=True)) for (m, _), s in zip(carry, ss)]
        ps = [jnp.exp(s - m).astype(BF16) for s, m in zip(ss, m_new)]
        pv = [jnp.dot(p, jnp.concatenate([v_ref[0, pl.ds(k0, t), h], ones], axis=1),
                      preferred_element_type=F32) for p, h in zip(ps, heads)]
        return tuple((mn, jnp.exp(m - mn) * acc + x) for (m, acc), mn, x in zip(carry, m_new, pv))

    init = tuple((jnp.full((t, 1), NEG_INF, F32), jnp.zeros((t, 2 * dh), F32)) for _ in heads)
    carry = lax.fori_loop(0, i, lambda j, c: step(j, c, False), init)
    carry = step(i, carry, True)
    for (_, acc), h in zip(carry, heads):
        o_ref[0, :, h] = (acc[:, :dh] / acc[:, dh:]).astype(o_ref.dtype)


def fox_attention(proj, cum_q, cum_k):
    B, S = proj.shape[:2]
    H, dh, t, nh = FOX_HEADS, FOX_HEAD_DIM, FOX_TILE, FOX_HEADS_PER_STEP
    G = H // nh
    return pl.pallas_call(
        _fox_body,
        out_shape=jax.ShapeDtypeStruct((B, S, H * dh), BF16),
        grid=(B, G, S // t),
        in_specs=[
            pl.BlockSpec((1, t, nh * dh), lambda b, h, i: (b, i, h)),
            pl.BlockSpec((1, S, nh * dh), lambda b, h, i: (b, 0, G + h)),
            pl.BlockSpec((1, S, nh * dh), lambda b, h, i: (b, 0, 2 * G + h)),
            pl.BlockSpec((1, t, LANE), lambda b, h, i: (b, i, 0)),
            pl.BlockSpec((1, nh, 1, S), lambda b, h, i: (b, h, 0, 0)),
        ],
        out_specs=pl.BlockSpec((1, t, nh * dh), lambda b, h, i: (b, i, h)),
        compiler_params=_cparams("parallel", "parallel", "arbitrary"),
        name="fox_attention",
    )(proj, proj, proj, cum_q, cum_k)


def _ret_body(lg_ref, q_ref, k_ref, v_ref, g_ref, cos_ref, sin_ref, gn_ref, o_ref, r_ref):
    C, dk = RET_CHUNK, RET_QK_DIM
    half = dk // 2
    lg = lg_ref[pl.program_id(1)]
    ii = lax.broadcasted_iota(jnp.int32, (C, C), 0)
    jj = lax.broadcasted_iota(jnp.int32, (C, C), 1)
    decay_mask = jnp.where(ii >= jj, jnp.exp((ii - jj).astype(F32) * lg), 0.0)
    ti = lax.broadcasted_iota(jnp.int32, (C, 1), 0).astype(F32)
    q_scale = jnp.exp((ti + 1.0) * lg)
    k_scale = jnp.exp((C - 1.0 - ti) * lg)
    chunk_decay = jnp.exp(jnp.full((1, 1), C, F32) * lg)
    r_ref[...] = jnp.zeros_like(r_ref)

    def rot(x, cos, sin):
        x1, x2 = x[:, :half], x[:, half:]
        return jnp.concatenate([x1 * cos - x2 * sin, x2 * cos + x1 * sin], axis=1)

    def chunk(c, carry):
        sl = pl.ds(pl.multiple_of(c * C, C), C)
        cos, sin = cos_ref[sl, :], sin_ref[sl, :]
        q = rot(q_ref[0, sl, :].astype(F32), cos, sin)
        k = rot(k_ref[0, sl, :].astype(F32), cos, sin) * (dk ** -0.5)
        v = v_ref[0, sl, :]
        qb = q.astype(BF16)
        inner = lax.dot_general(qb, k.astype(BF16), (((1,), (1,)), ((), ())),
                                preferred_element_type=F32) * decay_mask
        state = r_ref[...]
        o = (jnp.dot(inner.astype(BF16), v, preferred_element_type=F32)
             + jnp.dot(qb, state.astype(BF16), preferred_element_type=F32) * q_scale)
        r_ref[...] = state * chunk_decay + lax.dot_general(
            (k * k_scale).astype(BF16), v, (((0,), (0,)), ((), ())), preferred_element_type=F32)
        mu = jnp.mean(o, axis=-1, keepdims=True)
        d = o - mu
        on = d * lax.rsqrt(jnp.mean(d * d, axis=-1, keepdims=True) + RET_GN_EPS)
        g = g_ref[0, sl, :].astype(F32)
        o_ref[0, sl, :] = ((g * jax.nn.sigmoid(g)) * (on * gn_ref[...])).astype(o_ref.dtype)
        return carry

    lax.fori_loop(0, q_ref.shape[1] // C, chunk, 0)


def retention_core(proj, gn_g):
    B, S = proj.shape[:2]
    H, dk, dv = RET_HEADS, RET_QK_DIM, RET_V_DIM
    pos = jnp.arange(S, dtype=F32)
    inv_freq = RET_THETA ** (-jnp.arange(dk // 2, dtype=F32) / (dk // 2))
    ang = pos[:, None] * inv_freq
    log_gamma = jnp.log(1.0 - 2.0 ** (-5.0 - jnp.arange(H, dtype=F32)))
    tab = pl.BlockSpec((S, dk // 2), lambda b, h: (0, 0))
    return pl.pallas_call(
        _ret_body,
        out_shape=jax.ShapeDtypeStruct((B, S, H * dv), BF16),
        grid=(B, H),
        in_specs=[
            pl.BlockSpec(memory_space=pltpu.SMEM),
            pl.BlockSpec((1, S, dk), lambda b, h: (b, 0, h)),
            pl.BlockSpec((1, S, dk), lambda b, h: (b, 0, H + h)),
            pl.BlockSpec((1, S, dv), lambda b, h: (b, 0, 2 * H * dk // dv + h)),
            pl.BlockSpec((1, S, dv), lambda b, h: (b, 0, 2 * H * dk // dv + H + h)),
            tab, tab,
            pl.BlockSpec((1, dv), lambda b, h: (0, h)),
        ],
        out_specs=pl.BlockSpec((1, S, dv), lambda b, h: (b, 0, h)),
        scratch_shapes=[pltpu.VMEM((dk, dv), F32)],
        compiler_params=_cparams("parallel", "parallel"),
        name="retention_core",
    )(log_gamma, proj, proj, proj, proj, jnp.cos(ang), jnp.sin(ang), gn_g.reshape(1, H * dv))


def _head_norm(y, eps):
    y = y.astype(F32)
    mu = jnp.mean(y, axis=-1, keepdims=True)
    var = jnp.mean(jnp.square(y - mu), axis=-1, keepdims=True)
    return (y - mu) * lax.rsqrt(var + eps)


def _rotary(x, pos, theta, rot_dim):
    half = rot_dim // 2
    inv_freq = theta ** (-jnp.arange(half, dtype=F32) / half)
    ang = pos.astype(F32)[:, None] * inv_freq
    shape = (1, ang.shape[0]) + (1,) * (x.ndim - 3) + (half,)
    cos = jnp.cos(ang).reshape(shape)
    sin = jnp.sin(ang).reshape(shape)
    xr = x[..., :rot_dim].astype(F32)
    x1, x2 = xr[..., :half], xr[..., half:]
    rot = jnp.concatenate([x1 * cos - x2 * sin, x2 * cos + x1 * sin], axis=-1).astype(x.dtype)
    if rot_dim == x.shape[-1]:
        return rot
    return jnp.concatenate([rot, x[..., rot_dim:]], axis=-1)


def _pad_cols(w, n):
    return jnp.pad(w, ((0, 0), (0, n - w.shape[1])))


def _nsa_core(proj, cmp_pe, cmp_w1, cmp_w2):
    B, S = proj.shape[:2]
    H, G, dh = NSA_HEADS, NSA_KV_GROUPS, NSA_HEAD_DIM
    R = H // G
    L, stride, blk, W = NSA_CMP_LEN, NSA_CMP_STRIDE, NSA_SEL_BLOCK, NSA_WINDOW
    scale = dh ** -0.5
    pos = jnp.arange(S)
    hq, hk = H * dh, G * dh
    q = proj[..., :hq].reshape(B, S, G, R, dh)
    kv = proj[..., hq:hq + 6 * hk].reshape(B, S, 6, G, dh)
    gates = jax.nn.sigmoid(proj[..., hq + 6 * hk:hq + 6 * hk + 3 * H].reshape(B, S, G, R, 3))
    q = _rotary(q, pos, ROPE_THETA, NSA_ROPE_DIM)
    kc, ks, kw = (_rotary(kv[:, :, c], pos, ROPE_THETA, NSA_ROPE_DIM) for c in (0, 2, 4))
    vc, vs, vw = kv[:, :, 1], kv[:, :, 3], kv[:, :, 5]

    n_cmp = (S - L) // stride + 1
    idx_np = np.arange(n_cmp)[:, None] * stride + np.arange(L)[None, :]
    cmp_end = jnp.asarray(idx_np[:, -1])

    def compress(t, c):
        blocks = t[:, idx_np] + cmp_pe[c][None, None, :, None, :]
        hid = jax.nn.gelu(jnp.einsum('bnlgd,ldf->bngf', blocks, cmp_w1[c]))
        return jnp.einsum('bngf,fe->bnge', hid, cmp_w2[c])

    k_cmp, v_cmp = compress(kc, 0), compress(vc, 1)
    s_cmp = jnp.einsum('bsgrd,bngd->bgrsn', q, k_cmp).astype(F32) * scale
    m_cmp = cmp_end[None, :] <= pos[:, None]
    p_cmp = jax.nn.softmax(jnp.where(m_cmp, s_cmp, NEG_INF), axis=-1) * m_cmp
    o_cmp = jnp.einsum('bgrsn,bngd->bsgrd', p_cmp, v_cmp)

    n_blk = S // blk
    starts = np.arange(n_cmp) * stride
    bstart = np.arange(n_blk) * blk
    overlap = (starts[:, None] <= bstart[None, :] + blk - 1) & (starts[:, None] + L - 1 >= bstart[None, :])
    imp = jnp.einsum('bgrsn,nj->bgsj', p_cmp, jnp.asarray(overlap, F32))
    imp = imp.reshape(B, G, n_blk, blk, n_blk).sum(axis=3)
    qb_i = np.arange(n_blk)[:, None]
    kb_j = np.arange(n_blk)[None, :]
    valid = kb_j <= qb_i
    forced = (kb_j == 0) | (kb_j == qb_i) | (kb_j == qb_i - 1)
    imp = jnp.where(forced, jnp.inf, jnp.where(valid, imp, -jnp.inf))
    n_sel = min(NSA_N_SEL, n_blk)
    _, sel = lax.top_k(imp, n_sel)

    flat = lambda t: t.reshape(B, S, -1).astype(BF16)
    o_sel, o_win = nsa_selected_window(sel, flat(q * scale), flat(ks), flat(vs), flat(kw), flat(vw))
    o_sel = o_sel.reshape(B, S, G, R, dh)
    o_win = o_win.reshape(B, S, G, R, dh)
    o = gates[..., 0, None] * o_cmp + gates[..., 1, None] * o_sel + gates[..., 2, None] * o_win
    return o.reshape(B * S, H * dh)


NSA_IN_PADDED = 42 * LANE


def _nsa_mixer(h, g, w_in, cmp_pe, cmp_w1, cmp_w2, B, S):
    T = h.shape[0]
    proj = norm_matmul(h, g, _pad_cols(w_in, NSA_IN_PADDED).astype(BF16))
    qkv, kcvc = nsa_prepare(proj, S)
    cmp = nsa_compress(kcvc.reshape(B, S, -1), cmp_pe, cmp_w1.astype(BF16), cmp_w2.astype(BF16))
    qkv = qkv.reshape(B, S, -1)
    o_cmp, sel = nsa_compressed_attention(qkv, cmp)
    o_sel, o_win = nsa_selected_window(sel[..., :NSA_N_SEL], qkv)
    return nsa_combine(proj, o_cmp.reshape(T, -1), o_sel.reshape(T, -1), o_win.reshape(T, -1))


def _rwkv_mixer(h, g, mu, w_rkv, w0, w1, w2, a0, a1, a2, g1, g2, k_k, k_a, r_k, ln_gb, B, S):
    T, D = h.shape
    pad_c = lambda w: _pad_cols(w, LANE)
    pad_r = lambda w: jnp.pad(w, ((0, LANE - w.shape[0]), (0, 0)))
    w_main = jnp.concatenate([w_rkv[0], w_rkv[1], w_rkv[2]], axis=1).astype(BF16)
    w_hidden = jnp.concatenate([pad_c(w1), pad_c(a1), g1], axis=1).astype(BF16)
    proj, hidden = rwkv_mix_project(h, g, mu, w_main, w_hidden, S)
    lw, a, kk, k2, gate = rwkv_gates(proj, hidden, jnp.stack([w0, a0, k_k, k_a]), pad_r(w2).astype(BF16),
                                     pad_r(a2).astype(BF16), g2.astype(BF16))
    as3 = lambda t: t.reshape(B, S, -1)
    y = rwkv_recurrence(as3(proj), as3(lw), as3(k2), as3(kk), as3(a))
    vec = jnp.stack([ln_gb[0], ln_gb[1], r_k.reshape(D)])
    return rwkv_post(y.reshape(T, D), proj, k2, gate, vec)


def _fox_mixer(h, g, w_in, b_f, B, S):
    HD = FOX_HEADS * FOX_HEAD_DIM
    w_qkv = jnp.concatenate([w_in[:, :HD] * FOX_HEAD_DIM ** -0.5, w_in[:, HD:3 * HD]], axis=1)
    proj = norm_matmul(h, g, w_qkv.astype(BF16), out_dtype=BF16)
    z = norm_matmul(h, g, _pad_cols(w_in[:, 3 * HD:], LANE).astype(BF16))
    log_f = jax.nn.log_sigmoid(z.reshape(B, S, LANE) + jnp.pad(b_f, (0, LANE - FOX_HEADS)))
    cum = jnp.cumsum(log_f, axis=1)
    cum_k = jnp.transpose(cum[:, :, :FOX_HEADS], (0, 2, 1))[:, :, None, :]
    return fox_attention(proj.reshape(B, S, 3 * HD), cum, cum_k).reshape(B * S, HD)


def kernel(x, p, norm_g, ffn_w_in, ffn_w_out, ple_w_proj, ple_w_gate, nsa_w_in, nsa_cmp_pe, nsa_cmp_w1, nsa_cmp_w2, nsa_w_out, rwkv_mu, rwkv_w_rkv, rwkv_w0, rwkv_w1, rwkv_w2, rwkv_a0, rwkv_a1, rwkv_a2, rwkv_g1, rwkv_g2, rwkv_k_k, rwkv_k_a, rwkv_r_k, rwkv_ln, rwkv_w_out, fox_w_in, fox_b_f, fox_w_out, ret_w_in, ret_gn_g, ret_w_out):
    B, S, D = x.shape
    T = B * S
    h = x.reshape(T, D)
    bf = lambda w: w.astype(BF16)
    ffn_w_in, ffn_w_out, ple_w_proj, ple_w_gate = bf(ffn_w_in), bf(ffn_w_out), bf(ple_w_proj), bf(ple_w_gate)
    p = p.reshape(DEPTH, T, PLE_DIM)
    for i in range(DEPTH):
        m, j = i % N_MIXERS, i // N_MIXERS
        ng = norm_g[i]
        h = ffn_half_step(h, ng[0], ng[1], ffn_w_in, ffn_w_out, i, 0)
        if m == 0:
            y = _nsa_mixer(h, ng[2], nsa_w_in[j], nsa_cmp_pe[j], nsa_cmp_w1[j], nsa_cmp_w2[j], B, S)
            w_out = nsa_w_out[j]
        elif m == 1:
            y = _rwkv_mixer(h, ng[2], rwkv_mu[j], rwkv_w_rkv[j], rwkv_w0[j], rwkv_w1[j], rwkv_w2[j],
                            rwkv_a0[j], rwkv_a1[j], rwkv_a2[j], rwkv_g1[j], rwkv_g2[j],
                            rwkv_k_k[j], rwkv_k_a[j], rwkv_r_k[j], rwkv_ln[j], B, S)
            w_out = rwkv_w_out[j]
        elif m == 2:
            y = _fox_mixer(h, ng[2], fox_w_in[j], fox_b_f[j], B, S)
            w_out = fox_w_out[j]
        else:
            proj = norm_matmul(h, ng[2], bf(ret_w_in[j]), out_dtype=BF16)
            y = retention_core(proj.reshape(B, S, RET_IN), ret_gn_g[j]).reshape(T, -1)
            w_out = ret_w_out[j]
        h = matmul_norm_residual(y, bf(w_out), ng[3], h)
        h = ffn_half_step(h, ng[4], ng[5], ffn_w_in, ffn_w_out, i, 1)
        h = ple_step(h, p, ng[6], ng[7], ple_w_proj, ple_w_gate, i)
    return h.reshape(B, S, D)
```

```python
import functools

import jax
import jax.numpy as jnp
import numpy as np
from jax import lax
from jax.experimental import pallas as pl
from jax.experimental.pallas import tpu as pltpu

D_MODEL = 2048
BATCH = 16
SEQ = 2048
DEPTH = 4
N_MIXERS = 4
PLE_DIM = 256
D_FF = 5632
RMS_EPS = 1e-6
NEG_INF = -1e30

NSA_HEADS = 16
NSA_KV_GROUPS = 4
NSA_HEAD_DIM = D_MODEL // NSA_HEADS
NSA_CMP_LEN = 32
NSA_CMP_STRIDE = 16
NSA_CMP_HIDDEN = 2 * NSA_HEAD_DIM
NSA_SEL_BLOCK = 64
NSA_N_SEL = 8
NSA_WINDOW = 512
NSA_ROPE_DIM = NSA_HEAD_DIM // 4
ROPE_THETA = 500000.0
NSA_IN = NSA_HEADS * NSA_HEAD_DIM + 6 * NSA_KV_GROUPS * NSA_HEAD_DIM + 3 * NSA_HEADS

RWKV_HEAD_DIM = 64
RWKV_HEADS = D_MODEL // RWKV_HEAD_DIM
RWKV_GN_EPS = 64e-5

FOX_HEADS = 16
FOX_HEAD_DIM = D_MODEL // FOX_HEADS
FOX_BLOCK = 128
FOX_IN = 3 * FOX_HEADS * FOX_HEAD_DIM + FOX_HEADS

RET_HEADS = 8
RET_QK_DIM = D_MODEL // RET_HEADS
RET_V_DIM = 2 * D_MODEL // RET_HEADS
RET_CHUNK = 128
RET_THETA = 10000.0
RET_GN_EPS = 1e-5
RET_IN = 2 * RET_HEADS * RET_QK_DIM + 2 * RET_HEADS * RET_V_DIM

V7X_VMEM_LIMIT_BYTES = 56 * 1024 * 1024
LANE = 128

F32 = jnp.float32
BF16 = jnp.bfloat16


def _cparams(*sem):
    return pltpu.CompilerParams(dimension_semantics=sem, vmem_limit_bytes=V7X_VMEM_LIMIT_BYTES)


def _rms(x, g):
    return x * lax.rsqrt(jnp.mean(x * x, axis=-1, keepdims=True) + RMS_EPS) * g


def _ffn_body(h_ref, g0_ref, g1_ref, wg_ref, wu_ref, wo_ref, o_ref, xn_ref, acc_ref):
    f = pl.program_id(1)

    @pl.when(f == 0)
    def _():
        xn_ref[...] = _rms(h_ref[...], g0_ref[...]).astype(BF16)
        acc_ref[...] = jnp.zeros_like(acc_ref)

    xn = xn_ref[...]
    gate = jnp.dot(xn, wg_ref[...], preferred_element_type=F32)
    up = jnp.dot(xn, wu_ref[...], preferred_element_type=F32)
    act = (gate * jax.nn.sigmoid(gate)) * up
    acc_ref[...] += jnp.dot(act.astype(BF16), wo_ref[...], preferred_element_type=F32)

    @pl.when(f == pl.num_programs(1) - 1)
    def _():
        o_ref[...] = h_ref[...] + 0.5 * _rms(acc_ref[...], g1_ref[...])


def ffn_half_step(h, g0, g1, w_in, w_out, layer, half, *, tm=512, tf=512):
    T, D = h.shape
    nf = D_FF // tf
    return pl.pallas_call(
        _ffn_body,
        out_shape=jax.ShapeDtypeStruct((T, D), F32),
        grid=(T // tm, nf),
        in_specs=[
            pl.BlockSpec((tm, D), lambda i, f: (i, 0)),
            pl.BlockSpec((1, D), lambda i, f: (0, 0)),
            pl.BlockSpec((1, D), lambda i, f: (0, 0)),
            pl.BlockSpec((None, None, D, tf), lambda i, f: (layer, half, 0, f)),
            pl.BlockSpec((None, None, D, tf), lambda i, f: (layer, half, 0, f + nf)),
            pl.BlockSpec((None, None, tf, D), lambda i, f: (layer, half, f, 0)),
        ],
        out_specs=pl.BlockSpec((tm, D), lambda i, f: (i, 0)),
        scratch_shapes=[pltpu.VMEM((tm, D), BF16), pltpu.VMEM((tm, D), F32)],
        compiler_params=_cparams("parallel", "arbitrary"),
        name="ffn_half_step",
    )(h, g0.reshape(1, D), g1.reshape(1, D), w_in, w_in, w_out)


def _norm_mm_body(x_ref, g_ref, w_ref, o_ref, xn_ref):
    @pl.when(pl.program_id(1) == 0)
    def _():
        xn_ref[...] = _rms(x_ref[...], g_ref[...]).astype(BF16)

    o_ref[...] = jnp.dot(xn_ref[...], w_ref[...], preferred_element_type=F32).astype(o_ref.dtype)


def _mm_body(x_ref, w_ref, o_ref):
    o_ref[...] = jnp.dot(x_ref[...].astype(BF16), w_ref[...],
                         preferred_element_type=F32).astype(o_ref.dtype)


def _col_tile(n, cap=1024):
    best = LANE
    for t in range(LANE, cap + 1, LANE):
        if n % t == 0:
            best = t
    return best


def norm_matmul(x, g, w, *, out_dtype=F32, tm=1024):
    T, K = x.shape
    N = w.shape[1]
    tn = _col_tile(N)
    return pl.pallas_call(
        _norm_mm_body,
        out_shape=jax.ShapeDtypeStruct((T, N), out_dtype),
        grid=(T // tm, N // tn),
        in_specs=[
            pl.BlockSpec((tm, K), lambda i, j: (i, 0)),
            pl.BlockSpec((1, K), lambda i, j: (0, 0)),
            pl.BlockSpec((K, tn), lambda i, j: (0, j)),
        ],
        out_specs=pl.BlockSpec((tm, tn), lambda i, j: (i, j)),
        scratch_shapes=[pltpu.VMEM((tm, K), BF16)],
        compiler_params=_cparams("parallel", "arbitrary"),
        name="norm_matmul",
    )(x, g.reshape(1, K), w)


def matmul(x, w, *, out_dtype=F32, tm=1024):
    T, K = x.shape
    N = w.shape[1]
    tn = _col_tile(N)
    return pl.pallas_call(
        _mm_body,
        out_shape=jax.ShapeDtypeStruct((T, N), out_dtype),
        grid=(T // tm, N // tn),
        in_specs=[
            pl.BlockSpec((tm, K), lambda i, j: (i, 0)),
            pl.BlockSpec((K, tn), lambda i, j: (0, j)),
        ],
        out_specs=pl.BlockSpec((tm, tn), lambda i, j: (i, j)),
        compiler_params=_cparams("parallel", "arbitrary"),
        name="matmul",
    )(x, w)


def _mm_res_body(y_ref, w_ref, g_ref, h_ref, o_ref, acc_ref):
    k = pl.program_id(1)

    @pl.when(k == 0)
    def _():
        acc_ref[...] = jnp.zeros_like(acc_ref)

    acc_ref[...] += jnp.dot(y_ref[...].astype(BF16), w_ref[...], preferred_element_type=F32)

    @pl.when(k == pl.num_programs(1) - 1)
    def _():
        o_ref[...] = h_ref[...] + _rms(acc_ref[...], g_ref[...])


def _mm_res_single_body(y_ref, w_ref, g_ref, h_ref, o_ref):
    acc = jnp.dot(y_ref[...].astype(BF16), w_ref[...], preferred_element_type=F32)
    o_ref[...] = h_ref[...] + _rms(acc, g_ref[...])


def matmul_norm_residual(y, w, g, h, *, tm=512, tk=2048):
    T, K = y.shape
    D = w.shape[1]
    if K == tk:
        return pl.pallas_call(
            _mm_res_single_body,
            out_shape=jax.ShapeDtypeStruct((T, D), F32),
            grid=(T // tm,),
            in_specs=[
                pl.BlockSpec((tm, K), lambda i: (i, 0)),
                pl.BlockSpec((K, D), lambda i: (0, 0)),
                pl.BlockSpec((1, D), lambda i: (0, 0)),
                pl.BlockSpec((tm, D), lambda i: (i, 0)),
            ],
            out_specs=pl.BlockSpec((tm, D), lambda i: (i, 0)),
            compiler_params=_cparams("parallel"),
            name="matmul_norm_residual",
        )(y, w, g.reshape(1, D), h)
    return pl.pallas_call(
        _mm_res_body,
        out_shape=jax.ShapeDtypeStruct((T, D), F32),
        grid=(T // tm, K // tk),
        in_specs=[
            pl.BlockSpec((tm, tk), lambda i, k: (i, k)),
            pl.BlockSpec((tk, D), lambda i, k: (k, 0)),
            pl.BlockSpec((1, D), lambda i, k: (0, 0)),
            pl.BlockSpec((tm, D), lambda i, k: (i, 0)),
        ],
        out_specs=pl.BlockSpec((tm, D), lambda i, k: (i, 0)),
        scratch_shapes=[pltpu.VMEM((tm, D), F32)],
        compiler_params=_cparams("parallel", "arbitrary"),
        name="matmul_norm_residual",
    )(y, w, g.reshape(1, D), h)


def _ple_body(h_ref, p_ref, g6_ref, g7_ref, wp_ref, wg_ref, o_ref):
    h = h_ref[...]
    xn = _rms(h, g6_ref[...]).astype(BF16)
    z = jnp.dot(xn, wg_ref[...], preferred_element_type=F32)
    e = jnp.dot(p_ref[...].astype(BF16), wp_ref[...], preferred_element_type=F32)
    o_ref[...] = h + _rms(e * jax.nn.sigmoid(z), g7_ref[...])


def ple_step(h, p, g6, g7, wp, wg, layer, *, tm=256):
    T, D = h.shape
    P = p.shape[2]
    return pl.pallas_call(
        _ple_body,
        out_shape=jax.ShapeDtypeStruct((T, D), F32),
        grid=(T // tm,),
        in_specs=[
            pl.BlockSpec((tm, D), lambda i: (i, 0)),
            pl.BlockSpec((None, tm, P), lambda i: (layer, i, 0)),
            pl.BlockSpec((1, D), lambda i: (0, 0)),
            pl.BlockSpec((1, D), lambda i: (0, 0)),
            pl.BlockSpec((None, P, D), lambda i: (layer, 0, 0)),
            pl.BlockSpec((None, D, D), lambda i: (layer, 0, 0)),
        ],
        out_specs=pl.BlockSpec((tm, D), lambda i: (i, 0)),
        compiler_params=_cparams("parallel"),
        name="ple_step",
    )(h, p, g6.reshape(1, D), g7.reshape(1, D), wp, wg)


RWKV_CHUNK = 64
RWKV_PAIRS_PER_STEP = 16
RWKV_TIME_BLOCK = 256


def _rwkv_body(r_ref, lw_ref, k_ref, v_ref, kk_ref, a_ref, y_ref, s_ref):
    C = RWKV_CHUNK
    N = RWKV_HEAD_DIM
    lane = lax.broadcasted_iota(jnp.int32, (C, 2 * N), 1)
    row = lax.broadcasted_iota(jnp.int32, (C, 2 * N), 0)
    first_head = lane < N
    ri = lax.broadcasted_iota(jnp.int32, (2 * C, 2 * C), 0)
    ci = lax.broadcasted_iota(jnp.int32, (2 * C, 2 * C), 1)
    strict = ri > ci
    incl = ri >= ci
    eye = jnp.where(ri == ci, 1.0, 0.0).astype(F32)
    corner = [((ri >> (lvl + 1)) == (ci >> (lvl + 1))) & ((ri & (1 << lvl)) != 0) & ((ci & (1 << lvl)) == 0)
              for lvl in range(C.bit_length() - 1)]

    def stack(x):
        return jnp.concatenate([jnp.where(first_head, x, 0.0), jnp.where(first_head, 0.0, x)], axis=0)

    def nt(a, b):
        return lax.dot_general(a.astype(BF16), b.astype(BF16), (((1,), (1,)), ((), ())),
                               preferred_element_type=F32)

    def nn(a, b):
        return jnp.dot(a.astype(BF16), b.astype(BF16), preferred_element_type=F32)

    def tn(a, b):
        return lax.dot_general(a.astype(BF16), b.astype(BF16), (((0,), (0,)), ((), ())),
                               preferred_element_type=F32)

    n_pairs = r_ref.shape[2] // (2 * N)
    lanes = [slice(2 * N * j, 2 * N * (j + 1)) for j in range(n_pairs)]

    def prep(sl, ln):
        r, lw, k, v, kk, a = (ref[0, sl, ln] for ref in (r_ref, lw_ref, k_ref, v_ref, kk_ref, a_ref))
        cl = lw
        for sh in (1, 2, 4, 8, 16, 32):
            cl = cl + jnp.where(row >= sh, pltpu.roll(cl, sh, axis=0), 0.0)
        mid = cl[C // 2 - 1:C // 2, :]
        last = cl[C - 1:C, :]
        e_neg = jnp.exp(mid - cl)
        e_end = jnp.exp(last - mid)
        b_til = stack(kk * a * e_neg)
        k_til = stack(k * e_neg)
        return dict(
            a_bar=stack(-kk * jnp.exp(cl - lw - mid)), r_bar=stack(r * jnp.exp(cl - mid)),
            b_til=b_til, k_til=k_til, v_st=stack(v), e_mid=jnp.exp(mid), w_tot=jnp.exp(last),
            bk_end=jnp.concatenate([b_til * e_end, k_til * e_end], axis=0))

    def chunk(c, states):
        sl = pl.ds(pl.multiple_of(c * C, C), C)
        ps = [prep(sl, ln) for ln in lanes]
        gs = [nt(jnp.concatenate([p["a_bar"], p["r_bar"]], axis=0),
                 jnp.concatenate([p["b_til"], p["k_til"]], axis=0)) for p in ps]
        a_ab = [jnp.where(strict, g[:2 * C, :2 * C], 0.0) for g in gs]
        a_ak = [jnp.where(strict, g[:2 * C, 2 * C:], 0.0) for g in gs]
        a_rb = [jnp.where(incl, g[2 * C:, :2 * C], 0.0) for g in gs]
        a_rk = [jnp.where(incl, g[2 * C:, 2 * C:], 0.0) for g in gs]
        inv = [eye + jnp.where(corner[0], x, 0.0) for x in a_ab]
        for lvl in range(1, len(corner)):
            tmp = [nn(jnp.where(corner[lvl], x, 0.0), t) for x, t in zip(a_ab, inv)]
            inv = [t + nn(t, x) for t, x in zip(inv, tmp)]
        s_mid = [st * p["e_mid"] for st, p in zip(states, ps)]
        rhs = [nt(p["a_bar"], sm) + nn(ak, p["v_st"]) for p, sm, ak in zip(ps, s_mid, a_ak)]
        us = [nn(t, x) for t, x in zip(inv, rhs)]
        ys = [nt(p["r_bar"], sm) + nn(rb, u) + nn(rk, p["v_st"])
              for p, sm, rb, rk, u in zip(ps, s_mid, a_rb, a_rk, us)]
        for ln, y in zip(lanes, ys):
            y_ref[0, sl, ln] = y[:C] + y[C:]
        return tuple(st * p["w_tot"] + tn(jnp.concatenate([u, p["v_st"]], axis=0), p["bk_end"])
                     for st, p, u in zip(states, ps, us))

    @pl.when(pl.program_id(2) == 0)
    def _():
        s_ref[...] = jnp.zeros_like(s_ref)

    states = lax.fori_loop(0, r_ref.shape[1] // C, chunk, tuple(s_ref[j] for j in range(n_pairs)))
    for j, st in enumerate(states):
        s_ref[j] = st


def rwkv_recurrence(proj, lw, k, kk, a):
    B, S, D = lw.shape
    lanes = 2 * RWKV_HEAD_DIM * RWKV_PAIRS_PER_STEP
    ts = min(S, RWKV_TIME_BLOCK)
    spec = pl.BlockSpec((1, ts, lanes), lambda b, j, t: (b, t, j))
    v_spec = pl.BlockSpec((1, ts, lanes), lambda b, j, t: (b, t, 2 * D // lanes + j))
    return pl.pallas_call(
        _rwkv_body,
        out_shape=jax.ShapeDtypeStruct((B, S, D), F32),
        grid=(B, D // lanes, S // ts),
        in_specs=[spec, spec, spec, v_spec, spec, spec],
        out_specs=spec,
        scratch_shapes=[pltpu.VMEM((RWKV_PAIRS_PER_STEP, 2 * RWKV_HEAD_DIM, 2 * RWKV_HEAD_DIM), F32)],
        compiler_params=_cparams("parallel", "parallel", "arbitrary"),
        name="rwkv_recurrence",
    )(proj, lw, k, proj, kk, a)


RWKV_PROJ_TN = 1024
RWKV_HIDDEN = 4 * LANE


def _rwkv_shift(first_of_seq, h_ref, hp_ref, g_ref):
    u = _rms(h_ref[...], g_ref[...])
    prev = jnp.where(first_of_seq, 0.0, _rms(hp_ref[7:8, :], g_ref[...]))
    row = lax.broadcasted_iota(jnp.int32, u.shape, 0)
    return u, jnp.where(row == 0, prev, pltpu.roll(u, 1, axis=0)) - u


def _rwkv_proj_body(tiles_per_seq, h_ref, hp_ref, g_ref, mu_ref, w_ref, o_ref, xm_ref):
    i, j = pl.program_id(0), pl.program_id(1)

    @pl.when(j == 0)
    def _():
        u, xx = _rwkv_shift(lax.rem(i, tiles_per_seq) == 0, h_ref, hp_ref, g_ref)
        for c in range(3):
            xm_ref[c] = (u + xx * mu_ref[c:c + 1, :]).astype(BF16)

    o_ref[...] = jnp.dot(xm_ref[j // (pl.num_programs(1) // 3)], w_ref[...], preferred_element_type=F32)


def _rwkv_hidden_body(tiles_per_seq, h_ref, hp_ref, g_ref, mu_ref, w_ref, o_ref):
    u, xx = _rwkv_shift(lax.rem(pl.program_id(0), tiles_per_seq) == 0, h_ref, hp_ref, g_ref)
    mix = lambda c: (u + xx * mu_ref[c:c + 1, :]).astype(BF16)
    o_ref[:, :LANE] = jnp.tanh(jnp.dot(mix(3), w_ref[:, :LANE], preferred_element_type=F32))
    o_ref[:, LANE:2 * LANE] = jnp.dot(mix(4), w_ref[:, LANE:2 * LANE], preferred_element_type=F32)
    o_ref[:, 2 * LANE:] = jax.nn.sigmoid(jnp.dot(mix(5), w_ref[:, 2 * LANE:], preferred_element_type=F32))


def rwkv_mix_project(h, g, mu, w_rkv, w_hidden, seq_len, *, tm=512):
    T, D = h.shape
    tn = RWKV_PROJ_TN
    prev_rows = lambda i, *_: (jnp.maximum(i * (tm // 8) - 1, 0), 0)
    rkv = pl.pallas_call(
        functools.partial(_rwkv_proj_body, seq_len // tm),
        out_shape=jax.ShapeDtypeStruct((T, 3 * D), F32),
        grid=(T // tm, 3 * D // tn),
        in_specs=[
            pl.BlockSpec((tm, D), lambda i, j: (i, 0)),
            pl.BlockSpec((8, D), prev_rows),
            pl.BlockSpec((1, D), lambda i, j: (0, 0)),
            pl.BlockSpec((6, D), lambda i, j: (0, 0)),
            pl.BlockSpec((D, tn), lambda i, j: (0, j)),
        ],
        out_specs=pl.BlockSpec((tm, tn), lambda i, j: (i, j)),
        scratch_shapes=[pltpu.VMEM((3, tm, D), BF16)],
        compiler_params=_cparams("parallel", "arbitrary"),
        name="rwkv_mix_project",
    )(h, h, g.reshape(1, D), mu, w_rkv)
    hidden = pl.pallas_call(
        functools.partial(_rwkv_hidden_body, seq_len // tm),
        out_shape=jax.ShapeDtypeStruct((T, RWKV_HIDDEN), F32),
        grid=(T // tm,),
        in_specs=[
            pl.BlockSpec((tm, D), lambda i: (i, 0)),
            pl.BlockSpec((8, D), prev_rows),
            pl.BlockSpec((1, D), lambda i: (0, 0)),
            pl.BlockSpec((6, D), lambda i: (0, 0)),
            pl.BlockSpec((D, RWKV_HIDDEN), lambda i: (0, 0)),
        ],
        out_specs=pl.BlockSpec((tm, RWKV_HIDDEN), lambda i: (i, 0)),
        compiler_params=_cparams("parallel"),
        name="rwkv_mix_hidden",
    )(h, h, g.reshape(1, D), mu, w_hidden)
    return rkv, hidden


def _head_sum(x, ones_bd):
    hi = x.astype(BF16)
    lo = (x - hi.astype(F32)).astype(BF16)
    return (jnp.dot(hi, ones_bd, preferred_element_type=F32) + jnp.dot(lo, ones_bd, preferred_element_type=F32))


def _head_ones():
    shift = RWKV_HEAD_DIM.bit_length() - 1
    r = lax.broadcasted_iota(jnp.int32, (LANE, LANE), 0) >> shift
    c = lax.broadcasted_iota(jnp.int32, (LANE, LANE), 1) >> shift
    return jnp.where(r == c, 1.0, 0.0).astype(BF16)


def _rwkv_gates_body(k_ref, hid_ref, vec_ref, w2_ref, a2_ref, g2_ref, lw_ref, a_ref, kk_ref, k2_ref, gate_ref):
    hid = hid_ref[...].astype(BF16)
    z = vec_ref[0:1, :] + jnp.dot(hid[:, :LANE], w2_ref[...], preferred_element_type=F32)
    softplus = jnp.maximum(-z, 0.0) + jnp.log(1.0 + jnp.exp(-jnp.abs(z)))
    lw_ref[...] = -jnp.exp(-softplus - 0.5)
    a = jax.nn.sigmoid(vec_ref[1:2, :] + jnp.dot(hid[:, LANE:2 * LANE], a2_ref[...], preferred_element_type=F32))
    a_ref[...] = a
    gate_ref[...] = jnp.dot(hid[:, 2 * LANE:], g2_ref[...], preferred_element_type=F32)
    k = k_ref[...]
    k2_ref[...] = k * (1.0 + (a - 1.0) * vec_ref[3:4, :])
    kk = k * vec_ref[2:3, :]
    ones_bd = _head_ones()
    for c in range(k.shape[1] // LANE):
        blk = kk[:, c * LANE:(c + 1) * LANE]
        norm = jnp.sqrt(_head_sum(blk * blk, ones_bd))
        kk_ref[:, c * LANE:(c + 1) * LANE] = blk / jnp.maximum(norm, 1e-12)


def rwkv_gates(proj, hidden, vec, w2, a2, g2, *, tm=256):
    T = proj.shape[0]
    D = w2.shape[1]
    row = pl.BlockSpec((tm, D), lambda i: (i, 0))
    full = lambda w: pl.BlockSpec(w.shape, lambda i: (0, 0))
    return pl.pallas_call(
        _rwkv_gates_body,
        out_shape=tuple(jax.ShapeDtypeStruct((T, D), F32) for _ in range(5)),
        grid=(T // tm,),
        in_specs=[
            pl.BlockSpec((tm, D), lambda i: (i, 1)),
            pl.BlockSpec((tm, RWKV_HIDDEN), lambda i: (i, 0)),
            full(vec), full(w2), full(a2), full(g2),
        ],
        out_specs=(row,) * 5,
        compiler_params=_cparams("parallel"),
        name="rwkv_gates",
    )(proj, hidden, vec, w2, a2, g2)


def _rwkv_post_body(y_ref, r_ref, v_ref, k2_ref, gate_ref, vec_ref, o_ref):
    ones_bd = _head_ones()
    inv_n = 1.0 / RWKV_HEAD_DIM
    for c in range(y_ref.shape[1] // LANE):
        sl = slice(c * LANE, (c + 1) * LANE)
        y = y_ref[:, sl]
        d = y - _head_sum(y, ones_bd) * inv_n
        yn = d * lax.rsqrt(_head_sum(d * d, ones_bd) * inv_n + RWKV_GN_EPS)
        bonus = _head_sum(r_ref[:, sl] * k2_ref[:, sl] * vec_ref[2:3, sl], ones_bd) * v_ref[:, sl]
        o_ref[:, sl] = ((yn * vec_ref[0:1, sl] + vec_ref[1:2, sl] + bonus) * gate_ref[:, sl]).astype(o_ref.dtype)


def rwkv_post(y, proj, k2, gate, vec, *, tm=256):
    T, D = y.shape
    row = pl.BlockSpec((tm, D), lambda i: (i, 0))
    return pl.pallas_call(
        _rwkv_post_body,
        out_shape=jax.ShapeDtypeStruct((T, D), BF16),
        grid=(T // tm,),
        in_specs=[row, row, pl.BlockSpec((tm, D), lambda i: (i, 2)), row, row,
                  pl.BlockSpec(vec.shape, lambda i: (0, 0))],
        out_specs=row,
        compiler_params=_cparams("parallel"),
        name="rwkv_post",
    )(y, proj, proj, k2, gate, vec)


NSA_QBLOCKS_PER_ITER = 2


def _nsa_sel_win_body(sel_ref, q_ref, ks_ref, vs_ref, kw_ref, vw_ref, osel_ref, owin_ref):
    blk, n_sel, W = NSA_SEL_BLOCK, NSA_N_SEL, NSA_WINDOW
    R, dh = NSA_HEADS // NSA_KV_GROUPS, NSA_HEAD_DIM
    S = q_ref.shape[1]
    n_blk = S // blk
    base = (pl.program_id(0) * NSA_KV_GROUPS + pl.program_id(1)) * (n_blk * n_sel)
    row_t = lax.broadcasted_iota(jnp.int32, (R * blk, 1), 0) & (blk - 1)
    col_s = lax.broadcasted_iota(jnp.int32, (1, n_sel * blk), 1)
    col_w = lax.broadcasted_iota(jnp.int32, (1, W + blk), 1)
    ones_s = jnp.ones((n_sel * blk, dh), BF16)
    ones_w = jnp.ones((W + blk, dh), BF16)

    def nt(a, b):
        return lax.dot_general(a, b, (((1,), (1,)), ((), ())), preferred_element_type=F32)

    def probs(s, mask):
        s = jnp.where(mask, s, NEG_INF)
        return jnp.exp(s - jnp.max(s, axis=-1, keepdims=True)).astype(BF16)

    def qblocks(it, carry):
        ids = [it * NSA_QBLOCKS_PER_ITER + n for n in range(NSA_QBLOCKS_PER_ITER)]
        q0s = [pl.multiple_of(i * blk, blk) for i in ids]
        w0s = [pl.multiple_of(jnp.maximum(q0 - W, 0), blk) for q0 in q0s]
        starts = [[sel_ref[base + i * n_sel + j] * blk for j in range(n_sel)] for i in ids]
        qss = []
        for q0 in q0s:
            qb = q_ref[0, pl.ds(q0, blk), :]
            qss.append(jnp.concatenate([qb[:, r * dh:(r + 1) * dh] for r in range(R)], axis=0))
        s_sel = [nt(qs, jnp.concatenate([ks_ref[0, pl.ds(pl.multiple_of(x, blk), blk), :] for x in st], axis=0))
                 for qs, st in zip(qss, starts)]
        s_win = [nt(qs, kw_ref[0, pl.ds(w0, W + blk), :]) for qs, w0 in zip(qss, w0s)]
        p_sel, p_win = [], []
        for q0, w0, st, ss, sw in zip(q0s, w0s, starts, s_sel, s_win):
            qpos = q0 + row_t
            tok = col_s & (blk - 1)
            for j in range(n_sel):
                tok = tok + jnp.where((col_s >> (blk.bit_length() - 1)) == j, st[j], 0)
            dist = qpos - (w0 + col_w)
            p_sel.append(probs(ss, tok <= qpos))
            p_win.append(probs(sw, (dist >= 0) & (dist < W)))
        o_sel = [jnp.dot(p, jnp.concatenate(
            [jnp.concatenate([vs_ref[0, pl.ds(pl.multiple_of(x, blk), blk), :] for x in st], axis=0), ones_s],
            axis=1), preferred_element_type=F32) for p, st in zip(p_sel, starts)]
        o_win = [jnp.dot(p, jnp.concatenate([vw_ref[0, pl.ds(w0, W + blk), :], ones_w], axis=1),
                         preferred_element_type=F32) for p, w0 in zip(p_win, w0s)]
        for q0, os_, ow in zip(q0s, o_sel, o_win):
            os_ = (os_[:, :dh] / os_[:, dh:]).astype(osel_ref.dtype)
            ow = (ow[:, :dh] / ow[:, dh:]).astype(owin_ref.dtype)
            for r in range(R):
                osel_ref[0, pl.ds(q0, blk), r * dh:(r + 1) * dh] = os_[r * blk:(r + 1) * blk]
                owin_ref[0, pl.ds(q0, blk), r * dh:(r + 1) * dh] = ow[r * blk:(r + 1) * blk]
        return carry

    lax.fori_loop(0, n_blk // NSA_QBLOCKS_PER_ITER, qblocks, 0)


def nsa_selected_window(sel, qkv):
    B, S = qkv.shape[:2]
    G, dh, HD = NSA_KV_GROUPS, NSA_HEAD_DIM, NSA_HEADS * NSA_HEAD_DIM
    qspec = pl.BlockSpec((1, S, HD // G), lambda b, g, sel: (b, 0, g))
    kspec = lambda c: pl.BlockSpec((1, S, dh), lambda b, g, sel: (b, 0, HD // dh + c * G + g))
    return pl.pallas_call(
        _nsa_sel_win_body,
        out_shape=(jax.ShapeDtypeStruct((B, S, HD), BF16), jax.ShapeDtypeStruct((B, S, HD), BF16)),
        grid_spec=pltpu.PrefetchScalarGridSpec(
            num_scalar_prefetch=1,
            grid=(B, G),
            in_specs=[qspec, kspec(2), kspec(3), kspec(4), kspec(5)],
            out_specs=(qspec, qspec),
        ),
        compiler_params=_cparams("parallel", "parallel"),
        name="nsa_selected_window",
    )(sel.reshape(-1), qkv, qkv, qkv, qkv, qkv)


def _nsa_rope_tables(S):
    half = NSA_ROPE_DIM // 2
    inv_freq = ROPE_THETA ** (-jnp.arange(half, dtype=F32) / half)
    ang = jnp.arange(S, dtype=F32)[:, None] * inv_freq
    cos, sin = jnp.cos(ang), jnp.sin(ang)
    rest = NSA_HEAD_DIM - NSA_ROPE_DIM
    cos_t = jnp.concatenate([cos, cos, jnp.ones((S, rest), F32)], axis=1)
    sin_t = jnp.concatenate([-sin, sin, jnp.zeros((S, rest), F32)], axis=1)
    return cos_t, sin_t


def _nsa_prep_body(x_ref, cos_ref, sin_ref, o_ref, kv_ref):
    dh, G = NSA_HEAD_DIM, NSA_KV_GROUPS
    half = NSA_ROPE_DIM // 2
    n_q = NSA_HEADS
    cos, sin = cos_ref[...], sin_ref[...]
    low = lax.broadcasted_iota(jnp.int32, cos.shape, 1) < half
    for blk in range(n_q + 6 * G):
        sl = slice(blk * dh, (blk + 1) * dh)
        x = x_ref[:, sl]
        c = (blk - n_q) // G
        if blk < n_q or c in (0, 2, 4):
            swapped = jnp.where(low, pltpu.roll(x, dh - half, axis=1), pltpu.roll(x, half, axis=1))
            x = x * cos + swapped * sin
        if blk < n_q:
            x = x * dh ** -0.5
        elif c in (0, 1):
            kv_ref[:, (blk - n_q) * dh:(blk - n_q + 1) * dh] = x
        o_ref[:, sl] = x.astype(o_ref.dtype)


def nsa_prepare(proj, S, *, tm=256):
    T = proj.shape[0]
    dh, G = NSA_HEAD_DIM, NSA_KV_GROUPS
    n = (NSA_HEADS + 6 * G) * dh
    cos_t, sin_t = _nsa_rope_tables(S)
    tab = pl.BlockSpec((tm, dh), lambda i: (i % (S // tm), 0))
    return pl.pallas_call(
        _nsa_prep_body,
        out_shape=(jax.ShapeDtypeStruct((T, n), BF16), jax.ShapeDtypeStruct((T, 2 * G * dh), F32)),
        grid=(T // tm,),
        in_specs=[pl.BlockSpec((tm, n), lambda i: (i, 0)), tab, tab],
        out_specs=(pl.BlockSpec((tm, n), lambda i: (i, 0)), pl.BlockSpec((tm, 2 * G * dh), lambda i: (i, 0))),
        compiler_params=_cparams("parallel"),
        name="nsa_prepare",
    )(proj, cos_t, sin_t)


def _nsa_compress_body(kc_ref, vc_ref, pe_ref, w1_ref, w2_ref, o_ref):
    L, stride = NSA_CMP_LEN, NSA_CMP_STRIDE
    n_grp = kc_ref.shape[1] // stride
    for c, ref in enumerate((kc_ref, vc_ref)):
        first = jnp.zeros((n_grp, NSA_CMP_HIDDEN), F32)
        second = jnp.zeros((n_grp, NSA_CMP_HIDDEN), F32)
        for l in range(stride):
            rows = ref[0, pl.ds(l, n_grp, stride=stride), :]
            first += jnp.dot((rows + pe_ref[c, l:l + 1, :]).astype(BF16), w1_ref[c, l],
                             preferred_element_type=F32)
            second += jnp.dot((rows + pe_ref[c, stride + l:stride + l + 1, :]).astype(BF16),
                              w1_ref[c, stride + l], preferred_element_type=F32)
        hid = jax.nn.gelu(first + pltpu.roll(second, n_grp - 1, axis=0))
        o_ref[0, 0, c] = jnp.dot(hid.astype(BF16), w2_ref[c], preferred_element_type=F32)


def nsa_compress(kcvc, pe, w1, w2):
    B, S = kcvc.shape[:2]
    G, dh = NSA_KV_GROUPS, NSA_HEAD_DIM
    full = lambda w: pl.BlockSpec(w.shape, lambda b, g: (0,) * w.ndim)
    return pl.pallas_call(
        _nsa_compress_body,
        out_shape=jax.ShapeDtypeStruct((B, G, 2, S // NSA_CMP_STRIDE, dh), F32),
        grid=(B, G),
        in_specs=[pl.BlockSpec((1, S, dh), lambda b, g: (b, 0, g)),
                  pl.BlockSpec((1, S, dh), lambda b, g: (b, 0, G + g)),
                  full(pe), full(w1), full(w2)],
        out_specs=pl.BlockSpec((1, 1, 2, S // NSA_CMP_STRIDE, dh), lambda b, g: (b, g, 0, 0, 0)),
        compiler_params=_cparams("parallel", "parallel"),
        name="nsa_compress",
    )(kcvc, kcvc, pe, w1, w2)


NSA_CMP_TILE = 512


def _nsa_cmp_body(q_ref, cmp_ref, o_ref, sel_ref):
    dh, blk = NSA_HEAD_DIM, NSA_SEL_BLOCK
    R = NSA_HEADS // NSA_KV_GROUPS
    tq = q_ref.shape[1]
    n_cmp = cmp_ref.shape[3]
    nq = tq // blk
    i = pl.program_id(2)
    k_cmp = cmp_ref[0, 0, 0].astype(BF16)
    v_cmp = cmp_ref[0, 0, 1].astype(BF16)
    pos = i * tq + lax.broadcasted_iota(jnp.int32, (tq, 1), 0)
    n_id = lax.broadcasted_iota(jnp.int32, (1, n_cmp), 1)
    visible = n_id * NSA_CMP_STRIDE + (NSA_CMP_LEN - 1) <= pos
    start = lax.broadcasted_iota(jnp.int32, (n_cmp, LANE), 0) * NSA_CMP_STRIDE
    bstart = lax.broadcasted_iota(jnp.int32, (n_cmp, LANE), 1) * blk
    overlap = jnp.where((start <= bstart + blk - 1) & (start + NSA_CMP_LEN - 1 >= bstart), 1.0, 0.0).astype(BF16)
    imp = jnp.zeros((tq, LANE), F32)
    for r in range(R):
        s = lax.dot_general(q_ref[0, :, r * dh:(r + 1) * dh], k_cmp, (((1,), (1,)), ((), ())),
                            preferred_element_type=F32)
        s = jnp.where(visible, s, NEG_INF)
        e = jnp.exp(s - jnp.max(s, axis=-1, keepdims=True))
        p = jnp.where(visible, e / jnp.sum(e, axis=-1, keepdims=True), 0.0).astype(BF16)
        o_ref[0, :, r * dh:(r + 1) * dh] = jnp.dot(p, v_cmp, preferred_element_type=F32).astype(o_ref.dtype)
        imp += jnp.dot(p, overlap, preferred_element_type=F32)
    imp = imp.reshape(nq, blk, LANE).sum(axis=1)
    qb = i * nq + lax.broadcasted_iota(jnp.int32, (nq, 1), 0)
    kb = lax.broadcasted_iota(jnp.int32, (nq, LANE), 1)
    forced = (kb == 0) | (kb == qb) | (kb == qb - 1)
    val = jnp.where(forced, jnp.inf, jnp.where(kb <= qb, imp, -jnp.inf))
    kb_f = kb.astype(F32)
    avail = kb >= 0
    picks = jnp.zeros((nq, LANE), F32)
    for t in range(NSA_N_SEL):
        best = jnp.max(jnp.where(avail, val, -jnp.inf), axis=-1, keepdims=True)
        pick = jnp.min(jnp.where(avail & (val == best), kb_f, float(LANE)), axis=-1, keepdims=True)
        picks = jnp.where(kb == t, pick, picks)
        avail = avail & (kb_f != pick)
    sel_ref[0, 0] = picks.astype(jnp.int32)


def nsa_compressed_attention(qkv, cmp):
    B, S = qkv.shape[:2]
    G, dh, HD, tq = NSA_KV_GROUPS, NSA_HEAD_DIM, NSA_HEADS * NSA_HEAD_DIM, NSA_CMP_TILE
    nq = tq // NSA_SEL_BLOCK
    qspec = pl.BlockSpec((1, tq, HD // G), lambda b, g, i: (b, i, g))
    return pl.pallas_call(
        _nsa_cmp_body,
        out_shape=(jax.ShapeDtypeStruct((B, S, HD), BF16),
                   jax.ShapeDtypeStruct((B, G, S // NSA_SEL_BLOCK, LANE), jnp.int32)),
        grid=(B, G, S // tq),
        in_specs=[qspec, pl.BlockSpec((1, 1) + cmp.shape[2:], lambda b, g, i: (b, g, 0, 0, 0))],
        out_specs=(qspec, pl.BlockSpec((1, 1, nq, LANE), lambda b, g, i: (b, g, i, 0))),
        compiler_params=_cparams("parallel", "parallel", "arbitrary"),
        name="nsa_compressed_attention",
    )(qkv, cmp)


def _nsa_combine_body(z_ref, oc_ref, os_ref, ow_ref, o_ref):
    dh = NSA_HEAD_DIM
    gates = jax.nn.sigmoid(z_ref[...])
    for hh in range(NSA_HEADS):
        sl = slice(hh * dh, (hh + 1) * dh)
        o_ref[:, sl] = (gates[:, 3 * hh:3 * hh + 1] * oc_ref[:, sl].astype(F32)
                        + gates[:, 3 * hh + 1:3 * hh + 2] * os_ref[:, sl].astype(F32)
                        + gates[:, 3 * hh + 2:3 * hh + 3] * ow_ref[:, sl].astype(F32)).astype(o_ref.dtype)


def nsa_combine(proj, o_cmp, o_sel, o_win, *, tm=256):
    T, HD = o_cmp.shape
    row = pl.BlockSpec((tm, HD), lambda i: (i, 0))
    gate_blk = (NSA_HEADS + 6 * NSA_KV_GROUPS) * NSA_HEAD_DIM // LANE
    return pl.pallas_call(
        _nsa_combine_body,
        out_shape=jax.ShapeDtypeStruct((T, HD), BF16),
        grid=(T // tm,),
        in_specs=[pl.BlockSpec((tm, LANE), lambda i: (i, gate_blk)), row, row, row],
        out_specs=row,
        compiler_params=_cparams("parallel"),
        name="nsa_combine",
    )(proj, o_cmp, o_sel, o_win)


FOX_TILE = 256
FOX_HEADS_PER_STEP = 8


def _fox_body(q_ref, k_ref, v_ref, cq_ref, ck_ref, o_ref):
    t, dh = FOX_TILE, FOX_HEAD_DIM
    nh = q_ref.shape[2] // dh
    heads = [slice(dh * j, dh * (j + 1)) for j in range(nh)]
    i = pl.program_id(2)
    qs = [q_ref[0, :, h] for h in heads]
    cq_all = cq_ref[0]
    head_lane = lax.broadcasted_iota(jnp.int32, cq_all.shape, 1) - pl.program_id(1) * nh
    cqs = [jnp.sum(jnp.where(head_lane == j, cq_all, 0.0), axis=-1, keepdims=True) for j in range(nh)]
    causal = (lax.broadcasted_iota(jnp.int32, (t, t), 0) >= lax.broadcasted_iota(jnp.int32, (t, t), 1))
    ones = jnp.ones((t, dh), BF16)

    def step(j, carry, diagonal):
        k0 = pl.multiple_of(j * t, t)
        ss = [lax.dot_general(q, k_ref[0, pl.ds(k0, t), h], (((1,), (1,)), ((), ())),
                              preferred_element_type=F32) + (cq - ck_ref[0, n, :, pl.ds(k0, t)])
              for n, (q, cq, h) in enumerate(zip(qs, cqs, heads))]
        if diagonal:
            ss = [jnp.where(causal, s, NEG_INF) for s in ss]
        m_new = [jnp.maximum(m, jnp.max(s, axis=-1, keepdims=True)) for (m, _), s in zip(carry, ss)]
        ps = [jnp.exp(s - m).astype(BF16) for s, m in zip(ss, m_new)]
        pv = [jnp.dot(p, jnp.concatenate([v_ref[0, pl.ds(k0, t), h], ones], axis=1),
                      preferred_element_type=F32) for p, h in zip(ps, heads)]
        return tuple((mn, jnp.exp(m - mn) * acc + x) for (m, acc), mn, x in zip(carry, m_new, pv))

    init = tuple((jnp.full((t, 1), NEG_INF, F32), jnp.zeros((t, 2 * dh), F32)) for _ in heads)
    carry = lax.fori_loop(0, i, lambda j, c: step(j, c, False), init)
    carry = step(i, carry, True)
    for (_, acc), h in zip(carry, heads):
        o_ref[0, :, h] = (acc[:, :dh] / acc[:, dh:]).astype(o_ref.dtype)


def fox_attention(proj, cum_q, cum_k):
    B, S = proj.shape[:2]
    H, dh, t, nh = FOX_HEADS, FOX_HEAD_DIM, FOX_TILE, FOX_HEADS_PER_STEP
    G = H // nh
    return pl.pallas_call(
        _fox_body,
        out_shape=jax.ShapeDtypeStruct((B, S, H * dh), BF16),
        grid=(B, G, S // t),
        in_specs=[
            pl.BlockSpec((1, t, nh * dh), lambda b, h, i: (b, i, h)),
            pl.BlockSpec((1, S, nh * dh), lambda b, h, i: (b, 0, G + h)),
            pl.BlockSpec((1, S, nh * dh), lambda b, h, i: (b, 0, 2 * G + h)),
            pl.BlockSpec((1, t, LANE), lambda b, h, i: (b, i, 0)),
            pl.BlockSpec((1, nh, 1, S), lambda b, h, i: (b, h, 0, 0)),
        ],
        out_specs=pl.BlockSpec((1, t, nh * dh), lambda b, h, i: (b, i, h)),
        compiler_params=_cparams("parallel", "parallel", "arbitrary"),
        name="fox_attention",
    )(proj, proj, proj, cum_q, cum_k)


RET_HEADS_PER_STEP = 2


def _ret_body(lg_ref, q_ref, k_ref, v_ref, g_ref, cos_ref, sin_ref, gn_ref, o_ref, r_ref):
    C, dk, dv = RET_CHUNK, RET_QK_DIM, RET_V_DIM
    half = dk // 2
    nh = r_ref.shape[0]
    qk = [slice(dk * n, dk * (n + 1)) for n in range(nh)]
    vv = [slice(dv * n, dv * (n + 1)) for n in range(nh)]
    lgs = [lg_ref[pl.program_id(1) * nh + n] for n in range(nh)]
    ii = lax.broadcasted_iota(jnp.int32, (C, C), 0)
    jj = lax.broadcasted_iota(jnp.int32, (C, C), 1)
    ti = lax.broadcasted_iota(jnp.int32, (C, 1), 0).astype(F32)
    decay_mask = [jnp.where(ii >= jj, jnp.exp((ii - jj).astype(F32) * lg), 0.0) for lg in lgs]
    q_scale = [jnp.exp((ti + 1.0) * lg) for lg in lgs]
    k_scale = [jnp.exp((C - 1.0 - ti) * lg) for lg in lgs]
    chunk_decay = [jnp.exp(jnp.full((1, 1), C, F32) * lg) for lg in lgs]
    r_ref[...] = jnp.zeros_like(r_ref)

    def rot(x, cos, sin):
        x1, x2 = x[:, :half], x[:, half:]
        return jnp.concatenate([x1 * cos - x2 * sin, x2 * cos + x1 * sin], axis=1)

    def chunk(c, carry):
        sl = pl.ds(pl.multiple_of(c * C, C), C)
        cos, sin = cos_ref[sl, :], sin_ref[sl, :]
        qs = [rot(q_ref[0, sl, h].astype(F32), cos, sin).astype(BF16) for h in qk]
        ks = [rot(k_ref[0, sl, h].astype(F32), cos, sin) * (dk ** -0.5) for h in qk]
        vs = [v_ref[0, sl, h] for h in vv]
        inner = [lax.dot_general(q, k.astype(BF16), (((1,), (1,)), ((), ())), preferred_element_type=F32) * dm
                 for q, k, dm in zip(qs, ks, decay_mask)]
        states = [r_ref[n] for n in range(nh)]
        cross = [jnp.dot(q, st.astype(BF16), preferred_element_type=F32) * sc
                 for q, st, sc in zip(qs, states, q_scale)]
        os_ = [jnp.dot(x.astype(BF16), v, preferred_element_type=F32) + cr for x, v, cr in zip(inner, vs, cross)]
        upd = [lax.dot_general((k * sc).astype(BF16), v, (((0,), (0,)), ((), ())), preferred_element_type=F32)
               for k, sc, v in zip(ks, k_scale, vs)]
        for n in range(nh):
            r_ref[n] = states[n] * chunk_decay[n] + upd[n]
        for o, h in zip(os_, vv):
            d = o - jnp.mean(o, axis=-1, keepdims=True)
            on = d * lax.rsqrt(jnp.mean(d * d, axis=-1, keepdims=True) + RET_GN_EPS)
            g = g_ref[0, sl, h].astype(F32)
            o_ref[0, sl, h] = ((g * jax.nn.sigmoid(g)) * (on * gn_ref[:, h])).astype(o_ref.dtype)
        return carry

    lax.fori_loop(0, q_ref.shape[1] // C, chunk, 0)


def retention_core(proj, gn_g):
    B, S = proj.shape[:2]
    H, dk, dv, nh = RET_HEADS, RET_QK_DIM, RET_V_DIM, RET_HEADS_PER_STEP
    G = H // nh
    pos = jnp.arange(S, dtype=F32)
    inv_freq = RET_THETA ** (-jnp.arange(dk // 2, dtype=F32) / (dk // 2))
    ang = pos[:, None] * inv_freq
    log_gamma = jnp.log(1.0 - 2.0 ** (-5.0 - jnp.arange(H, dtype=F32)))
    tab = pl.BlockSpec((S, dk // 2), lambda b, h: (0, 0))
    v_first = 2 * H * dk // (nh * dv)
    return pl.pallas_call(
        _ret_body,
        out_shape=jax.ShapeDtypeStruct((B, S, H * dv), BF16),
        grid=(B, G),
        in_specs=[
            pl.BlockSpec(memory_space=pltpu.SMEM),
            pl.BlockSpec((1, S, nh * dk), lambda b, h: (b, 0, h)),
            pl.BlockSpec((1, S, nh * dk), lambda b, h: (b, 0, G + h)),
            pl.BlockSpec((1, S, nh * dv), lambda b, h: (b, 0, v_first + h)),
            pl.BlockSpec((1, S, nh * dv), lambda b, h: (b, 0, v_first + G + h)),
            tab, tab,
            pl.BlockSpec((1, nh * dv), lambda b, h: (0, h)),
        ],
        out_specs=pl.BlockSpec((1, S, nh * dv), lambda b, h: (b, 0, h)),
        scratch_shapes=[pltpu.VMEM((nh, dk, dv), F32)],
        compiler_params=_cparams("parallel", "parallel"),
        name="retention_core",
    )(log_gamma, proj, proj, proj, proj, jnp.cos(ang), jnp.sin(ang), gn_g.reshape(1, H * dv))


def _head_norm(y, eps):
    y = y.astype(F32)
    mu = jnp.mean(y, axis=-1, keepdims=True)
    var = jnp.mean(jnp.square(y - mu), axis=-1, keepdims=True)
    return (y - mu) * lax.rsqrt(var + eps)


def _rotary(x, pos, theta, rot_dim):
    half = rot_dim // 2
    inv_freq = theta ** (-jnp.arange(half, dtype=F32) / half)
    ang = pos.astype(F32)[:, None] * inv_freq
    shape = (1, ang.shape[0]) + (1,) * (x.ndim - 3) + (half,)
    cos = jnp.cos(ang).reshape(shape)
    sin = jnp.sin(ang).reshape(shape)
    xr = x[..., :rot_dim].astype(F32)
    x1, x2 = xr[..., :half], xr[..., half:]
    rot = jnp.concatenate([x1 * cos - x2 * sin, x2 * cos + x1 * sin], axis=-1).astype(x.dtype)
    if rot_dim == x.shape[-1]:
        return rot
    return jnp.concatenate([rot, x[..., rot_dim:]], axis=-1)


def _pad_cols(w, n):
    return jnp.pad(w, ((0, 0), (0, n - w.shape[1])))


def _nsa_core(proj, cmp_pe, cmp_w1, cmp_w2):
    B, S = proj.shape[:2]
    H, G, dh = NSA_HEADS, NSA_KV_GROUPS, NSA_HEAD_DIM
    R = H // G
    L, stride, blk, W = NSA_CMP_LEN, NSA_CMP_STRIDE, NSA_SEL_BLOCK, NSA_WINDOW
    scale = dh ** -0.5
    pos = jnp.arange(S)
    hq, hk = H * dh, G * dh
    q = proj[..., :hq].reshape(B, S, G, R, dh)
    kv = proj[..., hq:hq + 6 * hk].reshape(B, S, 6, G, dh)
    gates = jax.nn.sigmoid(proj[..., hq + 6 * hk:hq + 6 * hk + 3 * H].reshape(B, S, G, R, 3))
    q = _rotary(q, pos, ROPE_THETA, NSA_ROPE_DIM)
    kc, ks, kw = (_rotary(kv[:, :, c], pos, ROPE_THETA, NSA_ROPE_DIM) for c in (0, 2, 4))
    vc, vs, vw = kv[:, :, 1], kv[:, :, 3], kv[:, :, 5]

    n_cmp = (S - L) // stride + 1
    idx_np = np.arange(n_cmp)[:, None] * stride + np.arange(L)[None, :]
    cmp_end = jnp.asarray(idx_np[:, -1])

    def compress(t, c):
        blocks = t[:, idx_np] + cmp_pe[c][None, None, :, None, :]
        hid = jax.nn.gelu(jnp.einsum('bnlgd,ldf->bngf', blocks, cmp_w1[c]))
        return jnp.einsum('bngf,fe->bnge', hid, cmp_w2[c])

    k_cmp, v_cmp = compress(kc, 0), compress(vc, 1)
    s_cmp = jnp.einsum('bsgrd,bngd->bgrsn', q, k_cmp).astype(F32) * scale
    m_cmp = cmp_end[None, :] <= pos[:, None]
    p_cmp = jax.nn.softmax(jnp.where(m_cmp, s_cmp, NEG_INF), axis=-1) * m_cmp
    o_cmp = jnp.einsum('bgrsn,bngd->bsgrd', p_cmp, v_cmp)

    n_blk = S // blk
    starts = np.arange(n_cmp) * stride
    bstart = np.arange(n_blk) * blk
    overlap = (starts[:, None] <= bstart[None, :] + blk - 1) & (starts[:, None] + L - 1 >= bstart[None, :])
    imp = jnp.einsum('bgrsn,nj->bgsj', p_cmp, jnp.asarray(overlap, F32))
    imp = imp.reshape(B, G, n_blk, blk, n_blk).sum(axis=3)
    qb_i = np.arange(n_blk)[:, None]
    kb_j = np.arange(n_blk)[None, :]
    valid = kb_j <= qb_i
    forced = (kb_j == 0) | (kb_j == qb_i) | (kb_j == qb_i - 1)
    imp = jnp.where(forced, jnp.inf, jnp.where(valid, imp, -jnp.inf))
    n_sel = min(NSA_N_SEL, n_blk)
    _, sel = lax.top_k(imp, n_sel)

    flat = lambda t: t.reshape(B, S, -1).astype(BF16)
    o_sel, o_win = nsa_selected_window(sel, flat(q * scale), flat(ks), flat(vs), flat(kw), flat(vw))
    o_sel = o_sel.reshape(B, S, G, R, dh)
    o_win = o_win.reshape(B, S, G, R, dh)
    o = gates[..., 0, None] * o_cmp + gates[..., 1, None] * o_sel + gates[..., 2, None] * o_win
    return o.reshape(B * S, H * dh)


NSA_IN_PADDED = 42 * LANE


def _nsa_mixer(h, g, w_in, cmp_pe, cmp_w1, cmp_w2, B, S):
    T = h.shape[0]
    proj = norm_matmul(h, g, _pad_cols(w_in, NSA_IN_PADDED).astype(BF16))
    qkv, kcvc = nsa_prepare(proj, S)
    cmp = nsa_compress(kcvc.reshape(B, S, -1), cmp_pe, cmp_w1.astype(BF16), cmp_w2.astype(BF16))
    qkv = qkv.reshape(B, S, -1)
    o_cmp, sel = nsa_compressed_attention(qkv, cmp)
    o_sel, o_win = nsa_selected_window(sel[..., :NSA_N_SEL], qkv)
    return nsa_combine(proj, o_cmp.reshape(T, -1), o_sel.reshape(T, -1), o_win.reshape(T, -1))


def _rwkv_mixer(h, g, mu, w_rkv, w0, w1, w2, a0, a1, a2, g1, g2, k_k, k_a, r_k, ln_gb, B, S):
    T, D = h.shape
    pad_c = lambda w: _pad_cols(w, LANE)
    pad_r = lambda w: jnp.pad(w, ((0, LANE - w.shape[0]), (0, 0)))
    w_main = jnp.concatenate([w_rkv[0], w_rkv[1], w_rkv[2]], axis=1).astype(BF16)
    w_hidden = jnp.concatenate([pad_c(w1), pad_c(a1), g1], axis=1).astype(BF16)
    proj, hidden = rwkv_mix_project(h, g, mu, w_main, w_hidden, S)
    lw, a, kk, k2, gate = rwkv_gates(proj, hidden, jnp.stack([w0, a0, k_k, k_a]), pad_r(w2).astype(BF16),
                                     pad_r(a2).astype(BF16), g2.astype(BF16))
    as3 = lambda t: t.reshape(B, S, -1)
    y = rwkv_recurrence(as3(proj), as3(lw), as3(k2), as3(kk), as3(a))
    vec = jnp.stack([ln_gb[0], ln_gb[1], r_k.reshape(D)])
    return rwkv_post(y.reshape(T, D), proj, k2, gate, vec)


def _fox_mixer(h, g, w_in, b_f, B, S):
    HD = FOX_HEADS * FOX_HEAD_DIM
    w_qkv = jnp.concatenate([w_in[:, :HD] * FOX_HEAD_DIM ** -0.5, w_in[:, HD:3 * HD]], axis=1)
    proj = norm_matmul(h, g, w_qkv.astype(BF16), out_dtype=BF16)
    z = norm_matmul(h, g, _pad_cols(w_in[:, 3 * HD:], LANE).astype(BF16))
    log_f = jax.nn.log_sigmoid(z.reshape(B, S, LANE) + jnp.pad(b_f, (0, LANE - FOX_HEADS)))
    cum = jnp.cumsum(log_f, axis=1)
    cum_k = jnp.transpose(cum[:, :, :FOX_HEADS], (0, 2, 1))[:, :, None, :]
    return fox_attention(proj.reshape(B, S, 3 * HD), cum, cum_k).reshape(B * S, HD)


def kernel(x, p, norm_g, ffn_w_in, ffn_w_out, ple_w_proj, ple_w_gate, nsa_w_in, nsa_cmp_pe, nsa_cmp_w1, nsa_cmp_w2, nsa_w_out, rwkv_mu, rwkv_w_rkv, rwkv_w0, rwkv_w1, rwkv_w2, rwkv_a0, rwkv_a1, rwkv_a2, rwkv_g1, rwkv_g2, rwkv_k_k, rwkv_k_a, rwkv_r_k, rwkv_ln, rwkv_w_out, fox_w_in, fox_b_f, fox_w_out, ret_w_in, ret_gn_g, ret_w_out):
    B, S, D = x.shape
    T = B * S
    h = x.reshape(T, D)
    bf = lambda w: w.astype(BF16)
    ffn_w_in, ffn_w_out, ple_w_proj, ple_w_gate = bf(ffn_w_in), bf(ffn_w_out), bf(ple_w_proj), bf(ple_w_gate)
    p = p.reshape(DEPTH, T, PLE_DIM)
    for i in range(DEPTH):
        m, j = i % N_MIXERS, i // N_MIXERS
        ng = norm_g[i]
        h = ffn_half_step(h, ng[0], ng[1], ffn_w_in, ffn_w_out, i, 0)
        if m == 0:
            y = _nsa_mixer(h, ng[2], nsa_w_in[j], nsa_cmp_pe[j], nsa_cmp_w1[j], nsa_cmp_w2[j], B, S)
            w_out = nsa_w_out[j]
        elif m == 1:
            y = _rwkv_mixer(h, ng[2], rwkv_mu[j], rwkv_w_rkv[j], rwkv_w0[j], rwkv_w1[j], rwkv_w2[j],
                            rwkv_a0[j], rwkv_a1[j], rwkv_a2[j], rwkv_g1[j], rwkv_g2[j],
                            rwkv_k_k[j], rwkv_k_a[j], rwkv_r_k[j], rwkv_ln[j], B, S)
            w_out = rwkv_w_out[j]
        elif m == 2:
            y = _fox_mixer(h, ng[2], fox_w_in[j], fox_b_f[j], B, S)
            w_out = fox_w_out[j]
        else:
            proj = norm_matmul(h, ng[2], bf(ret_w_in[j]), out_dtype=BF16)
            y = retention_core(proj.reshape(B, S, RET_IN), ret_gn_g[j]).reshape(T, -1)
            w_out = ret_w_out[j]
        h = matmul_norm_residual(y, bf(w_out), ng[3], h)
        h = ffn_half_step(h, ng[4], ng[5], ffn_w_in, ffn_w_out, i, 1)
        h = ple_step(h, p, ng[6], ng[7], ple_w_proj, ple_w_gate, i)
    return h.reshape(B, S, D)
```

```python
import functools

import jax
import jax.numpy as jnp
from jax import lax
from jax.experimental import pallas as pl
from jax.experimental.pallas import tpu as pltpu

D_MODEL = 2048
DEPTH = 4
N_MIXERS = 4
PLE_DIM = 256
D_FF = 5632
RMS_EPS = 1e-6
NEG_INF = -1e30

NSA_HEADS = 16
NSA_KV_GROUPS = 4
NSA_HEAD_DIM = D_MODEL // NSA_HEADS
NSA_CMP_LEN = 32
NSA_CMP_STRIDE = 16
NSA_CMP_HIDDEN = 2 * NSA_HEAD_DIM
NSA_SEL_BLOCK = 64
NSA_N_SEL = 8
NSA_WINDOW = 512
NSA_ROPE_DIM = NSA_HEAD_DIM // 4
ROPE_THETA = 500000.0

RWKV_HEAD_DIM = 64
RWKV_GN_EPS = 64e-5

FOX_HEADS = 16
FOX_HEAD_DIM = D_MODEL // FOX_HEADS

RET_HEADS = 8
RET_QK_DIM = D_MODEL // RET_HEADS
RET_V_DIM = 2 * D_MODEL // RET_HEADS
RET_CHUNK = 128
RET_THETA = 10000.0
RET_GN_EPS = 1e-5
RET_IN = 2 * RET_HEADS * RET_QK_DIM + 2 * RET_HEADS * RET_V_DIM

V7X_VMEM_LIMIT_BYTES = 56 * 1024 * 1024
V7X_VMEM_LIMIT_BYTES_LARGE = 62 * 1024 * 1024
LANE = 128

F32 = jnp.float32
BF16 = jnp.bfloat16


def _cparams(*sem):
    return pltpu.CompilerParams(dimension_semantics=sem, vmem_limit_bytes=V7X_VMEM_LIMIT_BYTES)


def _rms(x, g):
    return x * lax.rsqrt(jnp.mean(x * x, axis=-1, keepdims=True) + RMS_EPS) * g


def _ffn_body(h_ref, g0_ref, g1_ref, wg_ref, wu_ref, wo_ref, o_ref, xn_ref):
    f = pl.program_id(1)

    @pl.when(f == 0)
    def _():
        xn_ref[...] = _rms(h_ref[...], g0_ref[...]).astype(BF16)
        o_ref[...] = jnp.zeros_like(o_ref)

    xn = xn_ref[...]
    gate = jnp.dot(xn, wg_ref[...], preferred_element_type=F32)
    up = jnp.dot(xn, wu_ref[...], preferred_element_type=F32)
    act = (gate * jax.nn.sigmoid(gate)) * up
    o_ref[...] += jnp.dot(act.astype(BF16), wo_ref[...], preferred_element_type=F32)

    @pl.when(f == pl.num_programs(1) - 1)
    def _():
        o_ref[...] = h_ref[...] + 0.5 * _rms(o_ref[...], g1_ref[...])


def ffn_half_step(h, g0, g1, w_in, w_out, layer, half, *, tm=1024, tf=512):
    T, D = h.shape
    nf = D_FF // tf
    return pl.pallas_call(
        _ffn_body,
        out_shape=jax.ShapeDtypeStruct((T, D), F32),
        grid=(T // tm, nf),
        in_specs=[
            pl.BlockSpec((tm, D), lambda i, f: (i, 0)),
            pl.BlockSpec((1, D), lambda i, f: (0, 0)),
            pl.BlockSpec((1, D), lambda i, f: (0, 0)),
            pl.BlockSpec((None, None, D, tf), lambda i, f: (layer, half, 0, f)),
            pl.BlockSpec((None, None, D, tf), lambda i, f: (layer, half, 0, f + nf)),
            pl.BlockSpec((None, None, tf, D), lambda i, f: (layer, half, f, 0)),
        ],
        out_specs=pl.BlockSpec((tm, D), lambda i, f: (i, 0)),
        scratch_shapes=[pltpu.VMEM((tm, D), BF16)],
        compiler_params=pltpu.CompilerParams(dimension_semantics=("parallel", "arbitrary"),
                                             vmem_limit_bytes=V7X_VMEM_LIMIT_BYTES_LARGE),
        name="ffn_half_step",
    )(h, g0.reshape(1, D), g1.reshape(1, D), w_in, w_in, w_out)


def _norm_mm_body(x_ref, g_ref, w_ref, o_ref, xn_ref):
    @pl.when(pl.program_id(1) == 0)
    def _():
        xn_ref[...] = _rms(x_ref[...], g_ref[...]).astype(BF16)

    o_ref[...] = jnp.dot(xn_ref[...], w_ref[...], preferred_element_type=F32).astype(o_ref.dtype)


def _col_tile(n, cap=1024):
    best = LANE
    for t in range(LANE, cap + 1, LANE):
        if n % t == 0:
            best = t
    return best


def norm_matmul(x, g, w, *, out_dtype=F32, tm=1024):
    T, K = x.shape
    N = w.shape[1]
    tn = _col_tile(N)
    return pl.pallas_call(
        _norm_mm_body,
        out_shape=jax.ShapeDtypeStruct((T, N), out_dtype),
        grid=(T // tm, N // tn),
        in_specs=[
            pl.BlockSpec((tm, K), lambda i, j: (i, 0)),
            pl.BlockSpec((1, K), lambda i, j: (0, 0)),
            pl.BlockSpec((K, tn), lambda i, j: (0, j)),
        ],
        out_specs=pl.BlockSpec((tm, tn), lambda i, j: (i, j)),
        scratch_shapes=[pltpu.VMEM((tm, K), BF16)],
        compiler_params=_cparams("parallel", "arbitrary"),
        name="norm_matmul",
    )(x, g.reshape(1, K), w)


def _mm_res_body(y_ref, w_ref, g_ref, h_ref, o_ref, acc_ref):
    k = pl.program_id(1)

    @pl.when(k == 0)
    def _():
        acc_ref[...] = jnp.zeros_like(acc_ref)

    acc_ref[...] += jnp.dot(y_ref[...].astype(BF16), w_ref[...], preferred_element_type=F32)

    @pl.when(k == pl.num_programs(1) - 1)
    def _():
        o_ref[...] = h_ref[...] + _rms(acc_ref[...], g_ref[...])


def _mm_res_single_body(y_ref, w_ref, g_ref, h_ref, o_ref):
    acc = jnp.dot(y_ref[...].astype(BF16), w_ref[...], preferred_element_type=F32)
    o_ref[...] = h_ref[...] + _rms(acc, g_ref[...])


def matmul_norm_residual(y, w, g, h, *, tm=512, tk=2048):
    T, K = y.shape
    D = w.shape[1]
    if K == tk:
        return pl.pallas_call(
            _mm_res_single_body,
            out_shape=jax.ShapeDtypeStruct((T, D), F32),
            grid=(T // tm,),
            in_specs=[
                pl.BlockSpec((tm, K), lambda i: (i, 0)),
                pl.BlockSpec((K, D), lambda i: (0, 0)),
                pl.BlockSpec((1, D), lambda i: (0, 0)),
                pl.BlockSpec((tm, D), lambda i: (i, 0)),
            ],
            out_specs=pl.BlockSpec((tm, D), lambda i: (i, 0)),
            compiler_params=_cparams("parallel"),
            name="matmul_norm_residual",
        )(y, w, g.reshape(1, D), h)
    return pl.pallas_call(
        _mm_res_body,
        out_shape=jax.ShapeDtypeStruct((T, D), F32),
        grid=(T // tm, K // tk),
        in_specs=[
            pl.BlockSpec((tm, tk), lambda i, k: (i, k)),
            pl.BlockSpec((tk, D), lambda i, k: (k, 0)),
            pl.BlockSpec((1, D), lambda i, k: (0, 0)),
            pl.BlockSpec((tm, D), lambda i, k: (i, 0)),
        ],
        out_specs=pl.BlockSpec((tm, D), lambda i, k: (i, 0)),
        scratch_shapes=[pltpu.VMEM((tm, D), F32)],
        compiler_params=_cparams("parallel", "arbitrary"),
        name="matmul_norm_residual",
    )(y, w, g.reshape(1, D), h)


PLE_SUBTILES = 2


def _ple_body(h_ref, p_ref, g6_ref, g7_ref, wp_ref, wg_ref, o_ref):
    rows = h_ref.shape[0] // PLE_SUBTILES
    parts = [slice(n * rows, (n + 1) * rows) for n in range(PLE_SUBTILES)]
    xns = [_rms(h_ref[r, :], g6_ref[...]).astype(BF16) for r in parts]
    zs = [jnp.dot(xn, wg_ref[...], preferred_element_type=F32) for xn in xns]
    es = [jnp.dot(p_ref[r, :].astype(BF16), wp_ref[...], preferred_element_type=F32) for r in parts]
    for r, z, e in zip(parts, zs, es):
        o_ref[r, :] = h_ref[r, :] + _rms(e * jax.nn.sigmoid(z), g7_ref[...])


def ple_step(h, p, g6, g7, wp, wg, layer, *, tm=512):
    T, D = h.shape
    P = p.shape[2]
    return pl.pallas_call(
        _ple_body,
        out_shape=jax.ShapeDtypeStruct((T, D), F32),
        grid=(T // tm,),
        in_specs=[
            pl.BlockSpec((tm, D), lambda i: (i, 0)),
            pl.BlockSpec((None, tm, P), lambda i: (layer, i, 0)),
            pl.BlockSpec((1, D), lambda i: (0, 0)),
            pl.BlockSpec((1, D), lambda i: (0, 0)),
            pl.BlockSpec((None, P, D), lambda i: (layer, 0, 0)),
            pl.BlockSpec((None, D, D), lambda i: (layer, 0, 0)),
        ],
        out_specs=pl.BlockSpec((tm, D), lambda i: (i, 0)),
        compiler_params=_cparams("parallel"),
        name="ple_step",
    )(h, p, g6.reshape(1, D), g7.reshape(1, D), wp, wg)


RWKV_CHUNK = 64
RWKV_PAIRS_PER_STEP = 16
RWKV_TIME_BLOCK = 256


def _rwkv_body(r_ref, lw_ref, k_ref, v_ref, kk_ref, a_ref, y_ref, s_ref):
    C = RWKV_CHUNK
    N = RWKV_HEAD_DIM
    lane = lax.broadcasted_iota(jnp.int32, (C, 2 * N), 1)
    row = lax.broadcasted_iota(jnp.int32, (C, 2 * N), 0)
    first_head = lane < N
    ri = lax.broadcasted_iota(jnp.int32, (2 * C, 2 * C), 0)
    ci = lax.broadcasted_iota(jnp.int32, (2 * C, 2 * C), 1)
    strict = ri > ci
    incl = ri >= ci
    eye = jnp.where(ri == ci, 1.0, 0.0).astype(F32)
    corner = [((ri >> (lvl + 1)) == (ci >> (lvl + 1))) & ((ri & (1 << lvl)) != 0) & ((ci & (1 << lvl)) == 0)
              for lvl in range(C.bit_length() - 1)]

    def stack(x):
        return jnp.concatenate([jnp.where(first_head, x, 0.0), jnp.where(first_head, 0.0, x)], axis=0)

    def nt(a, b):
        return lax.dot_general(a.astype(BF16), b.astype(BF16), (((1,), (1,)), ((), ())),
                               preferred_element_type=F32)

    def nn(a, b):
        return jnp.dot(a.astype(BF16), b.astype(BF16), preferred_element_type=F32)

    def tn(a, b):
        return lax.dot_general(a.astype(BF16), b.astype(BF16), (((0,), (0,)), ((), ())),
                               preferred_element_type=F32)

    n_pairs = r_ref.shape[2] // (2 * N)
    lanes = [slice(2 * N * j, 2 * N * (j + 1)) for j in range(n_pairs)]

    def prep(sl, ln):
        r, lw, k, v, kk, a = (ref[0, sl, ln] for ref in (r_ref, lw_ref, k_ref, v_ref, kk_ref, a_ref))
        cl = lw
        for sh in (1, 2, 4, 8, 16, 32):
            cl = cl + jnp.where(row >= sh, pltpu.roll(cl, sh, axis=0), 0.0)
        mid = cl[C // 2 - 1:C // 2, :]
        last = cl[C - 1:C, :]
        e_neg = jnp.exp(mid - cl)
        e_end = jnp.exp(last - mid)
        b_til = stack(kk * a * e_neg)
        k_til = stack(k * e_neg)
        return dict(
            a_bar=stack(-kk * jnp.exp(cl - lw - mid)), r_bar=stack(r * jnp.exp(cl - mid)),
            b_til=b_til, k_til=k_til, v_st=stack(v), e_mid=jnp.exp(mid), w_tot=jnp.exp(last),
            bk_end=jnp.concatenate([b_til * e_end, k_til * e_end], axis=0))

    def chunk(c, states):
        sl = pl.ds(pl.multiple_of(c * C, C), C)
        ps = [prep(sl, ln) for ln in lanes]
        gs = [nt(jnp.concatenate([p["a_bar"], p["r_bar"]], axis=0),
                 jnp.concatenate([p["b_til"], p["k_til"]], axis=0)) for p in ps]
        a_ab = [jnp.where(strict, g[:2 * C, :2 * C], 0.0) for g in gs]
        a_ak = [jnp.where(strict, g[:2 * C, 2 * C:], 0.0) for g in gs]
        a_rb = [jnp.where(incl, g[2 * C:, :2 * C], 0.0) for g in gs]
        a_rk = [jnp.where(incl, g[2 * C:, 2 * C:], 0.0) for g in gs]
        inv = [eye + jnp.where(corner[0], x, 0.0) for x in a_ab]
        for lvl in range(1, len(corner)):
            tmp = [nn(jnp.where(corner[lvl], x, 0.0), t) for x, t in zip(a_ab, inv)]
            inv = [t + nn(t, x) for t, x in zip(inv, tmp)]
        s_mid = [st * p["e_mid"] for st, p in zip(states, ps)]
        rhs = [nt(p["a_bar"], sm) + nn(ak, p["v_st"]) for p, sm, ak in zip(ps, s_mid, a_ak)]
        us = [nn(t, x) for t, x in zip(inv, rhs)]
        ys = [nt(p["r_bar"], sm) + nn(rb, u) + nn(rk, p["v_st"])
              for p, sm, rb, rk, u in zip(ps, s_mid, a_rb, a_rk, us)]
        for ln, y in zip(lanes, ys):
            y_ref[0, sl, ln] = y[:C] + y[C:]
        return tuple(st * p["w_tot"] + tn(jnp.concatenate([u, p["v_st"]], axis=0), p["bk_end"])
                     for st, p, u in zip(states, ps, us))

    @pl.when(pl.program_id(2) == 0)
    def _():
        s_ref[...] = jnp.zeros_like(s_ref)

    states = lax.fori_loop(0, r_ref.shape[1] // C, chunk, tuple(s_ref[j] for j in range(n_pairs)))
    for j, st in enumerate(states):
        s_ref[j] = st


def rwkv_recurrence(proj, lw, k, kk, a):
    B, S, D = lw.shape
    lanes = 2 * RWKV_HEAD_DIM * RWKV_PAIRS_PER_STEP
    ts = min(S, RWKV_TIME_BLOCK)
    spec = pl.BlockSpec((1, ts, lanes), lambda b, j, t: (b, t, j))
    v_spec = pl.BlockSpec((1, ts, lanes), lambda b, j, t: (b, t, 2 * D // lanes + j))
    return pl.pallas_call(
        _rwkv_body,
        out_shape=jax.ShapeDtypeStruct((B, S, D), F32),
        grid=(B, D // lanes, S // ts),
        in_specs=[spec, spec, spec, v_spec, spec, spec],
        out_specs=spec,
        scratch_shapes=[pltpu.VMEM((RWKV_PAIRS_PER_STEP, 2 * RWKV_HEAD_DIM, 2 * RWKV_HEAD_DIM), F32)],
        compiler_params=_cparams("parallel", "parallel", "arbitrary"),
        name="rwkv_recurrence",
    )(proj, lw, k, proj, kk, a)


RWKV_PROJ_TN = 1024
RWKV_HIDDEN = 4 * LANE


RWKV_MIX_ROWS = 64


def _rwkv_mixes(first_of_seq, h_ref, hp_ref, g_ref, mu_ref, xm_ref, mixes):
    rows = RWKV_MIX_ROWS
    g = g_ref[...]
    before_tile = jnp.where(first_of_seq, 0.0, _rms(hp_ref[7:8, :], g))
    row = lax.broadcasted_iota(jnp.int32, (rows, h_ref.shape[1]), 0)

    def chunk(c, prev):
        sl = pl.ds(pl.multiple_of(c * rows, rows), rows)
        u = _rms(h_ref[sl, :], g)
        xx = jnp.where(row == 0, prev, pltpu.roll(u, 1, axis=0)) - u
        for n, m in enumerate(mixes):
            xm_ref[n, sl, :] = (u + xx * mu_ref[m:m + 1, :]).astype(BF16)
        return u[rows - 1:rows, :]

    lax.fori_loop(0, h_ref.shape[0] // rows, chunk, before_tile)


def _rwkv_proj_body(tiles_per_seq, h_ref, hp_ref, g_ref, mu_ref, w_ref, o_ref, xm_ref):
    i, j = pl.program_id(0), pl.program_id(1)

    @pl.when(j == 0)
    def _():
        _rwkv_mixes(lax.rem(i, tiles_per_seq) == 0, h_ref, hp_ref, g_ref, mu_ref, xm_ref, (0, 1, 2))

    o_ref[...] = jnp.dot(xm_ref[j // (pl.num_programs(1) // 3)], w_ref[...], preferred_element_type=F32)


def _rwkv_hidden_body(tiles_per_seq, h_ref, hp_ref, g_ref, mu_ref, w_ref, o_ref):
    g = g_ref[...]
    u = _rms(h_ref[...], g)
    before_tile = jnp.where(lax.rem(pl.program_id(0), tiles_per_seq) == 0, 0.0, _rms(hp_ref[7:8, :], g))
    row = lax.broadcasted_iota(jnp.int32, u.shape, 0)
    xx = jnp.where(row == 0, before_tile, pltpu.roll(u, 1, axis=0)) - u
    mix = lambda c: (u + xx * mu_ref[c:c + 1, :]).astype(BF16)
    o_ref[:, :LANE] = jnp.tanh(jnp.dot(mix(3), w_ref[:, :LANE], preferred_element_type=F32))
    o_ref[:, LANE:2 * LANE] = jnp.dot(mix(4), w_ref[:, LANE:2 * LANE], preferred_element_type=F32)
    o_ref[:, 2 * LANE:] = jax.nn.sigmoid(jnp.dot(mix(5), w_ref[:, 2 * LANE:], preferred_element_type=F32))


def rwkv_mix_project(h, g, mu, w_rkv, w_hidden, seq_len, *, tm=512):
    T, D = h.shape
    tn = RWKV_PROJ_TN
    prev_rows = lambda i, *_: (jnp.maximum(i * (tm // 8) - 1, 0), 0)
    rkv = pl.pallas_call(
        functools.partial(_rwkv_proj_body, seq_len // tm),
        out_shape=jax.ShapeDtypeStruct((T, 3 * D), F32),
        grid=(T // tm, 3 * D // tn),
        in_specs=[
            pl.BlockSpec((tm, D), lambda i, j: (i, 0)),
            pl.BlockSpec((8, D), prev_rows),
            pl.BlockSpec((1, D), lambda i, j: (0, 0)),
            pl.BlockSpec((6, D), lambda i, j: (0, 0)),
            pl.BlockSpec((D, tn), lambda i, j: (0, j)),
        ],
        out_specs=pl.BlockSpec((tm, tn), lambda i, j: (i, j)),
        scratch_shapes=[pltpu.VMEM((3, tm, D), BF16)],
        compiler_params=_cparams("parallel", "arbitrary"),
        name="rwkv_mix_project",
    )(h, h, g.reshape(1, D), mu, w_rkv)
    hidden = pl.pallas_call(
        functools.partial(_rwkv_hidden_body, seq_len // tm),
        out_shape=jax.ShapeDtypeStruct((T, RWKV_HIDDEN), F32),
        grid=(T // tm,),
        in_specs=[
            pl.BlockSpec((tm, D), lambda i: (i, 0)),
            pl.BlockSpec((8, D), prev_rows),
            pl.BlockSpec((1, D), lambda i: (0, 0)),
            pl.BlockSpec((6, D), lambda i: (0, 0)),
            pl.BlockSpec((D, RWKV_HIDDEN), lambda i: (0, 0)),
        ],
        out_specs=pl.BlockSpec((tm, RWKV_HIDDEN), lambda i: (i, 0)),
        compiler_params=_cparams("parallel"),
        name="rwkv_mix_hidden",
    )(h, h, g.reshape(1, D), mu, w_hidden)
    return rkv, hidden


def _head_sum(x, ones_bd):
    hi = x.astype(BF16)
    lo = (x - hi.astype(F32)).astype(BF16)
    return (jnp.dot(hi, ones_bd, preferred_element_type=F32) + jnp.dot(lo, ones_bd, preferred_element_type=F32))


def _head_ones():
    shift = RWKV_HEAD_DIM.bit_length() - 1
    r = lax.broadcasted_iota(jnp.int32, (LANE, LANE), 0) >> shift
    c = lax.broadcasted_iota(jnp.int32, (LANE, LANE), 1) >> shift
    return jnp.where(r == c, 1.0, 0.0).astype(BF16)


def _rwkv_gates_body(k_ref, hid_ref, vec_ref, w2_ref, a2_ref, g2_ref, lw_ref, a_ref, kk_ref, k2_ref, gate_ref):
    hid = hid_ref[...].astype(BF16)
    z = vec_ref[0:1, :] + jnp.dot(hid[:, :LANE], w2_ref[...], preferred_element_type=F32)
    softplus = jnp.maximum(-z, 0.0) + jnp.log(1.0 + jnp.exp(-jnp.abs(z)))
    lw_ref[...] = -jnp.exp(-softplus - 0.5)
    a = jax.nn.sigmoid(vec_ref[1:2, :] + jnp.dot(hid[:, LANE:2 * LANE], a2_ref[...], preferred_element_type=F32))
    a_ref[...] = a
    gate_ref[...] = jnp.dot(hid[:, 2 * LANE:], g2_ref[...], preferred_element_type=F32)
    k = k_ref[...]
    k2_ref[...] = k * (1.0 + (a - 1.0) * vec_ref[3:4, :])
    kk = k * vec_ref[2:3, :]
    ones_bd = _head_ones()
    for c in range(k.shape[1] // LANE):
        blk = kk[:, c * LANE:(c + 1) * LANE]
        norm = jnp.sqrt(_head_sum(blk * blk, ones_bd))
        kk_ref[:, c * LANE:(c + 1) * LANE] = blk / jnp.maximum(norm, 1e-12)


def rwkv_gates(proj, hidden, vec, w2, a2, g2, *, tm=256):
    T = proj.shape[0]
    D = w2.shape[1]
    row = pl.BlockSpec((tm, D), lambda i: (i, 0))
    full = lambda w: pl.BlockSpec(w.shape, lambda i: (0, 0))
    return pl.pallas_call(
        _rwkv_gates_body,
        out_shape=tuple(jax.ShapeDtypeStruct((T, D), F32) for _ in range(5)),
        grid=(T // tm,),
        in_specs=[
            pl.BlockSpec((tm, D), lambda i: (i, 1)),
            pl.BlockSpec((tm, RWKV_HIDDEN), lambda i: (i, 0)),
            full(vec), full(w2), full(a2), full(g2),
        ],
        out_specs=(row,) * 5,
        compiler_params=_cparams("parallel"),
        name="rwkv_gates",
    )(proj, hidden, vec, w2, a2, g2)


def _rwkv_post_body(y_ref, r_ref, v_ref, k2_ref, gate_ref, vec_ref, o_ref):
    ones_bd = _head_ones()
    inv_n = 1.0 / RWKV_HEAD_DIM
    for c in range(y_ref.shape[1] // LANE):
        sl = slice(c * LANE, (c + 1) * LANE)
        y = y_ref[:, sl]
        d = y - _head_sum(y, ones_bd) * inv_n
        yn = d * lax.rsqrt(_head_sum(d * d, ones_bd) * inv_n + RWKV_GN_EPS)
        bonus = _head_sum(r_ref[:, sl] * k2_ref[:, sl] * vec_ref[2:3, sl], ones_bd) * v_ref[:, sl]
        o_ref[:, sl] = ((yn * vec_ref[0:1, sl] + vec_ref[1:2, sl] + bonus) * gate_ref[:, sl]).astype(o_ref.dtype)


def rwkv_post(y, proj, k2, gate, vec, *, tm=256):
    T, D = y.shape
    row = pl.BlockSpec((tm, D), lambda i: (i, 0))
    return pl.pallas_call(
        _rwkv_post_body,
        out_shape=jax.ShapeDtypeStruct((T, D), BF16),
        grid=(T // tm,),
        in_specs=[row, row, pl.BlockSpec((tm, D), lambda i: (i, 2)), row, row,
                  pl.BlockSpec(vec.shape, lambda i: (0, 0))],
        out_specs=row,
        compiler_params=_cparams("parallel"),
        name="rwkv_post",
    )(y, proj, proj, k2, gate, vec)


NSA_QBLOCKS_PER_ITER = 4


def _nsa_sel_win_body(sel_ref, q_ref, ks_ref, vs_ref, kw_ref, vw_ref, osel_ref, owin_ref):
    blk, n_sel, W = NSA_SEL_BLOCK, NSA_N_SEL, NSA_WINDOW
    R, dh = NSA_HEADS // NSA_KV_GROUPS, NSA_HEAD_DIM
    S = q_ref.shape[1]
    n_blk = S // blk
    base = (pl.program_id(0) * NSA_KV_GROUPS + pl.program_id(1)) * (n_blk * n_sel)
    row_t = lax.broadcasted_iota(jnp.int32, (R * blk, 1), 0) & (blk - 1)
    col_s = lax.broadcasted_iota(jnp.int32, (1, n_sel * blk), 1)
    col_w = lax.broadcasted_iota(jnp.int32, (1, W + blk), 1)
    ones_s = jnp.ones((n_sel * blk, dh), BF16)
    ones_w = jnp.ones((W + blk, dh), BF16)

    def nt(a, b):
        return lax.dot_general(a, b, (((1,), (1,)), ((), ())), preferred_element_type=F32)

    def probs(s, mask):
        s = jnp.where(mask, s, NEG_INF)
        return jnp.exp(s - jnp.max(s, axis=-1, keepdims=True)).astype(BF16)

    def qblocks(it, carry):
        ids = [it * NSA_QBLOCKS_PER_ITER + n for n in range(NSA_QBLOCKS_PER_ITER)]
        q0s = [pl.multiple_of(i * blk, blk) for i in ids]
        w0s = [pl.multiple_of(jnp.maximum(q0 - W, 0), blk) for q0 in q0s]
        starts = [[sel_ref[base + i * n_sel + j] * blk for j in range(n_sel)] for i in ids]
        qss = []
        for q0 in q0s:
            qb = q_ref[0, pl.ds(q0, blk), :]
            qss.append(jnp.concatenate([qb[:, r * dh:(r + 1) * dh] for r in range(R)], axis=0))
        s_sel = [nt(qs, jnp.concatenate([ks_ref[0, pl.ds(pl.multiple_of(x, blk), blk), :] for x in st], axis=0))
                 for qs, st in zip(qss, starts)]
        s_win = [nt(qs, kw_ref[0, pl.ds(w0, W + blk), :]) for qs, w0 in zip(qss, w0s)]
        p_sel, p_win = [], []
        for q0, w0, st, ss, sw in zip(q0s, w0s, starts, s_sel, s_win):
            qpos = q0 + row_t
            tok = col_s & (blk - 1)
            for j in range(n_sel):
                tok = tok + jnp.where((col_s >> (blk.bit_length() - 1)) == j, st[j], 0)
            dist = qpos - (w0 + col_w)
            p_sel.append(probs(ss, tok <= qpos))
            p_win.append(probs(sw, (dist >= 0) & (dist < W)))
        o_sel = [jnp.dot(p, jnp.concatenate(
            [jnp.concatenate([vs_ref[0, pl.ds(pl.multiple_of(x, blk), blk), :] for x in st], axis=0), ones_s],
            axis=1), preferred_element_type=F32) for p, st in zip(p_sel, starts)]
        o_win = [jnp.dot(p, jnp.concatenate([vw_ref[0, pl.ds(w0, W + blk), :], ones_w], axis=1),
                         preferred_element_type=F32) for p, w0 in zip(p_win, w0s)]
        for q0, os_, ow in zip(q0s, o_sel, o_win):
            os_ = (os_[:, :dh] / os_[:, dh:]).astype(osel_ref.dtype)
            ow = (ow[:, :dh] / ow[:, dh:]).astype(owin_ref.dtype)
            for r in range(R):
                osel_ref[0, pl.ds(q0, blk), r * dh:(r + 1) * dh] = os_[r * blk:(r + 1) * blk]
                owin_ref[0, pl.ds(q0, blk), r * dh:(r + 1) * dh] = ow[r * blk:(r + 1) * blk]
        return carry

    lax.fori_loop(0, n_blk // NSA_QBLOCKS_PER_ITER, qblocks, 0)


def nsa_selected_window(sel, qkv):
    B, S = qkv.shape[:2]
    G, dh, HD = NSA_KV_GROUPS, NSA_HEAD_DIM, NSA_HEADS * NSA_HEAD_DIM
    qspec = pl.BlockSpec((1, S, HD // G), lambda b, g, sel: (b, 0, g))
    kspec = lambda c: pl.BlockSpec((1, S, dh), lambda b, g, sel: (b, 0, HD // dh + c * G + g))
    return pl.pallas_call(
        _nsa_sel_win_body,
        out_shape=(jax.ShapeDtypeStruct((B, S, HD), BF16), jax.ShapeDtypeStruct((B, S, HD), BF16)),
        grid_spec=pltpu.PrefetchScalarGridSpec(
            num_scalar_prefetch=1,
            grid=(B, G),
            in_specs=[qspec, kspec(2), kspec(3), kspec(4), kspec(5)],
            out_specs=(qspec, qspec),
        ),
        compiler_params=_cparams("parallel", "parallel"),
        name="nsa_selected_window",
    )(sel.reshape(-1), qkv, qkv, qkv, qkv, qkv)


def _nsa_rope_tables(S):
    half = NSA_ROPE_DIM // 2
    inv_freq = ROPE_THETA ** (-jnp.arange(half, dtype=F32) / half)
    ang = jnp.arange(S, dtype=F32)[:, None] * inv_freq
    cos, sin = jnp.cos(ang), jnp.sin(ang)
    rest = NSA_HEAD_DIM - NSA_ROPE_DIM
    cos_t = jnp.concatenate([cos, cos, jnp.ones((S, rest), F32)], axis=1)
    sin_t = jnp.concatenate([-sin, sin, jnp.zeros((S, rest), F32)], axis=1)
    return cos_t, sin_t


def _nsa_prep_body(x_ref, cos_ref, sin_ref, o_ref, kv_ref):
    dh, G = NSA_HEAD_DIM, NSA_KV_GROUPS
    half = NSA_ROPE_DIM // 2
    n_q = NSA_HEADS
    cos, sin = cos_ref[...], sin_ref[...]
    low = lax.broadcasted_iota(jnp.int32, cos.shape, 1) < half
    for blk in range(n_q + 6 * G):
        sl = slice(blk * dh, (blk + 1) * dh)
        x = x_ref[:, sl]
        c = (blk - n_q) // G
        if blk < n_q or c in (0, 2, 4):
            swapped = jnp.where(low, pltpu.roll(x, dh - half, axis=1), pltpu.roll(x, half, axis=1))
            x = x * cos + swapped * sin
        if blk < n_q:
            x = x * dh ** -0.5
        elif c in (0, 1):
            kv_ref[:, (blk - n_q) * dh:(blk - n_q + 1) * dh] = x
        o_ref[:, sl] = x.astype(o_ref.dtype)


def nsa_prepare(proj, S, *, tm=256):
    T = proj.shape[0]
    dh, G = NSA_HEAD_DIM, NSA_KV_GROUPS
    n = (NSA_HEADS + 6 * G) * dh
    cos_t, sin_t = _nsa_rope_tables(S)
    tab = pl.BlockSpec((tm, dh), lambda i: (i % (S // tm), 0))
    return pl.pallas_call(
        _nsa_prep_body,
        out_shape=(jax.ShapeDtypeStruct((T, n), BF16), jax.ShapeDtypeStruct((T, 2 * G * dh), F32)),
        grid=(T // tm,),
        in_specs=[pl.BlockSpec((tm, n), lambda i: (i, 0)), tab, tab],
        out_specs=(pl.BlockSpec((tm, n), lambda i: (i, 0)), pl.BlockSpec((tm, 2 * G * dh), lambda i: (i, 0))),
        compiler_params=_cparams("parallel"),
        name="nsa_prepare",
    )(proj, cos_t, sin_t)


def _nsa_compress_body(kc_ref, vc_ref, pe_ref, w1_ref, w2_ref, o_ref):
    L, stride = NSA_CMP_LEN, NSA_CMP_STRIDE
    n_grp = kc_ref.shape[1] // stride
    for c, ref in enumerate((kc_ref, vc_ref)):
        first = jnp.zeros((n_grp, NSA_CMP_HIDDEN), F32)
        second = jnp.zeros((n_grp, NSA_CMP_HIDDEN), F32)
        for l in range(stride):
            rows = ref[0, pl.ds(l, n_grp, stride=stride), :]
            first += jnp.dot((rows + pe_ref[c, l:l + 1, :]).astype(BF16), w1_ref[c, l],
                             preferred_element_type=F32)
            second += jnp.dot((rows + pe_ref[c, stride + l:stride + l + 1, :]).astype(BF16),
                              w1_ref[c, stride + l], preferred_element_type=F32)
        hid = jax.nn.gelu(first + pltpu.roll(second, n_grp - 1, axis=0))
        o_ref[0, 0, c] = jnp.dot(hid.astype(BF16), w2_ref[c], preferred_element_type=F32)


def nsa_compress(kcvc, pe, w1, w2):
    B, S = kcvc.shape[:2]
    G, dh = NSA_KV_GROUPS, NSA_HEAD_DIM
    full = lambda w: pl.BlockSpec(w.shape, lambda b, g: (0,) * w.ndim)
    return pl.pallas_call(
        _nsa_compress_body,
        out_shape=jax.ShapeDtypeStruct((B, G, 2, S // NSA_CMP_STRIDE, dh), F32),
        grid=(B, G),
        in_specs=[pl.BlockSpec((1, S, dh), lambda b, g: (b, 0, g)),
                  pl.BlockSpec((1, S, dh), lambda b, g: (b, 0, G + g)),
                  full(pe), full(w1), full(w2)],
        out_specs=pl.BlockSpec((1, 1, 2, S // NSA_CMP_STRIDE, dh), lambda b, g: (b, g, 0, 0, 0)),
        compiler_params=_cparams("parallel", "parallel"),
        name="nsa_compress",
    )(kcvc, kcvc, pe, w1, w2)


NSA_CMP_TILE = 512


def _nsa_cmp_body(q_ref, cmp_ref, o_ref, sel_ref):
    dh, blk = NSA_HEAD_DIM, NSA_SEL_BLOCK
    R = NSA_HEADS // NSA_KV_GROUPS
    tq = q_ref.shape[1]
    n_cmp = cmp_ref.shape[3]
    nq = tq // blk
    i = pl.program_id(2)
    k_cmp = cmp_ref[0, 0, 0].astype(BF16)
    v_cmp = cmp_ref[0, 0, 1].astype(BF16)
    pos = i * tq + lax.broadcasted_iota(jnp.int32, (tq, 1), 0)
    n_id = lax.broadcasted_iota(jnp.int32, (1, n_cmp), 1)
    visible = n_id * NSA_CMP_STRIDE + (NSA_CMP_LEN - 1) <= pos
    start = lax.broadcasted_iota(jnp.int32, (n_cmp, LANE), 0) * NSA_CMP_STRIDE
    bstart = lax.broadcasted_iota(jnp.int32, (n_cmp, LANE), 1) * blk
    overlap = jnp.where((start <= bstart + blk - 1) & (start + NSA_CMP_LEN - 1 >= bstart), 1.0, 0.0).astype(BF16)
    heads = [slice(r * dh, (r + 1) * dh) for r in range(R)]
    ss = [lax.dot_general(q_ref[0, :, h], k_cmp, (((1,), (1,)), ((), ())), preferred_element_type=F32)
          for h in heads]
    ss = [jnp.where(visible, s, NEG_INF) for s in ss]
    es = [jnp.exp(s - jnp.max(s, axis=-1, keepdims=True)) for s in ss]
    ps = [jnp.where(visible, e / jnp.sum(e, axis=-1, keepdims=True), 0.0).astype(BF16) for e in es]
    v_and_overlap = jnp.concatenate([v_cmp, overlap], axis=1)
    outs = [jnp.dot(p, v_and_overlap, preferred_element_type=F32) for p in ps]
    for h, o in zip(heads, outs):
        o_ref[0, :, h] = o[:, :dh].astype(o_ref.dtype)
    imp = sum(o[:, dh:] for o in outs)
    imp = imp.reshape(nq, blk, LANE).sum(axis=1)
    qb = i * nq + lax.broadcasted_iota(jnp.int32, (nq, 1), 0)
    kb = lax.broadcasted_iota(jnp.int32, (nq, LANE), 1)
    forced = (kb == 0) | (kb == qb) | (kb == qb - 1)
    val = jnp.where(forced, jnp.inf, jnp.where(kb <= qb, imp, -jnp.inf))
    kb_f = kb.astype(F32)
    avail = kb >= 0
    picks = jnp.zeros((nq, LANE), F32)
    for t in range(NSA_N_SEL):
        best = jnp.max(jnp.where(avail, val, -jnp.inf), axis=-1, keepdims=True)
        pick = jnp.min(jnp.where(avail & (val == best), kb_f, float(LANE)), axis=-1, keepdims=True)
        picks = jnp.where(kb == t, pick, picks)
        avail = avail & (kb_f != pick)
    sel_ref[0, 0] = picks.astype(jnp.int32)


def nsa_compressed_attention(qkv, cmp):
    B, S = qkv.shape[:2]
    G, dh, HD, tq = NSA_KV_GROUPS, NSA_HEAD_DIM, NSA_HEADS * NSA_HEAD_DIM, NSA_CMP_TILE
    nq = tq // NSA_SEL_BLOCK
    qspec = pl.BlockSpec((1, tq, HD // G), lambda b, g, i: (b, i, g))
    return pl.pallas_call(
        _nsa_cmp_body,
        out_shape=(jax.ShapeDtypeStruct((B, S, HD), BF16),
                   jax.ShapeDtypeStruct((B, G, S // NSA_SEL_BLOCK, LANE), jnp.int32)),
        grid=(B, G, S // tq),
        in_specs=[qspec, pl.BlockSpec((1, 1) + cmp.shape[2:], lambda b, g, i: (b, g, 0, 0, 0))],
        out_specs=(qspec, pl.BlockSpec((1, 1, nq, LANE), lambda b, g, i: (b, g, i, 0))),
        compiler_params=_cparams("parallel", "parallel", "arbitrary"),
        name="nsa_compressed_attention",
    )(qkv, cmp)


def _nsa_combine_body(z_ref, oc_ref, os_ref, ow_ref, o_ref):
    dh = NSA_HEAD_DIM
    gates = jax.nn.sigmoid(z_ref[...])
    for hh in range(NSA_HEADS):
        sl = slice(hh * dh, (hh + 1) * dh)
        o_ref[:, sl] = (gates[:, 3 * hh:3 * hh + 1] * oc_ref[:, sl].astype(F32)
                        + gates[:, 3 * hh + 1:3 * hh + 2] * os_ref[:, sl].astype(F32)
                        + gates[:, 3 * hh + 2:3 * hh + 3] * ow_ref[:, sl].astype(F32)).astype(o_ref.dtype)


def nsa_combine(proj, o_cmp, o_sel, o_win, *, tm=256):
    T, HD = o_cmp.shape
    row = pl.BlockSpec((tm, HD), lambda i: (i, 0))
    gate_blk = (NSA_HEADS + 6 * NSA_KV_GROUPS) * NSA_HEAD_DIM // LANE
    return pl.pallas_call(
        _nsa_combine_body,
        out_shape=jax.ShapeDtypeStruct((T, HD), BF16),
        grid=(T // tm,),
        in_specs=[pl.BlockSpec((tm, LANE), lambda i: (i, gate_blk)), row, row, row],
        out_specs=row,
        compiler_params=_cparams("parallel"),
        name="nsa_combine",
    )(proj, o_cmp, o_sel, o_win)


FOX_TILE = 256
FOX_HEADS_PER_STEP = 8


def _fox_body(q_ref, k_ref, v_ref, cq_ref, ck_ref, o_ref):
    t, dh = FOX_TILE, FOX_HEAD_DIM
    nh = q_ref.shape[2] // dh
    heads = [slice(dh * j, dh * (j + 1)) for j in range(nh)]
    i = pl.program_id(2)
    qs = [q_ref[0, :, h] for h in heads]
    cq_all = cq_ref[0]
    head_lane = lax.broadcasted_iota(jnp.int32, cq_all.shape, 1) - pl.program_id(1) * nh
    cqs = [jnp.sum(jnp.where(head_lane == j, cq_all, 0.0), axis=-1, keepdims=True) for j in range(nh)]
    causal = (lax.broadcasted_iota(jnp.int32, (t, t), 0) >= lax.broadcasted_iota(jnp.int32, (t, t), 1))
    ones = jnp.ones((t, dh), BF16)

    def step(j, carry, diagonal):
        k0 = pl.multiple_of(j * t, t)
        ss = [lax.dot_general(q, k_ref[0, pl.ds(k0, t), h], (((1,), (1,)), ((), ())),
                              preferred_element_type=F32) + (cq - ck_ref[0, n, :, pl.ds(k0, t)])
              for n, (q, cq, h) in enumerate(zip(qs, cqs, heads))]
        if diagonal:
            ss = [jnp.where(causal, s, NEG_INF) for s in ss]
        m_new = [jnp.maximum(m, jnp.max(s, axis=-1, keepdims=True)) for (m, _), s in zip(carry, ss)]
        ps = [jnp.exp(s - m).astype(BF16) for s, m in zip(ss, m_new)]
        pv = [jnp.dot(p, jnp.concatenate([v_ref[0, pl.ds(k0, t), h], ones], axis=1),
                      preferred_element_type=F32) for p, h in zip(ps, heads)]
        return tuple((mn, jnp.exp(m - mn) * acc + x) for (m, acc), mn, x in zip(carry, m_new, pv))

    init = tuple((jnp.full((t, 1), NEG_INF, F32), jnp.zeros((t, 2 * dh), F32)) for _ in heads)
    carry = lax.fori_loop(0, i, lambda j, c: step(j, c, False), init)
    carry = step(i, carry, True)
    for (_, acc), h in zip(carry, heads):
        o_ref[0, :, h] = (acc[:, :dh] / acc[:, dh:]).astype(o_ref.dtype)


def fox_attention(proj, cum_q, cum_k):
    B, S = proj.shape[:2]
    H, dh, t, nh = FOX_HEADS, FOX_HEAD_DIM, FOX_TILE, FOX_HEADS_PER_STEP
    G = H // nh
    return pl.pallas_call(
        _fox_body,
        out_shape=jax.ShapeDtypeStruct((B, S, H * dh), BF16),
        grid=(B, G, S // t),
        in_specs=[
            pl.BlockSpec((1, t, nh * dh), lambda b, h, i: (b, i, h)),
            pl.BlockSpec((1, S, nh * dh), lambda b, h, i: (b, 0, G + h)),
            pl.BlockSpec((1, S, nh * dh), lambda b, h, i: (b, 0, 2 * G + h)),
            pl.BlockSpec((1, t, LANE), lambda b, h, i: (b, i, 0)),
            pl.BlockSpec((1, nh, 1, S), lambda b, h, i: (b, h, 0, 0)),
        ],
        out_specs=pl.BlockSpec((1, t, nh * dh), lambda b, h, i: (b, i, h)),
        compiler_params=_cparams("parallel", "parallel", "arbitrary"),
        name="fox_attention",
    )(proj, proj, proj, cum_q, cum_k)


RET_HEADS_PER_STEP = 2


def _ret_body(lg_ref, q_ref, k_ref, v_ref, g_ref, cos_ref, sin_ref, gn_ref, o_ref, r_ref):
    C, dk, dv = RET_CHUNK, RET_QK_DIM, RET_V_DIM
    half = dk // 2
    nh = r_ref.shape[0]
    qk = [slice(dk * n, dk * (n + 1)) for n in range(nh)]
    vv = [slice(dv * n, dv * (n + 1)) for n in range(nh)]
    lgs = [lg_ref[pl.program_id(1) * nh + n] for n in range(nh)]
    ii = lax.broadcasted_iota(jnp.int32, (C, C), 0)
    jj = lax.broadcasted_iota(jnp.int32, (C, C), 1)
    ti = lax.broadcasted_iota(jnp.int32, (C, 1), 0).astype(F32)
    decay_mask = [jnp.where(ii >= jj, jnp.exp((ii - jj).astype(F32) * lg), 0.0) for lg in lgs]
    q_scale = [jnp.exp((ti + 1.0) * lg) for lg in lgs]
    k_scale = [jnp.exp((C - 1.0 - ti) * lg) for lg in lgs]
    chunk_decay = [jnp.exp(jnp.full((1, 1), C, F32) * lg) for lg in lgs]
    r_ref[...] = jnp.zeros_like(r_ref)

    def rot(x, cos, sin):
        x1, x2 = x[:, :half], x[:, half:]
        return jnp.concatenate([x1 * cos - x2 * sin, x2 * cos + x1 * sin], axis=1)

    def chunk(c, carry):
        sl = pl.ds(pl.multiple_of(c * C, C), C)
        cos, sin = cos_ref[sl, :], sin_ref[sl, :]
        qs = [rot(q_ref[0, sl, h].astype(F32), cos, sin).astype(BF16) for h in qk]
        ks = [rot(k_ref[0, sl, h].astype(F32), cos, sin) * (dk ** -0.5) for h in qk]
        vs = [v_ref[0, sl, h] for h in vv]
        inner = [lax.dot_general(q, k.astype(BF16), (((1,), (1,)), ((), ())), preferred_element_type=F32) * dm
                 for q, k, dm in zip(qs, ks, decay_mask)]
        states = [r_ref[n] for n in range(nh)]
        cross = [jnp.dot(q, st.astype(BF16), preferred_element_type=F32) * sc
                 for q, st, sc in zip(qs, states, q_scale)]
        os_ = [jnp.dot(x.astype(BF16), v, preferred_element_type=F32) + cr for x, v, cr in zip(inner, vs, cross)]
        upd = [lax.dot_general((k * sc).astype(BF16), v, (((0,), (0,)), ((), ())), preferred_element_type=F32)
               for k, sc, v in zip(ks, k_scale, vs)]
        for n in range(nh):
            r_ref[n] = states[n] * chunk_decay[n] + upd[n]
        for o, h in zip(os_, vv):
            d = o - jnp.mean(o, axis=-1, keepdims=True)
            on = d * lax.rsqrt(jnp.mean(d * d, axis=-1, keepdims=True) + RET_GN_EPS)
            g = g_ref[0, sl, h].astype(F32)
            o_ref[0, sl, h] = ((g * jax.nn.sigmoid(g)) * (on * gn_ref[:, h])).astype(o_ref.dtype)
        return carry

    lax.fori_loop(0, q_ref.shape[1] // C, chunk, 0)


def retention_core(proj, gn_g):
    B, S = proj.shape[:2]
    H, dk, dv, nh = RET_HEADS, RET_QK_DIM, RET_V_DIM, RET_HEADS_PER_STEP
    G = H // nh
    pos = jnp.arange(S, dtype=F32)
    inv_freq = RET_THETA ** (-jnp.arange(dk // 2, dtype=F32) / (dk // 2))
    ang = pos[:, None] * inv_freq
    log_gamma = jnp.log(1.0 - 2.0 ** (-5.0 - jnp.arange(H, dtype=F32)))
    tab = pl.BlockSpec((S, dk // 2), lambda b, h: (0, 0))
    v_first = 2 * H * dk // (nh * dv)
    return pl.pallas_call(
        _ret_body,
        out_shape=jax.ShapeDtypeStruct((B, S, H * dv), BF16),
        grid=(B, G),
        in_specs=[
            pl.BlockSpec(memory_space=pltpu.SMEM),
            pl.BlockSpec((1, S, nh * dk), lambda b, h: (b, 0, h)),
            pl.BlockSpec((1, S, nh * dk), lambda b, h: (b, 0, G + h)),
            pl.BlockSpec((1, S, nh * dv), lambda b, h: (b, 0, v_first + h)),
            pl.BlockSpec((1, S, nh * dv), lambda b, h: (b, 0, v_first + G + h)),
            tab, tab,
            pl.BlockSpec((1, nh * dv), lambda b, h: (0, h)),
        ],
        out_specs=pl.BlockSpec((1, S, nh * dv), lambda b, h: (b, 0, h)),
        scratch_shapes=[pltpu.VMEM((nh, dk, dv), F32)],
        compiler_params=_cparams("parallel", "parallel"),
        name="retention_core",
    )(log_gamma, proj, proj, proj, proj, jnp.cos(ang), jnp.sin(ang), gn_g.reshape(1, H * dv))


def _pad_cols(w, n):
    return jnp.pad(w, ((0, 0), (0, n - w.shape[1])))


NSA_IN_PADDED = 42 * LANE


def _nsa_mixer(h, g, w_in, cmp_pe, cmp_w1, cmp_w2, B, S):
    T = h.shape[0]
    proj = norm_matmul(h, g, _pad_cols(w_in, NSA_IN_PADDED).astype(BF16))
    qkv, kcvc = nsa_prepare(proj, S)
    cmp = nsa_compress(kcvc.reshape(B, S, -1), cmp_pe, cmp_w1.astype(BF16), cmp_w2.astype(BF16))
    qkv = qkv.reshape(B, S, -1)
    o_cmp, sel = nsa_compressed_attention(qkv, cmp)
    o_sel, o_win = nsa_selected_window(sel[..., :NSA_N_SEL], qkv)
    return nsa_combine(proj, o_cmp.reshape(T, -1), o_sel.reshape(T, -1), o_win.reshape(T, -1))


def _rwkv_mixer(h, g, mu, w_rkv, w0, w1, w2, a0, a1, a2, g1, g2, k_k, k_a, r_k, ln_gb, B, S):
    T, D = h.shape
    pad_c = lambda w: _pad_cols(w, LANE)
    pad_r = lambda w: jnp.pad(w, ((0, LANE - w.shape[0]), (0, 0)))
    w_main = jnp.concatenate([w_rkv[0], w_rkv[1], w_rkv[2]], axis=1).astype(BF16)
    w_hidden = jnp.concatenate([pad_c(w1), pad_c(a1), g1], axis=1).astype(BF16)
    proj, hidden = rwkv_mix_project(h, g, mu, w_main, w_hidden, S)
    lw, a, kk, k2, gate = rwkv_gates(proj, hidden, jnp.stack([w0, a0, k_k, k_a]), pad_r(w2).astype(BF16),
                                     pad_r(a2).astype(BF16), g2.astype(BF16))
    as3 = lambda t: t.reshape(B, S, -1)
    y = rwkv_recurrence(as3(proj), as3(lw), as3(k2), as3(kk), as3(a))
    vec = jnp.stack([ln_gb[0], ln_gb[1], r_k.reshape(D)])
    return rwkv_post(y.reshape(T, D), proj, k2, gate, vec)


def _fox_mixer(h, g, w_in, b_f, B, S):
    HD = FOX_HEADS * FOX_HEAD_DIM
    w_qkv = jnp.concatenate([w_in[:, :HD] * FOX_HEAD_DIM ** -0.5, w_in[:, HD:3 * HD]], axis=1)
    proj = norm_matmul(h, g, w_qkv.astype(BF16), out_dtype=BF16)
    z = norm_matmul(h, g, _pad_cols(w_in[:, 3 * HD:], LANE).astype(BF16))
    log_f = jax.nn.log_sigmoid(z.reshape(B, S, LANE) + jnp.pad(b_f, (0, LANE - FOX_HEADS)))
    cum = jnp.cumsum(log_f, axis=1)
    cum_k = jnp.transpose(cum[:, :, :FOX_HEADS], (0, 2, 1))[:, :, None, :]
    return fox_attention(proj.reshape(B, S, 3 * HD), cum, cum_k).reshape(B * S, HD)


def kernel(x, p, norm_g, ffn_w_in, ffn_w_out, ple_w_proj, ple_w_gate, nsa_w_in, nsa_cmp_pe, nsa_cmp_w1, nsa_cmp_w2, nsa_w_out, rwkv_mu, rwkv_w_rkv, rwkv_w0, rwkv_w1, rwkv_w2, rwkv_a0, rwkv_a1, rwkv_a2, rwkv_g1, rwkv_g2, rwkv_k_k, rwkv_k_a, rwkv_r_k, rwkv_ln, rwkv_w_out, fox_w_in, fox_b_f, fox_w_out, ret_w_in, ret_gn_g, ret_w_out):
    B, S, D = x.shape
    T = B * S
    h = x.reshape(T, D)
    bf = lambda w: w.astype(BF16)
    ffn_w_in, ffn_w_out, ple_w_proj, ple_w_gate = bf(ffn_w_in), bf(ffn_w_out), bf(ple_w_proj), bf(ple_w_gate)
    p = p.reshape(DEPTH, T, PLE_DIM)
    for i in range(DEPTH):
        m, j = i % N_MIXERS, i // N_MIXERS
        ng = norm_g[i]
        h = ffn_half_step(h, ng[0], ng[1], ffn_w_in, ffn_w_out, i, 0)
        if m == 0:
            y = _nsa_mixer(h, ng[2], nsa_w_in[j], nsa_cmp_pe[j], nsa_cmp_w1[j], nsa_cmp_w2[j], B, S)
            w_out = nsa_w_out[j]
        elif m == 1:
            y = _rwkv_mixer(h, ng[2], rwkv_mu[j], rwkv_w_rkv[j], rwkv_w0[j], rwkv_w1[j], rwkv_w2[j],
                            rwkv_a0[j], rwkv_a1[j], rwkv_a2[j], rwkv_g1[j], rwkv_g2[j],
                            rwkv_k_k[j], rwkv_k_a[j], rwkv_r_k[j], rwkv_ln[j], B, S)
            w_out = rwkv_w_out[j]
        elif m == 2:
            y = _fox_mixer(h, ng[2], fox_w_in[j], fox_b_f[j], B, S)
            w_out = fox_w_out[j]
        else:
            proj = norm_matmul(h, ng[2], bf(ret_w_in[j]), out_dtype=BF16)
            y = retention_core(proj.reshape(B, S, RET_IN), ret_gn_g[j]).reshape(T, -1)
            w_out = ret_w_out[j]
        h = matmul_norm_residual(y, bf(w_out), ng[3], h)
        h = ffn_half_step(h, ng[4], ng[5], ffn_w_in, ffn_w_out, i, 1)
        h = ple_step(h, p, ng[6], ng[7], ple_w_proj, ple_w_gate, i)
    return h.reshape(B, S, D)
```

```python
import functools

import jax
import jax.numpy as jnp
from jax import lax
from jax.experimental import pallas as pl
from jax.experimental.pallas import tpu as pltpu

D_MODEL = 2048
DEPTH = 4
N_MIXERS = 4
PLE_DIM = 256
D_FF = 5632
RMS_EPS = 1e-6
NEG_INF = -1e30

NSA_HEADS = 16
NSA_KV_GROUPS = 4
NSA_HEAD_DIM = D_MODEL // NSA_HEADS
NSA_CMP_LEN = 32
NSA_CMP_STRIDE = 16
NSA_CMP_HIDDEN = 2 * NSA_HEAD_DIM
NSA_SEL_BLOCK = 64
NSA_N_SEL = 8
NSA_WINDOW = 512
NSA_ROPE_DIM = NSA_HEAD_DIM // 4
ROPE_THETA = 500000.0

RWKV_HEAD_DIM = 64
RWKV_GN_EPS = 64e-5

FOX_HEADS = 16
FOX_HEAD_DIM = D_MODEL // FOX_HEADS

RET_HEADS = 8
RET_QK_DIM = D_MODEL // RET_HEADS
RET_V_DIM = 2 * D_MODEL // RET_HEADS
RET_CHUNK = 128
RET_THETA = 10000.0
RET_GN_EPS = 1e-5
RET_IN = 2 * RET_HEADS * RET_QK_DIM + 2 * RET_HEADS * RET_V_DIM

V7X_VMEM_LIMIT_BYTES = 56 * 1024 * 1024
V7X_VMEM_LIMIT_BYTES_LARGE = 62 * 1024 * 1024
LANE = 128
SUBLANE = 8

F32 = jnp.float32
BF16 = jnp.bfloat16


def _cparams(*sem):
    return pltpu.CompilerParams(dimension_semantics=sem, vmem_limit_bytes=V7X_VMEM_LIMIT_BYTES)


def _rms(x, g):
    return x * lax.rsqrt(jnp.mean(x * x, axis=-1, keepdims=True) + RMS_EPS) * g


def _ffn_body(h_ref, g0_ref, g1_ref, wg_ref, wu_ref, wo_ref, o_ref, xn_ref):
    f = pl.program_id(1)

    @pl.when(f == 0)
    def _():
        xn_ref[...] = _rms(h_ref[...], g0_ref[...]).astype(BF16)
        o_ref[...] = jnp.zeros_like(o_ref)

    xn = xn_ref[...]
    gate = jnp.dot(xn, wg_ref[...], preferred_element_type=F32)
    up = jnp.dot(xn, wu_ref[...], preferred_element_type=F32)
    act = (gate * jax.nn.sigmoid(gate)) * up
    o_ref[...] += jnp.dot(act.astype(BF16), wo_ref[...], preferred_element_type=F32)

    @pl.when(f == pl.num_programs(1) - 1)
    def _():
        o_ref[...] = h_ref[...] + 0.5 * _rms(o_ref[...], g1_ref[...])


def ffn_half_step(h, g0, g1, w_in, w_out, layer, half, *, tm=1024, tf=512):
    T, D = h.shape
    nf = D_FF // tf
    return pl.pallas_call(
        _ffn_body,
        out_shape=jax.ShapeDtypeStruct((T, D), F32),
        grid=(T // tm, nf),
        in_specs=[
            pl.BlockSpec((tm, D), lambda i, f: (i, 0)),
            pl.BlockSpec((1, D), lambda i, f: (0, 0)),
            pl.BlockSpec((1, D), lambda i, f: (0, 0)),
            pl.BlockSpec((None, None, D, tf), lambda i, f: (layer, half, 0, f)),
            pl.BlockSpec((None, None, D, tf), lambda i, f: (layer, half, 0, f + nf)),
            pl.BlockSpec((None, None, tf, D), lambda i, f: (layer, half, f, 0)),
        ],
        out_specs=pl.BlockSpec((tm, D), lambda i, f: (i, 0)),
        scratch_shapes=[pltpu.VMEM((tm, D), BF16)],
        compiler_params=pltpu.CompilerParams(dimension_semantics=("parallel", "arbitrary"),
                                             vmem_limit_bytes=V7X_VMEM_LIMIT_BYTES_LARGE),
        name="ffn_half_step",
    )(h, g0.reshape(1, D), g1.reshape(1, D), w_in, w_in, w_out)


def _norm_mm_body(x_ref, g_ref, w_ref, o_ref, xn_ref):
    @pl.when(pl.program_id(1) == 0)
    def _():
        xn_ref[...] = _rms(x_ref[...], g_ref[...]).astype(BF16)

    o_ref[...] = jnp.dot(xn_ref[...], w_ref[...], preferred_element_type=F32).astype(o_ref.dtype)


def _col_tile(n, cap=1024):
    best = LANE
    for t in range(LANE, cap + 1, LANE):
        if n % t == 0:
            best = t
    return best


def norm_matmul(x, g, w, *, out_dtype=F32, tm=1024):
    T, K = x.shape
    N = w.shape[1]
    tn = _col_tile(N)
    return pl.pallas_call(
        _norm_mm_body,
        out_shape=jax.ShapeDtypeStruct((T, N), out_dtype),
        grid=(T // tm, N // tn),
        in_specs=[
            pl.BlockSpec((tm, K), lambda i, j: (i, 0)),
            pl.BlockSpec((1, K), lambda i, j: (0, 0)),
            pl.BlockSpec((K, tn), lambda i, j: (0, j)),
        ],
        out_specs=pl.BlockSpec((tm, tn), lambda i, j: (i, j)),
        scratch_shapes=[pltpu.VMEM((tm, K), BF16)],
        compiler_params=_cparams("parallel", "arbitrary"),
        name="norm_matmul",
    )(x, g.reshape(1, K), w)


def _mm_res_body(y_ref, w_ref, g_ref, h_ref, o_ref, acc_ref):
    k = pl.program_id(1)

    @pl.when(k == 0)
    def _():
        acc_ref[...] = jnp.zeros_like(acc_ref)

    acc_ref[...] += jnp.dot(y_ref[...].astype(BF16), w_ref[...], preferred_element_type=F32)

    @pl.when(k == pl.num_programs(1) - 1)
    def _():
        o_ref[...] = h_ref[...] + _rms(acc_ref[...], g_ref[...])


def _mm_res_single_body(y_ref, w_ref, g_ref, h_ref, o_ref):
    acc = jnp.dot(y_ref[...].astype(BF16), w_ref[...], preferred_element_type=F32)
    o_ref[...] = h_ref[...] + _rms(acc, g_ref[...])


def matmul_norm_residual(y, w, g, h, *, tm=512, tk=2048):
    T, K = y.shape
    D = w.shape[1]
    if K == tk:
        return pl.pallas_call(
            _mm_res_single_body,
            out_shape=jax.ShapeDtypeStruct((T, D), F32),
            grid=(T // tm,),
            in_specs=[
                pl.BlockSpec((tm, K), lambda i: (i, 0)),
                pl.BlockSpec((K, D), lambda i: (0, 0)),
                pl.BlockSpec((1, D), lambda i: (0, 0)),
                pl.BlockSpec((tm, D), lambda i: (i, 0)),
            ],
            out_specs=pl.BlockSpec((tm, D), lambda i: (i, 0)),
            compiler_params=_cparams("parallel"),
            name="matmul_norm_residual",
        )(y, w, g.reshape(1, D), h)
    return pl.pallas_call(
        _mm_res_body,
        out_shape=jax.ShapeDtypeStruct((T, D), F32),
        grid=(T // tm, K // tk),
        in_specs=[
            pl.BlockSpec((tm, tk), lambda i, k: (i, k)),
            pl.BlockSpec((tk, D), lambda i, k: (k, 0)),
            pl.BlockSpec((1, D), lambda i, k: (0, 0)),
            pl.BlockSpec((tm, D), lambda i, k: (i, 0)),
        ],
        out_specs=pl.BlockSpec((tm, D), lambda i, k: (i, 0)),
        scratch_shapes=[pltpu.VMEM((tm, D), F32)],
        compiler_params=_cparams("parallel", "arbitrary"),
        name="matmul_norm_residual",
    )(y, w, g.reshape(1, D), h)


PLE_SUBTILES = 2


def _ple_body(h_ref, p_ref, g6_ref, g7_ref, wp_ref, wg_ref, o_ref):
    rows = h_ref.shape[0] // PLE_SUBTILES
    parts = [slice(n * rows, (n + 1) * rows) for n in range(PLE_SUBTILES)]
    xns = [_rms(h_ref[r, :], g6_ref[...]).astype(BF16) for r in parts]
    zs = [jnp.dot(xn, wg_ref[...], preferred_element_type=F32) for xn in xns]
    es = [jnp.dot(p_ref[r, :].astype(BF16), wp_ref[...], preferred_element_type=F32) for r in parts]
    for r, z, e in zip(parts, zs, es):
        o_ref[r, :] = h_ref[r, :] + _rms(e * jax.nn.sigmoid(z), g7_ref[...])


def ple_step(h, p, g6, g7, wp, wg, layer, *, tm=512):
    T, D = h.shape
    P = p.shape[2]
    return pl.pallas_call(
        _ple_body,
        out_shape=jax.ShapeDtypeStruct((T, D), F32),
        grid=(T // tm,),
        in_specs=[
            pl.BlockSpec((tm, D), lambda i: (i, 0)),
            pl.BlockSpec((None, tm, P), lambda i: (layer, i, 0)),
            pl.BlockSpec((1, D), lambda i: (0, 0)),
            pl.BlockSpec((1, D), lambda i: (0, 0)),
            pl.BlockSpec((None, P, D), lambda i: (layer, 0, 0)),
            pl.BlockSpec((None, D, D), lambda i: (layer, 0, 0)),
        ],
        out_specs=pl.BlockSpec((tm, D), lambda i: (i, 0)),
        compiler_params=_cparams("parallel"),
        name="ple_step",
    )(h, p, g6.reshape(1, D), g7.reshape(1, D), wp, wg)


RWKV_CHUNK = 64
RWKV_PAIRS_PER_STEP = 16
RWKV_TIME_BLOCK = 256


def _rwkv_body(r_ref, lw_ref, k_ref, v_ref, kk_ref, a_ref, y_ref, s_ref):
    C = RWKV_CHUNK
    N = RWKV_HEAD_DIM
    lane = lax.broadcasted_iota(jnp.int32, (C, 2 * N), 1)
    row = lax.broadcasted_iota(jnp.int32, (C, 2 * N), 0)
    first_head = lane < N
    ri = lax.broadcasted_iota(jnp.int32, (2 * C, 2 * C), 0)
    ci = lax.broadcasted_iota(jnp.int32, (2 * C, 2 * C), 1)
    strict = ri > ci
    incl = ri >= ci
    eye = jnp.where(ri == ci, 1.0, 0.0).astype(F32)
    corner = [((ri >> (lvl + 1)) == (ci >> (lvl + 1))) & ((ri & (1 << lvl)) != 0) & ((ci & (1 << lvl)) == 0)
              for lvl in range(C.bit_length() - 1)]

    def stack(x):
        return jnp.concatenate([jnp.where(first_head, x, 0.0), jnp.where(first_head, 0.0, x)], axis=0)

    def nt(a, b):
        return lax.dot_general(a.astype(BF16), b.astype(BF16), (((1,), (1,)), ((), ())),
                               preferred_element_type=F32)

    def nn(a, b):
        return jnp.dot(a.astype(BF16), b.astype(BF16), preferred_element_type=F32)

    def tn(a, b):
        return lax.dot_general(a.astype(BF16), b.astype(BF16), (((0,), (0,)), ((), ())),
                               preferred_element_type=F32)

    n_pairs = r_ref.shape[2] // (2 * N)
    lanes = [slice(2 * N * j, 2 * N * (j + 1)) for j in range(n_pairs)]

    def prep(sl, ln):
        r, lw, k, v, kk, a = (ref[0, sl, ln] for ref in (r_ref, lw_ref, k_ref, v_ref, kk_ref, a_ref))
        cl = lw
        for sh in (1, 2, 4, 8, 16, 32):
            cl = cl + jnp.where(row >= sh, pltpu.roll(cl, sh, axis=0), 0.0)
        mid = cl[C // 2 - 1:C // 2, :]
        last = cl[C - 1:C, :]
        e_neg = jnp.exp(mid - cl)
        e_end = jnp.exp(last - mid)
        b_til = stack(kk * a * e_neg)
        k_til = stack(k * e_neg)
        return dict(
            a_bar=stack(-kk * jnp.exp(cl - lw - mid)), r_bar=stack(r * jnp.exp(cl - mid)),
            b_til=b_til, k_til=k_til, v_st=stack(v), e_mid=jnp.exp(mid), w_tot=jnp.exp(last),
            bk_end=jnp.concatenate([b_til * e_end, k_til * e_end], axis=0))

    def chunk(c, states):
        sl = pl.ds(pl.multiple_of(c * C, C), C)
        ps = [prep(sl, ln) for ln in lanes]
        gs = [nt(jnp.concatenate([p["a_bar"], p["r_bar"]], axis=0),
                 jnp.concatenate([p["b_til"], p["k_til"]], axis=0)) for p in ps]
        a_ab = [jnp.where(strict, g[:2 * C, :2 * C], 0.0) for g in gs]
        a_ak = [jnp.where(strict, g[:2 * C, 2 * C:], 0.0) for g in gs]
        a_rb = [jnp.where(incl, g[2 * C:, :2 * C], 0.0) for g in gs]
        a_rk = [jnp.where(incl, g[2 * C:, 2 * C:], 0.0) for g in gs]
        inv = [eye + jnp.where(corner[0], x, 0.0) for x in a_ab]
        for lvl in range(1, len(corner)):
            tmp = [nn(jnp.where(corner[lvl], x, 0.0), t) for x, t in zip(a_ab, inv)]
            inv = [t + nn(t, x) for t, x in zip(inv, tmp)]
        s_mid = [st * p["e_mid"] for st, p in zip(states, ps)]
        rhs = [nt(p["a_bar"], sm) + nn(ak, p["v_st"]) for p, sm, ak in zip(ps, s_mid, a_ak)]
        us = [nn(t, x) for t, x in zip(inv, rhs)]
        ys = [nt(p["r_bar"], sm) + nn(rb, u) + nn(rk, p["v_st"])
              for p, sm, rb, rk, u in zip(ps, s_mid, a_rb, a_rk, us)]
        for ln, y in zip(lanes, ys):
            y_ref[0, sl, ln] = y[:C] + y[C:]
        return tuple(st * p["w_tot"] + tn(jnp.concatenate([u, p["v_st"]], axis=0), p["bk_end"])
                     for st, p, u in zip(states, ps, us))

    @pl.when(pl.program_id(2) == 0)
    def _():
        s_ref[...] = jnp.zeros_like(s_ref)

    states = lax.fori_loop(0, r_ref.shape[1] // C, chunk, tuple(s_ref[j] for j in range(n_pairs)))
    for j, st in enumerate(states):
        s_ref[j] = st


def rwkv_recurrence(proj, lw, k, kk, a):
    B, S, D = lw.shape
    lanes = 2 * RWKV_HEAD_DIM * RWKV_PAIRS_PER_STEP
    ts = min(S, RWKV_TIME_BLOCK)
    spec = pl.BlockSpec((1, ts, lanes), lambda b, j, t: (b, t, j))
    v_spec = pl.BlockSpec((1, ts, lanes), lambda b, j, t: (b, t, 2 * D // lanes + j))
    return pl.pallas_call(
        _rwkv_body,
        out_shape=jax.ShapeDtypeStruct((B, S, D), F32),
        grid=(B, D // lanes, S // ts),
        in_specs=[spec, spec, spec, v_spec, spec, spec],
        out_specs=spec,
        scratch_shapes=[pltpu.VMEM((RWKV_PAIRS_PER_STEP, 2 * RWKV_HEAD_DIM, 2 * RWKV_HEAD_DIM), F32)],
        compiler_params=_cparams("parallel", "parallel", "arbitrary"),
        name="rwkv_recurrence",
    )(proj, lw, k, proj, kk, a)


RWKV_PROJ_TN = 1024
RWKV_HIDDEN = 4 * LANE


RWKV_MIX_ROWS = 64


def _rwkv_mixes(first_of_seq, h_ref, hp_ref, g_ref, mu_ref, xm_ref, mixes):
    rows = RWKV_MIX_ROWS
    g = g_ref[...]
    before_tile = jnp.where(first_of_seq, 0.0, _rms(hp_ref[SUBLANE - 1:SUBLANE, :], g))
    row = lax.broadcasted_iota(jnp.int32, (rows, h_ref.shape[1]), 0)

    def chunk(c, prev):
        sl = pl.ds(pl.multiple_of(c * rows, rows), rows)
        u = _rms(h_ref[sl, :], g)
        xx = jnp.where(row == 0, prev, pltpu.roll(u, 1, axis=0)) - u
        for n, m in enumerate(mixes):
            xm_ref[n, sl, :] = (u + xx * mu_ref[m:m + 1, :]).astype(BF16)
        return u[rows - 1:rows, :]

    lax.fori_loop(0, h_ref.shape[0] // rows, chunk, before_tile)


def _rwkv_proj_body(tiles_per_seq, h_ref, hp_ref, g_ref, mu_ref, w_ref, o_ref, xm_ref):
    i, j = pl.program_id(0), pl.program_id(1)

    @pl.when(j == 0)
    def _():
        _rwkv_mixes(lax.rem(i, tiles_per_seq) == 0, h_ref, hp_ref, g_ref, mu_ref, xm_ref, (0, 1, 2))

    o_ref[...] = jnp.dot(xm_ref[j // (pl.num_programs(1) // 3)], w_ref[...], preferred_element_type=F32)


def _rwkv_hidden_body(tiles_per_seq, h_ref, hp_ref, g_ref, mu_ref, w_ref, o_ref):
    g = g_ref[...]
    u = _rms(h_ref[...], g)
    before_tile = jnp.where(lax.rem(pl.program_id(0), tiles_per_seq) == 0, 0.0,
                            _rms(hp_ref[SUBLANE - 1:SUBLANE, :], g))
    row = lax.broadcasted_iota(jnp.int32, u.shape, 0)
    xx = jnp.where(row == 0, before_tile, pltpu.roll(u, 1, axis=0)) - u
    mix = lambda c: (u + xx * mu_ref[c:c + 1, :]).astype(BF16)
    o_ref[:, :LANE] = jnp.tanh(jnp.dot(mix(3), w_ref[:, :LANE], preferred_element_type=F32))
    o_ref[:, LANE:2 * LANE] = jnp.dot(mix(4), w_ref[:, LANE:2 * LANE], preferred_element_type=F32)
    o_ref[:, 2 * LANE:] = jax.nn.sigmoid(jnp.dot(mix(5), w_ref[:, 2 * LANE:], preferred_element_type=F32))


def rwkv_mix_project(h, g, mu, w_rkv, w_hidden, seq_len, *, tm=1024, tm_hidden=512):
    T, D = h.shape
    tn = RWKV_PROJ_TN

    def prev_rows(rows):
        return lambda i, *_: (jnp.maximum(i * (rows // SUBLANE) - 1, 0), 0)

    rkv = pl.pallas_call(
        functools.partial(_rwkv_proj_body, seq_len // tm),
        out_shape=jax.ShapeDtypeStruct((T, 3 * D), F32),
        grid=(T // tm, 3 * D // tn),
        in_specs=[
            pl.BlockSpec((tm, D), lambda i, j: (i, 0)),
            pl.BlockSpec((SUBLANE, D), prev_rows(tm)),
            pl.BlockSpec((1, D), lambda i, j: (0, 0)),
            pl.BlockSpec((6, D), lambda i, j: (0, 0)),
            pl.BlockSpec((D, tn), lambda i, j: (0, j)),
        ],
        out_specs=pl.BlockSpec((tm, tn), lambda i, j: (i, j)),
        scratch_shapes=[pltpu.VMEM((3, tm, D), BF16)],
        compiler_params=_cparams("parallel", "arbitrary"),
        name="rwkv_mix_project",
    )(h, h, g.reshape(1, D), mu, w_rkv)
    tm = tm_hidden
    hidden = pl.pallas_call(
        functools.partial(_rwkv_hidden_body, seq_len // tm),
        out_shape=jax.ShapeDtypeStruct((T, RWKV_HIDDEN), F32),
        grid=(T // tm,),
        in_specs=[
            pl.BlockSpec((tm, D), lambda i: (i, 0)),
            pl.BlockSpec((SUBLANE, D), prev_rows(tm)),
            pl.BlockSpec((1, D), lambda i: (0, 0)),
            pl.BlockSpec((6, D), lambda i: (0, 0)),
            pl.BlockSpec((D, RWKV_HIDDEN), lambda i: (0, 0)),
        ],
        out_specs=pl.BlockSpec((tm, RWKV_HIDDEN), lambda i: (i, 0)),
        compiler_params=_cparams("parallel"),
        name="rwkv_mix_hidden",
    )(h, h, g.reshape(1, D), mu, w_hidden)
    return rkv, hidden


def _head_sum(x, ones_bd):
    hi = x.astype(BF16)
    lo = (x - hi.astype(F32)).astype(BF16)
    return (jnp.dot(hi, ones_bd, preferred_element_type=F32) + jnp.dot(lo, ones_bd, preferred_element_type=F32))


def _head_ones():
    shift = RWKV_HEAD_DIM.bit_length() - 1
    r = lax.broadcasted_iota(jnp.int32, (LANE, LANE), 0) >> shift
    c = lax.broadcasted_iota(jnp.int32, (LANE, LANE), 1) >> shift
    return jnp.where(r == c, 1.0, 0.0).astype(BF16)


def _rwkv_gates_body(k_ref, hid_ref, vec_ref, w2_ref, a2_ref, g2_ref, lw_ref, a_ref, kk_ref, k2_ref, gate_ref):
    hid = hid_ref[...].astype(BF16)
    z = vec_ref[0:1, :] + jnp.dot(hid[:, :LANE], w2_ref[...], preferred_element_type=F32)
    softplus = jnp.maximum(-z, 0.0) + jnp.log(1.0 + jnp.exp(-jnp.abs(z)))
    lw_ref[...] = -jnp.exp(-softplus - 0.5)
    a = jax.nn.sigmoid(vec_ref[1:2, :] + jnp.dot(hid[:, LANE:2 * LANE], a2_ref[...], preferred_element_type=F32))
    a_ref[...] = a
    gate_ref[...] = jnp.dot(hid[:, 2 * LANE:], g2_ref[...], preferred_element_type=F32)
    k = k_ref[...]
    k2_ref[...] = k * (1.0 + (a - 1.0) * vec_ref[3:4, :])
    kk = k * vec_ref[2:3, :]
    ones_bd = _head_ones()
    for c in range(k.shape[1] // LANE):
        blk = kk[:, c * LANE:(c + 1) * LANE]
        norm = jnp.sqrt(_head_sum(blk * blk, ones_bd))
        kk_ref[:, c * LANE:(c + 1) * LANE] = blk / jnp.maximum(norm, 1e-12)


def rwkv_gates(proj, hidden, vec, w2, a2, g2, *, tm=256):
    T = proj.shape[0]
    D = w2.shape[1]
    row = pl.BlockSpec((tm, D), lambda i: (i, 0))
    full = lambda w: pl.BlockSpec(w.shape, lambda i: (0, 0))
    return pl.pallas_call(
        _rwkv_gates_body,
        out_shape=tuple(jax.ShapeDtypeStruct((T, D), F32) for _ in range(5)),
        grid=(T // tm,),
        in_specs=[
            pl.BlockSpec((tm, D), lambda i: (i, 1)),
            pl.BlockSpec((tm, RWKV_HIDDEN), lambda i: (i, 0)),
            full(vec), full(w2), full(a2), full(g2),
        ],
        out_specs=(row,) * 5,
        compiler_params=_cparams("parallel"),
        name="rwkv_gates",
    )(proj, hidden, vec, w2, a2, g2)


def _rwkv_post_body(y_ref, r_ref, v_ref, k2_ref, gate_ref, vec_ref, o_ref):
    ones_bd = _head_ones()
    inv_n = 1.0 / RWKV_HEAD_DIM
    for c in range(y_ref.shape[1] // LANE):
        sl = slice(c * LANE, (c + 1) * LANE)
        y = y_ref[:, sl]
        d = y - _head_sum(y, ones_bd) * inv_n
        yn = d * lax.rsqrt(_head_sum(d * d, ones_bd) * inv_n + RWKV_GN_EPS)
        bonus = _head_sum(r_ref[:, sl] * k2_ref[:, sl] * vec_ref[2:3, sl], ones_bd) * v_ref[:, sl]
        o_ref[:, sl] = ((yn * vec_ref[0:1, sl] + vec_ref[1:2, sl] + bonus) * gate_ref[:, sl]).astype(o_ref.dtype)


def rwkv_post(y, proj, k2, gate, vec, *, tm=256):
    T, D = y.shape
    row = pl.BlockSpec((tm, D), lambda i: (i, 0))
    return pl.pallas_call(
        _rwkv_post_body,
        out_shape=jax.ShapeDtypeStruct((T, D), BF16),
        grid=(T // tm,),
        in_specs=[row, row, pl.BlockSpec((tm, D), lambda i: (i, 2)), row, row,
                  pl.BlockSpec(vec.shape, lambda i: (0, 0))],
        out_specs=row,
        compiler_params=_cparams("parallel"),
        name="rwkv_post",
    )(y, proj, proj, k2, gate, vec)


NSA_QBLOCKS_PER_ITER = 4


def _nsa_sel_win_body(sel_ref, q_ref, ks_ref, vs_ref, kw_ref, vw_ref, osel_ref, owin_ref):
    blk, n_sel, W = NSA_SEL_BLOCK, NSA_N_SEL, NSA_WINDOW
    R, dh = NSA_HEADS // NSA_KV_GROUPS, NSA_HEAD_DIM
    S = q_ref.shape[1]
    n_blk = S // blk
    base = (pl.program_id(0) * NSA_KV_GROUPS + pl.program_id(1)) * (n_blk * n_sel)
    row_t = lax.broadcasted_iota(jnp.int32, (R * blk, 1), 0) & (blk - 1)
    col_s = lax.broadcasted_iota(jnp.int32, (1, n_sel * blk), 1)
    col_w = lax.broadcasted_iota(jnp.int32, (1, W + blk), 1)
    ones_s = jnp.ones((n_sel * blk, dh), BF16)
    ones_w = jnp.ones((W + blk, dh), BF16)

    def nt(a, b):
        return lax.dot_general(a, b, (((1,), (1,)), ((), ())), preferred_element_type=F32)

    def probs(s, mask):
        s = jnp.where(mask, s, NEG_INF)
        return jnp.exp(s - jnp.max(s, axis=-1, keepdims=True)).astype(BF16)

    def qblocks(it, carry):
        ids = [it * NSA_QBLOCKS_PER_ITER + n for n in range(NSA_QBLOCKS_PER_ITER)]
        q0s = [pl.multiple_of(i * blk, blk) for i in ids]
        w0s = [pl.multiple_of(jnp.maximum(q0 - W, 0), blk) for q0 in q0s]
        starts = [[sel_ref[base + i * n_sel + j] * blk for j in range(n_sel)] for i in ids]
        qss = []
        for q0 in q0s:
            qb = q_ref[0, pl.ds(q0, blk), :]
            qss.append(jnp.concatenate([qb[:, r * dh:(r + 1) * dh] for r in range(R)], axis=0))
        s_sel = [nt(qs, jnp.concatenate([ks_ref[0, pl.ds(pl.multiple_of(x, blk), blk), :] for x in st], axis=0))
                 for qs, st in zip(qss, starts)]
        s_win = [nt(qs, kw_ref[0, pl.ds(w0, W + blk), :]) for qs, w0 in zip(qss, w0s)]
        p_sel, p_win = [], []
        for q0, w0, st, ss, sw in zip(q0s, w0s, starts, s_sel, s_win):
            qpos = q0 + row_t
            tok = col_s & (blk - 1)
            for j in range(n_sel):
                tok = tok + jnp.where((col_s >> (blk.bit_length() - 1)) == j, st[j], 0)
            dist = qpos - (w0 + col_w)
            p_sel.append(probs(ss, tok <= qpos))
            p_win.append(probs(sw, (dist >= 0) & (dist < W)))
        o_sel = [jnp.dot(p, jnp.concatenate(
            [jnp.concatenate([vs_ref[0, pl.ds(pl.multiple_of(x, blk), blk), :] for x in st], axis=0), ones_s],
            axis=1), preferred_element_type=F32) for p, st in zip(p_sel, starts)]
        o_win = [jnp.dot(p, jnp.concatenate([vw_ref[0, pl.ds(w0, W + blk), :], ones_w], axis=1),
                         preferred_element_type=F32) for p, w0 in zip(p_win, w0s)]
        for q0, os_, ow in zip(q0s, o_sel, o_win):
            os_ = (os_[:, :dh] / os_[:, dh:]).astype(osel_ref.dtype)
            ow = (ow[:, :dh] / ow[:, dh:]).astype(owin_ref.dtype)
            for r in range(R):
                osel_ref[0, pl.ds(q0, blk), r * dh:(r + 1) * dh] = os_[r * blk:(r + 1) * blk]
                owin_ref[0, pl.ds(q0, blk), r * dh:(r + 1) * dh] = ow[r * blk:(r + 1) * blk]
        return carry

    lax.fori_loop(0, n_blk // NSA_QBLOCKS_PER_ITER, qblocks, 0)


def nsa_selected_window(sel, qkv):
    B, S = qkv.shape[:2]
    G, dh, HD = NSA_KV_GROUPS, NSA_HEAD_DIM, NSA_HEADS * NSA_HEAD_DIM
    qspec = pl.BlockSpec((1, S, HD // G), lambda b, g, sel: (b, 0, g))
    kspec = lambda c: pl.BlockSpec((1, S, dh), lambda b, g, sel: (b, 0, HD // dh + c * G + g))
    return pl.pallas_call(
        _nsa_sel_win_body,
        out_shape=(jax.ShapeDtypeStruct((B, S, HD), BF16), jax.ShapeDtypeStruct((B, S, HD), BF16)),
        grid_spec=pltpu.PrefetchScalarGridSpec(
            num_scalar_prefetch=1,
            grid=(B, G),
            in_specs=[qspec, kspec(2), kspec(3), kspec(4), kspec(5)],
            out_specs=(qspec, qspec),
        ),
        compiler_params=_cparams("parallel", "parallel"),
        name="nsa_selected_window",
    )(sel.reshape(-1), qkv, qkv, qkv, qkv, qkv)


def _nsa_rope_tables(S):
    half = NSA_ROPE_DIM // 2
    inv_freq = ROPE_THETA ** (-jnp.arange(half, dtype=F32) / half)
    ang = jnp.arange(S, dtype=F32)[:, None] * inv_freq
    cos, sin = jnp.cos(ang), jnp.sin(ang)
    rest = NSA_HEAD_DIM - NSA_ROPE_DIM
    cos_t = jnp.concatenate([cos, cos, jnp.ones((S, rest), F32)], axis=1)
    sin_t = jnp.concatenate([-sin, sin, jnp.zeros((S, rest), F32)], axis=1)
    return cos_t, sin_t


def _nsa_prep_body(x_ref, cos_ref, sin_ref, o_ref, kv_ref):
    dh, G = NSA_HEAD_DIM, NSA_KV_GROUPS
    half = NSA_ROPE_DIM // 2
    n_q = NSA_HEADS
    cos, sin = cos_ref[...], sin_ref[...]
    low = lax.broadcasted_iota(jnp.int32, cos.shape, 1) < half
    for blk in range(n_q + 6 * G):
        sl = slice(blk * dh, (blk + 1) * dh)
        x = x_ref[:, sl]
        c = (blk - n_q) // G
        if blk < n_q or c in (0, 2, 4):
            swapped = jnp.where(low, pltpu.roll(x, dh - half, axis=1), pltpu.roll(x, half, axis=1))
            x = x * cos + swapped * sin
        if blk < n_q:
            x = x * dh ** -0.5
        elif c in (0, 1):
            kv_ref[:, (blk - n_q) * dh:(blk - n_q + 1) * dh] = x
        o_ref[:, sl] = x.astype(o_ref.dtype)


def nsa_prepare(proj, S, *, tm=256):
    T = proj.shape[0]
    dh, G = NSA_HEAD_DIM, NSA_KV_GROUPS
    n = (NSA_HEADS + 6 * G) * dh
    cos_t, sin_t = _nsa_rope_tables(S)
    tab = pl.BlockSpec((tm, dh), lambda i: (i % (S // tm), 0))
    return pl.pallas_call(
        _nsa_prep_body,
        out_shape=(jax.ShapeDtypeStruct((T, n), BF16), jax.ShapeDtypeStruct((T, 2 * G * dh), F32)),
        grid=(T // tm,),
        in_specs=[pl.BlockSpec((tm, n), lambda i: (i, 0)), tab, tab],
        out_specs=(pl.BlockSpec((tm, n), lambda i: (i, 0)), pl.BlockSpec((tm, 2 * G * dh), lambda i: (i, 0))),
        compiler_params=_cparams("parallel"),
        name="nsa_prepare",
    )(proj, cos_t, sin_t)


def _nsa_compress_body(kc_ref, vc_ref, pe_ref, w1_ref, w2_ref, o_ref):
    L, stride = NSA_CMP_LEN, NSA_CMP_STRIDE
    n_grp = kc_ref.shape[1] // stride
    for c, ref in enumerate((kc_ref, vc_ref)):
        first = jnp.zeros((n_grp, NSA_CMP_HIDDEN), F32)
        second = jnp.zeros((n_grp, NSA_CMP_HIDDEN), F32)
        for l in range(stride):
            rows = ref[0, pl.ds(l, n_grp, stride=stride), :]
            first += jnp.dot((rows + pe_ref[c, l:l + 1, :]).astype(BF16), w1_ref[c, l],
                             preferred_element_type=F32)
            second += jnp.dot((rows + pe_ref[c, stride + l:stride + l + 1, :]).astype(BF16),
                              w1_ref[c, stride + l], preferred_element_type=F32)
        hid = jax.nn.gelu(first + pltpu.roll(second, n_grp - 1, axis=0))
        o_ref[0, 0, c] = jnp.dot(hid.astype(BF16), w2_ref[c], preferred_element_type=F32)


def nsa_compress(kcvc, pe, w1, w2):
    B, S = kcvc.shape[:2]
    G, dh = NSA_KV_GROUPS, NSA_HEAD_DIM
    full = lambda w: pl.BlockSpec(w.shape, lambda b, g: (0,) * w.ndim)
    return pl.pallas_call(
        _nsa_compress_body,
        out_shape=jax.ShapeDtypeStruct((B, G, 2, S // NSA_CMP_STRIDE, dh), F32),
        grid=(B, G),
        in_specs=[pl.BlockSpec((1, S, dh), lambda b, g: (b, 0, g)),
                  pl.BlockSpec((1, S, dh), lambda b, g: (b, 0, G + g)),
                  full(pe), full(w1), full(w2)],
        out_specs=pl.BlockSpec((1, 1, 2, S // NSA_CMP_STRIDE, dh), lambda b, g: (b, g, 0, 0, 0)),
        compiler_params=_cparams("parallel", "parallel"),
        name="nsa_compress",
    )(kcvc, kcvc, pe, w1, w2)


NSA_CMP_TILE = 512


def _nsa_cmp_body(q_ref, cmp_ref, o_ref, sel_ref):
    dh, blk = NSA_HEAD_DIM, NSA_SEL_BLOCK
    R = NSA_HEADS // NSA_KV_GROUPS
    tq = q_ref.shape[1]
    n_cmp = cmp_ref.shape[3]
    nq = tq // blk
    i = pl.program_id(2)
    k_cmp = cmp_ref[0, 0, 0].astype(BF16)
    v_cmp = cmp_ref[0, 0, 1].astype(BF16)
    pos = i * tq + lax.broadcasted_iota(jnp.int32, (tq, 1), 0)
    n_id = lax.broadcasted_iota(jnp.int32, (1, n_cmp), 1)
    visible = n_id * NSA_CMP_STRIDE + (NSA_CMP_LEN - 1) <= pos
    start = lax.broadcasted_iota(jnp.int32, (n_cmp, LANE), 0) * NSA_CMP_STRIDE
    bstart = lax.broadcasted_iota(jnp.int32, (n_cmp, LANE), 1) * blk
    overlap = jnp.where((start <= bstart + blk - 1) & (start + NSA_CMP_LEN - 1 >= bstart), 1.0, 0.0).astype(BF16)
    heads = [slice(r * dh, (r + 1) * dh) for r in range(R)]
    ss = [lax.dot_general(q_ref[0, :, h], k_cmp, (((1,), (1,)), ((), ())), preferred_element_type=F32)
          for h in heads]
    ss = [jnp.where(visible, s, NEG_INF) for s in ss]
    es = [jnp.exp(s - jnp.max(s, axis=-1, keepdims=True)) for s in ss]
    ps = [jnp.where(visible, e / jnp.sum(e, axis=-1, keepdims=True), 0.0).astype(BF16) for e in es]
    v_and_overlap = jnp.concatenate([v_cmp, overlap], axis=1)
    outs = [jnp.dot(p, v_and_overlap, preferred_element_type=F32) for p in ps]
    for h, o in zip(heads, outs):
        o_ref[0, :, h] = o[:, :dh].astype(o_ref.dtype)
    imp = sum(o[:, dh:] for o in outs)
    imp = imp.reshape(nq, blk, LANE).sum(axis=1)
    qb = i * nq + lax.broadcasted_iota(jnp.int32, (nq, 1), 0)
    kb = lax.broadcasted_iota(jnp.int32, (nq, LANE), 1)
    forced = (kb == 0) | (kb == qb) | (kb == qb - 1)
    val = jnp.where(forced, jnp.inf, jnp.where(kb <= qb, imp, -jnp.inf))
    kb_f = kb.astype(F32)
    avail = kb >= 0
    picks = jnp.zeros((nq, LANE), F32)
    for t in range(NSA_N_SEL):
        best = jnp.max(jnp.where(avail, val, -jnp.inf), axis=-1, keepdims=True)
        pick = jnp.min(jnp.where(avail & (val == best), kb_f, float(LANE)), axis=-1, keepdims=True)
        picks = jnp.where(kb == t, pick, picks)
        avail = avail & (kb_f != pick)
    sel_ref[0, 0] = picks.astype(jnp.int32)


def nsa_compressed_attention(qkv, cmp):
    B, S = qkv.shape[:2]
    G, dh, HD, tq = NSA_KV_GROUPS, NSA_HEAD_DIM, NSA_HEADS * NSA_HEAD_DIM, NSA_CMP_TILE
    nq = tq // NSA_SEL_BLOCK
    qspec = pl.BlockSpec((1, tq, HD // G), lambda b, g, i: (b, i, g))
    return pl.pallas_call(
        _nsa_cmp_body,
        out_shape=(jax.ShapeDtypeStruct((B, S, HD), BF16),
                   jax.ShapeDtypeStruct((B, G, S // NSA_SEL_BLOCK, LANE), jnp.int32)),
        grid=(B, G, S // tq),
        in_specs=[qspec, pl.BlockSpec((1, 1) + cmp.shape[2:], lambda b, g, i: (b, g, 0, 0, 0))],
        out_specs=(qspec, pl.BlockSpec((1, 1, nq, LANE), lambda b, g, i: (b, g, i, 0))),
        compiler_params=_cparams("parallel", "parallel", "arbitrary"),
        name="nsa_compressed_attention",
    )(qkv, cmp)


def _nsa_combine_body(z_ref, oc_ref, os_ref, ow_ref, o_ref):
    dh = NSA_HEAD_DIM
    gates = jax.nn.sigmoid(z_ref[...])
    for hh in range(NSA_HEADS):
        sl = slice(hh * dh, (hh + 1) * dh)
        o_ref[:, sl] = (gates[:, 3 * hh:3 * hh + 1] * oc_ref[:, sl].astype(F32)
                        + gates[:, 3 * hh + 1:3 * hh + 2] * os_ref[:, sl].astype(F32)
                        + gates[:, 3 * hh + 2:3 * hh + 3] * ow_ref[:, sl].astype(F32)).astype(o_ref.dtype)


def nsa_combine(proj, o_cmp, o_sel, o_win, *, tm=256):
    T, HD = o_cmp.shape
    row = pl.BlockSpec((tm, HD), lambda i: (i, 0))
    gate_blk = (NSA_HEADS + 6 * NSA_KV_GROUPS) * NSA_HEAD_DIM // LANE
    return pl.pallas_call(
        _nsa_combine_body,
        out_shape=jax.ShapeDtypeStruct((T, HD), BF16),
        grid=(T // tm,),
        in_specs=[pl.BlockSpec((tm, LANE), lambda i: (i, gate_blk)), row, row, row],
        out_specs=row,
        compiler_params=_cparams("parallel"),
        name="nsa_combine",
    )(proj, o_cmp, o_sel, o_win)


FOX_TILE = 256
FOX_HEADS_PER_STEP = 8


def _fox_body(q_ref, k_ref, v_ref, cq_ref, ck_ref, o_ref):
    t, dh = FOX_TILE, FOX_HEAD_DIM
    nh = q_ref.shape[2] // dh
    heads = [slice(dh * j, dh * (j + 1)) for j in range(nh)]
    i = pl.program_id(2)
    qs = [q_ref[0, :, h] for h in heads]
    cq_all = cq_ref[0]
    head_lane = lax.broadcasted_iota(jnp.int32, cq_all.shape, 1) - pl.program_id(1) * nh
    cqs = [jnp.sum(jnp.where(head_lane == j, cq_all, 0.0), axis=-1, keepdims=True) for j in range(nh)]
    causal = (lax.broadcasted_iota(jnp.int32, (t, t), 0) >= lax.broadcasted_iota(jnp.int32, (t, t), 1))
    ones = jnp.ones((t, dh), BF16)

    def step(j, carry, diagonal):
        k0 = pl.multiple_of(j * t, t)
        ss = [lax.dot_general(q, k_ref[0, pl.ds(k0, t), h], (((1,), (1,)), ((), ())),
                              preferred_element_type=F32) + (cq - ck_ref[0, n, :, pl.ds(k0, t)])
              for n, (q, cq, h) in enumerate(zip(qs, cqs, heads))]
        if diagonal:
            ss = [jnp.where(causal, s, NEG_INF) for s in ss]
        m_new = [jnp.maximum(m, jnp.max(s, axis=-1, keepdims=True)) for (m, _), s in zip(carry, ss)]
        ps = [jnp.exp(s - m).astype(BF16) for s, m in zip(ss, m_new)]
        pv = [jnp.dot(p, jnp.concatenate([v_ref[0, pl.ds(k0, t), h], ones], axis=1),
                      preferred_element_type=F32) for p, h in zip(ps, heads)]
        return tuple((mn, jnp.exp(m - mn) * acc + x) for (m, acc), mn, x in zip(carry, m_new, pv))

    init = tuple((jnp.full((t, 1), NEG_INF, F32), jnp.zeros((t, 2 * dh), F32)) for _ in heads)
    carry = lax.fori_loop(0, i, lambda j, c: step(j, c, False), init)
    carry = step(i, carry, True)
    for (_, acc), h in zip(carry, heads):
        o_ref[0, :, h] = (acc[:, :dh] / acc[:, dh:]).astype(o_ref.dtype)


def fox_attention(proj, cum_q, cum_k):
    B, S = proj.shape[:2]
    H, dh, t, nh = FOX_HEADS, FOX_HEAD_DIM, FOX_TILE, FOX_HEADS_PER_STEP
    G = H // nh
    return pl.pallas_call(
        _fox_body,
        out_shape=jax.ShapeDtypeStruct((B, S, H * dh), BF16),
        grid=(B, G, S // t),
        in_specs=[
            pl.BlockSpec((1, t, nh * dh), lambda b, h, i: (b, i, h)),
            pl.BlockSpec((1, S, nh * dh), lambda b, h, i: (b, 0, G + h)),
            pl.BlockSpec((1, S, nh * dh), lambda b, h, i: (b, 0, 2 * G + h)),
            pl.BlockSpec((1, t, LANE), lambda b, h, i: (b, i, 0)),
            pl.BlockSpec((1, nh, 1, S), lambda b, h, i: (b, h, 0, 0)),
        ],
        out_specs=pl.BlockSpec((1, t, nh * dh), lambda b, h, i: (b, i, h)),
        compiler_params=_cparams("parallel", "parallel", "arbitrary"),
        name="fox_attention",
    )(proj, proj, proj, cum_q, cum_k)


RET_HEADS_PER_STEP = 2


def _ret_body(lg_ref, q_ref, k_ref, v_ref, g_ref, cos_ref, sin_ref, gn_ref, o_ref, r_ref):
    C, dk, dv = RET_CHUNK, RET_QK_DIM, RET_V_DIM
    half = dk // 2
    nh = r_ref.shape[0]
    qk = [slice(dk * n, dk * (n + 1)) for n in range(nh)]
    vv = [slice(dv * n, dv * (n + 1)) for n in range(nh)]
    lgs = [lg_ref[pl.program_id(1) * nh + n] for n in range(nh)]
    ii = lax.broadcasted_iota(jnp.int32, (C, C), 0)
    jj = lax.broadcasted_iota(jnp.int32, (C, C), 1)
    ti = lax.broadcasted_iota(jnp.int32, (C, 1), 0).astype(F32)
    decay_mask = [jnp.where(ii >= jj, jnp.exp((ii - jj).astype(F32) * lg), 0.0) for lg in lgs]
    q_scale = [jnp.exp((ti + 1.0) * lg) for lg in lgs]
    k_scale = [jnp.exp((C - 1.0 - ti) * lg) for lg in lgs]
    chunk_decay = [jnp.exp(jnp.full((1, 1), C, F32) * lg) for lg in lgs]
    r_ref[...] = jnp.zeros_like(r_ref)

    def rot(x, cos, sin):
        x1, x2 = x[:, :half], x[:, half:]
        return jnp.concatenate([x1 * cos - x2 * sin, x2 * cos + x1 * sin], axis=1)

    def chunk(c, carry):
        sl = pl.ds(pl.multiple_of(c * C, C), C)
        cos, sin = cos_ref[sl, :], sin_ref[sl, :]
        qs = [rot(q_ref[0, sl, h].astype(F32), cos, sin).astype(BF16) for h in qk]
        ks = [rot(k_ref[0, sl, h].astype(F32), cos, sin) * (dk ** -0.5) for h in qk]
        vs = [v_ref[0, sl, h] for h in vv]
        inner = [lax.dot_general(q, k.astype(BF16), (((1,), (1,)), ((), ())), preferred_element_type=F32) * dm
                 for q, k, dm in zip(qs, ks, decay_mask)]
        states = [r_ref[n] for n in range(nh)]
        cross = [jnp.dot(q, st.astype(BF16), preferred_element_type=F32) * sc
                 for q, st, sc in zip(qs, states, q_scale)]
        os_ = [jnp.dot(x.astype(BF16), v, preferred_element_type=F32) + cr for x, v, cr in zip(inner, vs, cross)]
        upd = [lax.dot_general((k * sc).astype(BF16), v, (((0,), (0,)), ((), ())), preferred_element_type=F32)
               for k, sc, v in zip(ks, k_scale, vs)]
        for n in range(nh):
            r_ref[n] = states[n] * chunk_decay[n] + upd[n]
        for o, h in zip(os_, vv):
            d = o - jnp.mean(o, axis=-1, keepdims=True)
            on = d * lax.rsqrt(jnp.mean(d * d, axis=-1, keepdims=True) + RET_GN_EPS)
            g = g_ref[0, sl, h].astype(F32)
            o_ref[0, sl, h] = ((g * jax.nn.sigmoid(g)) * (on * gn_ref[:, h])).astype(o_ref.dtype)
        return carry

    lax.fori_loop(0, q_ref.shape[1] // C, chunk, 0)


def retention_core(proj, gn_g):
    B, S = proj.shape[:2]
    H, dk, dv, nh = RET_HEADS, RET_QK_DIM, RET_V_DIM, RET_HEADS_PER_STEP
    G = H // nh
    pos = jnp.arange(S, dtype=F32)
    inv_freq = RET_THETA ** (-jnp.arange(dk // 2, dtype=F32) / (dk // 2))
    ang = pos[:, None] * inv_freq
    log_gamma = jnp.log(1.0 - 2.0 ** (-5.0 - jnp.arange(H, dtype=F32)))
    tab = pl.BlockSpec((S, dk // 2), lambda b, h: (0, 0))
    v_first = 2 * H * dk // (nh * dv)
    return pl.pallas_call(
        _ret_body,
        out_shape=jax.ShapeDtypeStruct((B, S, H * dv), BF16),
        grid=(B, G),
        in_specs=[
            pl.BlockSpec(memory_space=pltpu.SMEM),
            pl.BlockSpec((1, S, nh * dk), lambda b, h: (b, 0, h)),
            pl.BlockSpec((1, S, nh * dk), lambda b, h: (b, 0, G + h)),
            pl.BlockSpec((1, S, nh * dv), lambda b, h: (b, 0, v_first + h)),
            pl.BlockSpec((1, S, nh * dv), lambda b, h: (b, 0, v_first + G + h)),
            tab, tab,
            pl.BlockSpec((1, nh * dv), lambda b, h: (0, h)),
        ],
        out_specs=pl.BlockSpec((1, S, nh * dv), lambda b, h: (b, 0, h)),
        scratch_shapes=[pltpu.VMEM((nh, dk, dv), F32)],
        compiler_params=_cparams("parallel", "parallel"),
        name="retention_core",
    )(log_gamma, proj, proj, proj, proj, jnp.cos(ang), jnp.sin(ang), gn_g.reshape(1, H * dv))


def _pad_cols(w, n):
    return jnp.pad(w, ((0, 0), (0, n - w.shape[1])))


NSA_IN_PADDED = 42 * LANE


def _nsa_mixer(h, g, w_in, cmp_pe, cmp_w1, cmp_w2, B, S):
    T = h.shape[0]
    proj = norm_matmul(h, g, _pad_cols(w_in, NSA_IN_PADDED).astype(BF16))
    qkv, kcvc = nsa_prepare(proj, S)
    cmp = nsa_compress(kcvc.reshape(B, S, -1), cmp_pe, cmp_w1.astype(BF16), cmp_w2.astype(BF16))
    qkv = qkv.reshape(B, S, -1)
    o_cmp, sel = nsa_compressed_attention(qkv, cmp)
    o_sel, o_win = nsa_selected_window(sel[..., :NSA_N_SEL], qkv)
    return nsa_combine(proj, o_cmp.reshape(T, -1), o_sel.reshape(T, -1), o_win.reshape(T, -1))


def _rwkv_mixer(h, g, mu, w_rkv, w0, w1, w2, a0, a1, a2, g1, g2, k_k, k_a, r_k, ln_gb, B, S):
    T, D = h.shape
    pad_c = lambda w: _pad_cols(w, LANE)
    pad_r = lambda w: jnp.pad(w, ((0, LANE - w.shape[0]), (0, 0)))
    w_main = jnp.concatenate([w_rkv[0], w_rkv[1], w_rkv[2]], axis=1).astype(BF16)
    w_hidden = jnp.concatenate([pad_c(w1), pad_c(a1), g1], axis=1).astype(BF16)
    proj, hidden = rwkv_mix_project(h, g, mu, w_main, w_hidden, S)
    lw, a, kk, k2, gate = rwkv_gates(proj, hidden, jnp.stack([w0, a0, k_k, k_a]), pad_r(w2).astype(BF16),
                                     pad_r(a2).astype(BF16), g2.astype(BF16))
    as3 = lambda t: t.reshape(B, S, -1)
    y = rwkv_recurrence(as3(proj), as3(lw), as3(k2), as3(kk), as3(a))
    vec = jnp.stack([ln_gb[0], ln_gb[1], r_k.reshape(D)])
    return rwkv_post(y.reshape(T, D), proj, k2, gate, vec)


def _fox_mixer(h, g, w_in, b_f, B, S):
    HD = FOX_HEADS * FOX_HEAD_DIM
    w_qkv = jnp.concatenate([w_in[:, :HD] * FOX_HEAD_DIM ** -0.5, w_in[:, HD:3 * HD]], axis=1)
    proj = norm_matmul(h, g, w_qkv.astype(BF16), out_dtype=BF16)
    z = norm_matmul(h, g, _pad_cols(w_in[:, 3 * HD:], LANE).astype(BF16))
    log_f = jax.nn.log_sigmoid(z.reshape(B, S, LANE) + jnp.pad(b_f, (0, LANE - FOX_HEADS)))
    cum = jnp.cumsum(log_f, axis=1)
    cum_k = jnp.transpose(cum[:, :, :FOX_HEADS], (0, 2, 1))[:, :, None, :]
    return fox_attention(proj.reshape(B, S, 3 * HD), cum, cum_k).reshape(B * S, HD)


def kernel(x, p, norm_g, ffn_w_in, ffn_w_out, ple_w_proj, ple_w_gate, nsa_w_in, nsa_cmp_pe, nsa_cmp_w1, nsa_cmp_w2, nsa_w_out, rwkv_mu, rwkv_w_rkv, rwkv_w0, rwkv_w1, rwkv_w2, rwkv_a0, rwkv_a1, rwkv_a2, rwkv_g1, rwkv_g2, rwkv_k_k, rwkv_k_a, rwkv_r_k, rwkv_ln, rwkv_w_out, fox_w_in, fox_b_f, fox_w_out, ret_w_in, ret_gn_g, ret_w_out):
    B, S, D = x.shape
    T = B * S
    h = x.reshape(T, D)
    bf = lambda w: w.astype(BF16)
    ffn_w_in, ffn_w_out, ple_w_proj, ple_w_gate = bf(ffn_w_in), bf(ffn_w_out), bf(ple_w_proj), bf(ple_w_gate)
    p = p.reshape(DEPTH, T, PLE_DIM)
    for i in range(DEPTH):
        m, j = i % N_MIXERS, i // N_MIXERS
        ng = norm_g[i]
        h = ffn_half_step(h, ng[0], ng[1], ffn_w_in, ffn_w_out, i, 0)
        if m == 0:
            y = _nsa_mixer(h, ng[2], nsa_w_in[j], nsa_cmp_pe[j], nsa_cmp_w1[j], nsa_cmp_w2[j], B, S)
            w_out = nsa_w_out[j]
        elif m == 1:
            y = _rwkv_mixer(h, ng[2], rwkv_mu[j], rwkv_w_rkv[j], rwkv_w0[j], rwkv_w1[j], rwkv_w2[j],
                            rwkv_a0[j], rwkv_a1[j], rwkv_a2[j], rwkv_g1[j], rwkv_g2[j],
                            rwkv_k_k[j], rwkv_k_a[j], rwkv_r_k[j], rwkv_ln[j], B, S)
            w_out = rwkv_w_out[j]
        elif m == 2:
            y = _fox_mixer(h, ng[2], fox_w_in[j], fox_b_f[j], B, S)
            w_out = fox_w_out[j]
        else:
            proj = norm_matmul(h, ng[2], bf(ret_w_in[j]), out_dtype=BF16)
            y = retention_core(proj.reshape(B, S, RET_IN), ret_gn_g[j]).reshape(T, -1)
            w_out = ret_w_out[j]
        h = matmul_norm_residual(y, bf(w_out), ng[3], h)
        h = ffn_half_step(h, ng[4], ng[5], ffn_w_in, ffn_w_out, i, 1)
        h = ple_step(h, p, ng[6], ng[7], ple_w_proj, ple_w_gate, i)
    return h.reshape(B, S, D)
```

```python
import functools

import jax
import jax.numpy as jnp
from jax import lax
from jax.experimental import pallas as pl
from jax.experimental.pallas import tpu as pltpu

D_MODEL = 2048
DEPTH = 4
N_MIXERS = 4
PLE_DIM = 256
D_FF = 5632
RMS_EPS = 1e-6
NEG_INF = -1e30

NSA_HEADS = 16
NSA_KV_GROUPS = 4
NSA_HEAD_DIM = D_MODEL // NSA_HEADS
NSA_CMP_LEN = 32
NSA_CMP_STRIDE = 16
NSA_CMP_HIDDEN = 2 * NSA_HEAD_DIM
NSA_SEL_BLOCK = 64
NSA_N_SEL = 8
NSA_WINDOW = 512
NSA_ROPE_DIM = NSA_HEAD_DIM // 4
ROPE_THETA = 500000.0

RWKV_HEAD_DIM = 64
RWKV_GN_EPS = 64e-5

FOX_HEADS = 16
FOX_HEAD_DIM = D_MODEL // FOX_HEADS

RET_HEADS = 8
RET_QK_DIM = D_MODEL // RET_HEADS
RET_V_DIM = 2 * D_MODEL // RET_HEADS
RET_CHUNK = 128
RET_THETA = 10000.0
RET_GN_EPS = 1e-5
RET_IN = 2 * RET_HEADS * RET_QK_DIM + 2 * RET_HEADS * RET_V_DIM

V7X_VMEM_LIMIT_BYTES = 56 * 1024 * 1024
V7X_VMEM_LIMIT_BYTES_LARGE = 62 * 1024 * 1024
LANE = 128
SUBLANE = 8

F32 = jnp.float32
BF16 = jnp.bfloat16


def _cparams(*sem):
    return pltpu.CompilerParams(dimension_semantics=sem, vmem_limit_bytes=V7X_VMEM_LIMIT_BYTES)


def _rms(x, g):
    return x * lax.rsqrt(jnp.mean(x * x, axis=-1, keepdims=True) + RMS_EPS) * g


def _ffn_body(h_ref, g0_ref, g1_ref, wg_ref, wu_ref, wo_ref, o_ref, xn_ref):
    f = pl.program_id(1)

    @pl.when(f == 0)
    def _():
        xn_ref[...] = _rms(h_ref[...], g0_ref[...]).astype(BF16)
        o_ref[...] = jnp.zeros_like(o_ref)

    xn = xn_ref[...]
    gate = jnp.dot(xn, wg_ref[...], preferred_element_type=F32)
    up = jnp.dot(xn, wu_ref[...], preferred_element_type=F32)
    act = (gate * jax.nn.sigmoid(gate)) * up
    o_ref[...] += jnp.dot(act.astype(BF16), wo_ref[...], preferred_element_type=F32)

    @pl.when(f == pl.num_programs(1) - 1)
    def _():
        o_ref[...] = h_ref[...] + 0.5 * _rms(o_ref[...], g1_ref[...])


def ffn_half_step(h, g0, g1, w_in, w_out, layer, half, *, tm=1024, tf=512):
    T, D = h.shape
    nf = D_FF // tf
    return pl.pallas_call(
        _ffn_body,
        out_shape=jax.ShapeDtypeStruct((T, D), F32),
        grid=(T // tm, nf),
        in_specs=[
            pl.BlockSpec((tm, D), lambda i, f: (i, 0)),
            pl.BlockSpec((1, D), lambda i, f: (0, 0)),
            pl.BlockSpec((1, D), lambda i, f: (0, 0)),
            pl.BlockSpec((None, None, D, tf), lambda i, f: (layer, half, 0, f)),
            pl.BlockSpec((None, None, D, tf), lambda i, f: (layer, half, 0, f + nf)),
            pl.BlockSpec((None, None, tf, D), lambda i, f: (layer, half, f, 0)),
        ],
        out_specs=pl.BlockSpec((tm, D), lambda i, f: (i, 0)),
        scratch_shapes=[pltpu.VMEM((tm, D), BF16)],
        compiler_params=pltpu.CompilerParams(dimension_semantics=("parallel", "arbitrary"),
                                             vmem_limit_bytes=V7X_VMEM_LIMIT_BYTES_LARGE),
        name="ffn_half_step",
    )(h, g0.reshape(1, D), g1.reshape(1, D), w_in, w_in, w_out)


def _norm_mm_body(x_ref, g_ref, w_ref, o_ref, xn_ref):
    @pl.when(pl.program_id(1) == 0)
    def _():
        xn_ref[...] = _rms(x_ref[...], g_ref[...]).astype(BF16)

    o_ref[...] = jnp.dot(xn_ref[...], w_ref[...], preferred_element_type=F32).astype(o_ref.dtype)


def _col_tile(n, cap=1024):
    best = LANE
    for t in range(LANE, cap + 1, LANE):
        if n % t == 0:
            best = t
    return best


def norm_matmul(x, g, w, *, out_dtype=F32, tm=1024):
    T, K = x.shape
    N = w.shape[1]
    tn = _col_tile(N)
    return pl.pallas_call(
        _norm_mm_body,
        out_shape=jax.ShapeDtypeStruct((T, N), out_dtype),
        grid=(T // tm, N // tn),
        in_specs=[
            pl.BlockSpec((tm, K), lambda i, j: (i, 0)),
            pl.BlockSpec((1, K), lambda i, j: (0, 0)),
            pl.BlockSpec((K, tn), lambda i, j: (0, j)),
        ],
        out_specs=pl.BlockSpec((tm, tn), lambda i, j: (i, j)),
        scratch_shapes=[pltpu.VMEM((tm, K), BF16)],
        compiler_params=_cparams("parallel", "arbitrary"),
        name="norm_matmul",
    )(x, g.reshape(1, K), w)


def _mm_res_body(y_ref, w_ref, g_ref, h_ref, o_ref, acc_ref):
    k = pl.program_id(1)

    @pl.when(k == 0)
    def _():
        acc_ref[...] = jnp.zeros_like(acc_ref)

    acc_ref[...] += jnp.dot(y_ref[...].astype(BF16), w_ref[...], preferred_element_type=F32)

    @pl.when(k == pl.num_programs(1) - 1)
    def _():
        o_ref[...] = h_ref[...] + _rms(acc_ref[...], g_ref[...])


def _mm_res_single_body(y_ref, w_ref, g_ref, h_ref, o_ref):
    acc = jnp.dot(y_ref[...].astype(BF16), w_ref[...], preferred_element_type=F32)
    o_ref[...] = h_ref[...] + _rms(acc, g_ref[...])


def matmul_norm_residual(y, w, g, h, *, tm=512, tk=2048):
    T, K = y.shape
    D = w.shape[1]
    if K == tk:
        return pl.pallas_call(
            _mm_res_single_body,
            out_shape=jax.ShapeDtypeStruct((T, D), F32),
            grid=(T // tm,),
            in_specs=[
                pl.BlockSpec((tm, K), lambda i: (i, 0)),
                pl.BlockSpec((K, D), lambda i: (0, 0)),
                pl.BlockSpec((1, D), lambda i: (0, 0)),
                pl.BlockSpec((tm, D), lambda i: (i, 0)),
            ],
            out_specs=pl.BlockSpec((tm, D), lambda i: (i, 0)),
            compiler_params=_cparams("parallel"),
            name="matmul_norm_residual",
        )(y, w, g.reshape(1, D), h)
    return pl.pallas_call(
        _mm_res_body,
        out_shape=jax.ShapeDtypeStruct((T, D), F32),
        grid=(T // tm, K // tk),
        in_specs=[
            pl.BlockSpec((tm, tk), lambda i, k: (i, k)),
            pl.BlockSpec((tk, D), lambda i, k: (k, 0)),
            pl.BlockSpec((1, D), lambda i, k: (0, 0)),
            pl.BlockSpec((tm, D), lambda i, k: (i, 0)),
        ],
        out_specs=pl.BlockSpec((tm, D), lambda i, k: (i, 0)),
        scratch_shapes=[pltpu.VMEM((tm, D), F32)],
        compiler_params=_cparams("parallel", "arbitrary"),
        name="matmul_norm_residual",
    )(y, w, g.reshape(1, D), h)


PLE_SUBTILES = 2


def _ple_body(h_ref, p_ref, g6_ref, g7_ref, wp_ref, wg_ref, o_ref):
    rows = h_ref.shape[0] // PLE_SUBTILES
    parts = [slice(n * rows, (n + 1) * rows) for n in range(PLE_SUBTILES)]
    xns = [_rms(h_ref[r, :], g6_ref[...]).astype(BF16) for r in parts]
    zs = [jnp.dot(xn, wg_ref[...], preferred_element_type=F32) for xn in xns]
    es = [jnp.dot(p_ref[r, :].astype(BF16), wp_ref[...], preferred_element_type=F32) for r in parts]
    for r, z, e in zip(parts, zs, es):
        o_ref[r, :] = h_ref[r, :] + _rms(e * jax.nn.sigmoid(z), g7_ref[...])


def ple_step(h, p, g6, g7, wp, wg, layer, *, tm=512):
    T, D = h.shape
    P = p.shape[2]
    return pl.pallas_call(
        _ple_body,
        out_shape=jax.ShapeDtypeStruct((T, D), F32),
        grid=(T // tm,),
        in_specs=[
            pl.BlockSpec((tm, D), lambda i: (i, 0)),
            pl.BlockSpec((None, tm, P), lambda i: (layer, i, 0)),
            pl.BlockSpec((1, D), lambda i: (0, 0)),
            pl.BlockSpec((1, D), lambda i: (0, 0)),
            pl.BlockSpec((None, P, D), lambda i: (layer, 0, 0)),
            pl.BlockSpec((None, D, D), lambda i: (layer, 0, 0)),
        ],
        out_specs=pl.BlockSpec((tm, D), lambda i: (i, 0)),
        compiler_params=_cparams("parallel"),
        name="ple_step",
    )(h, p, g6.reshape(1, D), g7.reshape(1, D), wp, wg)


RWKV_CHUNK = 64
RWKV_PAIRS_PER_STEP = 16
RWKV_TIME_BLOCK = 256


def _rwkv_body(r_ref, lw_ref, k_ref, v_ref, kk_ref, a_ref, y_ref, s_ref):
    C = RWKV_CHUNK
    N = RWKV_HEAD_DIM
    lane = lax.broadcasted_iota(jnp.int32, (C, 2 * N), 1)
    row = lax.broadcasted_iota(jnp.int32, (C, 2 * N), 0)
    first_head = lane < N
    ri = lax.broadcasted_iota(jnp.int32, (2 * C, 2 * C), 0)
    ci = lax.broadcasted_iota(jnp.int32, (2 * C, 2 * C), 1)
    strict = ri > ci
    incl = ri >= ci
    eye = jnp.where(ri == ci, 1.0, 0.0).astype(F32)
    corner = [((ri >> (lvl + 1)) == (ci >> (lvl + 1))) & ((ri & (1 << lvl)) != 0) & ((ci & (1 << lvl)) == 0)
              for lvl in range(C.bit_length() - 1)]

    def stack(x):
        return jnp.concatenate([jnp.where(first_head, x, 0.0), jnp.where(first_head, 0.0, x)], axis=0)

    def nt(a, b):
        return lax.dot_general(a.astype(BF16), b.astype(BF16), (((1,), (1,)), ((), ())),
                               preferred_element_type=F32)

    def nn(a, b):
        return jnp.dot(a.astype(BF16), b.astype(BF16), preferred_element_type=F32)

    def tn(a, b):
        return lax.dot_general(a.astype(BF16), b.astype(BF16), (((0,), (0,)), ((), ())),
                               preferred_element_type=F32)

    n_pairs = r_ref.shape[2] // (2 * N)
    lanes = [slice(2 * N * j, 2 * N * (j + 1)) for j in range(n_pairs)]

    def prep(sl, ln):
        r, lw, k, v, kk, a = (ref[0, sl, ln] for ref in (r_ref, lw_ref, k_ref, v_ref, kk_ref, a_ref))
        cl = lw
        for sh in (1, 2, 4, 8, 16, 32):
            cl = cl + jnp.where(row >= sh, pltpu.roll(cl, sh, axis=0), 0.0)
        mid = cl[C // 2 - 1:C // 2, :]
        last = cl[C - 1:C, :]
        e_neg = jnp.exp(mid - cl)
        e_end = jnp.exp(last - mid)
        b_til = stack(kk * a * e_neg)
        k_til = stack(k * e_neg)
        return dict(
            a_bar=stack(-kk * jnp.exp(cl - lw - mid)), r_bar=stack(r * jnp.exp(cl - mid)),
            b_til=b_til, k_til=k_til, v_st=stack(v), e_mid=jnp.exp(mid), w_tot=jnp.exp(last),
            bk_end=jnp.concatenate([b_til * e_end, k_til * e_end], axis=0))

    def chunk(c, states):
        sl = pl.ds(pl.multiple_of(c * C, C), C)
        ps = [prep(sl, ln) for ln in lanes]
        gs = [nt(jnp.concatenate([p["a_bar"], p["r_bar"]], axis=0),
                 jnp.concatenate([p["b_til"], p["k_til"]], axis=0)) for p in ps]
        a_ab = [jnp.where(strict, g[:2 * C, :2 * C], 0.0) for g in gs]
        a_ak = [jnp.where(strict, g[:2 * C, 2 * C:], 0.0) for g in gs]
        a_rb = [jnp.where(incl, g[2 * C:, :2 * C], 0.0) for g in gs]
        a_rk = [jnp.where(incl, g[2 * C:, 2 * C:], 0.0) for g in gs]
        inv = [eye + jnp.where(corner[0], x, 0.0) for x in a_ab]
        for lvl in range(1, len(corner)):
            tmp = [nn(jnp.where(corner[lvl], x, 0.0), t) for x, t in zip(a_ab, inv)]
            inv = [t + nn(t, x) for t, x in zip(inv, tmp)]
        s_mid = [st * p["e_mid"] for st, p in zip(states, ps)]
        rhs = [nt(p["a_bar"], sm) + nn(ak, p["v_st"]) for p, sm, ak in zip(ps, s_mid, a_ak)]
        us = [nn(t, x) for t, x in zip(inv, rhs)]
        ys = [nt(p["r_bar"], sm) + nn(rb, u) + nn(rk, p["v_st"])
              for p, sm, rb, rk, u in zip(ps, s_mid, a_rb, a_rk, us)]
        for ln, y in zip(lanes, ys):
            y_ref[0, sl, ln] = y[:C] + y[C:]
        return tuple(st * p["w_tot"] + tn(jnp.concatenate([u, p["v_st"]], axis=0), p["bk_end"])
                     for st, p, u in zip(states, ps, us))

    @pl.when(pl.program_id(2) == 0)
    def _():
        s_ref[...] = jnp.zeros_like(s_ref)

    states = lax.fori_loop(0, r_ref.shape[1] // C, chunk, tuple(s_ref[j] for j in range(n_pairs)))
    for j, st in enumerate(states):
        s_ref[j] = st


def rwkv_recurrence(proj, lw, k, kk, a):
    B, S, D = lw.shape
    lanes = 2 * RWKV_HEAD_DIM * RWKV_PAIRS_PER_STEP
    ts = min(S, RWKV_TIME_BLOCK)
    spec = pl.BlockSpec((1, ts, lanes), lambda b, j, t: (b, t, j))
    v_spec = pl.BlockSpec((1, ts, lanes), lambda b, j, t: (b, t, 2 * D // lanes + j))
    return pl.pallas_call(
        _rwkv_body,
        out_shape=jax.ShapeDtypeStruct((B, S, D), F32),
        grid=(B, D // lanes, S // ts),
        in_specs=[spec, spec, spec, v_spec, spec, spec],
        out_specs=spec,
        scratch_shapes=[pltpu.VMEM((RWKV_PAIRS_PER_STEP, 2 * RWKV_HEAD_DIM, 2 * RWKV_HEAD_DIM), F32)],
        compiler_params=_cparams("parallel", "parallel", "arbitrary"),
        name="rwkv_recurrence",
    )(proj, lw, k, proj, kk, a)


RWKV_PROJ_TN = 1024
RWKV_HIDDEN = 4 * LANE


RWKV_MIX_ROWS = 64


def _rwkv_mixes(first_of_seq, h_ref, hp_ref, g_ref, mu_ref, xm_ref, mixes):
    rows = RWKV_MIX_ROWS
    g = g_ref[...]
    before_tile = jnp.where(first_of_seq, 0.0, _rms(hp_ref[SUBLANE - 1:SUBLANE, :], g))
    row = lax.broadcasted_iota(jnp.int32, (rows, h_ref.shape[1]), 0)

    def chunk(c, prev):
        sl = pl.ds(pl.multiple_of(c * rows, rows), rows)
        u = _rms(h_ref[sl, :], g)
        xx = jnp.where(row == 0, prev, pltpu.roll(u, 1, axis=0)) - u
        for n, m in enumerate(mixes):
            xm_ref[n, sl, :] = (u + xx * mu_ref[m:m + 1, :]).astype(BF16)
        return u[rows - 1:rows, :]

    lax.fori_loop(0, h_ref.shape[0] // rows, chunk, before_tile)


def _rwkv_proj_body(tiles_per_seq, h_ref, hp_ref, g_ref, mu_ref, w_ref, o_ref, xm_ref):
    i, j = pl.program_id(0), pl.program_id(1)

    @pl.when(j == 0)
    def _():
        _rwkv_mixes(lax.rem(i, tiles_per_seq) == 0, h_ref, hp_ref, g_ref, mu_ref, xm_ref, (0, 1, 2))

    o_ref[...] = jnp.dot(xm_ref[j // (pl.num_programs(1) // 3)], w_ref[...], preferred_element_type=F32)


def _rwkv_hidden_body(tiles_per_seq, h_ref, hp_ref, g_ref, mu_ref, w_ref, o_ref):
    g = g_ref[...]
    u = _rms(h_ref[...], g)
    before_tile = jnp.where(lax.rem(pl.program_id(0), tiles_per_seq) == 0, 0.0,
                            _rms(hp_ref[SUBLANE - 1:SUBLANE, :], g))
    row = lax.broadcasted_iota(jnp.int32, u.shape, 0)
    xx = jnp.where(row == 0, before_tile, pltpu.roll(u, 1, axis=0)) - u
    mix = lambda c: (u + xx * mu_ref[c:c + 1, :]).astype(BF16)
    o_ref[:, :LANE] = jnp.tanh(jnp.dot(mix(3), w_ref[:, :LANE], preferred_element_type=F32))
    o_ref[:, LANE:2 * LANE] = jnp.dot(mix(4), w_ref[:, LANE:2 * LANE], preferred_element_type=F32)
    o_ref[:, 2 * LANE:] = jax.nn.sigmoid(jnp.dot(mix(5), w_ref[:, 2 * LANE:], preferred_element_type=F32))


def rwkv_mix_project(h, g, mu, w_rkv, w_hidden, seq_len, *, tm=1024, tm_hidden=512):
    T, D = h.shape
    tn = RWKV_PROJ_TN

    def prev_rows(rows):
        return lambda i, *_: (jnp.maximum(i * (rows // SUBLANE) - 1, 0), 0)

    rkv = pl.pallas_call(
        functools.partial(_rwkv_proj_body, seq_len // tm),
        out_shape=jax.ShapeDtypeStruct((T, 3 * D), F32),
        grid=(T // tm, 3 * D // tn),
        in_specs=[
            pl.BlockSpec((tm, D), lambda i, j: (i, 0)),
            pl.BlockSpec((SUBLANE, D), prev_rows(tm)),
            pl.BlockSpec((1, D), lambda i, j: (0, 0)),
            pl.BlockSpec((6, D), lambda i, j: (0, 0)),
            pl.BlockSpec((D, tn), lambda i, j: (0, j)),
        ],
        out_specs=pl.BlockSpec((tm, tn), lambda i, j: (i, j)),
        scratch_shapes=[pltpu.VMEM((3, tm, D), BF16)],
        compiler_params=_cparams("parallel", "arbitrary"),
        name="rwkv_mix_project",
    )(h, h, g.reshape(1, D), mu, w_rkv)
    tm = tm_hidden
    hidden = pl.pallas_call(
        functools.partial(_rwkv_hidden_body, seq_len // tm),
        out_shape=jax.ShapeDtypeStruct((T, RWKV_HIDDEN), F32),
        grid=(T // tm,),
        in_specs=[
            pl.BlockSpec((tm, D), lambda i: (i, 0)),
            pl.BlockSpec((SUBLANE, D), prev_rows(tm)),
            pl.BlockSpec((1, D), lambda i: (0, 0)),
            pl.BlockSpec((6, D), lambda i: (0, 0)),
            pl.BlockSpec((D, RWKV_HIDDEN), lambda i: (0, 0)),
        ],
        out_specs=pl.BlockSpec((tm, RWKV_HIDDEN), lambda i: (i, 0)),
        compiler_params=_cparams("parallel"),
        name="rwkv_mix_hidden",
    )(h, h, g.reshape(1, D), mu, w_hidden)
    return rkv, hidden


def _head_sum(x, ones_bd):
    hi = x.astype(BF16)
    lo = (x - hi.astype(F32)).astype(BF16)
    return (jnp.dot(hi, ones_bd, preferred_element_type=F32) + jnp.dot(lo, ones_bd, preferred_element_type=F32))


def _head_ones():
    shift = RWKV_HEAD_DIM.bit_length() - 1
    r = lax.broadcasted_iota(jnp.int32, (LANE, LANE), 0) >> shift
    c = lax.broadcasted_iota(jnp.int32, (LANE, LANE), 1) >> shift
    return jnp.where(r == c, 1.0, 0.0).astype(BF16)


def _rwkv_gates_body(k_ref, hid_ref, vec_ref, w2_ref, a2_ref, g2_ref, lw_ref, a_ref, kk_ref, k2_ref, gate_ref):
    hid = hid_ref[...].astype(BF16)
    z = vec_ref[0:1, :] + jnp.dot(hid[:, :LANE], w2_ref[...], preferred_element_type=F32)
    softplus = jnp.maximum(-z, 0.0) + jnp.log(1.0 + jnp.exp(-jnp.abs(z)))
    lw_ref[...] = -jnp.exp(-softplus - 0.5)
    a = jax.nn.sigmoid(vec_ref[1:2, :] + jnp.dot(hid[:, LANE:2 * LANE], a2_ref[...], preferred_element_type=F32))
    a_ref[...] = a
    gate_ref[...] = jnp.dot(hid[:, 2 * LANE:], g2_ref[...], preferred_element_type=F32)
    k = k_ref[...]
    k2_ref[...] = k * (1.0 + (a - 1.0) * vec_ref[3:4, :])
    kk = k * vec_ref[2:3, :]
    ones_bd = _head_ones()
    for c in range(k.shape[1] // LANE):
        blk = kk[:, c * LANE:(c + 1) * LANE]
        norm = jnp.sqrt(_head_sum(blk * blk, ones_bd))
        kk_ref[:, c * LANE:(c + 1) * LANE] = blk / jnp.maximum(norm, 1e-12)


def rwkv_gates(proj, hidden, vec, w2, a2, g2, *, tm=256):
    T = proj.shape[0]
    D = w2.shape[1]
    row = pl.BlockSpec((tm, D), lambda i: (i, 0))
    full = lambda w: pl.BlockSpec(w.shape, lambda i: (0, 0))
    return pl.pallas_call(
        _rwkv_gates_body,
        out_shape=tuple(jax.ShapeDtypeStruct((T, D), F32) for _ in range(5)),
        grid=(T // tm,),
        in_specs=[
            pl.BlockSpec((tm, D), lambda i: (i, 1)),
            pl.BlockSpec((tm, RWKV_HIDDEN), lambda i: (i, 0)),
            full(vec), full(w2), full(a2), full(g2),
        ],
        out_specs=(row,) * 5,
        compiler_params=_cparams("parallel"),
        name="rwkv_gates",
    )(proj, hidden, vec, w2, a2, g2)


def _rwkv_post_body(y_ref, r_ref, v_ref, k2_ref, gate_ref, vec_ref, o_ref):
    ones_bd = _head_ones()
    inv_n = 1.0 / RWKV_HEAD_DIM
    for c in range(y_ref.shape[1] // LANE):
        sl = slice(c * LANE, (c + 1) * LANE)
        y = y_ref[:, sl]
        d = y - _head_sum(y, ones_bd) * inv_n
        yn = d * lax.rsqrt(_head_sum(d * d, ones_bd) * inv_n + RWKV_GN_EPS)
        bonus = _head_sum(r_ref[:, sl] * k2_ref[:, sl] * vec_ref[2:3, sl], ones_bd) * v_ref[:, sl]
        o_ref[:, sl] = ((yn * vec_ref[0:1, sl] + vec_ref[1:2, sl] + bonus) * gate_ref[:, sl]).astype(o_ref.dtype)


def rwkv_post(y, proj, k2, gate, vec, *, tm=256):
    T, D = y.shape
    row = pl.BlockSpec((tm, D), lambda i: (i, 0))
    return pl.pallas_call(
        _rwkv_post_body,
        out_shape=jax.ShapeDtypeStruct((T, D), BF16),
        grid=(T // tm,),
        in_specs=[row, row, pl.BlockSpec((tm, D), lambda i: (i, 2)), row, row,
                  pl.BlockSpec(vec.shape, lambda i: (0, 0))],
        out_specs=row,
        compiler_params=_cparams("parallel"),
        name="rwkv_post",
    )(y, proj, proj, k2, gate, vec)


NSA_QBLOCKS_PER_ITER = 4


def _nsa_sel_win_body(sel_ref, q_ref, ks_ref, vs_ref, kw_ref, vw_ref, osel_ref, owin_ref):
    blk, n_sel, W = NSA_SEL_BLOCK, NSA_N_SEL, NSA_WINDOW
    R, dh = NSA_HEADS // NSA_KV_GROUPS, NSA_HEAD_DIM
    S = q_ref.shape[1]
    n_blk = S // blk
    base = (pl.program_id(0) * NSA_KV_GROUPS + pl.program_id(1)) * (n_blk * n_sel)
    row_t = lax.broadcasted_iota(jnp.int32, (R * blk, 1), 0) & (blk - 1)
    col_s = lax.broadcasted_iota(jnp.int32, (1, n_sel * blk), 1)
    col_w = lax.broadcasted_iota(jnp.int32, (1, W + blk), 1)
    ones_s = jnp.ones((n_sel * blk, dh), BF16)
    ones_w = jnp.ones((W + blk, dh), BF16)

    def nt(a, b):
        return lax.dot_general(a, b, (((1,), (1,)), ((), ())), preferred_element_type=F32)

    def probs(s, mask):
        s = jnp.where(mask, s, NEG_INF)
        return jnp.exp(s - jnp.max(s, axis=-1, keepdims=True)).astype(BF16)

    def qblocks(it, carry):
        ids = [it * NSA_QBLOCKS_PER_ITER + n for n in range(NSA_QBLOCKS_PER_ITER)]
        q0s = [pl.multiple_of(i * blk, blk) for i in ids]
        w0s = [pl.multiple_of(jnp.maximum(q0 - W, 0), blk) for q0 in q0s]
        starts = [[sel_ref[base + i * n_sel + j] * blk for j in range(n_sel)] for i in ids]
        qss = []
        for q0 in q0s:
            qb = q_ref[0, pl.ds(q0, blk), :]
            qss.append(jnp.concatenate([qb[:, r * dh:(r + 1) * dh] for r in range(R)], axis=0))
        s_sel = [nt(qs, jnp.concatenate([ks_ref[0, pl.ds(pl.multiple_of(x, blk), blk), :] for x in st], axis=0))
                 for qs, st in zip(qss, starts)]
        s_win = [nt(qs, kw_ref[0, pl.ds(w0, W + blk), :]) for qs, w0 in zip(qss, w0s)]
        p_sel, p_win = [], []
        for q0, w0, st, ss, sw in zip(q0s, w0s, starts, s_sel, s_win):
            qpos = q0 + row_t
            tok = col_s & (blk - 1)
            for j in range(n_sel):
                tok = tok + jnp.where((col_s >> (blk.bit_length() - 1)) == j, st[j], 0)
            dist = qpos - (w0 + col_w)
            p_sel.append(probs(ss, tok <= qpos))
            p_win.append(probs(sw, (dist >= 0) & (dist < W)))
        o_sel = [jnp.dot(p, jnp.concatenate(
            [jnp.concatenate([vs_ref[0, pl.ds(pl.multiple_of(x, blk), blk), :] for x in st], axis=0), ones_s],
            axis=1), preferred_element_type=F32) for p, st in zip(p_sel, starts)]
        o_win = [jnp.dot(p, jnp.concatenate([vw_ref[0, pl.ds(w0, W + blk), :], ones_w], axis=1),
                         preferred_element_type=F32) for p, w0 in zip(p_win, w0s)]
        for q0, os_, ow in zip(q0s, o_sel, o_win):
            os_ = (os_[:, :dh] / os_[:, dh:]).astype(osel_ref.dtype)
            ow = (ow[:, :dh] / ow[:, dh:]).astype(owin_ref.dtype)
            for r in range(R):
                osel_ref[0, pl.ds(q0, blk), r * dh:(r + 1) * dh] = os_[r * blk:(r + 1) * blk]
                owin_ref[0, pl.ds(q0, blk), r * dh:(r + 1) * dh] = ow[r * blk:(r + 1) * blk]
        return carry

    lax.fori_loop(0, n_blk // NSA_QBLOCKS_PER_ITER, qblocks, 0)


def nsa_selected_window(sel, qkv):
    B, S = qkv.shape[:2]
    G, dh, HD = NSA_KV_GROUPS, NSA_HEAD_DIM, NSA_HEADS * NSA_HEAD_DIM
    qspec = pl.BlockSpec((1, S, HD // G), lambda b, g, sel: (b, 0, g))
    kspec = lambda c: pl.BlockSpec((1, S, dh), lambda b, g, sel: (b, 0, HD // dh + c * G + g))
    return pl.pallas_call(
        _nsa_sel_win_body,
        out_shape=(jax.ShapeDtypeStruct((B, S, HD), BF16), jax.ShapeDtypeStruct((B, S, HD), BF16)),
        grid_spec=pltpu.PrefetchScalarGridSpec(
            num_scalar_prefetch=1,
            grid=(B, G),
            in_specs=[qspec, kspec(2), kspec(3), kspec(4), kspec(5)],
            out_specs=(qspec, qspec),
        ),
        compiler_params=_cparams("parallel", "parallel"),
        name="nsa_selected_window",
    )(sel.reshape(-1), qkv, qkv, qkv, qkv, qkv)


def _nsa_rope_tables(S):
    half = NSA_ROPE_DIM // 2
    inv_freq = ROPE_THETA ** (-jnp.arange(half, dtype=F32) / half)
    ang = jnp.arange(S, dtype=F32)[:, None] * inv_freq
    cos, sin = jnp.cos(ang), jnp.sin(ang)
    rest = NSA_HEAD_DIM - NSA_ROPE_DIM
    cos_t = jnp.concatenate([cos, cos, jnp.ones((S, rest), F32)], axis=1)
    sin_t = jnp.concatenate([-sin, sin, jnp.zeros((S, rest), F32)], axis=1)
    return cos_t, sin_t


def _nsa_prep_body(x_ref, cos_ref, sin_ref, o_ref, kv_ref):
    dh, G = NSA_HEAD_DIM, NSA_KV_GROUPS
    half = NSA_ROPE_DIM // 2
    n_q = NSA_HEADS
    cos, sin = cos_ref[...], sin_ref[...]
    low = lax.broadcasted_iota(jnp.int32, cos.shape, 1) < half
    for blk in range(n_q + 6 * G):
        sl = slice(blk * dh, (blk + 1) * dh)
        x = x_ref[:, sl]
        c = (blk - n_q) // G
        if blk < n_q or c in (0, 2, 4):
            swapped = jnp.where(low, pltpu.roll(x, dh - half, axis=1), pltpu.roll(x, half, axis=1))
            x = x * cos + swapped * sin
        if blk < n_q:
            x = x * dh ** -0.5
        elif c in (0, 1):
            kv_ref[:, (blk - n_q) * dh:(blk - n_q + 1) * dh] = x
        o_ref[:, sl] = x.astype(o_ref.dtype)


def nsa_prepare(proj, S, *, tm=256):
    T = proj.shape[0]
    dh, G = NSA_HEAD_DIM, NSA_KV_GROUPS
    n = (NSA_HEADS + 6 * G) * dh
    cos_t, sin_t = _nsa_rope_tables(S)
    tab = pl.BlockSpec((tm, dh), lambda i: (i % (S // tm), 0))
    return pl.pallas_call(
        _nsa_prep_body,
        out_shape=(jax.ShapeDtypeStruct((T, n), BF16), jax.ShapeDtypeStruct((T, 2 * G * dh), F32)),
        grid=(T // tm,),
        in_specs=[pl.BlockSpec((tm, n), lambda i: (i, 0)), tab, tab],
        out_specs=(pl.BlockSpec((tm, n), lambda i: (i, 0)), pl.BlockSpec((tm, 2 * G * dh), lambda i: (i, 0))),
        compiler_params=_cparams("parallel"),
        name="nsa_prepare",
    )(proj, cos_t, sin_t)


def _nsa_compress_body(kc_ref, vc_ref, pe_ref, w1_ref, w2_ref, o_ref):
    L, stride = NSA_CMP_LEN, NSA_CMP_STRIDE
    n_grp = kc_ref.shape[1] // stride
    for c, ref in enumerate((kc_ref, vc_ref)):
        first = jnp.zeros((n_grp, NSA_CMP_HIDDEN), F32)
        second = jnp.zeros((n_grp, NSA_CMP_HIDDEN), F32)
        for l in range(stride):
            rows = ref[0, pl.ds(l, n_grp, stride=stride), :]
            first += jnp.dot((rows + pe_ref[c, l:l + 1, :]).astype(BF16), w1_ref[c, l],
                             preferred_element_type=F32)
            second += jnp.dot((rows + pe_ref[c, stride + l:stride + l + 1, :]).astype(BF16),
                              w1_ref[c, stride + l], preferred_element_type=F32)
        hid = jax.nn.gelu(first + pltpu.roll(second, n_grp - 1, axis=0))
        o_ref[0, 0, c] = jnp.dot(hid.astype(BF16), w2_ref[c], preferred_element_type=F32)


def nsa_compress(kcvc, pe, w1, w2):
    B, S = kcvc.shape[:2]
    G, dh = NSA_KV_GROUPS, NSA_HEAD_DIM
    full = lambda w: pl.BlockSpec(w.shape, lambda b, g: (0,) * w.ndim)
    return pl.pallas_call(
        _nsa_compress_body,
        out_shape=jax.ShapeDtypeStruct((B, G, 2, S // NSA_CMP_STRIDE, dh), F32),
        grid=(B, G),
        in_specs=[pl.BlockSpec((1, S, dh), lambda b, g: (b, 0, g)),
                  pl.BlockSpec((1, S, dh), lambda b, g: (b, 0, G + g)),
                  full(pe), full(w1), full(w2)],
        out_specs=pl.BlockSpec((1, 1, 2, S // NSA_CMP_STRIDE, dh), lambda b, g: (b, g, 0, 0, 0)),
        compiler_params=_cparams("parallel", "parallel"),
        name="nsa_compress",
    )(kcvc, kcvc, pe, w1, w2)


NSA_CMP_TILE = 512


def _nsa_cmp_body(q_ref, cmp_ref, o_ref, sel_ref):
    dh, blk = NSA_HEAD_DIM, NSA_SEL_BLOCK
    R = NSA_HEADS // NSA_KV_GROUPS
    tq = q_ref.shape[1]
    n_cmp = cmp_ref.shape[3]
    nq = tq // blk
    i = pl.program_id(2)
    k_cmp = cmp_ref[0, 0, 0].astype(BF16)
    v_cmp = cmp_ref[0, 0, 1].astype(BF16)
    pos = i * tq + lax.broadcasted_iota(jnp.int32, (tq, 1), 0)
    n_id = lax.broadcasted_iota(jnp.int32, (1, n_cmp), 1)
    visible = n_id * NSA_CMP_STRIDE + (NSA_CMP_LEN - 1) <= pos
    start = lax.broadcasted_iota(jnp.int32, (n_cmp, LANE), 0) * NSA_CMP_STRIDE
    bstart = lax.broadcasted_iota(jnp.int32, (n_cmp, LANE), 1) * blk
    overlap = jnp.where((start <= bstart + blk - 1) & (start + NSA_CMP_LEN - 1 >= bstart), 1.0, 0.0).astype(BF16)
    heads = [slice(r * dh, (r + 1) * dh) for r in range(R)]
    ss = [lax.dot_general(q_ref[0, :, h], k_cmp, (((1,), (1,)), ((), ())), preferred_element_type=F32)
          for h in heads]
    ss = [jnp.where(visible, s, NEG_INF) for s in ss]
    es = [jnp.exp(s - jnp.max(s, axis=-1, keepdims=True)) for s in ss]
    ps = [jnp.where(visible, e / jnp.sum(e, axis=-1, keepdims=True), 0.0).astype(BF16) for e in es]
    v_and_overlap = jnp.concatenate([v_cmp, overlap], axis=1)
    outs = [jnp.dot(p, v_and_overlap, preferred_element_type=F32) for p in ps]
    for h, o in zip(heads, outs):
        o_ref[0, :, h] = o[:, :dh].astype(o_ref.dtype)
    imp = sum(o[:, dh:] for o in outs)
    imp = imp.reshape(nq, blk, LANE).sum(axis=1)
    qb = i * nq + lax.broadcasted_iota(jnp.int32, (nq, 1), 0)
    kb = lax.broadcasted_iota(jnp.int32, (nq, LANE), 1)
    forced = (kb == 0) | (kb == qb) | (kb == qb - 1)
    val = jnp.where(forced, jnp.inf, jnp.where(kb <= qb, imp, -jnp.inf))
    kb_f = kb.astype(F32)
    avail = kb >= 0
    picks = jnp.zeros((nq, LANE), F32)
    for t in range(NSA_N_SEL):
        best = jnp.max(jnp.where(avail, val, -jnp.inf), axis=-1, keepdims=True)
        pick = jnp.min(jnp.where(avail & (val == best), kb_f, float(LANE)), axis=-1, keepdims=True)
        picks = jnp.where(kb == t, pick, picks)
        avail = avail & (kb_f != pick)
    sel_ref[0, 0] = picks.astype(jnp.int32)


def nsa_compressed_attention(qkv, cmp):
    B, S = qkv.shape[:2]
    G, dh, HD, tq = NSA_KV_GROUPS, NSA_HEAD_DIM, NSA_HEADS * NSA_HEAD_DIM, NSA_CMP_TILE
    nq = tq // NSA_SEL_BLOCK
    qspec = pl.BlockSpec((1, tq, HD // G), lambda b, g, i: (b, i, g))
    return pl.pallas_call(
        _nsa_cmp_body,
        out_shape=(jax.ShapeDtypeStruct((B, S, HD), BF16),
                   jax.ShapeDtypeStruct((B, G, S // NSA_SEL_BLOCK, LANE), jnp.int32)),
        grid=(B, G, S // tq),
        in_specs=[qspec, pl.BlockSpec((1, 1) + cmp.shape[2:], lambda b, g, i: (b, g, 0, 0, 0))],
        out_specs=(qspec, pl.BlockSpec((1, 1, nq, LANE), lambda b, g, i: (b, g, i, 0))),
        compiler_params=_cparams("parallel", "parallel", "arbitrary"),
        name="nsa_compressed_attention",
    )(qkv, cmp)


def _nsa_combine_body(z_ref, oc_ref, os_ref, ow_ref, o_ref):
    dh = NSA_HEAD_DIM
    gates = jax.nn.sigmoid(z_ref[...])
    for hh in range(NSA_HEADS):
        sl = slice(hh * dh, (hh + 1) * dh)
        o_ref[:, sl] = (gates[:, 3 * hh:3 * hh + 1] * oc_ref[:, sl].astype(F32)
                        + gates[:, 3 * hh + 1:3 * hh + 2] * os_ref[:, sl].astype(F32)
                        + gates[:, 3 * hh + 2:3 * hh + 3] * ow_ref[:, sl].astype(F32)).astype(o_ref.dtype)


def nsa_combine(proj, o_cmp, o_sel, o_win, *, tm=256):
    T, HD = o_cmp.shape
    row = pl.BlockSpec((tm, HD), lambda i: (i, 0))
    gate_blk = (NSA_HEADS + 6 * NSA_KV_GROUPS) * NSA_HEAD_DIM // LANE
    return pl.pallas_call(
        _nsa_combine_body,
        out_shape=jax.ShapeDtypeStruct((T, HD), BF16),
        grid=(T // tm,),
        in_specs=[pl.BlockSpec((tm, LANE), lambda i: (i, gate_blk)), row, row, row],
        out_specs=row,
        compiler_params=_cparams("parallel"),
        name="nsa_combine",
    )(proj, o_cmp, o_sel, o_win)


FOX_TILE = 256
FOX_HEADS_PER_STEP = 8


def _fox_body(q_ref, k_ref, v_ref, cq_ref, ck_ref, o_ref):
    t, dh = FOX_TILE, FOX_HEAD_DIM
    nh = q_ref.shape[2] // dh
    heads = [slice(dh * j, dh * (j + 1)) for j in range(nh)]
    i = pl.program_id(2)
    qs = [q_ref[0, :, h] for h in heads]
    cq_all = cq_ref[0]
    head_lane = lax.broadcasted_iota(jnp.int32, cq_all.shape, 1) - pl.program_id(1) * nh
    cqs = [jnp.sum(jnp.where(head_lane == j, cq_all, 0.0), axis=-1, keepdims=True) for j in range(nh)]
    causal = (lax.broadcasted_iota(jnp.int32, (t, t), 0) >= lax.broadcasted_iota(jnp.int32, (t, t), 1))
    ones = jnp.ones((t, dh), BF16)

    def step(j, carry, diagonal):
        k0 = pl.multiple_of(j * t, t)
        ss = [lax.dot_general(q, k_ref[0, pl.ds(k0, t), h], (((1,), (1,)), ((), ())),
                              preferred_element_type=F32) + (cq - ck_ref[0, n, :, pl.ds(k0, t)])
              for n, (q, cq, h) in enumerate(zip(qs, cqs, heads))]
        if diagonal:
            ss = [jnp.where(causal, s, NEG_INF) for s in ss]
        m_new = [jnp.maximum(m, jnp.max(s, axis=-1, keepdims=True)) for (m, _), s in zip(carry, ss)]
        ps = [jnp.exp(s - m).astype(BF16) for s, m in zip(ss, m_new)]
        pv = [jnp.dot(p, jnp.concatenate([v_ref[0, pl.ds(k0, t), h], ones], axis=1),
                      preferred_element_type=F32) for p, h in zip(ps, heads)]
        return tuple((mn, jnp.exp(m - mn) * acc + x) for (m, acc), mn, x in zip(carry, m_new, pv))

    init = tuple((jnp.full((t, 1), NEG_INF, F32), jnp.zeros((t, 2 * dh), F32)) for _ in heads)
    carry = lax.fori_loop(0, i, lambda j, c: step(j, c, False), init)
    carry = step(i, carry, True)
    for (_, acc), h in zip(carry, heads):
        o_ref[0, :, h] = (acc[:, :dh] / acc[:, dh:]).astype(o_ref.dtype)


def _fox_gate_body(z_ref, b_ref, o_ref):
    x = z_ref[0] + b_ref[...]
    acc = -(jnp.maximum(-x, 0.0) + jnp.log(1.0 + jnp.exp(-jnp.abs(x))))
    row = lax.broadcasted_iota(jnp.int32, acc.shape, 0)
    shift = 1
    while shift < acc.shape[0]:
        acc = acc + jnp.where(row >= shift, pltpu.roll(acc, shift, axis=0), 0.0)
        shift *= 2
    o_ref[0] = acc


def fox_gate_cumsum(z, b):
    B, S, L = z.shape
    blk = pl.BlockSpec((1, S, L), lambda i: (i, 0, 0))
    return pl.pallas_call(
        _fox_gate_body,
        out_shape=jax.ShapeDtypeStruct((B, S, L), F32),
        grid=(B,),
        in_specs=[blk, pl.BlockSpec((1, L), lambda i: (0, 0))],
        out_specs=blk,
        compiler_params=_cparams("parallel"),
        name="fox_gate_cumsum",
    )(z, b.reshape(1, L))


def fox_attention(proj, cum_q, cum_k):
    B, S = proj.shape[:2]
    H, dh, t, nh = FOX_HEADS, FOX_HEAD_DIM, FOX_TILE, FOX_HEADS_PER_STEP
    G = H // nh
    return pl.pallas_call(
        _fox_body,
        out_shape=jax.ShapeDtypeStruct((B, S, H * dh), BF16),
        grid=(B, G, S // t),
        in_specs=[
            pl.BlockSpec((1, t, nh * dh), lambda b, h, i: (b, i, h)),
            pl.BlockSpec((1, S, nh * dh), lambda b, h, i: (b, 0, G + h)),
            pl.BlockSpec((1, S, nh * dh), lambda b, h, i: (b, 0, 2 * G + h)),
            pl.BlockSpec((1, t, LANE), lambda b, h, i: (b, i, 0)),
            pl.BlockSpec((1, nh, 1, S), lambda b, h, i: (b, h, 0, 0)),
        ],
        out_specs=pl.BlockSpec((1, t, nh * dh), lambda b, h, i: (b, i, h)),
        compiler_params=_cparams("parallel", "parallel", "arbitrary"),
        name="fox_attention",
    )(proj, proj, proj, cum_q, cum_k)


RET_HEADS_PER_STEP = 2


def _ret_body(lg_ref, q_ref, k_ref, v_ref, g_ref, cos_ref, sin_ref, gn_ref, o_ref, r_ref):
    C, dk, dv = RET_CHUNK, RET_QK_DIM, RET_V_DIM
    half = dk // 2
    nh = r_ref.shape[0]
    qk = [slice(dk * n, dk * (n + 1)) for n in range(nh)]
    vv = [slice(dv * n, dv * (n + 1)) for n in range(nh)]
    lgs = [lg_ref[pl.program_id(1) * nh + n] for n in range(nh)]
    ii = lax.broadcasted_iota(jnp.int32, (C, C), 0)
    jj = lax.broadcasted_iota(jnp.int32, (C, C), 1)
    ti = lax.broadcasted_iota(jnp.int32, (C, 1), 0).astype(F32)
    decay_mask = [jnp.where(ii >= jj, jnp.exp((ii - jj).astype(F32) * lg), 0.0) for lg in lgs]
    q_scale = [jnp.exp((ti + 1.0) * lg) for lg in lgs]
    k_scale = [jnp.exp((C - 1.0 - ti) * lg) for lg in lgs]
    chunk_decay = [jnp.exp(jnp.full((1, 1), C, F32) * lg) for lg in lgs]
    r_ref[...] = jnp.zeros_like(r_ref)

    def rot(x, cos, sin):
        x1, x2 = x[:, :half], x[:, half:]
        return jnp.concatenate([x1 * cos - x2 * sin, x2 * cos + x1 * sin], axis=1)

    def chunk(c, carry):
        sl = pl.ds(pl.multiple_of(c * C, C), C)
        cos, sin = cos_ref[sl, :], sin_ref[sl, :]
        qs = [rot(q_ref[0, sl, h].astype(F32), cos, sin).astype(BF16) for h in qk]
        ks = [rot(k_ref[0, sl, h].astype(F32), cos, sin) * (dk ** -0.5) for h in qk]
        vs = [v_ref[0, sl, h] for h in vv]
        inner = [lax.dot_general(q, k.astype(BF16), (((1,), (1,)), ((), ())), preferred_element_type=F32) * dm
                 for q, k, dm in zip(qs, ks, decay_mask)]
        states = [r_ref[n] for n in range(nh)]
        cross = [jnp.dot(q, st.astype(BF16), preferred_element_type=F32) * sc
                 for q, st, sc in zip(qs, states, q_scale)]
        os_ = [jnp.dot(x.astype(BF16), v, preferred_element_type=F32) + cr for x, v, cr in zip(inner, vs, cross)]
        upd = [lax.dot_general((k * sc).astype(BF16), v, (((0,), (0,)), ((), ())), preferred_element_type=F32)
               for k, sc, v in zip(ks, k_scale, vs)]
        for n in range(nh):
            r_ref[n] = states[n] * chunk_decay[n] + upd[n]
        for o, h in zip(os_, vv):
            d = o - jnp.mean(o, axis=-1, keepdims=True)
            on = d * lax.rsqrt(jnp.mean(d * d, axis=-1, keepdims=True) + RET_GN_EPS)
            g = g_ref[0, sl, h].astype(F32)
            o_ref[0, sl, h] = ((g * jax.nn.sigmoid(g)) * (on * gn_ref[:, h])).astype(o_ref.dtype)
        return carry

    lax.fori_loop(0, q_ref.shape[1] // C, chunk, 0)


def retention_core(proj, gn_g):
    B, S = proj.shape[:2]
    H, dk, dv, nh = RET_HEADS, RET_QK_DIM, RET_V_DIM, RET_HEADS_PER_STEP
    G = H // nh
    pos = jnp.arange(S, dtype=F32)
    inv_freq = RET_THETA ** (-jnp.arange(dk // 2, dtype=F32) / (dk // 2))
    ang = pos[:, None] * inv_freq
    log_gamma = jnp.log(1.0 - 2.0 ** (-5.0 - jnp.arange(H, dtype=F32)))
    tab = pl.BlockSpec((S, dk // 2), lambda b, h: (0, 0))
    v_first = 2 * H * dk // (nh * dv)
    return pl.pallas_call(
        _ret_body,
        out_shape=jax.ShapeDtypeStruct((B, S, H * dv), BF16),
        grid=(B, G),
        in_specs=[
            pl.BlockSpec(memory_space=pltpu.SMEM),
            pl.BlockSpec((1, S, nh * dk), lambda b, h: (b, 0, h)),
            pl.BlockSpec((1, S, nh * dk), lambda b, h: (b, 0, G + h)),
            pl.BlockSpec((1, S, nh * dv), lambda b, h: (b, 0, v_first + h)),
            pl.BlockSpec((1, S, nh * dv), lambda b, h: (b, 0, v_first + G + h)),
            tab, tab,
            pl.BlockSpec((1, nh * dv), lambda b, h: (0, h)),
        ],
        out_specs=pl.BlockSpec((1, S, nh * dv), lambda b, h: (b, 0, h)),
        scratch_shapes=[pltpu.VMEM((nh, dk, dv), F32)],
        compiler_params=_cparams("parallel", "parallel"),
        name="retention_core",
    )(log_gamma, proj, proj, proj, proj, jnp.cos(ang), jnp.sin(ang), gn_g.reshape(1, H * dv))


def _pad_cols(w, n):
    return jnp.pad(w, ((0, 0), (0, n - w.shape[1])))


NSA_IN_PADDED = 42 * LANE


def _nsa_mixer(h, g, w_in, cmp_pe, cmp_w1, cmp_w2, B, S):
    T = h.shape[0]
    proj = norm_matmul(h, g, _pad_cols(w_in, NSA_IN_PADDED).astype(BF16))
    qkv, kcvc = nsa_prepare(proj, S)
    cmp = nsa_compress(kcvc.reshape(B, S, -1), cmp_pe, cmp_w1.astype(BF16), cmp_w2.astype(BF16))
    qkv = qkv.reshape(B, S, -1)
    o_cmp, sel = nsa_compressed_attention(qkv, cmp)
    o_sel, o_win = nsa_selected_window(sel[..., :NSA_N_SEL], qkv)
    return nsa_combine(proj, o_cmp.reshape(T, -1), o_sel.reshape(T, -1), o_win.reshape(T, -1))


def _rwkv_mixer(h, g, mu, w_rkv, w0, w1, w2, a0, a1, a2, g1, g2, k_k, k_a, r_k, ln_gb, B, S):
    T, D = h.shape
    pad_c = lambda w: _pad_cols(w, LANE)
    pad_r = lambda w: jnp.pad(w, ((0, LANE - w.shape[0]), (0, 0)))
    w_main = jnp.concatenate([w_rkv[0], w_rkv[1], w_rkv[2]], axis=1).astype(BF16)
    w_hidden = jnp.concatenate([pad_c(w1), pad_c(a1), g1], axis=1).astype(BF16)
    proj, hidden = rwkv_mix_project(h, g, mu, w_main, w_hidden, S)
    lw, a, kk, k2, gate = rwkv_gates(proj, hidden, jnp.stack([w0, a0, k_k, k_a]), pad_r(w2).astype(BF16),
                                     pad_r(a2).astype(BF16), g2.astype(BF16))
    as3 = lambda t: t.reshape(B, S, -1)
    y = rwkv_recurrence(as3(proj), as3(lw), as3(k2), as3(kk), as3(a))
    vec = jnp.stack([ln_gb[0], ln_gb[1], r_k.reshape(D)])
    return rwkv_post(y.reshape(T, D), proj, k2, gate, vec)


def _fox_mixer(h, g, w_in, b_f, B, S):
    HD = FOX_HEADS * FOX_HEAD_DIM
    w_qkv = jnp.concatenate([w_in[:, :HD] * FOX_HEAD_DIM ** -0.5, w_in[:, HD:3 * HD]], axis=1)
    proj = norm_matmul(h, g, w_qkv.astype(BF16), out_dtype=BF16)
    z = norm_matmul(h, g, _pad_cols(w_in[:, 3 * HD:], LANE).astype(BF16))
    cum = fox_gate_cumsum(z.reshape(B, S, LANE), jnp.pad(b_f, (0, LANE - FOX_HEADS)))
    cum_k = jnp.transpose(cum[:, :, :FOX_HEADS], (0, 2, 1))[:, :, None, :]
    return fox_attention(proj.reshape(B, S, 3 * HD), cum, cum_k).reshape(B * S, HD)


def kernel(x, p, norm_g, ffn_w_in, ffn_w_out, ple_w_proj, ple_w_gate, nsa_w_in, nsa_cmp_pe, nsa_cmp_w1, nsa_cmp_w2, nsa_w_out, rwkv_mu, rwkv_w_rkv, rwkv_w0, rwkv_w1, rwkv_w2, rwkv_a0, rwkv_a1, rwkv_a2, rwkv_g1, rwkv_g2, rwkv_k_k, rwkv_k_a, rwkv_r_k, rwkv_ln, rwkv_w_out, fox_w_in, fox_b_f, fox_w_out, ret_w_in, ret_gn_g, ret_w_out):
    B, S, D = x.shape
    T = B * S
    h = x.reshape(T, D)
    bf = lambda w: w.astype(BF16)
    ffn_w_in, ffn_w_out, ple_w_proj, ple_w_gate = bf(ffn_w_in), bf(ffn_w_out), bf(ple_w_proj), bf(ple_w_gate)
    p = p.reshape(DEPTH, T, PLE_DIM)
    for i in range(DEPTH):
        m, j = i % N_MIXERS, i // N_MIXERS
        ng = norm_g[i]
        h = ffn_half_step(h, ng[0], ng[1], ffn_w_in, ffn_w_out, i, 0)
        if m == 0:
            y = _nsa_mixer(h, ng[2], nsa_w_in[j], nsa_cmp_pe[j], nsa_cmp_w1[j], nsa_cmp_w2[j], B, S)
            w_out = nsa_w_out[j]
        elif m == 1:
            y = _rwkv_mixer(h, ng[2], rwkv_mu[j], rwkv_w_rkv[j], rwkv_w0[j], rwkv_w1[j], rwkv_w2[j],
                            rwkv_a0[j], rwkv_a1[j], rwkv_a2[j], rwkv_g1[j], rwkv_g2[j],
                            rwkv_k_k[j], rwkv_k_a[j], rwkv_r_k[j], rwkv_ln[j], B, S)
            w_out = rwkv_w_out[j]
        elif m == 2:
            y = _fox_mixer(h, ng[2], fox_w_in[j], fox_b_f[j], B, S)
            w_out = fox_w_out[j]
        else:
            proj = norm_matmul(h, ng[2], bf(ret_w_in[j]), out_dtype=BF16)
            y = retention_core(proj.reshape(B, S, RET_IN), ret_gn_g[j]).reshape(T, -1)
            w_out = ret_w_out[j]
        h = matmul_norm_residual(y, bf(w_out), ng[3], h)
        h = ffn_half_step(h, ng[4], ng[5], ffn_w_in, ffn_w_out, i, 1)
        h = ple_step(h, p, ng[6], ng[7], ple_w_proj, ple_w_gate, i)
    return h.reshape(B, S, D)
```

```python
import functools

import jax
import jax.numpy as jnp
from jax import lax
from jax.experimental import pallas as pl
from jax.experimental.pallas import tpu as pltpu

D_MODEL = 2048
DEPTH = 4
N_MIXERS = 4
PLE_DIM = 256
D_FF = 5632
RMS_EPS = 1e-6
NEG_INF = -1e30

NSA_HEADS = 16
NSA_KV_GROUPS = 4
NSA_HEAD_DIM = D_MODEL // NSA_HEADS
NSA_CMP_LEN = 32
NSA_CMP_STRIDE = 16
NSA_CMP_HIDDEN = 2 * NSA_HEAD_DIM
NSA_SEL_BLOCK = 64
NSA_N_SEL = 8
NSA_WINDOW = 512
NSA_ROPE_DIM = NSA_HEAD_DIM // 4
ROPE_THETA = 500000.0

RWKV_HEAD_DIM = 64
RWKV_GN_EPS = 64e-5

FOX_HEADS = 16
FOX_HEAD_DIM = D_MODEL // FOX_HEADS

RET_HEADS = 8
RET_QK_DIM = D_MODEL // RET_HEADS
RET_V_DIM = 2 * D_MODEL // RET_HEADS
RET_CHUNK = 128
RET_THETA = 10000.0
RET_GN_EPS = 1e-5
RET_IN = 2 * RET_HEADS * RET_QK_DIM + 2 * RET_HEADS * RET_V_DIM

V7X_VMEM_LIMIT_BYTES = 56 * 1024 * 1024
V7X_VMEM_LIMIT_BYTES_LARGE = 62 * 1024 * 1024
LANE = 128
SUBLANE = 8

F32 = jnp.float32
BF16 = jnp.bfloat16


def _cparams(*sem):
    return pltpu.CompilerParams(dimension_semantics=sem, vmem_limit_bytes=V7X_VMEM_LIMIT_BYTES)


def _rms(x, g):
    return x * lax.rsqrt(jnp.mean(x * x, axis=-1, keepdims=True) + RMS_EPS) * g


def _ffn_body(h_ref, g0_ref, g1_ref, wg_ref, wu_ref, wo_ref, o_ref, xn_ref):
    f = pl.program_id(1)

    @pl.when(f == 0)
    def _():
        xn_ref[...] = _rms(h_ref[...], g0_ref[...]).astype(BF16)
        o_ref[...] = jnp.zeros_like(o_ref)

    xn = xn_ref[...]
    gate = jnp.dot(xn, wg_ref[...], preferred_element_type=F32)
    up = jnp.dot(xn, wu_ref[...], preferred_element_type=F32)
    act = (gate * jax.nn.sigmoid(gate)) * up
    o_ref[...] += jnp.dot(act.astype(BF16), wo_ref[...], preferred_element_type=F32)

    @pl.when(f == pl.num_programs(1) - 1)
    def _():
        o_ref[...] = h_ref[...] + 0.5 * _rms(o_ref[...], g1_ref[...])


def ffn_half_step(h, g0, g1, w_in, w_out, layer, half, *, tm=1024, tf=512):
    T, D = h.shape
    nf = D_FF // tf
    return pl.pallas_call(
        _ffn_body,
        out_shape=jax.ShapeDtypeStruct((T, D), F32),
        grid=(T // tm, nf),
        in_specs=[
            pl.BlockSpec((tm, D), lambda i, f: (i, 0)),
            pl.BlockSpec((1, D), lambda i, f: (0, 0)),
            pl.BlockSpec((1, D), lambda i, f: (0, 0)),
            pl.BlockSpec((None, None, D, tf), lambda i, f: (layer, half, 0, f)),
            pl.BlockSpec((None, None, D, tf), lambda i, f: (layer, half, 0, f + nf)),
            pl.BlockSpec((None, None, tf, D), lambda i, f: (layer, half, f, 0)),
        ],
        out_specs=pl.BlockSpec((tm, D), lambda i, f: (i, 0)),
        scratch_shapes=[pltpu.VMEM((tm, D), BF16)],
        compiler_params=pltpu.CompilerParams(dimension_semantics=("parallel", "arbitrary"),
                                             vmem_limit_bytes=V7X_VMEM_LIMIT_BYTES_LARGE),
        name="ffn_half_step",
    )(h, g0.reshape(1, D), g1.reshape(1, D), w_in, w_in, w_out)


def _norm_mm_body(x_ref, g_ref, w_ref, o_ref, xn_ref):
    @pl.when(pl.program_id(1) == 0)
    def _():
        xn_ref[...] = _rms(x_ref[...], g_ref[...]).astype(BF16)

    o_ref[...] = jnp.dot(xn_ref[...], w_ref[...], preferred_element_type=F32).astype(o_ref.dtype)


def _col_tile(n, cap=1024):
    best = LANE
    for t in range(LANE, cap + 1, LANE):
        if n % t == 0:
            best = t
    return best


def norm_matmul(x, g, w, *, out_dtype=F32, tm=1024):
    T, K = x.shape
    N = w.shape[1]
    tn = _col_tile(N)
    return pl.pallas_call(
        _norm_mm_body,
        out_shape=jax.ShapeDtypeStruct((T, N), out_dtype),
        grid=(T // tm, N // tn),
        in_specs=[
            pl.BlockSpec((tm, K), lambda i, j: (i, 0)),
            pl.BlockSpec((1, K), lambda i, j: (0, 0)),
            pl.BlockSpec((K, tn), lambda i, j: (0, j)),
        ],
        out_specs=pl.BlockSpec((tm, tn), lambda i, j: (i, j)),
        scratch_shapes=[pltpu.VMEM((tm, K), BF16)],
        compiler_params=_cparams("parallel", "arbitrary"),
        name="norm_matmul",
    )(x, g.reshape(1, K), w)


def _mm_res_body(y_ref, w_ref, g_ref, h_ref, o_ref, acc_ref):
    k = pl.program_id(1)

    @pl.when(k == 0)
    def _():
        acc_ref[...] = jnp.zeros_like(acc_ref)

    acc_ref[...] += jnp.dot(y_ref[...].astype(BF16), w_ref[...], preferred_element_type=F32)

    @pl.when(k == pl.num_programs(1) - 1)
    def _():
        o_ref[...] = h_ref[...] + _rms(acc_ref[...], g_ref[...])


def _mm_res_single_body(y_ref, w_ref, g_ref, h_ref, o_ref):
    acc = jnp.dot(y_ref[...].astype(BF16), w_ref[...], preferred_element_type=F32)
    o_ref[...] = h_ref[...] + _rms(acc, g_ref[...])


def matmul_norm_residual(y, w, g, h, *, tm=512, tk=2048):
    T, K = y.shape
    D = w.shape[1]
    if K == tk:
        return pl.pallas_call(
            _mm_res_single_body,
            out_shape=jax.ShapeDtypeStruct((T, D), F32),
            grid=(T // tm,),
            in_specs=[
                pl.BlockSpec((tm, K), lambda i: (i, 0)),
                pl.BlockSpec((K, D), lambda i: (0, 0)),
                pl.BlockSpec((1, D), lambda i: (0, 0)),
                pl.BlockSpec((tm, D), lambda i: (i, 0)),
            ],
            out_specs=pl.BlockSpec((tm, D), lambda i: (i, 0)),
            compiler_params=_cparams("parallel"),
            name="matmul_norm_residual",
        )(y, w, g.reshape(1, D), h)
    return pl.pallas_call(
        _mm_res_body,
        out_shape=jax.ShapeDtypeStruct((T, D), F32),
        grid=(T // tm, K // tk),
        in_specs=[
            pl.BlockSpec((tm, tk), lambda i, k: (i, k)),
            pl.BlockSpec((tk, D), lambda i, k: (k, 0)),
            pl.BlockSpec((1, D), lambda i, k: (0, 0)),
            pl.BlockSpec((tm, D), lambda i, k: (i, 0)),
        ],
        out_specs=pl.BlockSpec((tm, D), lambda i, k: (i, 0)),
        scratch_shapes=[pltpu.VMEM((tm, D), F32)],
        compiler_params=_cparams("parallel", "arbitrary"),
        name="matmul_norm_residual",
    )(y, w, g.reshape(1, D), h)


PLE_SUBTILES = 2


def _ple_body(h_ref, p_ref, g6_ref, g7_ref, wp_ref, wg_ref, o_ref):
    rows = h_ref.shape[0] // PLE_SUBTILES
    parts = [slice(n * rows, (n + 1) * rows) for n in range(PLE_SUBTILES)]
    xns = [_rms(h_ref[r, :], g6_ref[...]).astype(BF16) for r in parts]
    zs = [jnp.dot(xn, wg_ref[...], preferred_element_type=F32) for xn in xns]
    es = [jnp.dot(p_ref[r, :].astype(BF16), wp_ref[...], preferred_element_type=F32) for r in parts]
    for r, z, e in zip(parts, zs, es):
        o_ref[r, :] = h_ref[r, :] + _rms(e * jax.nn.sigmoid(z), g7_ref[...])


def ple_step(h, p, g6, g7, wp, wg, layer, *, tm=512):
    T, D = h.shape
    P = p.shape[2]
    return pl.pallas_call(
        _ple_body,
        out_shape=jax.ShapeDtypeStruct((T, D), F32),
        grid=(T // tm,),
        in_specs=[
            pl.BlockSpec((tm, D), lambda i: (i, 0)),
            pl.BlockSpec((None, tm, P), lambda i: (layer, i, 0)),
            pl.BlockSpec((1, D), lambda i: (0, 0)),
            pl.BlockSpec((1, D), lambda i: (0, 0)),
            pl.BlockSpec((None, P, D), lambda i: (layer, 0, 0)),
            pl.BlockSpec((None, D, D), lambda i: (layer, 0, 0)),
        ],
        out_specs=pl.BlockSpec((tm, D), lambda i: (i, 0)),
        compiler_params=_cparams("parallel"),
        name="ple_step",
    )(h, p, g6.reshape(1, D), g7.reshape(1, D), wp, wg)


RWKV_CHUNK = 64
RWKV_PAIRS_PER_STEP = 16
RWKV_TIME_BLOCK = 256


def _rwkv_body(r_ref, lw_ref, k_ref, v_ref, kk_ref, a_ref, y_ref, s_ref):
    C = RWKV_CHUNK
    N = RWKV_HEAD_DIM
    lane = lax.broadcasted_iota(jnp.int32, (C, 2 * N), 1)
    row = lax.broadcasted_iota(jnp.int32, (C, 2 * N), 0)
    first_head = lane < N
    ri = lax.broadcasted_iota(jnp.int32, (2 * C, 2 * C), 0)
    ci = lax.broadcasted_iota(jnp.int32, (2 * C, 2 * C), 1)
    strict = ri > ci
    incl = ri >= ci
    eye = jnp.where(ri == ci, 1.0, 0.0).astype(F32)
    corner = [((ri >> (lvl + 1)) == (ci >> (lvl + 1))) & ((ri & (1 << lvl)) != 0) & ((ci & (1 << lvl)) == 0)
              for lvl in range(C.bit_length() - 1)]

    def stack(x):
        return jnp.concatenate([jnp.where(first_head, x, 0.0), jnp.where(first_head, 0.0, x)], axis=0)

    def nt(a, b):
        return lax.dot_general(a.astype(BF16), b.astype(BF16), (((1,), (1,)), ((), ())),
                               preferred_element_type=F32)

    def nn(a, b):
        return jnp.dot(a.astype(BF16), b.astype(BF16), preferred_element_type=F32)

    def tn(a, b):
        return lax.dot_general(a.astype(BF16), b.astype(BF16), (((0,), (0,)), ((), ())),
                               preferred_element_type=F32)

    n_pairs = r_ref.shape[2] // (2 * N)
    lanes = [slice(2 * N * j, 2 * N * (j + 1)) for j in range(n_pairs)]

    def prep(sl, ln):
        r, lw, k, v, kk, a = (ref[0, sl, ln] for ref in (r_ref, lw_ref, k_ref, v_ref, kk_ref, a_ref))
        cl = lw
        for sh in (1, 2, 4, 8, 16, 32):
            cl = cl + jnp.where(row >= sh, pltpu.roll(cl, sh, axis=0), 0.0)
        mid = cl[C // 2 - 1:C // 2, :]
        last = cl[C - 1:C, :]
        e_neg = jnp.exp(mid - cl)
        e_end = jnp.exp(last - mid)
        b_til = stack(kk * a * e_neg)
        k_til = stack(k * e_neg)
        return dict(
            a_bar=stack(-kk * jnp.exp(cl - lw - mid)), r_bar=stack(r * jnp.exp(cl - mid)),
            b_til=b_til, k_til=k_til, v_st=stack(v), e_mid=jnp.exp(mid), w_tot=jnp.exp(last),
            bk_end=jnp.concatenate([b_til * e_end, k_til * e_end], axis=0))

    def chunk(c, states):
        sl = pl.ds(pl.multiple_of(c * C, C), C)
        ps = [prep(sl, ln) for ln in lanes]
        gs = [nt(jnp.concatenate([p["a_bar"], p["r_bar"]], axis=0),
                 jnp.concatenate([p["b_til"], p["k_til"]], axis=0)) for p in ps]
        a_ab = [jnp.where(strict, g[:2 * C, :2 * C], 0.0) for g in gs]
        a_ak = [jnp.where(strict, g[:2 * C, 2 * C:], 0.0) for g in gs]
        a_rb = [jnp.where(incl, g[2 * C:, :2 * C], 0.0) for g in gs]
        a_rk = [jnp.where(incl, g[2 * C:, 2 * C:], 0.0) for g in gs]
        inv = [eye + jnp.where(corner[0], x, 0.0) for x in a_ab]
        for lvl in range(1, len(corner)):
            tmp = [nn(jnp.where(corner[lvl], x, 0.0), t) for x, t in zip(a_ab, inv)]
            inv = [t + nn(t, x) for t, x in zip(inv, tmp)]
        s_mid = [st * p["e_mid"] for st, p in zip(states, ps)]
        rhs = [nt(p["a_bar"], sm) + nn(ak, p["v_st"]) for p, sm, ak in zip(ps, s_mid, a_ak)]
        us = [nn(t, x) for t, x in zip(inv, rhs)]
        ys = [nt(p["r_bar"], sm) + nn(rb, u) + nn(rk, p["v_st"])
              for p, sm, rb, rk, u in zip(ps, s_mid, a_rb, a_rk, us)]
        for ln, y in zip(lanes, ys):
            y_ref[0, sl, ln] = y[:C] + y[C:]
        return tuple(st * p["w_tot"] + tn(jnp.concatenate([u, p["v_st"]], axis=0), p["bk_end"])
                     for st, p, u in zip(states, ps, us))

    @pl.when(pl.program_id(2) == 0)
    def _():
        s_ref[...] = jnp.zeros_like(s_ref)

    states = lax.fori_loop(0, r_ref.shape[1] // C, chunk, tuple(s_ref[j] for j in range(n_pairs)))
    for j, st in enumerate(states):
        s_ref[j] = st


def rwkv_recurrence(proj, lw, k, kk, a):
    B, S, D = lw.shape
    lanes = 2 * RWKV_HEAD_DIM * RWKV_PAIRS_PER_STEP
    ts = min(S, RWKV_TIME_BLOCK)
    spec = pl.BlockSpec((1, ts, lanes), lambda b, j, t: (b, t, j))
    v_spec = pl.BlockSpec((1, ts, lanes), lambda b, j, t: (b, t, 2 * D // lanes + j))
    return pl.pallas_call(
        _rwkv_body,
        out_shape=jax.ShapeDtypeStruct((B, S, D), F32),
        grid=(B, D // lanes, S // ts),
        in_specs=[spec, spec, spec, v_spec, spec, spec],
        out_specs=spec,
        scratch_shapes=[pltpu.VMEM((RWKV_PAIRS_PER_STEP, 2 * RWKV_HEAD_DIM, 2 * RWKV_HEAD_DIM), F32)],
        compiler_params=_cparams("parallel", "parallel", "arbitrary"),
        name="rwkv_recurrence",
    )(proj, lw, k, proj, kk, a)


RWKV_PROJ_TN = 1024
RWKV_HIDDEN = 4 * LANE


RWKV_MIX_ROWS = 64


def _rwkv_mixes(first_of_seq, h_ref, hp_ref, g_ref, mu_ref, xm_ref, mixes):
    rows = RWKV_MIX_ROWS
    g = g_ref[...]
    before_tile = jnp.where(first_of_seq, 0.0, _rms(hp_ref[SUBLANE - 1:SUBLANE, :], g))
    row = lax.broadcasted_iota(jnp.int32, (rows, h_ref.shape[1]), 0)

    def chunk(c, prev):
        sl = pl.ds(pl.multiple_of(c * rows, rows), rows)
        u = _rms(h_ref[sl, :], g)
        xx = jnp.where(row == 0, prev, pltpu.roll(u, 1, axis=0)) - u
        for n, m in enumerate(mixes):
            xm_ref[n, sl, :] = (u + xx * mu_ref[m:m + 1, :]).astype(BF16)
        return u[rows - 1:rows, :]

    lax.fori_loop(0, h_ref.shape[0] // rows, chunk, before_tile)


def _rwkv_proj_body(tiles_per_seq, h_ref, hp_ref, g_ref, mu_ref, w_ref, o_ref, xm_ref):
    i, j = pl.program_id(0), pl.program_id(1)

    @pl.when(j == 0)
    def _():
        _rwkv_mixes(lax.rem(i, tiles_per_seq) == 0, h_ref, hp_ref, g_ref, mu_ref, xm_ref, (0, 1, 2))

    o_ref[...] = jnp.dot(xm_ref[j // (pl.num_programs(1) // 3)], w_ref[...], preferred_element_type=F32)


def _rwkv_hidden_body(tiles_per_seq, h_ref, hp_ref, g_ref, mu_ref, w_ref, o_ref):
    g = g_ref[...]
    u = _rms(h_ref[...], g)
    before_tile = jnp.where(lax.rem(pl.program_id(0), tiles_per_seq) == 0, 0.0,
                            _rms(hp_ref[SUBLANE - 1:SUBLANE, :], g))
    row = lax.broadcasted_iota(jnp.int32, u.shape, 0)
    xx = jnp.where(row == 0, before_tile, pltpu.roll(u, 1, axis=0)) - u
    mix = lambda c: (u + xx * mu_ref[c:c + 1, :]).astype(BF16)
    o_ref[:, :LANE] = jnp.tanh(jnp.dot(mix(3), w_ref[:, :LANE], preferred_element_type=F32))
    o_ref[:, LANE:2 * LANE] = jnp.dot(mix(4), w_ref[:, LANE:2 * LANE], preferred_element_type=F32)
    o_ref[:, 2 * LANE:] = jax.nn.sigmoid(jnp.dot(mix(5), w_ref[:, 2 * LANE:], preferred_element_type=F32))


def rwkv_mix_project(h, g, mu, w_rkv, w_hidden, seq_len, *, tm=1024, tm_hidden=512):
    T, D = h.shape
    tn = RWKV_PROJ_TN

    def prev_rows(rows):
        return lambda i, *_: (jnp.maximum(i * (rows // SUBLANE) - 1, 0), 0)

    rkv = pl.pallas_call(
        functools.partial(_rwkv_proj_body, seq_len // tm),
        out_shape=jax.ShapeDtypeStruct((T, 3 * D), F32),
        grid=(T // tm, 3 * D // tn),
        in_specs=[
            pl.BlockSpec((tm, D), lambda i, j: (i, 0)),
            pl.BlockSpec((SUBLANE, D), prev_rows(tm)),
            pl.BlockSpec((1, D), lambda i, j: (0, 0)),
            pl.BlockSpec((6, D), lambda i, j: (0, 0)),
            pl.BlockSpec((D, tn), lambda i, j: (0, j)),
        ],
        out_specs=pl.BlockSpec((tm, tn), lambda i, j: (i, j)),
        scratch_shapes=[pltpu.VMEM((3, tm, D), BF16)],
        compiler_params=_cparams("parallel", "arbitrary"),
        name="rwkv_mix_project",
    )(h, h, g.reshape(1, D), mu, w_rkv)
    tm = tm_hidden
    hidden = pl.pallas_call(
        functools.partial(_rwkv_hidden_body, seq_len // tm),
        out_shape=jax.ShapeDtypeStruct((T, RWKV_HIDDEN), F32),
        grid=(T // tm,),
        in_specs=[
            pl.BlockSpec((tm, D), lambda i: (i, 0)),
            pl.BlockSpec((SUBLANE, D), prev_rows(tm)),
            pl.BlockSpec((1, D), lambda i: (0, 0)),
            pl.BlockSpec((6, D), lambda i: (0, 0)),
            pl.BlockSpec((D, RWKV_HIDDEN), lambda i: (0, 0)),
        ],
        out_specs=pl.BlockSpec((tm, RWKV_HIDDEN), lambda i: (i, 0)),
        compiler_params=_cparams("parallel"),
        name="rwkv_mix_hidden",
    )(h, h, g.reshape(1, D), mu, w_hidden)
    return rkv, hidden


def _head_sum(x, ones_bd):
    hi = x.astype(BF16)
    lo = (x - hi.astype(F32)).astype(BF16)
    return (jnp.dot(hi, ones_bd, preferred_element_type=F32) + jnp.dot(lo, ones_bd, preferred_element_type=F32))


def _head_ones():
    shift = RWKV_HEAD_DIM.bit_length() - 1
    r = lax.broadcasted_iota(jnp.int32, (LANE, LANE), 0) >> shift
    c = lax.broadcasted_iota(jnp.int32, (LANE, LANE), 1) >> shift
    return jnp.where(r == c, 1.0, 0.0).astype(BF16)


def _rwkv_gates_body(k_ref, hid_ref, vec_ref, w2_ref, a2_ref, lw_ref, a_ref, kk_ref, k2_ref):
    hid = hid_ref[...].astype(BF16)
    z = vec_ref[0:1, :] + jnp.dot(hid[:, :LANE], w2_ref[...], preferred_element_type=F32)
    softplus = jnp.maximum(-z, 0.0) + jnp.log(1.0 + jnp.exp(-jnp.abs(z)))
    lw_ref[...] = -jnp.exp(-softplus - 0.5)
    a = jax.nn.sigmoid(vec_ref[1:2, :] + jnp.dot(hid[:, LANE:2 * LANE], a2_ref[...], preferred_element_type=F32))
    a_ref[...] = a
    k = k_ref[...]
    k2_ref[...] = k * (1.0 + (a - 1.0) * vec_ref[3:4, :])
    kk = k * vec_ref[2:3, :]
    ones_bd = _head_ones()
    for c in range(k.shape[1] // LANE):
        blk = kk[:, c * LANE:(c + 1) * LANE]
        norm = jnp.sqrt(_head_sum(blk * blk, ones_bd))
        kk_ref[:, c * LANE:(c + 1) * LANE] = blk / jnp.maximum(norm, 1e-12)


def rwkv_gates(proj, hidden, vec, w2, a2, *, tm=256):
    T = proj.shape[0]
    D = w2.shape[1]
    row = pl.BlockSpec((tm, D), lambda i: (i, 0))
    full = lambda w: pl.BlockSpec(w.shape, lambda i: (0, 0))
    return pl.pallas_call(
        _rwkv_gates_body,
        out_shape=tuple(jax.ShapeDtypeStruct((T, D), F32) for _ in range(4)),
        grid=(T // tm,),
        in_specs=[
            pl.BlockSpec((tm, D), lambda i: (i, 1)),
            pl.BlockSpec((tm, RWKV_HIDDEN), lambda i: (i, 0)),
            full(vec), full(w2), full(a2),
        ],
        out_specs=(row,) * 4,
        compiler_params=_cparams("parallel"),
        name="rwkv_gates",
    )(proj, hidden, vec, w2, a2)


def _rwkv_post_body(y_ref, r_ref, v_ref, k2_ref, hid_ref, g2_ref, vec_ref, o_ref):
    ones_bd = _head_ones()
    inv_n = 1.0 / RWKV_HEAD_DIM
    gate = jnp.dot(hid_ref[:, 2 * LANE:].astype(BF16), g2_ref[...], preferred_element_type=F32)
    for c in range(y_ref.shape[1] // LANE):
        sl = slice(c * LANE, (c + 1) * LANE)
        y = y_ref[:, sl]
        d = y - _head_sum(y, ones_bd) * inv_n
        yn = d * lax.rsqrt(_head_sum(d * d, ones_bd) * inv_n + RWKV_GN_EPS)
        bonus = _head_sum(r_ref[:, sl] * k2_ref[:, sl] * vec_ref[2:3, sl], ones_bd) * v_ref[:, sl]
        o_ref[:, sl] = ((yn * vec_ref[0:1, sl] + vec_ref[1:2, sl] + bonus) * gate[:, sl]).astype(o_ref.dtype)


def rwkv_post(y, proj, k2, hidden, g2, vec, *, tm=256):
    T, D = y.shape
    row = pl.BlockSpec((tm, D), lambda i: (i, 0))
    return pl.pallas_call(
        _rwkv_post_body,
        out_shape=jax.ShapeDtypeStruct((T, D), BF16),
        grid=(T // tm,),
        in_specs=[row, row, pl.BlockSpec((tm, D), lambda i: (i, 2)), row,
                  pl.BlockSpec((tm, RWKV_HIDDEN), lambda i: (i, 0)),
                  pl.BlockSpec(g2.shape, lambda i: (0, 0)),
                  pl.BlockSpec(vec.shape, lambda i: (0, 0))],
        out_specs=row,
        compiler_params=_cparams("parallel"),
        name="rwkv_post",
    )(y, proj, proj, k2, hidden, g2, vec)


NSA_QBLOCKS_PER_ITER = 4


def _nsa_sel_win_body(sel_ref, q_ref, ks_ref, vs_ref, kw_ref, vw_ref, osel_ref, owin_ref):
    blk, n_sel, W = NSA_SEL_BLOCK, NSA_N_SEL, NSA_WINDOW
    R, dh = NSA_HEADS // NSA_KV_GROUPS, NSA_HEAD_DIM
    S = q_ref.shape[1]
    n_blk = S // blk
    base = (pl.program_id(0) * NSA_KV_GROUPS + pl.program_id(1)) * (n_blk * n_sel)
    row_t = lax.broadcasted_iota(jnp.int32, (R * blk, 1), 0) & (blk - 1)
    col_s = lax.broadcasted_iota(jnp.int32, (1, n_sel * blk), 1)
    col_w = lax.broadcasted_iota(jnp.int32, (1, W + blk), 1)
    ones_s = jnp.ones((n_sel * blk, dh), BF16)
    ones_w = jnp.ones((W + blk, dh), BF16)

    def nt(a, b):
        return lax.dot_general(a, b, (((1,), (1,)), ((), ())), preferred_element_type=F32)

    def probs(s, mask):
        s = jnp.where(mask, s, NEG_INF)
        return jnp.exp(s - jnp.max(s, axis=-1, keepdims=True)).astype(BF16)

    def qblocks(it, carry):
        ids = [it * NSA_QBLOCKS_PER_ITER + n for n in range(NSA_QBLOCKS_PER_ITER)]
        q0s = [pl.multiple_of(i * blk, blk) for i in ids]
        w0s = [pl.multiple_of(jnp.maximum(q0 - W, 0), blk) for q0 in q0s]
        starts = [[sel_ref[base + i * n_sel + j] * blk for j in range(n_sel)] for i in ids]
        qss = []
        for q0 in q0s:
            qb = q_ref[0, pl.ds(q0, blk), :]
            qss.append(jnp.concatenate([qb[:, r * dh:(r + 1) * dh] for r in range(R)], axis=0))
        s_sel = [nt(qs, jnp.concatenate([ks_ref[0, pl.ds(pl.multiple_of(x, blk), blk), :] for x in st], axis=0))
                 for qs, st in zip(qss, starts)]
        s_win = [nt(qs, kw_ref[0, pl.ds(w0, W + blk), :]) for qs, w0 in zip(qss, w0s)]
        p_sel, p_win = [], []
        for q0, w0, st, ss, sw in zip(q0s, w0s, starts, s_sel, s_win):
            qpos = q0 + row_t
            tok = col_s & (blk - 1)
            for j in range(n_sel):
                tok = tok + jnp.where((col_s >> (blk.bit_length() - 1)) == j, st[j], 0)
            dist = qpos - (w0 + col_w)
            p_sel.append(probs(ss, tok <= qpos))
            p_win.append(probs(sw, (dist >= 0) & (dist < W)))
        o_sel = [jnp.dot(p, jnp.concatenate(
            [jnp.concatenate([vs_ref[0, pl.ds(pl.multiple_of(x, blk), blk), :] for x in st], axis=0), ones_s],
            axis=1), preferred_element_type=F32) for p, st in zip(p_sel, starts)]
        o_win = [jnp.dot(p, jnp.concatenate([vw_ref[0, pl.ds(w0, W + blk), :], ones_w], axis=1),
                         preferred_element_type=F32) for p, w0 in zip(p_win, w0s)]
        for q0, os_, ow in zip(q0s, o_sel, o_win):
            os_ = (os_[:, :dh] / os_[:, dh:]).astype(osel_ref.dtype)
            ow = (ow[:, :dh] / ow[:, dh:]).astype(owin_ref.dtype)
            for r in range(R):
                osel_ref[0, pl.ds(q0, blk), r * dh:(r + 1) * dh] = os_[r * blk:(r + 1) * blk]
                owin_ref[0, pl.ds(q0, blk), r * dh:(r + 1) * dh] = ow[r * blk:(r + 1) * blk]
        return carry

    lax.fori_loop(0, n_blk // NSA_QBLOCKS_PER_ITER, qblocks, 0)


def nsa_selected_window(sel, qkv):
    B, S = qkv.shape[:2]
    G, dh, HD = NSA_KV_GROUPS, NSA_HEAD_DIM, NSA_HEADS * NSA_HEAD_DIM
    qspec = pl.BlockSpec((1, S, HD // G), lambda b, g, sel: (b, 0, g))
    kspec = lambda c: pl.BlockSpec((1, S, dh), lambda b, g, sel: (b, 0, HD // dh + c * G + g))
    return pl.pallas_call(
        _nsa_sel_win_body,
        out_shape=(jax.ShapeDtypeStruct((B, S, HD), BF16), jax.ShapeDtypeStruct((B, S, HD), BF16)),
        grid_spec=pltpu.PrefetchScalarGridSpec(
            num_scalar_prefetch=1,
            grid=(B, G),
            in_specs=[qspec, kspec(2), kspec(3), kspec(4), kspec(5)],
            out_specs=(qspec, qspec),
        ),
        compiler_params=_cparams("parallel", "parallel"),
        name="nsa_selected_window",
    )(sel.reshape(-1), qkv, qkv, qkv, qkv, qkv)


def _nsa_rope_tables(S):
    half = NSA_ROPE_DIM // 2
    inv_freq = ROPE_THETA ** (-jnp.arange(half, dtype=F32) / half)
    ang = jnp.arange(S, dtype=F32)[:, None] * inv_freq
    cos, sin = jnp.cos(ang), jnp.sin(ang)
    rest = NSA_HEAD_DIM - NSA_ROPE_DIM
    cos_t = jnp.concatenate([cos, cos, jnp.ones((S, rest), F32)], axis=1)
    sin_t = jnp.concatenate([-sin, sin, jnp.zeros((S, rest), F32)], axis=1)
    return cos_t, sin_t


def _nsa_prep_body(x_ref, cos_ref, sin_ref, o_ref, kv_ref):
    dh, G = NSA_HEAD_DIM, NSA_KV_GROUPS
    half = NSA_ROPE_DIM // 2
    n_q = NSA_HEADS
    cos, sin = cos_ref[...], sin_ref[...]
    low = lax.broadcasted_iota(jnp.int32, cos.shape, 1) < half
    for blk in range(n_q + 6 * G):
        sl = slice(blk * dh, (blk + 1) * dh)
        x = x_ref[:, sl]
        c = (blk - n_q) // G
        if blk < n_q or c in (0, 2, 4):
            swapped = jnp.where(low, pltpu.roll(x, dh - half, axis=1), pltpu.roll(x, half, axis=1))
            x = x * cos + swapped * sin
        if blk < n_q:
            x = x * dh ** -0.5
        elif c in (0, 1):
            kv_ref[:, (blk - n_q) * dh:(blk - n_q + 1) * dh] = x
        o_ref[:, sl] = x.astype(o_ref.dtype)


def nsa_prepare(proj, S, *, tm=256):
    T = proj.shape[0]
    dh, G = NSA_HEAD_DIM, NSA_KV_GROUPS
    n = (NSA_HEADS + 6 * G) * dh
    cos_t, sin_t = _nsa_rope_tables(S)
    tab = pl.BlockSpec((tm, dh), lambda i: (i % (S // tm), 0))
    return pl.pallas_call(
        _nsa_prep_body,
        out_shape=(jax.ShapeDtypeStruct((T, n), BF16), jax.ShapeDtypeStruct((T, 2 * G * dh), F32)),
        grid=(T // tm,),
        in_specs=[pl.BlockSpec((tm, n), lambda i: (i, 0)), tab, tab],
        out_specs=(pl.BlockSpec((tm, n), lambda i: (i, 0)), pl.BlockSpec((tm, 2 * G * dh), lambda i: (i, 0))),
        compiler_params=_cparams("parallel"),
        name="nsa_prepare",
    )(proj, cos_t, sin_t)


def _nsa_compress_body(kc_ref, vc_ref, pe_ref, w1_ref, w2_ref, o_ref):
    L, stride = NSA_CMP_LEN, NSA_CMP_STRIDE
    n_grp = kc_ref.shape[1] // stride
    for c, ref in enumerate((kc_ref, vc_ref)):
        first = jnp.zeros((n_grp, NSA_CMP_HIDDEN), F32)
        second = jnp.zeros((n_grp, NSA_CMP_HIDDEN), F32)
        for l in range(stride):
            rows = ref[0, pl.ds(l, n_grp, stride=stride), :]
            first += jnp.dot((rows + pe_ref[c, l:l + 1, :]).astype(BF16), w1_ref[c, l],
                             preferred_element_type=F32)
            second += jnp.dot((rows + pe_ref[c, stride + l:stride + l + 1, :]).astype(BF16),
                              w1_ref[c, stride + l], preferred_element_type=F32)
        hid = jax.nn.gelu(first + pltpu.roll(second, n_grp - 1, axis=0))
        o_ref[0, 0, c] = jnp.dot(hid.astype(BF16), w2_ref[c], preferred_element_type=F32)


def nsa_compress(kcvc, pe, w1, w2):
    B, S = kcvc.shape[:2]
    G, dh = NSA_KV_GROUPS, NSA_HEAD_DIM
    full = lambda w: pl.BlockSpec(w.shape, lambda b, g: (0,) * w.ndim)
    return pl.pallas_call(
        _nsa_compress_body,
        out_shape=jax.ShapeDtypeStruct((B, G, 2, S // NSA_CMP_STRIDE, dh), F32),
        grid=(B, G),
        in_specs=[pl.BlockSpec((1, S, dh), lambda b, g: (b, 0, g)),
                  pl.BlockSpec((1, S, dh), lambda b, g: (b, 0, G + g)),
                  full(pe), full(w1), full(w2)],
        out_specs=pl.BlockSpec((1, 1, 2, S // NSA_CMP_STRIDE, dh), lambda b, g: (b, g, 0, 0, 0)),
        compiler_params=_cparams("parallel", "parallel"),
        name="nsa_compress",
    )(kcvc, kcvc, pe, w1, w2)


NSA_CMP_TILE = 512


def _nsa_cmp_body(q_ref, cmp_ref, o_ref, sel_ref):
    dh, blk = NSA_HEAD_DIM, NSA_SEL_BLOCK
    R = NSA_HEADS // NSA_KV_GROUPS
    tq = q_ref.shape[1]
    n_cmp = cmp_ref.shape[3]
    nq = tq // blk
    i = pl.program_id(2)
    k_cmp = cmp_ref[0, 0, 0].astype(BF16)
    v_cmp = cmp_ref[0, 0, 1].astype(BF16)
    pos = i * tq + lax.broadcasted_iota(jnp.int32, (tq, 1), 0)
    n_id = lax.broadcasted_iota(jnp.int32, (1, n_cmp), 1)
    visible = n_id * NSA_CMP_STRIDE + (NSA_CMP_LEN - 1) <= pos
    start = lax.broadcasted_iota(jnp.int32, (n_cmp, LANE), 0) * NSA_CMP_STRIDE
    bstart = lax.broadcasted_iota(jnp.int32, (n_cmp, LANE), 1) * blk
    overlap = jnp.where((start <= bstart + blk - 1) & (start + NSA_CMP_LEN - 1 >= bstart), 1.0, 0.0).astype(BF16)
    heads = [slice(r * dh, (r + 1) * dh) for r in range(R)]
    ss = [lax.dot_general(q_ref[0, :, h], k_cmp, (((1,), (1,)), ((), ())), preferred_element_type=F32)
          for h in heads]
    ss = [jnp.where(visible, s, NEG_INF) for s in ss]
    es = [jnp.exp(s - jnp.max(s, axis=-1, keepdims=True)) for s in ss]
    ps = [jnp.where(visible, e / jnp.sum(e, axis=-1, keepdims=True), 0.0).astype(BF16) for e in es]
    v_and_overlap = jnp.concatenate([v_cmp, overlap], axis=1)
    outs = [jnp.dot(p, v_and_overlap, preferred_element_type=F32) for p in ps]
    for h, o in zip(heads, outs):
        o_ref[0, :, h] = o[:, :dh].astype(o_ref.dtype)
    imp = sum(o[:, dh:] for o in outs)
    imp = imp.reshape(nq, blk, LANE).sum(axis=1)
    qb = i * nq + lax.broadcasted_iota(jnp.int32, (nq, 1), 0)
    kb = lax.broadcasted_iota(jnp.int32, (nq, LANE), 1)
    forced = (kb == 0) | (kb == qb) | (kb == qb - 1)
    val = jnp.where(forced, jnp.inf, jnp.where(kb <= qb, imp, -jnp.inf))
    kb_f = kb.astype(F32)
    avail = kb >= 0
    picks = jnp.zeros((nq, LANE), F32)
    for t in range(NSA_N_SEL):
        best = jnp.max(jnp.where(avail, val, -jnp.inf), axis=-1, keepdims=True)
        pick = jnp.min(jnp.where(avail & (val == best), kb_f, float(LANE)), axis=-1, keepdims=True)
        picks = jnp.where(kb == t, pick, picks)
        avail = avail & (kb_f != pick)
    sel_ref[0, 0] = picks.astype(jnp.int32)


def nsa_compressed_attention(qkv, cmp):
    B, S = qkv.shape[:2]
    G, dh, HD, tq = NSA_KV_GROUPS, NSA_HEAD_DIM, NSA_HEADS * NSA_HEAD_DIM, NSA_CMP_TILE
    nq = tq // NSA_SEL_BLOCK
    qspec = pl.BlockSpec((1, tq, HD // G), lambda b, g, i: (b, i, g))
    return pl.pallas_call(
        _nsa_cmp_body,
        out_shape=(jax.ShapeDtypeStruct((B, S, HD), BF16),
                   jax.ShapeDtypeStruct((B, G, S // NSA_SEL_BLOCK, LANE), jnp.int32)),
        grid=(B, G, S // tq),
        in_specs=[qspec, pl.BlockSpec((1, 1) + cmp.shape[2:], lambda b, g, i: (b, g, 0, 0, 0))],
        out_specs=(qspec, pl.BlockSpec((1, 1, nq, LANE), lambda b, g, i: (b, g, i, 0))),
        compiler_params=_cparams("parallel", "parallel", "arbitrary"),
        name="nsa_compressed_attention",
    )(qkv, cmp)


def _nsa_combine_body(z_ref, oc_ref, os_ref, ow_ref, o_ref):
    dh = NSA_HEAD_DIM
    gates = jax.nn.sigmoid(z_ref[...])
    for hh in range(NSA_HEADS):
        sl = slice(hh * dh, (hh + 1) * dh)
        o_ref[:, sl] = (gates[:, 3 * hh:3 * hh + 1] * oc_ref[:, sl].astype(F32)
                        + gates[:, 3 * hh + 1:3 * hh + 2] * os_ref[:, sl].astype(F32)
                        + gates[:, 3 * hh + 2:3 * hh + 3] * ow_ref[:, sl].astype(F32)).astype(o_ref.dtype)


def nsa_combine(proj, o_cmp, o_sel, o_win, *, tm=256):
    T, HD = o_cmp.shape
    row = pl.BlockSpec((tm, HD), lambda i: (i, 0))
    gate_blk = (NSA_HEADS + 6 * NSA_KV_GROUPS) * NSA_HEAD_DIM // LANE
    return pl.pallas_call(
        _nsa_combine_body,
        out_shape=jax.ShapeDtypeStruct((T, HD), BF16),
        grid=(T // tm,),
        in_specs=[pl.BlockSpec((tm, LANE), lambda i: (i, gate_blk)), row, row, row],
        out_specs=row,
        compiler_params=_cparams("parallel"),
        name="nsa_combine",
    )(proj, o_cmp, o_sel, o_win)


FOX_TILE = 256
FOX_HEADS_PER_STEP = 8


def _fox_body(q_ref, k_ref, v_ref, cq_ref, ck_ref, o_ref):
    t, dh = FOX_TILE, FOX_HEAD_DIM
    nh = q_ref.shape[2] // dh
    heads = [slice(dh * j, dh * (j + 1)) for j in range(nh)]
    i = pl.program_id(2)
    qs = [q_ref[0, :, h] for h in heads]
    cq_all = cq_ref[0]
    head_lane = lax.broadcasted_iota(jnp.int32, cq_all.shape, 1) - pl.program_id(1) * nh
    cqs = [jnp.sum(jnp.where(head_lane == j, cq_all, 0.0), axis=-1, keepdims=True) for j in range(nh)]
    causal = (lax.broadcasted_iota(jnp.int32, (t, t), 0) >= lax.broadcasted_iota(jnp.int32, (t, t), 1))
    ones = jnp.ones((t, dh), BF16)

    def step(j, carry, diagonal):
        k0 = pl.multiple_of(j * t, t)
        ss = [lax.dot_general(q, k_ref[0, pl.ds(k0, t), h], (((1,), (1,)), ((), ())),
                              preferred_element_type=F32) + (cq - ck_ref[0, n, :, pl.ds(k0, t)])
              for n, (q, cq, h) in enumerate(zip(qs, cqs, heads))]
        if diagonal:
            ss = [jnp.where(causal, s, NEG_INF) for s in ss]
        m_new = [jnp.maximum(m, jnp.max(s, axis=-1, keepdims=True)) for (m, _), s in zip(carry, ss)]
        ps = [jnp.exp(s - m).astype(BF16) for s, m in zip(ss, m_new)]
        pv = [jnp.dot(p, jnp.concatenate([v_ref[0, pl.ds(k0, t), h], ones], axis=1),
                      preferred_element_type=F32) for p, h in zip(ps, heads)]
        return tuple((mn, jnp.exp(m - mn) * acc + x) for (m, acc), mn, x in zip(carry, m_new, pv))

    init = tuple((jnp.full((t, 1), NEG_INF, F32), jnp.zeros((t, 2 * dh), F32)) for _ in heads)
    carry = lax.fori_loop(0, i, lambda j, c: step(j, c, False), init)
    carry = step(i, carry, True)
    for (_, acc), h in zip(carry, heads):
        o_ref[0, :, h] = (acc[:, :dh] / acc[:, dh:]).astype(o_ref.dtype)


def _fox_gate_body(z_ref, b_ref, o_ref):
    x = z_ref[0] + b_ref[...]
    acc = -(jnp.maximum(-x, 0.0) + jnp.log(1.0 + jnp.exp(-jnp.abs(x))))
    row = lax.broadcasted_iota(jnp.int32, acc.shape, 0)
    shift = 1
    while shift < acc.shape[0]:
        acc = acc + jnp.where(row >= shift, pltpu.roll(acc, shift, axis=0), 0.0)
        shift *= 2
    o_ref[0] = acc


def fox_gate_cumsum(z, b):
    B, S, L = z.shape
    blk = pl.BlockSpec((1, S, L), lambda i: (i, 0, 0))
    return pl.pallas_call(
        _fox_gate_body,
        out_shape=jax.ShapeDtypeStruct((B, S, L), F32),
        grid=(B,),
        in_specs=[blk, pl.BlockSpec((1, L), lambda i: (0, 0))],
        out_specs=blk,
        compiler_params=_cparams("parallel"),
        name="fox_gate_cumsum",
    )(z, b.reshape(1, L))


def fox_attention(proj, cum_q, cum_k):
    B, S = proj.shape[:2]
    H, dh, t, nh = FOX_HEADS, FOX_HEAD_DIM, FOX_TILE, FOX_HEADS_PER_STEP
    G = H // nh
    return pl.pallas_call(
        _fox_body,
        out_shape=jax.ShapeDtypeStruct((B, S, H * dh), BF16),
        grid=(B, G, S // t),
        in_specs=[
            pl.BlockSpec((1, t, nh * dh), lambda b, h, i: (b, i, h)),
            pl.BlockSpec((1, S, nh * dh), lambda b, h, i: (b, 0, G + h)),
            pl.BlockSpec((1, S, nh * dh), lambda b, h, i: (b, 0, 2 * G + h)),
            pl.BlockSpec((1, t, LANE), lambda b, h, i: (b, i, 0)),
            pl.BlockSpec((1, nh, 1, S), lambda b, h, i: (b, h, 0, 0)),
        ],
        out_specs=pl.BlockSpec((1, t, nh * dh), lambda b, h, i: (b, i, h)),
        compiler_params=_cparams("parallel", "parallel", "arbitrary"),
        name="fox_attention",
    )(proj, proj, proj, cum_q, cum_k)


RET_HEADS_PER_STEP = 2


def _ret_body(lg_ref, q_ref, k_ref, v_ref, g_ref, cos_ref, sin_ref, gn_ref, o_ref, r_ref):
    C, dk, dv = RET_CHUNK, RET_QK_DIM, RET_V_DIM
    half = dk // 2
    nh = r_ref.shape[0]
    qk = [slice(dk * n, dk * (n + 1)) for n in range(nh)]
    vv = [slice(dv * n, dv * (n + 1)) for n in range(nh)]
    lgs = [lg_ref[pl.program_id(1) * nh + n] for n in range(nh)]
    ii = lax.broadcasted_iota(jnp.int32, (C, C), 0)
    jj = lax.broadcasted_iota(jnp.int32, (C, C), 1)
    ti = lax.broadcasted_iota(jnp.int32, (C, 1), 0).astype(F32)
    decay_mask = [jnp.where(ii >= jj, jnp.exp((ii - jj).astype(F32) * lg), 0.0) for lg in lgs]
    q_scale = [jnp.exp((ti + 1.0) * lg) for lg in lgs]
    k_scale = [jnp.exp((C - 1.0 - ti) * lg) for lg in lgs]
    chunk_decay = [jnp.exp(jnp.full((1, 1), C, F32) * lg) for lg in lgs]
    r_ref[...] = jnp.zeros_like(r_ref)

    def rot(x, cos, sin):
        x1, x2 = x[:, :half], x[:, half:]
        return jnp.concatenate([x1 * cos - x2 * sin, x2 * cos + x1 * sin], axis=1)

    def chunk(c, carry):
        sl = pl.ds(pl.multiple_of(c * C, C), C)
        cos, sin = cos_ref[sl, :], sin_ref[sl, :]
        qs = [rot(q_ref[0, sl, h].astype(F32), cos, sin).astype(BF16) for h in qk]
        ks = [rot(k_ref[0, sl, h].astype(F32), cos, sin) * (dk ** -0.5) for h in qk]
        vs = [v_ref[0, sl, h] for h in vv]
        inner = [lax.dot_general(q, k.astype(BF16), (((1,), (1,)), ((), ())), preferred_element_type=F32) * dm
                 for q, k, dm in zip(qs, ks, decay_mask)]
        states = [r_ref[n] for n in range(nh)]
        cross = [jnp.dot(q, st.astype(BF16), preferred_element_type=F32) * sc
                 for q, st, sc in zip(qs, states, q_scale)]
        os_ = [jnp.dot(x.astype(BF16), v, preferred_element_type=F32) + cr for x, v, cr in zip(inner, vs, cross)]
        upd = [lax.dot_general((k * sc).astype(BF16), v, (((0,), (0,)), ((), ())), preferred_element_type=F32)
               for k, sc, v in zip(ks, k_scale, vs)]
        for n in range(nh):
            r_ref[n] = states[n] * chunk_decay[n] + upd[n]
        for o, h in zip(os_, vv):
            d = o - jnp.mean(o, axis=-1, keepdims=True)
            on = d * lax.rsqrt(jnp.mean(d * d, axis=-1, keepdims=True) + RET_GN_EPS)
            g = g_ref[0, sl, h].astype(F32)
            o_ref[0, sl, h] = ((g * jax.nn.sigmoid(g)) * (on * gn_ref[:, h])).astype(o_ref.dtype)
        return carry

    lax.fori_loop(0, q_ref.shape[1] // C, chunk, 0)


def retention_core(proj, gn_g):
    B, S = proj.shape[:2]
    H, dk, dv, nh = RET_HEADS, RET_QK_DIM, RET_V_DIM, RET_HEADS_PER_STEP
    G = H // nh
    pos = jnp.arange(S, dtype=F32)
    inv_freq = RET_THETA ** (-jnp.arange(dk // 2, dtype=F32) / (dk // 2))
    ang = pos[:, None] * inv_freq
    log_gamma = jnp.log(1.0 - 2.0 ** (-5.0 - jnp.arange(H, dtype=F32)))
    tab = pl.BlockSpec((S, dk // 2), lambda b, h: (0, 0))
    v_first = 2 * H * dk // (nh * dv)
    return pl.pallas_call(
        _ret_body,
        out_shape=jax.ShapeDtypeStruct((B, S, H * dv), BF16),
        grid=(B, G),
        in_specs=[
            pl.BlockSpec(memory_space=pltpu.SMEM),
            pl.BlockSpec((1, S, nh * dk), lambda b, h: (b, 0, h)),
            pl.BlockSpec((1, S, nh * dk), lambda b, h: (b, 0, G + h)),
            pl.BlockSpec((1, S, nh * dv), lambda b, h: (b, 0, v_first + h)),
            pl.BlockSpec((1, S, nh * dv), lambda b, h: (b, 0, v_first + G + h)),
            tab, tab,
            pl.BlockSpec((1, nh * dv), lambda b, h: (0, h)),
        ],
        out_specs=pl.BlockSpec((1, S, nh * dv), lambda b, h: (b, 0, h)),
        scratch_shapes=[pltpu.VMEM((nh, dk, dv), F32)],
        compiler_params=_cparams("parallel", "parallel"),
        name="retention_core",
    )(log_gamma, proj, proj, proj, proj, jnp.cos(ang), jnp.sin(ang), gn_g.reshape(1, H * dv))


def _pad_cols(w, n):
    return jnp.pad(w, ((0, 0), (0, n - w.shape[1])))


NSA_IN_PADDED = 42 * LANE


def _nsa_mixer(h, g, w_in, cmp_pe, cmp_w1, cmp_w2, B, S):
    T = h.shape[0]
    proj = norm_matmul(h, g, _pad_cols(w_in, NSA_IN_PADDED).astype(BF16))
    qkv, kcvc = nsa_prepare(proj, S)
    cmp = nsa_compress(kcvc.reshape(B, S, -1), cmp_pe, cmp_w1.astype(BF16), cmp_w2.astype(BF16))
    qkv = qkv.reshape(B, S, -1)
    o_cmp, sel = nsa_compressed_attention(qkv, cmp)
    o_sel, o_win = nsa_selected_window(sel[..., :NSA_N_SEL], qkv)
    return nsa_combine(proj, o_cmp.reshape(T, -1), o_sel.reshape(T, -1), o_win.reshape(T, -1))


def _rwkv_mixer(h, g, mu, w_rkv, w0, w1, w2, a0, a1, a2, g1, g2, k_k, k_a, r_k, ln_gb, B, S):
    T, D = h.shape
    pad_c = lambda w: _pad_cols(w, LANE)
    pad_r = lambda w: jnp.pad(w, ((0, LANE - w.shape[0]), (0, 0)))
    w_main = jnp.concatenate([w_rkv[0], w_rkv[1], w_rkv[2]], axis=1).astype(BF16)
    w_hidden = jnp.concatenate([pad_c(w1), pad_c(a1), g1], axis=1).astype(BF16)
    proj, hidden = rwkv_mix_project(h, g, mu, w_main, w_hidden, S)
    lw, a, kk, k2 = rwkv_gates(proj, hidden, jnp.stack([w0, a0, k_k, k_a]), pad_r(w2).astype(BF16),
                               pad_r(a2).astype(BF16))
    as3 = lambda t: t.reshape(B, S, -1)
    y = rwkv_recurrence(as3(proj), as3(lw), as3(k2), as3(kk), as3(a))
    vec = jnp.stack([ln_gb[0], ln_gb[1], r_k.reshape(D)])
    return rwkv_post(y.reshape(T, D), proj, k2, hidden, g2.astype(BF16), vec)


def _fox_mixer(h, g, w_in, b_f, B, S):
    HD = FOX_HEADS * FOX_HEAD_DIM
    w_qkv = jnp.concatenate([w_in[:, :HD] * FOX_HEAD_DIM ** -0.5, w_in[:, HD:3 * HD]], axis=1)
    proj = norm_matmul(h, g, w_qkv.astype(BF16), out_dtype=BF16)
    z = norm_matmul(h, g, _pad_cols(w_in[:, 3 * HD:], LANE).astype(BF16))
    cum = fox_gate_cumsum(z.reshape(B, S, LANE), jnp.pad(b_f, (0, LANE - FOX_HEADS)))
    cum_k = jnp.transpose(cum[:, :, :FOX_HEADS], (0, 2, 1))[:, :, None, :]
    return fox_attention(proj.reshape(B, S, 3 * HD), cum, cum_k).reshape(B * S, HD)


def kernel(x, p, norm_g, ffn_w_in, ffn_w_out, ple_w_proj, ple_w_gate, nsa_w_in, nsa_cmp_pe, nsa_cmp_w1, nsa_cmp_w2, nsa_w_out, rwkv_mu, rwkv_w_rkv, rwkv_w0, rwkv_w1, rwkv_w2, rwkv_a0, rwkv_a1, rwkv_a2, rwkv_g1, rwkv_g2, rwkv_k_k, rwkv_k_a, rwkv_r_k, rwkv_ln, rwkv_w_out, fox_w_in, fox_b_f, fox_w_out, ret_w_in, ret_gn_g, ret_w_out):
    B, S, D = x.shape
    T = B * S
    assert D == D_MODEL and S % 1024 == 0, x.shape
    h = x.reshape(T, D)
    bf = lambda w: w.astype(BF16)
    ffn_w_in, ffn_w_out, ple_w_proj, ple_w_gate = bf(ffn_w_in), bf(ffn_w_out), bf(ple_w_proj), bf(ple_w_gate)
    p = p.reshape(DEPTH, T, PLE_DIM)
    for i in range(DEPTH):
        m, j = i % N_MIXERS, i // N_MIXERS
        ng = norm_g[i]
        h = ffn_half_step(h, ng[0], ng[1], ffn_w_in, ffn_w_out, i, 0)
        if m == 0:
            y = _nsa_mixer(h, ng[2], nsa_w_in[j], nsa_cmp_pe[j], nsa_cmp_w1[j], nsa_cmp_w2[j], B, S)
            w_out = nsa_w_out[j]
        elif m == 1:
            y = _rwkv_mixer(h, ng[2], rwkv_mu[j], rwkv_w_rkv[j], rwkv_w0[j], rwkv_w1[j], rwkv_w2[j],
                            rwkv_a0[j], rwkv_a1[j], rwkv_a2[j], rwkv_g1[j], rwkv_g2[j],
                            rwkv_k_k[j], rwkv_k_a[j], rwkv_r_k[j], rwkv_ln[j], B, S)
            w_out = rwkv_w_out[j]
        elif m == 2:
            y = _fox_mixer(h, ng[2], fox_w_in[j], fox_b_f[j], B, S)
            w_out = fox_w_out[j]
        else:
            proj = norm_matmul(h, ng[2], bf(ret_w_in[j]), out_dtype=BF16)
            y = retention_core(proj.reshape(B, S, RET_IN), ret_gn_g[j]).reshape(T, -1)
            w_out = ret_w_out[j]
        h = matmul_norm_residual(y, bf(w_out), ng[3], h)
        h = ffn_half_step(h, ng[4], ng[5], ffn_w_in, ffn_w_out, i, 1)
        h = ple_step(h, p, ng[6], ng[7], ple_w_proj, ple_w_gate, i)
    return h.reshape(B, S, D)
```

```python
import functools

import jax
import jax.numpy as jnp
from jax import lax
from jax.experimental import pallas as pl
from jax.experimental.pallas import tpu as pltpu

D_MODEL = 2048
DEPTH = 4
N_MIXERS = 4
PLE_DIM = 256
D_FF = 5632
RMS_EPS = 1e-6
NEG_INF = -1e30

NSA_HEADS = 16
NSA_KV_GROUPS = 4
NSA_HEAD_DIM = D_MODEL // NSA_HEADS
NSA_CMP_LEN = 32
NSA_CMP_STRIDE = 16
NSA_CMP_HIDDEN = 2 * NSA_HEAD_DIM
NSA_SEL_BLOCK = 64
NSA_N_SEL = 8
NSA_WINDOW = 512
NSA_ROPE_DIM = NSA_HEAD_DIM // 4
ROPE_THETA = 500000.0

RWKV_HEAD_DIM = 64
RWKV_GN_EPS = 64e-5

FOX_HEADS = 16
FOX_HEAD_DIM = D_MODEL // FOX_HEADS

RET_HEADS = 8
RET_QK_DIM = D_MODEL // RET_HEADS
RET_V_DIM = 2 * D_MODEL // RET_HEADS
RET_CHUNK = 128
RET_THETA = 10000.0
RET_GN_EPS = 1e-5
RET_IN = 2 * RET_HEADS * RET_QK_DIM + 2 * RET_HEADS * RET_V_DIM

V7X_VMEM_LIMIT_BYTES = 56 * 1024 * 1024
V7X_VMEM_LIMIT_BYTES_LARGE = 62 * 1024 * 1024
LANE = 128
SUBLANE = 8
LOG2E = 1.4426950408889634

F32 = jnp.float32
BF16 = jnp.bfloat16


def _cparams(*sem):
    return pltpu.CompilerParams(dimension_semantics=sem, vmem_limit_bytes=V7X_VMEM_LIMIT_BYTES)


def _rms(x, g):
    return x * lax.rsqrt(jnp.mean(x * x, axis=-1, keepdims=True) + RMS_EPS) * g


def _ffn_body(h_ref, g0_ref, g1_ref, wg_ref, wu_ref, wo_ref, o_ref, xn_ref):
    f = pl.program_id(1)

    @pl.when(f == 0)
    def _():
        xn_ref[...] = _rms(h_ref[...], g0_ref[...]).astype(BF16)
        o_ref[...] = jnp.zeros_like(o_ref)

    xn = xn_ref[...]
    gate = jnp.dot(xn, wg_ref[...], preferred_element_type=F32)
    up = jnp.dot(xn, wu_ref[...], preferred_element_type=F32)
    act = (gate * jax.nn.sigmoid(gate)) * up
    o_ref[...] += jnp.dot(act.astype(BF16), wo_ref[...], preferred_element_type=F32)

    @pl.when(f == pl.num_programs(1) - 1)
    def _():
        o_ref[...] = h_ref[...] + 0.5 * _rms(o_ref[...], g1_ref[...])


def ffn_half_step(h, g0, g1, w_in, w_out, layer, half, *, tm=1024, tf=512):
    T, D = h.shape
    nf = D_FF // tf
    return pl.pallas_call(
        _ffn_body,
        out_shape=jax.ShapeDtypeStruct((T, D), F32),
        grid=(T // tm, nf),
        in_specs=[
            pl.BlockSpec((tm, D), lambda i, f: (i, 0)),
            pl.BlockSpec((1, D), lambda i, f: (0, 0)),
            pl.BlockSpec((1, D), lambda i, f: (0, 0)),
            pl.BlockSpec((None, None, D, tf), lambda i, f: (layer, half, 0, f)),
            pl.BlockSpec((None, None, D, tf), lambda i, f: (layer, half, 0, f + nf)),
            pl.BlockSpec((None, None, tf, D), lambda i, f: (layer, half, f, 0)),
        ],
        out_specs=pl.BlockSpec((tm, D), lambda i, f: (i, 0)),
        scratch_shapes=[pltpu.VMEM((tm, D), BF16)],
        compiler_params=pltpu.CompilerParams(dimension_semantics=("parallel", "arbitrary"),
                                             vmem_limit_bytes=V7X_VMEM_LIMIT_BYTES_LARGE),
        name="ffn_half_step",
    )(h, g0.reshape(1, D), g1.reshape(1, D), w_in, w_in, w_out)


def _norm_mm_body(x_ref, g_ref, w_ref, o_ref, xn_ref):
    @pl.when(pl.program_id(1) == 0)
    def _():
        xn_ref[...] = _rms(x_ref[...], g_ref[...]).astype(BF16)

    o_ref[...] = jnp.dot(xn_ref[...], w_ref[...], preferred_element_type=F32).astype(o_ref.dtype)


def _col_tile(n, cap=1024):
    best = LANE
    for t in range(LANE, cap + 1, LANE):
        if n % t == 0:
            best = t
    return best


def norm_matmul(x, g, w, *, out_dtype=F32, tm=1024):
    T, K = x.shape
    N = w.shape[1]
    tn = _col_tile(N)
    return pl.pallas_call(
        _norm_mm_body,
        out_shape=jax.ShapeDtypeStruct((T, N), out_dtype),
        grid=(T // tm, N // tn),
        in_specs=[
            pl.BlockSpec((tm, K), lambda i, j: (i, 0)),
            pl.BlockSpec((1, K), lambda i, j: (0, 0)),
            pl.BlockSpec((K, tn), lambda i, j: (0, j)),
        ],
        out_specs=pl.BlockSpec((tm, tn), lambda i, j: (i, j)),
        scratch_shapes=[pltpu.VMEM((tm, K), BF16)],
        compiler_params=_cparams("parallel", "arbitrary"),
        name="norm_matmul",
    )(x, g.reshape(1, K), w)


def _mm_res_body(y_ref, w_ref, g_ref, h_ref, o_ref, acc_ref):
    k = pl.program_id(1)

    @pl.when(k == 0)
    def _():
        acc_ref[...] = jnp.zeros_like(acc_ref)

    acc_ref[...] += jnp.dot(y_ref[...].astype(BF16), w_ref[...], preferred_element_type=F32)

    @pl.when(k == pl.num_programs(1) - 1)
    def _():
        o_ref[...] = h_ref[...] + _rms(acc_ref[...], g_ref[...])


def _mm_res_single_body(y_ref, w_ref, g_ref, h_ref, o_ref):
    acc = jnp.dot(y_ref[...].astype(BF16), w_ref[...], preferred_element_type=F32)
    o_ref[...] = h_ref[...] + _rms(acc, g_ref[...])


def matmul_norm_residual(y, w, g, h, *, tm=512, tk=2048):
    T, K = y.shape
    D = w.shape[1]
    if K == tk:
        return pl.pallas_call(
            _mm_res_single_body,
            out_shape=jax.ShapeDtypeStruct((T, D), F32),
            grid=(T // tm,),
            in_specs=[
                pl.BlockSpec((tm, K), lambda i: (i, 0)),
                pl.BlockSpec((K, D), lambda i: (0, 0)),
                pl.BlockSpec((1, D), lambda i: (0, 0)),
                pl.BlockSpec((tm, D), lambda i: (i, 0)),
            ],
            out_specs=pl.BlockSpec((tm, D), lambda i: (i, 0)),
            compiler_params=_cparams("parallel"),
            name="matmul_norm_residual",
        )(y, w, g.reshape(1, D), h)
    return pl.pallas_call(
        _mm_res_body,
        out_shape=jax.ShapeDtypeStruct((T, D), F32),
        grid=(T // tm, K // tk),
        in_specs=[
            pl.BlockSpec((tm, tk), lambda i, k: (i, k)),
            pl.BlockSpec((tk, D), lambda i, k: (k, 0)),
            pl.BlockSpec((1, D), lambda i, k: (0, 0)),
            pl.BlockSpec((tm, D), lambda i, k: (i, 0)),
        ],
        out_specs=pl.BlockSpec((tm, D), lambda i, k: (i, 0)),
        scratch_shapes=[pltpu.VMEM((tm, D), F32)],
        compiler_params=_cparams("parallel", "arbitrary"),
        name="matmul_norm_residual",
    )(y, w, g.reshape(1, D), h)


PLE_SUBTILES = 2


def _ple_body(h_ref, p_ref, g6_ref, g7_ref, wp_ref, wg_ref, o_ref):
    rows = h_ref.shape[0] // PLE_SUBTILES
    parts = [slice(n * rows, (n + 1) * rows) for n in range(PLE_SUBTILES)]
    xns = [_rms(h_ref[r, :], g6_ref[...]).astype(BF16) for r in parts]
    zs = [jnp.dot(xn, wg_ref[...], preferred_element_type=F32) for xn in xns]
    es = [jnp.dot(p_ref[r, :].astype(BF16), wp_ref[...], preferred_element_type=F32) for r in parts]
    for r, z, e in zip(parts, zs, es):
        o_ref[r, :] = h_ref[r, :] + _rms(e * jax.nn.sigmoid(z), g7_ref[...])


def ple_step(h, p, g6, g7, wp, wg, layer, *, tm=512):
    T, D = h.shape
    P = p.shape[2]
    return pl.pallas_call(
        _ple_body,
        out_shape=jax.ShapeDtypeStruct((T, D), F32),
        grid=(T // tm,),
        in_specs=[
            pl.BlockSpec((tm, D), lambda i: (i, 0)),
            pl.BlockSpec((None, tm, P), lambda i: (layer, i, 0)),
            pl.BlockSpec((1, D), lambda i: (0, 0)),
            pl.BlockSpec((1, D), lambda i: (0, 0)),
            pl.BlockSpec((None, P, D), lambda i: (layer, 0, 0)),
            pl.BlockSpec((None, D, D), lambda i: (layer, 0, 0)),
        ],
        out_specs=pl.BlockSpec((tm, D), lambda i: (i, 0)),
        compiler_params=_cparams("parallel"),
        name="ple_step",
    )(h, p, g6.reshape(1, D), g7.reshape(1, D), wp, wg)


RWKV_CHUNK = 64
RWKV_PAIRS_PER_STEP = 16
RWKV_TIME_BLOCK = 256


def _rwkv_body(r_ref, lw_ref, k_ref, v_ref, kk_ref, a_ref, y_ref, s_ref):
    C = RWKV_CHUNK
    N = RWKV_HEAD_DIM
    lane = lax.broadcasted_iota(jnp.int32, (C, 2 * N), 1)
    row = lax.broadcasted_iota(jnp.int32, (C, 2 * N), 0)
    first_head = lane < N
    ri = lax.broadcasted_iota(jnp.int32, (2 * C, 2 * C), 0)
    ci = lax.broadcasted_iota(jnp.int32, (2 * C, 2 * C), 1)
    strict = ri > ci
    incl = ri >= ci
    eye = jnp.where(ri == ci, 1.0, 0.0).astype(F32)
    corner = [((ri >> (lvl + 1)) == (ci >> (lvl + 1))) & ((ri & (1 << lvl)) != 0) & ((ci & (1 << lvl)) == 0)
              for lvl in range(C.bit_length() - 1)]

    def stack(x):
        return jnp.concatenate([jnp.where(first_head, x, 0.0), jnp.where(first_head, 0.0, x)], axis=0)

    def nt(a, b):
        return lax.dot_general(a.astype(BF16), b.astype(BF16), (((1,), (1,)), ((), ())),
                               preferred_element_type=F32)

    def nn(a, b):
        return jnp.dot(a.astype(BF16), b.astype(BF16), preferred_element_type=F32)

    def tn(a, b):
        return lax.dot_general(a.astype(BF16), b.astype(BF16), (((0,), (0,)), ((), ())),
                               preferred_element_type=F32)

    n_pairs = r_ref.shape[2] // (2 * N)
    lanes = [slice(2 * N * j, 2 * N * (j + 1)) for j in range(n_pairs)]

    def prep(sl, ln):
        r, lw, k, v, kk, a = (ref[0, sl, ln] for ref in (r_ref, lw_ref, k_ref, v_ref, kk_ref, a_ref))
        cl = lw
        for sh in (1, 2, 4, 8, 16, 32):
            cl = cl + jnp.where(row >= sh, pltpu.roll(cl, sh, axis=0), 0.0)
        mid = cl[C // 2 - 1:C // 2, :]
        last = cl[C - 1:C, :]
        e_neg = jnp.exp(mid - cl)
        e_end = jnp.exp(last - mid)
        b_til = stack(kk * a * e_neg)
        k_til = stack(k * e_neg)
        return dict(
            a_bar=stack(-kk * jnp.exp(cl - lw - mid)), r_bar=stack(r * jnp.exp(cl - mid)),
            b_til=b_til, k_til=k_til, v_st=stack(v), e_mid=jnp.exp(mid), w_tot=jnp.exp(last),
            bk_end=jnp.concatenate([b_til * e_end, k_til * e_end], axis=0))

    def chunk(c, states):
        sl = pl.ds(pl.multiple_of(c * C, C), C)
        ps = [prep(sl, ln) for ln in lanes]
        gs = [nt(jnp.concatenate([p["a_bar"], p["r_bar"]], axis=0),
                 jnp.concatenate([p["b_til"], p["k_til"]], axis=0)) for p in ps]
        a_ab = [jnp.where(strict, g[:2 * C, :2 * C], 0.0) for g in gs]
        a_ak = [jnp.where(strict, g[:2 * C, 2 * C:], 0.0) for g in gs]
        a_rb = [jnp.where(incl, g[2 * C:, :2 * C], 0.0) for g in gs]
        a_rk = [jnp.where(incl, g[2 * C:, 2 * C:], 0.0) for g in gs]
        inv = [eye + jnp.where(corner[0], x, 0.0) for x in a_ab]
        for lvl in range(1, len(corner)):
            tmp = [nn(jnp.where(corner[lvl], x, 0.0), t) for x, t in zip(a_ab, inv)]
            inv = [t + nn(t, x) for t, x in zip(inv, tmp)]
        s_mid = [st * p["e_mid"] for st, p in zip(states, ps)]
        rhs = [nt(p["a_bar"], sm) + nn(ak, p["v_st"]) for p, sm, ak in zip(ps, s_mid, a_ak)]
        us = [nn(t, x) for t, x in zip(inv, rhs)]
        ys = [nt(p["r_bar"], sm) + nn(rb, u) + nn(rk, p["v_st"])
              for p, sm, rb, rk, u in zip(ps, s_mid, a_rb, a_rk, us)]
        for ln, y in zip(lanes, ys):
            y_ref[0, sl, ln] = y[:C] + y[C:]
        return tuple(st * p["w_tot"] + tn(jnp.concatenate([u, p["v_st"]], axis=0), p["bk_end"])
                     for st, p, u in zip(states, ps, us))

    @pl.when(pl.program_id(2) == 0)
    def _():
        s_ref[...] = jnp.zeros_like(s_ref)

    states = lax.fori_loop(0, r_ref.shape[1] // C, chunk, tuple(s_ref[j] for j in range(n_pairs)))
    for j, st in enumerate(states):
        s_ref[j] = st


def rwkv_recurrence(proj, lw, k, kk, a):
    B, S, D = lw.shape
    lanes = 2 * RWKV_HEAD_DIM * RWKV_PAIRS_PER_STEP
    ts = min(S, RWKV_TIME_BLOCK)
    spec = pl.BlockSpec((1, ts, lanes), lambda b, j, t: (b, t, j))
    v_spec = pl.BlockSpec((1, ts, lanes), lambda b, j, t: (b, t, 2 * D // lanes + j))
    return pl.pallas_call(
        _rwkv_body,
        out_shape=jax.ShapeDtypeStruct((B, S, D), F32),
        grid=(B, D // lanes, S // ts),
        in_specs=[spec, spec, spec, v_spec, spec, spec],
        out_specs=spec,
        scratch_shapes=[pltpu.VMEM((RWKV_PAIRS_PER_STEP, 2 * RWKV_HEAD_DIM, 2 * RWKV_HEAD_DIM), F32)],
        compiler_params=_cparams("parallel", "parallel", "arbitrary"),
        name="rwkv_recurrence",
    )(proj, lw, k, proj, kk, a)


RWKV_PROJ_TN = 1024
RWKV_HIDDEN = 4 * LANE


RWKV_MIX_ROWS = 64


def _rwkv_mixes(first_of_seq, h_ref, hp_ref, g_ref, mu_ref, xm_ref, mixes):
    rows = RWKV_MIX_ROWS
    g = g_ref[...]
    before_tile = jnp.where(first_of_seq, 0.0, _rms(hp_ref[SUBLANE - 1:SUBLANE, :], g))
    row = lax.broadcasted_iota(jnp.int32, (rows, h_ref.shape[1]), 0)

    def chunk(c, prev):
        sl = pl.ds(pl.multiple_of(c * rows, rows), rows)
        u = _rms(h_ref[sl, :], g)
        xx = jnp.where(row == 0, prev, pltpu.roll(u, 1, axis=0)) - u
        for n, m in enumerate(mixes):
            xm_ref[n, sl, :] = (u + xx * mu_ref[m:m + 1, :]).astype(BF16)
        return u[rows - 1:rows, :]

    lax.fori_loop(0, h_ref.shape[0] // rows, chunk, before_tile)


def _rwkv_proj_body(tiles_per_seq, h_ref, hp_ref, g_ref, mu_ref, w_ref, o_ref, xm_ref):
    i, j = pl.program_id(0), pl.program_id(1)

    @pl.when(j == 0)
    def _():
        _rwkv_mixes(lax.rem(i, tiles_per_seq) == 0, h_ref, hp_ref, g_ref, mu_ref, xm_ref, (0, 1, 2))

    o_ref[...] = jnp.dot(xm_ref[j // (pl.num_programs(1) // 3)], w_ref[...], preferred_element_type=F32)


def _rwkv_hidden_body(tiles_per_seq, h_ref, hp_ref, g_ref, mu_ref, w_ref, o_ref):
    g = g_ref[...]
    u = _rms(h_ref[...], g)
    before_tile = jnp.where(lax.rem(pl.program_id(0), tiles_per_seq) == 0, 0.0,
                            _rms(hp_ref[SUBLANE - 1:SUBLANE, :], g))
    row = lax.broadcasted_iota(jnp.int32, u.shape, 0)
    xx = jnp.where(row == 0, before_tile, pltpu.roll(u, 1, axis=0)) - u
    mix = lambda c: (u + xx * mu_ref[c:c + 1, :]).astype(BF16)
    o_ref[:, :LANE] = jnp.tanh(jnp.dot(mix(3), w_ref[:, :LANE], preferred_element_type=F32))
    o_ref[:, LANE:2 * LANE] = jnp.dot(mix(4), w_ref[:, LANE:2 * LANE], preferred_element_type=F32)
    o_ref[:, 2 * LANE:] = jax.nn.sigmoid(jnp.dot(mix(5), w_ref[:, 2 * LANE:], preferred_element_type=F32))


def rwkv_mix_project(h, g, mu, w_rkv, w_hidden, seq_len, *, tm=1024, tm_hidden=512):
    T, D = h.shape
    tn = RWKV_PROJ_TN

    def prev_rows(rows):
        return lambda i, *_: (jnp.maximum(i * (rows // SUBLANE) - 1, 0), 0)

    rkv = pl.pallas_call(
        functools.partial(_rwkv_proj_body, seq_len // tm),
        out_shape=jax.ShapeDtypeStruct((T, 3 * D), F32),
        grid=(T // tm, 3 * D // tn),
        in_specs=[
            pl.BlockSpec((tm, D), lambda i, j: (i, 0)),
            pl.BlockSpec((SUBLANE, D), prev_rows(tm)),
            pl.BlockSpec((1, D), lambda i, j: (0, 0)),
            pl.BlockSpec((6, D), lambda i, j: (0, 0)),
            pl.BlockSpec((D, tn), lambda i, j: (0, j)),
        ],
        out_specs=pl.BlockSpec((tm, tn), lambda i, j: (i, j)),
        scratch_shapes=[pltpu.VMEM((3, tm, D), BF16)],
        compiler_params=_cparams("parallel", "arbitrary"),
        name="rwkv_mix_project",
    )(h, h, g.reshape(1, D), mu, w_rkv)
    tm = tm_hidden
    hidden = pl.pallas_call(
        functools.partial(_rwkv_hidden_body, seq_len // tm),
        out_shape=jax.ShapeDtypeStruct((T, RWKV_HIDDEN), F32),
        grid=(T // tm,),
        in_specs=[
            pl.BlockSpec((tm, D), lambda i: (i, 0)),
            pl.BlockSpec((SUBLANE, D), prev_rows(tm)),
            pl.BlockSpec((1, D), lambda i: (0, 0)),
            pl.BlockSpec((6, D), lambda i: (0, 0)),
            pl.BlockSpec((D, RWKV_HIDDEN), lambda i: (0, 0)),
        ],
        out_specs=pl.BlockSpec((tm, RWKV_HIDDEN), lambda i: (i, 0)),
        compiler_params=_cparams("parallel"),
        name="rwkv_mix_hidden",
    )(h, h, g.reshape(1, D), mu, w_hidden)
    return rkv, hidden


def _head_sum(x, ones_bd):
    hi = x.astype(BF16)
    lo = (x - hi.astype(F32)).astype(BF16)
    return (jnp.dot(hi, ones_bd, preferred_element_type=F32) + jnp.dot(lo, ones_bd, preferred_element_type=F32))


def _head_ones():
    shift = RWKV_HEAD_DIM.bit_length() - 1
    r = lax.broadcasted_iota(jnp.int32, (LANE, LANE), 0) >> shift
    c = lax.broadcasted_iota(jnp.int32, (LANE, LANE), 1) >> shift
    return jnp.where(r == c, 1.0, 0.0).astype(BF16)


def _rwkv_gates_body(k_ref, hid_ref, vec_ref, w2_ref, a2_ref, g2_ref, lw_ref, a_ref, kk_ref, k2_ref, gate_ref):
    hid = hid_ref[...].astype(BF16)
    z = vec_ref[0:1, :] + jnp.dot(hid[:, :LANE], w2_ref[...], preferred_element_type=F32)
    softplus = jnp.maximum(-z, 0.0) + jnp.log(1.0 + jnp.exp(-jnp.abs(z)))
    lw_ref[...] = -jnp.exp(-softplus - 0.5)
    a = jax.nn.sigmoid(vec_ref[1:2, :] + jnp.dot(hid[:, LANE:2 * LANE], a2_ref[...], preferred_element_type=F32))
    a_ref[...] = a
    gate_ref[...] = jnp.dot(hid[:, 2 * LANE:], g2_ref[...], preferred_element_type=F32)
    k = k_ref[...]
    k2_ref[...] = k * (1.0 + (a - 1.0) * vec_ref[3:4, :])
    kk = k * vec_ref[2:3, :]
    ones_bd = _head_ones()
    for c in range(k.shape[1] // LANE):
        blk = kk[:, c * LANE:(c + 1) * LANE]
        norm = jnp.sqrt(_head_sum(blk * blk, ones_bd))
        kk_ref[:, c * LANE:(c + 1) * LANE] = blk / jnp.maximum(norm, 1e-12)


def rwkv_gates(proj, hidden, vec, w2, a2, g2, *, tm=256):
    T = proj.shape[0]
    D = w2.shape[1]
    row = pl.BlockSpec((tm, D), lambda i: (i, 0))
    full = lambda w: pl.BlockSpec(w.shape, lambda i: (0, 0))
    return pl.pallas_call(
        _rwkv_gates_body,
        out_shape=tuple(jax.ShapeDtypeStruct((T, D), F32) for _ in range(5)),
        grid=(T // tm,),
        in_specs=[
            pl.BlockSpec((tm, D), lambda i: (i, 1)),
            pl.BlockSpec((tm, RWKV_HIDDEN), lambda i: (i, 0)),
            full(vec), full(w2), full(a2), full(g2),
        ],
        out_specs=(row,) * 5,
        compiler_params=_cparams("parallel"),
        name="rwkv_gates",
    )(proj, hidden, vec, w2, a2, g2)


def _rwkv_post_body(y_ref, r_ref, v_ref, k2_ref, gate_ref, vec_ref, o_ref):
    ones_bd = _head_ones()
    inv_n = 1.0 / RWKV_HEAD_DIM
    for c in range(y_ref.shape[1] // LANE):
        sl = slice(c * LANE, (c + 1) * LANE)
        y = y_ref[:, sl]
        d = y - _head_sum(y, ones_bd) * inv_n
        yn = d * lax.rsqrt(_head_sum(d * d, ones_bd) * inv_n + RWKV_GN_EPS)
        bonus = _head_sum(r_ref[:, sl] * k2_ref[:, sl] * vec_ref[2:3, sl], ones_bd) * v_ref[:, sl]
        o_ref[:, sl] = ((yn * vec_ref[0:1, sl] + vec_ref[1:2, sl] + bonus) * gate_ref[:, sl]).astype(o_ref.dtype)


def rwkv_post(y, proj, k2, gate, vec, *, tm=256):
    T, D = y.shape
    row = pl.BlockSpec((tm, D), lambda i: (i, 0))
    return pl.pallas_call(
        _rwkv_post_body,
        out_shape=jax.ShapeDtypeStruct((T, D), BF16),
        grid=(T // tm,),
        in_specs=[row, row, pl.BlockSpec((tm, D), lambda i: (i, 2)), row, row,
                  pl.BlockSpec(vec.shape, lambda i: (0, 0))],
        out_specs=row,
        compiler_params=_cparams("parallel"),
        name="rwkv_post",
    )(y, proj, proj, k2, gate, vec)


NSA_QBLOCKS_PER_ITER = 4


def _nsa_sel_win_body(sel_ref, q_ref, ks_ref, vs_ref, kw_ref, vw_ref, osel_ref, owin_ref):
    blk, n_sel, W = NSA_SEL_BLOCK, NSA_N_SEL, NSA_WINDOW
    R, dh = NSA_HEADS // NSA_KV_GROUPS, NSA_HEAD_DIM
    S = q_ref.shape[1]
    n_blk = S // blk
    base = (pl.program_id(0) * NSA_KV_GROUPS + pl.program_id(1)) * (n_blk * n_sel)
    row_t = lax.broadcasted_iota(jnp.int32, (R * blk, 1), 0) & (blk - 1)
    col_s = lax.broadcasted_iota(jnp.int32, (1, n_sel * blk), 1)
    col_w = lax.broadcasted_iota(jnp.int32, (1, W + blk), 1)
    ones_s = jnp.ones((n_sel * blk, dh), BF16)
    ones_w = jnp.ones((W + blk, dh), BF16)

    def nt(a, b):
        return lax.dot_general(a, b, (((1,), (1,)), ((), ())), preferred_element_type=F32)

    def probs(s, mask):
        s = jnp.where(mask, s, NEG_INF)
        return jnp.exp2(s - jnp.max(s, axis=-1, keepdims=True)).astype(BF16)

    def qblocks(it, carry):
        ids = [it * NSA_QBLOCKS_PER_ITER + n for n in range(NSA_QBLOCKS_PER_ITER)]
        q0s = [pl.multiple_of(i * blk, blk) for i in ids]
        w0s = [pl.multiple_of(jnp.maximum(q0 - W, 0), blk) for q0 in q0s]
        starts = [[sel_ref[base + i * n_sel + j] * blk for j in range(n_sel)] for i in ids]
        qss = []
        for q0 in q0s:
            qb = q_ref[0, pl.ds(q0, blk), :]
            qss.append(jnp.concatenate([qb[:, r * dh:(r + 1) * dh] for r in range(R)], axis=0))
        s_sel = [nt(qs, jnp.concatenate([ks_ref[0, pl.ds(pl.multiple_of(x, blk), blk), :] for x in st], axis=0))
                 for qs, st in zip(qss, starts)]
        s_win = [nt(qs, kw_ref[0, pl.ds(w0, W + blk), :]) for qs, w0 in zip(qss, w0s)]
        p_sel, p_win = [], []
        for q0, w0, st, ss, sw in zip(q0s, w0s, starts, s_sel, s_win):
            qpos = q0 + row_t
            tok = col_s & (blk - 1)
            for j in range(n_sel):
                tok = tok + jnp.where((col_s >> (blk.bit_length() - 1)) == j, st[j], 0)
            dist = qpos - (w0 + col_w)
            p_sel.append(probs(ss, tok <= qpos))
            p_win.append(probs(sw, (dist >= 0) & (dist < W)))
        o_sel = [jnp.dot(p, jnp.concatenate(
            [jnp.concatenate([vs_ref[0, pl.ds(pl.multiple_of(x, blk), blk), :] for x in st], axis=0), ones_s],
            axis=1), preferred_element_type=F32) for p, st in zip(p_sel, starts)]
        o_win = [jnp.dot(p, jnp.concatenate([vw_ref[0, pl.ds(w0, W + blk), :], ones_w], axis=1),
                         preferred_element_type=F32) for p, w0 in zip(p_win, w0s)]
        for q0, os_, ow in zip(q0s, o_sel, o_win):
            os_ = (os_[:, :dh] / os_[:, dh:]).astype(osel_ref.dtype)
            ow = (ow[:, :dh] / ow[:, dh:]).astype(owin_ref.dtype)
            for r in range(R):
                osel_ref[0, pl.ds(q0, blk), r * dh:(r + 1) * dh] = os_[r * blk:(r + 1) * blk]
                owin_ref[0, pl.ds(q0, blk), r * dh:(r + 1) * dh] = ow[r * blk:(r + 1) * blk]
        return carry

    lax.fori_loop(0, n_blk // NSA_QBLOCKS_PER_ITER, qblocks, 0)


def nsa_selected_window(sel, qkv):
    B, S = qkv.shape[:2]
    G, dh, HD = NSA_KV_GROUPS, NSA_HEAD_DIM, NSA_HEADS * NSA_HEAD_DIM
    qspec = pl.BlockSpec((1, S, HD // G), lambda b, g, sel: (b, 0, g))
    kspec = lambda c: pl.BlockSpec((1, S, dh), lambda b, g, sel: (b, 0, HD // dh + c * G + g))
    return pl.pallas_call(
        _nsa_sel_win_body,
        out_shape=(jax.ShapeDtypeStruct((B, S, HD), BF16), jax.ShapeDtypeStruct((B, S, HD), BF16)),
        grid_spec=pltpu.PrefetchScalarGridSpec(
            num_scalar_prefetch=1,
            grid=(B, G),
            in_specs=[qspec, kspec(2), kspec(3), kspec(4), kspec(5)],
            out_specs=(qspec, qspec),
        ),
        compiler_params=_cparams("parallel", "parallel"),
        name="nsa_selected_window",
    )(sel.reshape(-1), qkv, qkv, qkv, qkv, qkv)


def _nsa_rope_tables(S):
    half = NSA_ROPE_DIM // 2
    inv_freq = ROPE_THETA ** (-jnp.arange(half, dtype=F32) / half)
    ang = jnp.arange(S, dtype=F32)[:, None] * inv_freq
    cos, sin = jnp.cos(ang), jnp.sin(ang)
    rest = NSA_HEAD_DIM - NSA_ROPE_DIM
    cos_t = jnp.concatenate([cos, cos, jnp.ones((S, rest), F32)], axis=1)
    sin_t = jnp.concatenate([-sin, sin, jnp.zeros((S, rest), F32)], axis=1)
    return cos_t, sin_t


def _nsa_prep_body(x_ref, cos_ref, sin_ref, o_ref, kv_ref):
    dh, G = NSA_HEAD_DIM, NSA_KV_GROUPS
    half = NSA_ROPE_DIM // 2
    n_q = NSA_HEADS
    cos, sin = cos_ref[...], sin_ref[...]
    low = lax.broadcasted_iota(jnp.int32, cos.shape, 1) < half
    for blk in range(n_q + 6 * G):
        sl = slice(blk * dh, (blk + 1) * dh)
        x = x_ref[:, sl]
        c = (blk - n_q) // G
        if blk < n_q or c in (0, 2, 4):
            swapped = jnp.where(low, pltpu.roll(x, dh - half, axis=1), pltpu.roll(x, half, axis=1))
            x = x * cos + swapped * sin
        if blk < n_q:
            x = x * (dh ** -0.5 * LOG2E)
        elif c in (0, 1):
            kv_ref[:, (blk - n_q) * dh:(blk - n_q + 1) * dh] = x
        o_ref[:, sl] = x.astype(o_ref.dtype)


def nsa_prepare(proj, S, *, tm=256):
    T = proj.shape[0]
    dh, G = NSA_HEAD_DIM, NSA_KV_GROUPS
    n = (NSA_HEADS + 6 * G) * dh
    cos_t, sin_t = _nsa_rope_tables(S)
    tab = pl.BlockSpec((tm, dh), lambda i: (i % (S // tm), 0))
    return pl.pallas_call(
        _nsa_prep_body,
        out_shape=(jax.ShapeDtypeStruct((T, n), BF16), jax.ShapeDtypeStruct((T, 2 * G * dh), F32)),
        grid=(T // tm,),
        in_specs=[pl.BlockSpec((tm, n), lambda i: (i, 0)), tab, tab],
        out_specs=(pl.BlockSpec((tm, n), lambda i: (i, 0)), pl.BlockSpec((tm, 2 * G * dh), lambda i: (i, 0))),
        compiler_params=_cparams("parallel"),
        name="nsa_prepare",
    )(proj, cos_t, sin_t)


def _nsa_compress_body(kc_ref, vc_ref, pe_ref, w1_ref, w2_ref, o_ref):
    L, stride = NSA_CMP_LEN, NSA_CMP_STRIDE
    n_grp = kc_ref.shape[1] // stride
    for c, ref in enumerate((kc_ref, vc_ref)):
        first = jnp.zeros((n_grp, NSA_CMP_HIDDEN), F32)
        second = jnp.zeros((n_grp, NSA_CMP_HIDDEN), F32)
        for l in range(stride):
            rows = ref[0, pl.ds(l, n_grp, stride=stride), :]
            first += jnp.dot((rows + pe_ref[c, l:l + 1, :]).astype(BF16), w1_ref[c, l],
                             preferred_element_type=F32)
            second += jnp.dot((rows + pe_ref[c, stride + l:stride + l + 1, :]).astype(BF16),
                              w1_ref[c, stride + l], preferred_element_type=F32)
        hid = jax.nn.gelu(first + pltpu.roll(second, n_grp - 1, axis=0))
        o_ref[0, 0, c] = jnp.dot(hid.astype(BF16), w2_ref[c], preferred_element_type=F32)


def nsa_compress(kcvc, pe, w1, w2):
    B, S = kcvc.shape[:2]
    G, dh = NSA_KV_GROUPS, NSA_HEAD_DIM
    full = lambda w: pl.BlockSpec(w.shape, lambda b, g: (0,) * w.ndim)
    return pl.pallas_call(
        _nsa_compress_body,
        out_shape=jax.ShapeDtypeStruct((B, G, 2, S // NSA_CMP_STRIDE, dh), F32),
        grid=(B, G),
        in_specs=[pl.BlockSpec((1, S, dh), lambda b, g: (b, 0, g)),
                  pl.BlockSpec((1, S, dh), lambda b, g: (b, 0, G + g)),
                  full(pe), full(w1), full(w2)],
        out_specs=pl.BlockSpec((1, 1, 2, S // NSA_CMP_STRIDE, dh), lambda b, g: (b, g, 0, 0, 0)),
        compiler_params=_cparams("parallel", "parallel"),
        name="nsa_compress",
    )(kcvc, kcvc, pe, w1, w2)


NSA_CMP_TILE = 512


def _nsa_cmp_body(q_ref, cmp_ref, o_ref, sel_ref):
    dh, blk = NSA_HEAD_DIM, NSA_SEL_BLOCK
    R = NSA_HEADS // NSA_KV_GROUPS
    tq = q_ref.shape[1]
    n_cmp = cmp_ref.shape[3]
    nq = tq // blk
    i = pl.program_id(2)
    k_cmp = cmp_ref[0, 0, 0].astype(BF16)
    v_cmp = cmp_ref[0, 0, 1].astype(BF16)
    pos = i * tq + lax.broadcasted_iota(jnp.int32, (tq, 1), 0)
    n_id = lax.broadcasted_iota(jnp.int32, (1, n_cmp), 1)
    visible = n_id * NSA_CMP_STRIDE + (NSA_CMP_LEN - 1) <= pos
    start = lax.broadcasted_iota(jnp.int32, (n_cmp, LANE), 0) * NSA_CMP_STRIDE
    bstart = lax.broadcasted_iota(jnp.int32, (n_cmp, LANE), 1) * blk
    overlap = jnp.where((start <= bstart + blk - 1) & (start + NSA_CMP_LEN - 1 >= bstart), 1.0, 0.0).astype(BF16)
    heads = [slice(r * dh, (r + 1) * dh) for r in range(R)]
    ss = [lax.dot_general(q_ref[0, :, h], k_cmp, (((1,), (1,)), ((), ())), preferred_element_type=F32)
          for h in heads]
    ss = [jnp.where(visible, s, NEG_INF) for s in ss]
    es = [jnp.exp2(s - jnp.max(s, axis=-1, keepdims=True)) for s in ss]
    ps = [jnp.where(visible, e / jnp.sum(e, axis=-1, keepdims=True), 0.0).astype(BF16) for e in es]
    v_and_overlap = jnp.concatenate([v_cmp, overlap], axis=1)
    outs = [jnp.dot(p, v_and_overlap, preferred_element_type=F32) for p in ps]
    for h, o in zip(heads, outs):
        o_ref[0, :, h] = o[:, :dh].astype(o_ref.dtype)
    imp = sum(o[:, dh:] for o in outs)
    imp = imp.reshape(nq, blk, LANE).sum(axis=1)
    qb = i * nq + lax.broadcasted_iota(jnp.int32, (nq, 1), 0)
    kb = lax.broadcasted_iota(jnp.int32, (nq, LANE), 1)
    forced = (kb == 0) | (kb == qb) | (kb == qb - 1)
    val = jnp.where(forced, jnp.inf, jnp.where(kb <= qb, imp, -jnp.inf))
    kb_f = kb.astype(F32)
    avail = kb >= 0
    picks = jnp.zeros((nq, LANE), F32)
    for t in range(NSA_N_SEL):
        best = jnp.max(jnp.where(avail, val, -jnp.inf), axis=-1, keepdims=True)
        pick = jnp.min(jnp.where(avail & (val == best), kb_f, float(LANE)), axis=-1, keepdims=True)
        picks = jnp.where(kb == t, pick, picks)
        avail = avail & (kb_f != pick)
    sel_ref[0, 0] = picks.astype(jnp.int32)


def nsa_compressed_attention(qkv, cmp):
    B, S = qkv.shape[:2]
    G, dh, HD, tq = NSA_KV_GROUPS, NSA_HEAD_DIM, NSA_HEADS * NSA_HEAD_DIM, NSA_CMP_TILE
    nq = tq // NSA_SEL_BLOCK
    qspec = pl.BlockSpec((1, tq, HD // G), lambda b, g, i: (b, i, g))
    return pl.pallas_call(
        _nsa_cmp_body,
        out_shape=(jax.ShapeDtypeStruct((B, S, HD), BF16),
                   jax.ShapeDtypeStruct((B, G, S // NSA_SEL_BLOCK, LANE), jnp.int32)),
        grid=(B, G, S // tq),
        in_specs=[qspec, pl.BlockSpec((1, 1) + cmp.shape[2:], lambda b, g, i: (b, g, 0, 0, 0))],
        out_specs=(qspec, pl.BlockSpec((1, 1, nq, LANE), lambda b, g, i: (b, g, i, 0))),
        compiler_params=_cparams("parallel", "parallel", "arbitrary"),
        name="nsa_compressed_attention",
    )(qkv, cmp)


def _nsa_combine_body(z_ref, oc_ref, os_ref, ow_ref, o_ref):
    dh = NSA_HEAD_DIM
    gates = jax.nn.sigmoid(z_ref[...])
    for hh in range(NSA_HEADS):
        sl = slice(hh * dh, (hh + 1) * dh)
        o_ref[:, sl] = (gates[:, 3 * hh:3 * hh + 1] * oc_ref[:, sl].astype(F32)
                        + gates[:, 3 * hh + 1:3 * hh + 2] * os_ref[:, sl].astype(F32)
                        + gates[:, 3 * hh + 2:3 * hh + 3] * ow_ref[:, sl].astype(F32)).astype(o_ref.dtype)


def nsa_combine(proj, o_cmp, o_sel, o_win, *, tm=256):
    T, HD = o_cmp.shape
    row = pl.BlockSpec((tm, HD), lambda i: (i, 0))
    gate_blk = (NSA_HEADS + 6 * NSA_KV_GROUPS) * NSA_HEAD_DIM // LANE
    return pl.pallas_call(
        _nsa_combine_body,
        out_shape=jax.ShapeDtypeStruct((T, HD), BF16),
        grid=(T // tm,),
        in_specs=[pl.BlockSpec((tm, LANE), lambda i: (i, gate_blk)), row, row, row],
        out_specs=row,
        compiler_params=_cparams("parallel"),
        name="nsa_combine",
    )(proj, o_cmp, o_sel, o_win)


FOX_TILE = 256
FOX_HEADS_PER_STEP = 8


def _fox_body(q_ref, k_ref, v_ref, cq_ref, ck_ref, o_ref):
    t, dh = FOX_TILE, FOX_HEAD_DIM
    nh = q_ref.shape[2] // dh
    heads = [slice(dh * j, dh * (j + 1)) for j in range(nh)]
    i = pl.program_id(2)
    qs = [q_ref[0, :, h] for h in heads]
    cq_all = cq_ref[0]
    head_lane = lax.broadcasted_iota(jnp.int32, cq_all.shape, 1) - pl.program_id(1) * nh
    cqs = [jnp.sum(jnp.where(head_lane == j, cq_all, 0.0), axis=-1, keepdims=True) for j in range(nh)]
    causal = (lax.broadcasted_iota(jnp.int32, (t, t), 0) >= lax.broadcasted_iota(jnp.int32, (t, t), 1))
    ones = jnp.ones((t, dh), BF16)

    def step(j, carry, diagonal):
        k0 = pl.multiple_of(j * t, t)
        ss = [lax.dot_general(q, k_ref[0, pl.ds(k0, t), h], (((1,), (1,)), ((), ())),
                              preferred_element_type=F32) + (cq - ck_ref[0, n, :, pl.ds(k0, t)])
              for n, (q, cq, h) in enumerate(zip(qs, cqs, heads))]
        if diagonal:
            ss = [jnp.where(causal, s, NEG_INF) for s in ss]
        m_new = [jnp.maximum(m, jnp.max(s, axis=-1, keepdims=True)) for (m, _), s in zip(carry, ss)]
        ps = [jnp.exp2(s - m).astype(BF16) for s, m in zip(ss, m_new)]
        pv = [jnp.dot(p, jnp.concatenate([v_ref[0, pl.ds(k0, t), h], ones], axis=1),
                      preferred_element_type=F32) for p, h in zip(ps, heads)]
        return tuple((mn, jnp.exp2(m - mn) * acc + x) for (m, acc), mn, x in zip(carry, m_new, pv))

    init = tuple((jnp.full((t, 1), NEG_INF, F32), jnp.zeros((t, 2 * dh), F32)) for _ in heads)
    carry = lax.fori_loop(0, i, lambda j, c: step(j, c, False), init)
    carry = step(i, carry, True)
    for (_, acc), h in zip(carry, heads):
        o_ref[0, :, h] = (acc[:, :dh] / acc[:, dh:]).astype(o_ref.dtype)


def _fox_gate_body(z_ref, b_ref, o_ref):
    x = z_ref[0] + b_ref[...]
    acc = -(jnp.maximum(-x, 0.0) + jnp.log(1.0 + jnp.exp(-jnp.abs(x))))
    row = lax.broadcasted_iota(jnp.int32, acc.shape, 0)
    shift = 1
    while shift < acc.shape[0]:
        acc = acc + jnp.where(row >= shift, pltpu.roll(acc, shift, axis=0), 0.0)
        shift *= 2
    o_ref[0] = acc * LOG2E


def fox_gate_cumsum(z, b):
    B, S, L = z.shape
    blk = pl.BlockSpec((1, S, L), lambda i: (i, 0, 0))
    return pl.pallas_call(
        _fox_gate_body,
        out_shape=jax.ShapeDtypeStruct((B, S, L), F32),
        grid=(B,),
        in_specs=[blk, pl.BlockSpec((1, L), lambda i: (0, 0))],
        out_specs=blk,
        compiler_params=_cparams("parallel"),
        name="fox_gate_cumsum",
    )(z, b.reshape(1, L))


def fox_attention(proj, cum_q, cum_k):
    B, S = proj.shape[:2]
    H, dh, t, nh = FOX_HEADS, FOX_HEAD_DIM, FOX_TILE, FOX_HEADS_PER_STEP
    G = H // nh
    return pl.pallas_call(
        _fox_body,
        out_shape=jax.ShapeDtypeStruct((B, S, H * dh), BF16),
        grid=(B, G, S // t),
        in_specs=[
            pl.BlockSpec((1, t, nh * dh), lambda b, h, i: (b, i, h)),
            pl.BlockSpec((1, S, nh * dh), lambda b, h, i: (b, 0, G + h)),
            pl.BlockSpec((1, S, nh * dh), lambda b, h, i: (b, 0, 2 * G + h)),
            pl.BlockSpec((1, t, LANE), lambda b, h, i: (b, i, 0)),
            pl.BlockSpec((1, nh, 1, S), lambda b, h, i: (b, h, 0, 0)),
        ],
        out_specs=pl.BlockSpec((1, t, nh * dh), lambda b, h, i: (b, i, h)),
        compiler_params=_cparams("parallel", "parallel", "arbitrary"),
        name="fox_attention",
    )(proj, proj, proj, cum_q, cum_k)


RET_HEADS_PER_STEP = 2


def _ret_body(lg_ref, q_ref, k_ref, v_ref, g_ref, cos_ref, sin_ref, gn_ref, o_ref, r_ref):
    C, dk, dv = RET_CHUNK, RET_QK_DIM, RET_V_DIM
    half = dk // 2
    nh = r_ref.shape[0]
    qk = [slice(dk * n, dk * (n + 1)) for n in range(nh)]
    vv = [slice(dv * n, dv * (n + 1)) for n in range(nh)]
    lgs = [lg_ref[pl.program_id(1) * nh + n] for n in range(nh)]
    ii = lax.broadcasted_iota(jnp.int32, (C, C), 0)
    jj = lax.broadcasted_iota(jnp.int32, (C, C), 1)
    ti = lax.broadcasted_iota(jnp.int32, (C, 1), 0).astype(F32)
    decay_mask = [jnp.where(ii >= jj, jnp.exp((ii - jj).astype(F32) * lg), 0.0) for lg in lgs]
    q_scale = [jnp.exp((ti + 1.0) * lg) for lg in lgs]
    k_scale = [jnp.exp((C - 1.0 - ti) * lg) for lg in lgs]
    chunk_decay = [jnp.exp(jnp.full((1, 1), C, F32) * lg) for lg in lgs]
    r_ref[...] = jnp.zeros_like(r_ref)

    def rot(x, cos, sin):
        x1, x2 = x[:, :half], x[:, half:]
        return jnp.concatenate([x1 * cos - x2 * sin, x2 * cos + x1 * sin], axis=1)

    def chunk(c, carry):
        sl = pl.ds(pl.multiple_of(c * C, C), C)
        cos, sin = cos_ref[sl, :], sin_ref[sl, :]
        qs = [rot(q_ref[0, sl, h].astype(F32), cos, sin).astype(BF16) for h in qk]
        ks = [rot(k_ref[0, sl, h].astype(F32), cos, sin) * (dk ** -0.5) for h in qk]
        vs = [v_ref[0, sl, h] for h in vv]
        inner = [lax.dot_general(q, k.astype(BF16), (((1,), (1,)), ((), ())), preferred_element_type=F32) * dm
                 for q, k, dm in zip(qs, ks, decay_mask)]
        states = [r_ref[n] for n in range(nh)]
        cross = [jnp.dot(q, st.astype(BF16), preferred_element_type=F32) * sc
                 for q, st, sc in zip(qs, states, q_scale)]
        os_ = [jnp.dot(x.astype(BF16), v, preferred_element_type=F32) + cr for x, v, cr in zip(inner, vs, cross)]
        upd = [lax.dot_general((k * sc).astype(BF16), v, (((0,), (0,)), ((), ())), preferred_element_type=F32)
               for k, sc, v in zip(ks, k_scale, vs)]
        for n in range(nh):
            r_ref[n] = states[n] * chunk_decay[n] + upd[n]
        for o, h in zip(os_, vv):
            d = o - jnp.mean(o, axis=-1, keepdims=True)
            on = d * lax.rsqrt(jnp.mean(d * d, axis=-1, keepdims=True) + RET_GN_EPS)
            g = g_ref[0, sl, h].astype(F32)
            o_ref[0, sl, h] = ((g * jax.nn.sigmoid(g)) * (on * gn_ref[:, h])).astype(o_ref.dtype)
        return carry

    lax.fori_loop(0, q_ref.shape[1] // C, chunk, 0)


def retention_core(proj, gn_g):
    B, S = proj.shape[:2]
    H, dk, dv, nh = RET_HEADS, RET_QK_DIM, RET_V_DIM, RET_HEADS_PER_STEP
    G = H // nh
    pos = jnp.arange(S, dtype=F32)
    inv_freq = RET_THETA ** (-jnp.arange(dk // 2, dtype=F32) / (dk // 2))
    ang = pos[:, None] * inv_freq
    log_gamma = jnp.log(1.0 - 2.0 ** (-5.0 - jnp.arange(H, dtype=F32)))
    tab = pl.BlockSpec((S, dk // 2), lambda b, h: (0, 0))
    v_first = 2 * H * dk // (nh * dv)
    return pl.pallas_call(
        _ret_body,
        out_shape=jax.ShapeDtypeStruct((B, S, H * dv), BF16),
        grid=(B, G),
        in_specs=[
            pl.BlockSpec(memory_space=pltpu.SMEM),
            pl.BlockSpec((1, S, nh * dk), lambda b, h: (b, 0, h)),
            pl.BlockSpec((1, S, nh * dk), lambda b, h: (b, 0, G + h)),
            pl.BlockSpec((1, S, nh * dv), lambda b, h: (b, 0, v_first + h)),
            pl.BlockSpec((1, S, nh * dv), lambda b, h: (b, 0, v_first + G + h)),
            tab, tab,
            pl.BlockSpec((1, nh * dv), lambda b, h: (0, h)),
        ],
        out_specs=pl.BlockSpec((1, S, nh * dv), lambda b, h: (b, 0, h)),
        scratch_shapes=[pltpu.VMEM((nh, dk, dv), F32)],
        compiler_params=_cparams("parallel", "parallel"),
        name="retention_core",
    )(log_gamma, proj, proj, proj, proj, jnp.cos(ang), jnp.sin(ang), gn_g.reshape(1, H * dv))


def _pad_cols(w, n):
    return jnp.pad(w, ((0, 0), (0, n - w.shape[1])))


NSA_IN_PADDED = 42 * LANE


def _nsa_mixer(h, g, w_in, cmp_pe, cmp_w1, cmp_w2, B, S):
    T = h.shape[0]
    proj = norm_matmul(h, g, _pad_cols(w_in, NSA_IN_PADDED).astype(BF16))
    qkv, kcvc = nsa_prepare(proj, S)
    cmp = nsa_compress(kcvc.reshape(B, S, -1), cmp_pe, cmp_w1.astype(BF16), cmp_w2.astype(BF16))
    qkv = qkv.reshape(B, S, -1)
    o_cmp, sel = nsa_compressed_attention(qkv, cmp)
    o_sel, o_win = nsa_selected_window(sel[..., :NSA_N_SEL], qkv)
    return nsa_combine(proj, o_cmp.reshape(T, -1), o_sel.reshape(T, -1), o_win.reshape(T, -1))


def _rwkv_mixer(h, g, mu, w_rkv, w0, w1, w2, a0, a1, a2, g1, g2, k_k, k_a, r_k, ln_gb, B, S):
    T, D = h.shape
    pad_c = lambda w: _pad_cols(w, LANE)
    pad_r = lambda w: jnp.pad(w, ((0, LANE - w.shape[0]), (0, 0)))
    w_main = jnp.concatenate([w_rkv[0], w_rkv[1], w_rkv[2]], axis=1).astype(BF16)
    w_hidden = jnp.concatenate([pad_c(w1), pad_c(a1), g1], axis=1).astype(BF16)
    proj, hidden = rwkv_mix_project(h, g, mu, w_main, w_hidden, S)
    lw, a, kk, k2, gate = rwkv_gates(proj, hidden, jnp.stack([w0, a0, k_k, k_a]), pad_r(w2).astype(BF16),
                                     pad_r(a2).astype(BF16), g2.astype(BF16))
    as3 = lambda t: t.reshape(B, S, -1)
    y = rwkv_recurrence(as3(proj), as3(lw), as3(k2), as3(kk), as3(a))
    vec = jnp.stack([ln_gb[0], ln_gb[1], r_k.reshape(D)])
    return rwkv_post(y.reshape(T, D), proj, k2, gate, vec)


def _fox_mixer(h, g, w_in, b_f, B, S):
    HD = FOX_HEADS * FOX_HEAD_DIM
    w_qkv = jnp.concatenate([w_in[:, :HD] * (FOX_HEAD_DIM ** -0.5 * LOG2E), w_in[:, HD:3 * HD]], axis=1)
    proj = norm_matmul(h, g, w_qkv.astype(BF16), out_dtype=BF16)
    z = norm_matmul(h, g, _pad_cols(w_in[:, 3 * HD:], LANE).astype(BF16))
    cum = fox_gate_cumsum(z.reshape(B, S, LANE), jnp.pad(b_f, (0, LANE - FOX_HEADS)))
    cum_k = jnp.transpose(cum[:, :, :FOX_HEADS], (0, 2, 1))[:, :, None, :]
    return fox_attention(proj.reshape(B, S, 3 * HD), cum, cum_k).reshape(B * S, HD)


def kernel(x, p, norm_g, ffn_w_in, ffn_w_out, ple_w_proj, ple_w_gate, nsa_w_in, nsa_cmp_pe, nsa_cmp_w1, nsa_cmp_w2, nsa_w_out, rwkv_mu, rwkv_w_rkv, rwkv_w0, rwkv_w1, rwkv_w2, rwkv_a0, rwkv_a1, rwkv_a2, rwkv_g1, rwkv_g2, rwkv_k_k, rwkv_k_a, rwkv_r_k, rwkv_ln, rwkv_w_out, fox_w_in, fox_b_f, fox_w_out, ret_w_in, ret_gn_g, ret_w_out):
    B, S, D = x.shape
    T = B * S
    h = x.reshape(T, D)
    bf = lambda w: w.astype(BF16)
    ffn_w_in, ffn_w_out, ple_w_proj, ple_w_gate = bf(ffn_w_in), bf(ffn_w_out), bf(ple_w_proj), bf(ple_w_gate)
    p = p.reshape(DEPTH, T, PLE_DIM)
    for i in range(DEPTH):
        m, j = i % N_MIXERS, i // N_MIXERS
        ng = norm_g[i]
        h = ffn_half_step(h, ng[0], ng[1], ffn_w_in, ffn_w_out, i, 0)
        if m == 0:
            y = _nsa_mixer(h, ng[2], nsa_w_in[j], nsa_cmp_pe[j], nsa_cmp_w1[j], nsa_cmp_w2[j], B, S)
            w_out = nsa_w_out[j]
        elif m == 1:
            y = _rwkv_mixer(h, ng[2], rwkv_mu[j], rwkv_w_rkv[j], rwkv_w0[j], rwkv_w1[j], rwkv_w2[j],
                            rwkv_a0[j], rwkv_a1[j], rwkv_a2[j], rwkv_g1[j], rwkv_g2[j],
                            rwkv_k_k[j], rwkv_k_a[j], rwkv_r_k[j], rwkv_ln[j], B, S)
            w_out = rwkv_w_out[j]
        elif m == 2:
            y = _fox_mixer(h, ng[2], fox_w_in[j], fox_b_f[j], B, S)
            w_out = fox_w_out[j]
        else:
            proj = norm_matmul(h, ng[2], bf(ret_w_in[j]), out_dtype=BF16)
            y = retention_core(proj.reshape(B, S, RET_IN), ret_gn_g[j]).reshape(T, -1)
            w_out = ret_w_out[j]
        h = matmul_norm_residual(y, bf(w_out), ng[3], h)
        h = ffn_half_step(h, ng[4], ng[5], ffn_w_in, ffn_w_out, i, 1)
        h = ple_step(h, p, ng[6], ng[7], ple_w_proj, ple_w_gate, i)
    return h.reshape(B, S, D)
```
